```python
import math
import jax, jax.numpy as jnp
from jax import lax
import numpy as np

D_MODEL = 2048
BATCH = 8
SEQ = 4096
DEPTH = 4
DEC_BATCH = 2
DEC_SEQ = 4096
PAST_LEN = 128

HEAD_DIM = 128
GRID_W = 64
A_HEADS = 8
A_KV = 2
A_WINDOW = 128
B_HEADS = 8
B_KV = 2
ROPE_THETA = 10000.0
QBLK = 128
C_HEADS = 8
C_KV = 2
NA_ROWS = 8
NA_COLS = 16
NA_QCOLS = 16
NA_KSPAN = 32
D_PAIRS = ((128, 1), (512, 4), (2048, 16))
D_SLOTS = 4
D_HEADS = D_SLOTS * len(D_PAIRS)
T5_BUCKETS = 32
T5_MAX_DIST = 2048
T5_COLS = A_HEADS + D_HEADS
D_FF = 5632
NORM_EPS = 1e-6
NEG_INF = -1e30
ATTN_SCALE = HEAD_DIM ** -0.5

AB_SIZES = (A_HEADS * HEAD_DIM, A_KV * HEAD_DIM, A_KV * HEAD_DIM,
            B_HEADS * HEAD_DIM, B_KV * HEAD_DIM, B_KV * HEAD_DIM)
CD_SIZES = (C_HEADS * HEAD_DIM, C_KV * HEAD_DIM, C_KV * HEAD_DIM,
            D_HEADS * HEAD_DIM, D_SLOTS * HEAD_DIM, D_SLOTS * HEAD_DIM)
AB_IN = sum(AB_SIZES)
AB_OUT = (A_HEADS + B_HEADS) * HEAD_DIM
CD_IN = sum(CD_SIZES)
CD_OUT = (C_HEADS + D_SLOTS) * HEAD_DIM
N_EVEN = (DEPTH + 1) // 2
N_ODD = DEPTH // 2

kernel_name = 'hybrid_bidir_encoder_macaron'


def _rms_norm(x, g):
    xf = x.astype(jnp.float32)
    y = xf * lax.rsqrt(jnp.mean(xf * xf, axis=-1, keepdims=True) + NORM_EPS)
    return (y * g.astype(jnp.float32)).astype(x.dtype)


def _swiglu(x, w_in, w_out):
    gate, up = jnp.split(x @ w_in, 2, axis=-1)
    return (jax.nn.silu(gate) * up) @ w_out


def _split(x, sizes):
    idx = [int(i) for i in np.cumsum(sizes)[:-1]]
    return jnp.split(x, idx, axis=-1)


def _t5_bucket(rel):
    half = T5_BUCKETS // 2
    max_exact = half // 2
    n = jnp.abs(rel)
    nf = jnp.maximum(n, 1).astype(jnp.float32)
    large = max_exact + (jnp.log(nf / max_exact) / math.log(T5_MAX_DIST / max_exact)
                         * (half - max_exact)).astype(jnp.int32)
    large = jnp.minimum(large, half - 1)
    return jnp.where(rel > 0, half, 0) + jnp.where(n < max_exact, n, large)


def _t5_bias(table_cols, rel, n_kv):
    tb = table_cols.astype(jnp.float32)[_t5_bucket(rel)]
    h = table_cols.shape[1]
    return jnp.moveaxis(tb, -1, 0).reshape((n_kv, h // n_kv) + rel.shape)


def _band_rel(blk):
    iq = jnp.arange(blk)[:, None]
    j = jnp.arange(3 * blk)[None, :]
    return j - blk - iq


def _banded_attention(q, k, v, bias, band, blk):
    n, L, kv, g, dh = q.shape
    nb = -(-L // blk)
    pad = nb * blk - L
    qp = jnp.pad(q, ((0, 0), (0, pad), (0, 0), (0, 0), (0, 0)))
    kp = jnp.pad(k, ((0, 0), (blk, blk + pad), (0, 0), (0, 0)))
    vp = jnp.pad(v, ((0, 0), (blk, blk + pad), (0, 0), (0, 0)))
    idx = jnp.arange(nb)[:, None] * blk + jnp.arange(3 * blk)[None, :]
    kpos = idx - blk
    valid = band[None] & ((kpos >= 0) & (kpos < L))[:, None, :]
    kb = kp[:, idx]
    vb = vp[:, idx]
    qb = qp.reshape(n, nb, blk, kv, g, dh)
    s = jnp.einsum('nbqkgd,nbjkd->nbkgqj', qb, kb).astype(jnp.float32) * ATTN_SCALE + bias
    s = jnp.where(valid[None, :, None, None], s, NEG_INF)
    m = jnp.max(s, axis=-1, keepdims=True)
    p = jnp.exp(s - m)
    l = jnp.sum(p, axis=-1, keepdims=True)
    o = jnp.einsum('nbkgqj,nbjkd->nbqkgd', (p / l).astype(v.dtype), vb).astype(jnp.float32)
    lse = jnp.transpose((m + jnp.log(l))[..., 0], (0, 1, 4, 2, 3))
    o = o.reshape(n, nb * blk, kv, g, dh)[:, :L]
    lse = lse.reshape(n, nb * blk, kv, g)[:, :L]
    return o, lse


def _axial_rope(x, t):
    n_pairs = HEAD_DIM // 2
    n_freq = n_pairs // 2
    pos = jnp.arange(t)
    row = (pos // GRID_W).astype(jnp.float32)
    col = (pos % GRID_W).astype(jnp.float32)
    omega = ROPE_THETA ** (-(jnp.arange(n_freq, dtype=jnp.float32) * 2.0 / n_pairs))
    ang = jnp.concatenate([row[:, None] * omega, col[:, None] * omega], axis=-1)
    bshape = (1, t) + (1,) * (x.ndim - 3) + (n_pairs,)
    cos = jnp.cos(ang).reshape(bshape)
    sin = jnp.sin(ang).reshape(bshape)
    xr = x.astype(jnp.float32).reshape(x.shape[:-1] + (n_pairs, 2))
    x0, x1 = xr[..., 0], xr[..., 1]
    out = jnp.stack([x0 * cos - x1 * sin, x0 * sin + x1 * cos], axis=-1)
    return out.reshape(x.shape).astype(x.dtype)


def _dense_blocked(q, k, v):
    b, t = q.shape[:2]
    nb = t // QBLK
    qb = jnp.moveaxis(q.reshape((b, nb, QBLK) + q.shape[2:]), 1, 0)

    def one_block(qblk):
        s = jnp.einsum('bqkgd,btkd->bkgqt', qblk, k).astype(jnp.float32) * ATTN_SCALE
        p = jax.nn.softmax(s, axis=-1)
        return jnp.einsum('bkgqt,btkd->bqkgd', p.astype(v.dtype), v)

    o = lax.map(one_block, qb)
    return jnp.moveaxis(o, 0, 1).reshape(b, t, -1)


def _neighbourhood_attention(q, k, v, rpb):
    b, t, kvh, g, dh = q.shape
    rows = t // GRID_W
    kr = min(NA_ROWS, rows)
    ncb = GRID_W // NA_QCOLS
    r = jnp.arange(rows)
    rs = jnp.clip(r - NA_ROWS // 2, 0, rows - kr)
    row_idx = rs[:, None] + jnp.arange(kr)[None, :]
    jb = jnp.arange(ncb)
    span0 = jnp.clip(jb * NA_QCOLS - NA_COLS // 2, 0, GRID_W - NA_KSPAN)
    col_idx = span0[:, None] + jnp.arange(NA_KSPAN)[None, :]
    qc = jb[:, None] * NA_QCOLS + jnp.arange(NA_QCOLS)[None, :]
    cs = jnp.clip(qc - NA_COLS // 2, 0, GRID_W - NA_COLS)
    colmask = (col_idx[:, None, :] >= cs[..., None]) & (col_idx[:, None, :] < cs[..., None] + NA_COLS)
    nkeys = kr * NA_KSPAN
    mask = jnp.broadcast_to(colmask[:, :, None, :], (ncb, NA_QCOLS, kr, NA_KSPAN)).reshape(ncb, NA_QCOLS, nkeys)
    ri = row_idx[:, None, :, None]
    ci = col_idx[None, :, None, :]
    kg = k.reshape(b, rows, GRID_W, kvh, dh)[:, ri, ci].reshape(b, rows, ncb, nkeys, kvh, dh)
    vg = v.reshape(b, rows, GRID_W, kvh, dh)[:, ri, ci].reshape(b, rows, ncb, nkeys, kvh, dh)
    qg = q.reshape(b, rows, ncb, NA_QCOLS, kvh, g, dh)
    dr_i = (row_idx - r[:, None]) + NA_ROWS - 1
    dc_i = jnp.clip(col_idx[:, None, :] - qc[..., None] + NA_COLS - 1, 0, 2 * NA_COLS - 2)
    rb = rpb.astype(jnp.float32)[:, dr_i[:, None, None, :, None], dc_i[None, :, :, None, :]]
    rb = jnp.transpose(rb.reshape(kvh, g, rows, ncb, NA_QCOLS, nkeys), (2, 3, 0, 1, 4, 5))
    s = jnp.einsum('brjqkgd,brjnkd->brjkgqn', qg, kg).astype(jnp.float32) * ATTN_SCALE
    s = jnp.where(mask[None, None, :, None, None], s + rb[None], NEG_INF)
    p = jax.nn.softmax(s, axis=-1)
    o = jnp.einsum('brjkgqn,brjnkd->brjqkgd', p.astype(v.dtype), vg)
    return o.reshape(b, t, kvh * g * dh)


def _to_sub(a, d):
    b, t = a.shape[:2]
    a = a.reshape((b, t // d, d) + a.shape[2:])
    return jnp.swapaxes(a, 1, 2).reshape((b * d, t // d) + a.shape[3:])


def _from_sub(a, b, d):
    n, u = a.shape[:2]
    a = a.reshape((b, d, u) + a.shape[2:])
    return jnp.swapaxes(a, 1, 2).reshape((b, u * d) + a.shape[3:])


def _mixer_ab(h, w_in, sink, q_gain, k_gain, w_out, t5_table):
    b, t, _ = h.shape
    qa, ka, va, qb, kb, vb = _split(h @ w_in, AB_SIZES)
    ga = A_HEADS // A_KV
    qa = qa.reshape(b, t, A_KV, ga, HEAD_DIM)
    ka = ka.reshape(b, t, A_KV, HEAD_DIM)
    va = va.reshape(b, t, A_KV, HEAD_DIM)
    rel = _band_rel(A_WINDOW)
    band = jnp.abs(rel) <= A_WINDOW
    bias_a = _t5_bias(t5_table[:, :A_HEADS], rel, A_KV)
    o_a, lse_a = _banded_attention(qa, ka, va, bias_a, band, A_WINDOW)
    keep = jax.nn.sigmoid(lse_a - sink.astype(jnp.float32).reshape(A_KV, ga))
    o_a = (o_a * keep[..., None]).astype(h.dtype).reshape(b, t, -1)
    gb = B_HEADS // B_KV
    qb = _axial_rope(_rms_norm(qb.reshape(b, t, B_KV, gb, HEAD_DIM), q_gain), t)
    kb = _axial_rope(_rms_norm(kb.reshape(b, t, B_KV, HEAD_DIM), k_gain), t)
    o_b = _dense_blocked(qb, kb, vb.reshape(b, t, B_KV, HEAD_DIM)).astype(h.dtype)
    return jnp.concatenate([o_a, o_b], axis=-1) @ w_out


def _mixer_cd(h, w_in, rpb, w_out, t5_table):
    b, t, _ = h.shape
    qc, kc, vc, qd, kd, vd = _split(h @ w_in, CD_SIZES)
    gc = C_HEADS // C_KV
    o_c = _neighbourhood_attention(qc.reshape(b, t, C_KV, gc, HEAD_DIM),
                                   kc.reshape(b, t, C_KV, HEAD_DIM),
                                   vc.reshape(b, t, C_KV, HEAD_DIM), rpb).astype(h.dtype)
    qd = qd.reshape(b, t, len(D_PAIRS), D_SLOTS, 1, HEAD_DIM)
    kd = kd.reshape(b, t, D_SLOTS, HEAD_DIM)
    vd = vd.reshape(b, t, D_SLOTS, HEAD_DIM)
    outs, lses = [], []
    for gi, (win, dil) in enumerate(D_PAIRS):
        hs = win // (2 * dil)
        rel_sub = _band_rel(hs)
        band = jnp.abs(rel_sub) <= hs
        col0 = A_HEADS + gi * D_SLOTS
        bias_g = _t5_bias(t5_table[:, col0:col0 + D_SLOTS], rel_sub * dil, D_SLOTS)
        o_g, lse_g = _banded_attention(_to_sub(qd[:, :, gi], dil), _to_sub(kd, dil),
                                       _to_sub(vd, dil), bias_g, band, hs)
        outs.append(_from_sub(o_g, b, dil))
        lses.append(_from_sub(lse_g, b, dil))
    wts = jax.nn.softmax(jnp.stack(lses), axis=0)
    o_d = jnp.sum(wts[..., None] * jnp.stack(outs), axis=0).astype(h.dtype).reshape(b, t, -1)
    return jnp.concatenate([o_c, o_d], axis=-1) @ w_out


def _trunk(x, p):
    for l in range(DEPTH):
        i = l // 2
        x = x + 0.5 * _swiglu(_rms_norm(x, p['norm_ffn1'][l]), p['ffn1_w_in'][l], p['ffn1_w_out'][l])
        hn = _rms_norm(x, p['norm_mix'][l])
        if l % 2 == 0:
            x = x + _mixer_ab(hn, p['ab_w_in'][i], p['ab_sink'][i], p['ab_q_gain'][i],
                              p['ab_k_gain'][i], p['ab_w_out'][i], p['t5_table'])
        else:
            x = x + _mixer_cd(hn, p['cd_w_in'][i], p['cd_rpb'][i], p['cd_w_out'][i], p['t5_table'])
        x = x + 0.5 * _swiglu(_rms_norm(x, p['norm_ffn2'][l]), p['ffn2_w_in'][l], p['ffn2_w_out'][l])
    return _rms_norm(x, p['final_norm'])


def setup_inputs(seed: int = 0) -> dict:
    key = jax.random.key(seed)
    ks = jax.random.split(key, 20)
    f32 = jnp.float32

    def w(k, shape, fan_in):
        return jax.random.normal(k, shape, f32) * fan_in ** -0.5

    def gain(k, shape):
        return 1.0 + 0.02 * jax.random.normal(k, shape, f32)

    return {
        'x_prompt': jax.random.normal(ks[0], (BATCH, SEQ, D_MODEL), f32),
        'x_sample': jax.random.normal(ks[1], (DEC_BATCH, DEC_SEQ, D_MODEL), f32),
        'norm_ffn1': gain(ks[2], (DEPTH, D_MODEL)),
        'ffn1_w_in': w(ks[3], (DEPTH, D_MODEL, 2 * D_FF), D_MODEL),
        'ffn1_w_out': w(ks[4], (DEPTH, D_FF, D_MODEL), D_FF),
        'norm_mix': gain(ks[5], (DEPTH, D_MODEL)),
        'ab_w_in': w(ks[6], (N_EVEN, D_MODEL, AB_IN), D_MODEL),
        'ab_sink': jax.random.normal(ks[7], (N_EVEN, A_HEADS), f32),
        'ab_q_gain': gain(ks[8], (N_EVEN, HEAD_DIM)),
        'ab_k_gain': gain(ks[9], (N_EVEN, HEAD_DIM)),
        'ab_w_out': w(ks[10], (N_EVEN, AB_OUT, D_MODEL), AB_OUT),
        'cd_w_in': w(ks[11], (N_ODD, D_MODEL, CD_IN), D_MODEL),
        'cd_rpb': 0.1 * jax.random.normal(ks[12], (N_ODD, C_HEADS, 2 * NA_ROWS - 1, 2 * NA_COLS - 1), f32),
        'cd_w_out': w(ks[13], (N_ODD, CD_OUT, D_MODEL), CD_OUT),
        'norm_ffn2': gain(ks[14], (DEPTH, D_MODEL)),
        'ffn2_w_in': w(ks[15], (DEPTH, D_MODEL, 2 * D_FF), D_MODEL),
        'ffn2_w_out': w(ks[16], (DEPTH, D_FF, D_MODEL), D_FF),
        't5_table': 0.1 * jax.random.normal(ks[17], (T5_BUCKETS, T5_COLS), f32),
        'final_norm': gain(ks[18], (D_MODEL,)),
    }


def reference(x_prompt, x_sample, norm_ffn1, ffn1_w_in, ffn1_w_out, norm_mix, ab_w_in, ab_sink,
              ab_q_gain, ab_k_gain, ab_w_out, cd_w_in, cd_rpb, cd_w_out, norm_ffn2, ffn2_w_in,
              ffn2_w_out, t5_table, final_norm):
    params = {
        'norm_ffn1': norm_ffn1, 'ffn1_w_in': ffn1_w_in, 'ffn1_w_out': ffn1_w_out,
        'norm_mix': norm_mix, 'ab_w_in': ab_w_in, 'ab_sink': ab_sink,
        'ab_q_gain': ab_q_gain, 'ab_k_gain': ab_k_gain, 'ab_w_out': ab_w_out,
        'cd_w_in': cd_w_in, 'cd_rpb': cd_rpb, 'cd_w_out': cd_w_out,
        'norm_ffn2': norm_ffn2, 'ffn2_w_in': ffn2_w_in, 'ffn2_w_out': ffn2_w_out,
        't5_table': t5_table, 'final_norm': final_norm,
    }
    y_prompt = _trunk(x_prompt, params)
    y_sample = _trunk(x_sample, params)
    return (y_prompt, y_sample)
```

```python
import functools
import math

import jax
import jax.numpy as jnp
import numpy as np
from jax import lax
from jax.experimental import pallas as pl
from jax.experimental.pallas import tpu as pltpu

HEAD_DIM = 128
GRID_W = 64
A_HEADS, A_KV, A_WINDOW = 8, 2, 128
B_HEADS, B_KV = 8, 2
ROPE_THETA = 10000.0
C_HEADS, C_KV = 8, 2
NA_ROWS, NA_COLS = 8, 16
D_PAIRS = ((128, 1), (512, 4), (2048, 16))
D_SLOTS = 4
D_HEADS = D_SLOTS * len(D_PAIRS)
T5_BUCKETS = 32
T5_MAX_DIST = 2048
NORM_EPS = 1e-6
NEG_INF = -1e30
ATTN_SCALE = HEAD_DIM ** -0.5

LANES = 128
MXU_COLS = 256
MIB = 1024 * 1024
VMEM_LIMIT_BYTES = 56 * MIB

BF16 = jnp.bfloat16
F32 = jnp.float32


def _params(semantics):
    return pltpu.CompilerParams(dimension_semantics=semantics, vmem_limit_bytes=VMEM_LIMIT_BYTES)


def _rms(x, gain):
    ms = jnp.mean(x * x, axis=-1, keepdims=True)
    return x * lax.rsqrt(ms + NORM_EPS) * gain


def _ffn_body(*refs, n_ff, final_norm):
    if final_norm:
        x_ref, gain_ref, wg_ref, wu_ref, wo_ref, fgain_ref, o_ref, h_scr = refs
    else:
        x_ref, gain_ref, wg_ref, wu_ref, wo_ref, o_ref, h_scr = refs
    j = pl.program_id(1)

    @pl.when(j == 0)
    def _():
        x = x_ref[...]
        h_scr[...] = _rms(x, gain_ref[...]).astype(BF16)
        o_ref[...] = x

    h = h_scr[...]
    g = jnp.dot(h, wg_ref[...], preferred_element_type=F32)
    u = jnp.dot(h, wu_ref[...], preferred_element_type=F32)
    a = (g * jax.nn.sigmoid(g) * (0.5 * u)).astype(BF16)
    o_ref[...] += jnp.dot(a, wo_ref[...], preferred_element_type=F32)

    if final_norm:
        @pl.when(j == n_ff - 1)
        def _():
            o_ref[...] = _rms(o_ref[...], fgain_ref[...])


def _ffn(x, gain, w_in, w_out, final_gain=None, *, tm=512, tf=512):
    n, d = x.shape
    f = w_out.shape[0]
    n_ff = f // tf
    assert n % tm == 0 and f % tf == 0 and w_in.shape == (d, 2 * f)
    final_norm = final_gain is not None
    in_specs = [
        pl.BlockSpec((tm, d), lambda i, j: (i, 0)),
        pl.BlockSpec((1, d), lambda i, j: (0, 0)),
        pl.BlockSpec((d, tf), lambda i, j: (0, j)),
        pl.BlockSpec((d, tf), lambda i, j: (0, j + n_ff)),
        pl.BlockSpec((tf, d), lambda i, j: (j, 0)),
    ]
    args = [x, gain.reshape(1, d), w_in, w_in, w_out]
    if final_norm:
        in_specs.append(pl.BlockSpec((1, d), lambda i, j: (0, 0)))
        args.append(final_gain.reshape(1, d))
    return pl.pallas_call(
        functools.partial(_ffn_body, n_ff=n_ff, final_norm=final_norm),
        grid=(n // tm, n_ff),
        in_specs=in_specs,
        out_specs=pl.BlockSpec((tm, d), lambda i, j: (i, 0)),
        out_shape=jax.ShapeDtypeStruct((n, d), F32),
        scratch_shapes=[pltpu.VMEM((tm, d), BF16)],
        compiler_params=_params(("parallel", "arbitrary")),
        name="ffn",
    )(*args)


def _rope_norm(r, gain, cos, sin):
    y = _rms(r, gain)
    return y * cos + pltpu.roll(y, HEAD_DIM // 2, axis=1) * sin


def _inproj_body(*refs, kinds, rope):
    if rope:
        x_ref, gain_ref, w_ref, qg_ref, kg_ref, cos_ref, sin_ref, o_ref = refs
    else:
        x_ref, gain_ref, w_ref, o_ref = refs
    h = _rms(x_ref[...], gain_ref[...]).astype(BF16)
    for c, kind in enumerate(kinds):
        lo = c * MXU_COLS
        r = jnp.dot(h, w_ref[:, lo:lo + MXU_COLS], preferred_element_type=F32)
        if kind == "scale":
            r = r * ATTN_SCALE
        elif kind in ("qnorm", "knorm"):
            hg = qg_ref[...] if kind == "qnorm" else kg_ref[...]
            cos = cos_ref[...]
            sin = sin_ref[...]
            parts = []
            for hh in range(MXU_COLS // HEAD_DIM):
                y = _rope_norm(r[:, hh * HEAD_DIM:(hh + 1) * HEAD_DIM], hg, cos, sin)
                if kind == "qnorm":
                    y = y * ATTN_SCALE
                parts.append(y)
            r = jnp.concatenate(parts, axis=1)
        o_ref[:, lo:lo + MXU_COLS] = r.astype(BF16)


def _inproj(x, gain, w, kinds, seq_len, rope_args=None, *, tm=512):
    n, d = x.shape
    c = w.shape[1]
    assert n % tm == 0 and seq_len % tm == 0 and c == MXU_COLS * len(kinds)
    rope = rope_args is not None
    in_specs = [
        pl.BlockSpec((tm, d), lambda i: (i, 0)),
        pl.BlockSpec((1, d), lambda i: (0, 0)),
        pl.BlockSpec((d, c), lambda i: (0, 0), pipeline_mode=pl.Buffered(1)),
    ]
    args = [x, gain.reshape(1, d), w]
    if rope:
        q_gain, k_gain, cos, sin = rope_args
        tiles_per_seq = seq_len // tm
        in_specs += [
            pl.BlockSpec((1, HEAD_DIM), lambda i: (0, 0)),
            pl.BlockSpec((1, HEAD_DIM), lambda i: (0, 0)),
            pl.BlockSpec((tm, HEAD_DIM), lambda i: (i % tiles_per_seq, 0)),
            pl.BlockSpec((tm, HEAD_DIM), lambda i: (i % tiles_per_seq, 0)),
        ]
        args += [q_gain.reshape(1, HEAD_DIM), k_gain.reshape(1, HEAD_DIM), cos, sin]
    return pl.pallas_call(
        functools.partial(_inproj_body, kinds=tuple(kinds), rope=rope),
        grid=(n // tm,),
        in_specs=in_specs,
        out_specs=pl.BlockSpec((tm, c), lambda i: (i, 0)),
        out_shape=jax.ShapeDtypeStruct((n, c), BF16),
        compiler_params=_params(("parallel",)),
        name="inproj",
    )(*args)


def _outproj_ab_body(x_ref, a_ref, b_ref, w_ref, o_ref):
    ka = a_ref.shape[1]
    acc = jnp.dot(a_ref[...], w_ref[0:ka, :], preferred_element_type=F32)
    acc = acc + jnp.dot(b_ref[...], w_ref[ka:, :], preferred_element_type=F32)
    o_ref[...] = x_ref[...] + acc


def _outproj_ab(x, oa, ob, w, *, tm=512):
    n, d = x.shape
    ka, kb = oa.shape[1], ob.shape[1]
    assert n % tm == 0 and w.shape == (ka + kb, d)
    return pl.pallas_call(
        _outproj_ab_body,
        grid=(n // tm,),
        in_specs=[
            pl.BlockSpec((tm, d), lambda i: (i, 0)),
            pl.BlockSpec((tm, ka), lambda i: (i, 0)),
            pl.BlockSpec((tm, kb), lambda i: (i, 0)),
            pl.BlockSpec((ka + kb, d), lambda i: (0, 0), pipeline_mode=pl.Buffered(1)),
        ],
        out_specs=pl.BlockSpec((tm, d), lambda i: (i, 0)),
        out_shape=jax.ShapeDtypeStruct((n, d), F32),
        compiler_params=_params(("parallel",)),
        name="outproj_ab",
    )(x, oa, ob, w)


def _outproj_cd_body(x_ref, c_ref, o0_ref, o1_ref, o2_ref, l0_ref, l1_ref, l2_ref, w_ref, o_ref):
    l0, l1, l2 = l0_ref[...], l1_ref[...], l2_ref[...]
    m = jnp.maximum(jnp.maximum(l0, l1), l2)
    e0, e1, e2 = jnp.exp(l0 - m), jnp.exp(l1 - m), jnp.exp(l2 - m)
    den = e0 + e1 + e2
    od = (e0 / den) * o0_ref[...] + (e1 / den) * o1_ref[...] + (e2 / den) * o2_ref[...]
    kc = c_ref.shape[1]
    acc = jnp.dot(c_ref[...], w_ref[0:kc, :], preferred_element_type=F32)
    acc = acc + jnp.dot(od.astype(BF16), w_ref[kc:, :], preferred_element_type=F32)
    o_ref[...] = x_ref[...] + acc


def _outproj_cd(x, oc, outs, lses, w, *, tm=512):
    n, d = x.shape
    kc, kd = oc.shape[1], outs[0].shape[1]
    assert n % tm == 0 and w.shape == (kc + kd, d)
    row = lambda i: (i, 0)
    return pl.pallas_call(
        _outproj_cd_body,
        grid=(n // tm,),
        in_specs=[pl.BlockSpec((tm, d), row), pl.BlockSpec((tm, kc), row)]
        + [pl.BlockSpec((tm, kd), row)] * 6
        + [pl.BlockSpec((kc + kd, d), lambda i: (0, 0), pipeline_mode=pl.Buffered(1))],
        out_specs=pl.BlockSpec((tm, d), row),
        out_shape=jax.ShapeDtypeStruct((n, d), F32),
        compiler_params=_params(("parallel",)),
        name="outproj_cd",
    )(x, oc, *outs, *lses, w)


def _stack_heads(q, g):
    return jnp.concatenate([q[:, i * HEAD_DIM:(i + 1) * HEAD_DIM] for i in range(g)], axis=0)


def _softmax_pv(s, v):
    m = jnp.max(s, axis=-1, keepdims=True)
    p = jnp.exp(s - m)
    l = jnp.sum(p, axis=-1, keepdims=True)
    o = jnp.dot(p.astype(BF16), v, preferred_element_type=F32) / l
    return o, m, l


def _qk(q, k):
    return lax.dot_general(q, k, (((1,), (1,)), ((), ())), preferred_element_type=F32)


def _dense_body(q_ref, k_ref, v_ref, o_ref, *, g):
    tq = q_ref.shape[1]
    s = _qk(_stack_heads(q_ref[0], g), k_ref[0])
    o, _, _ = _softmax_pv(s, v_ref[0])
    for i in range(g):
        o_ref[0, :, i * HEAD_DIM:(i + 1) * HEAD_DIM] = o[i * tq:(i + 1) * tq].astype(o_ref.dtype)


def _dense_attention(qkv, q_col, k_col, v_col, n_kv, g, *, tq=128):
    b, t, _ = qkv.shape
    gw = g * HEAD_DIM
    assert t % tq == 0 and q_col % gw == 0
    qb, kb, vb = q_col // gw, k_col // HEAD_DIM, v_col // HEAD_DIM
    return pl.pallas_call(
        functools.partial(_dense_body, g=g),
        grid=(b, n_kv, t // tq),
        in_specs=[
            pl.BlockSpec((1, tq, gw), lambda n, h, i: (n, i, qb + h)),
            pl.BlockSpec((1, t, HEAD_DIM), lambda n, h, i: (n, 0, kb + h)),
            pl.BlockSpec((1, t, HEAD_DIM), lambda n, h, i: (n, 0, vb + h)),
        ],
        out_specs=pl.BlockSpec((1, tq, gw), lambda n, h, i: (n, i, h)),
        out_shape=jax.ShapeDtypeStruct((b, t, n_kv * gw), BF16),
        compiler_params=_params(("parallel", "parallel", "arbitrary")),
        name="dense_attn",
    )(qkv, qkv, qkv)


def _band_body(*refs, n_kv, g, w, seq, kw, has_sink, want_lse):
    refs = list(refs)
    q_ref, k_ref, v_ref, bias_ref = refs[:4]
    refs = refs[4:]
    sink_ref = refs.pop(0) if has_sink else None
    o_ref = refs.pop(0)
    lse_ref = refs.pop(0) if want_lse else None
    tq = q_ref.shape[1]
    gw = g * HEAD_DIM
    qi = pl.program_id(1)
    kstart = pl.multiple_of(jnp.clip(qi * tq - w, 0, seq - kw), 64)
    for h in range(n_kv):
        k = k_ref[0, pl.ds(kstart, kw), h * HEAD_DIM:(h + 1) * HEAD_DIM]
        v = v_ref[0, pl.ds(kstart, kw), h * HEAD_DIM:(h + 1) * HEAD_DIM]
        q = _stack_heads(q_ref[0, :, h * gw:(h + 1) * gw], g)
        s = _qk(q, k) + bias_ref[0, h]
        o, m, l = _softmax_pv(s, v)
        lse = m + jnp.log(l)
        for i in range(g):
            rows = slice(i * tq, (i + 1) * tq)
            cols = slice(h * gw + i * HEAD_DIM, h * gw + (i + 1) * HEAD_DIM)
            oi = o[rows]
            if has_sink:
                oi = oi * jax.nn.sigmoid(lse[rows] - sink_ref[h * g + i])
            o_ref[0, :, cols] = oi.astype(o_ref.dtype)
            if want_lse:
                lse_ref[0, :, cols] = jnp.broadcast_to(lse[rows], (tq, HEAD_DIM))


def _band_geometry(seq, w):
    if seq <= 4 * w:
        return seq, seq, (0,)
    tq = 2 * w if w < 128 else w
    tq = min(tq, 256)
    kw = tq + 2 * w
    assert seq % tq == 0 and seq >= kw
    return tq, kw, (0, -w, -2 * w)


def _band_attention(q_arr, q_col, k_arr, k_col, v_arr, v_col, bias, n_kv, g, w, geometry,
                    sink=None, want_lse=False, out_dtype=BF16):
    n, seq, _ = q_arr.shape
    tq, kw, shifts = geometry
    nq = seq // tq
    gw = g * HEAD_DIM
    qw, kvw = n_kv * gw, n_kv * HEAD_DIM
    assert q_col % qw == 0 and k_col % kvw == 0 and v_col % kvw == 0
    qb, kb, vb = q_col // qw, k_col // kvw, v_col // kvw
    if len(shifts) == 1:
        case = lambda i: 0
    else:
        case = lambda i: jnp.where(i == 0, 0, jnp.where(i == nq - 1, 2, 1))
    in_specs = [
        pl.BlockSpec((1, tq, qw), lambda b, i: (b, i, qb)),
        pl.BlockSpec((1, seq, kvw), lambda b, i: (b, 0, kb)),
        pl.BlockSpec((1, seq, kvw), lambda b, i: (b, 0, vb)),
        pl.BlockSpec((1, n_kv, g * tq, kw), lambda b, i: (case(i), 0, 0, 0)),
    ]
    args = [q_arr, k_arr, v_arr, bias]
    if sink is not None:
        in_specs.append(pl.BlockSpec(memory_space=pltpu.SMEM))
        args.append(sink.astype(F32))
    o_spec = pl.BlockSpec((1, tq, qw), lambda b, i: (b, i, 0))
    o_shape = jax.ShapeDtypeStruct((n, seq, qw), out_dtype)
    out_specs, out_shape = o_spec, o_shape
    if want_lse:
        out_specs = [o_spec, o_spec]
        out_shape = [o_shape, jax.ShapeDtypeStruct((n, seq, qw), F32)]
    return pl.pallas_call(
        functools.partial(_band_body, n_kv=n_kv, g=g, w=w, seq=seq, kw=kw,
                          has_sink=sink is not None, want_lse=want_lse),
        grid=(n, nq),
        in_specs=in_specs,
        out_specs=out_specs,
        out_shape=out_shape,
        compiler_params=_params(("parallel", "arbitrary")),
        name="band_attn",
    )(*args)


def _na_body(q_ref, k_ref, v_ref, bias_ref, o_ref, *, g, rows, rows_per_step):
    rb = pl.program_id(2)
    kwin = NA_ROWS * GRID_W

    def one_row(rr, carry):
        r = rb * rows_per_step + rr
        rs = jnp.clip(r - NA_ROWS // 2, 0, rows - NA_ROWS)
        off = rs - r + NA_ROWS - 1
        kstart = pl.multiple_of(rs * GRID_W, GRID_W)
        qstart = pl.multiple_of(rr * GRID_W, GRID_W)
        k = k_ref[0, pl.ds(kstart, kwin), :]
        v = v_ref[0, pl.ds(kstart, kwin), :]
        q = _stack_heads(q_ref[0, pl.ds(qstart, GRID_W), :], g)
        s = _qk(q, k) + bias_ref[0, off]
        o, _, _ = _softmax_pv(s, v)
        for i in range(g):
            o_ref[0, pl.ds(qstart, GRID_W), i * HEAD_DIM:(i + 1) * HEAD_DIM] = (
                o[i * GRID_W:(i + 1) * GRID_W].astype(o_ref.dtype))
        return carry

    lax.fori_loop(0, rows_per_step, one_row, 0)


def _na_attention(qkv, q_col, k_col, v_col, bias, n_kv, g, *, rows_per_step=8):
    b, t, _ = qkv.shape
    rows = t // GRID_W
    gw = g * HEAD_DIM
    assert rows % rows_per_step == 0 and rows >= NA_ROWS
    qb, kb, vb = q_col // gw, k_col // HEAD_DIM, v_col // HEAD_DIM
    tq = rows_per_step * GRID_W
    return pl.pallas_call(
        functools.partial(_na_body, g=g, rows=rows, rows_per_step=rows_per_step),
        grid=(b, n_kv, rows // rows_per_step),
        in_specs=[
            pl.BlockSpec((1, tq, gw), lambda n, h, i: (n, i, qb + h)),
            pl.BlockSpec((1, t, HEAD_DIM), lambda n, h, i: (n, 0, kb + h)),
            pl.BlockSpec((1, t, HEAD_DIM), lambda n, h, i: (n, 0, vb + h)),
            pl.BlockSpec((1, NA_ROWS, g * GRID_W, NA_ROWS * GRID_W), lambda n, h, i: (h, 0, 0, 0)),
        ],
        out_specs=pl.BlockSpec((1, tq, gw), lambda n, h, i: (n, i, h)),
        out_shape=jax.ShapeDtypeStruct((b, t, n_kv * gw), BF16),
        compiler_params=_params(("parallel", "parallel", "arbitrary")),
        name="na_attn",
    )(qkv, qkv, qkv, bias)


def _t5_bucket(rel):
    half = T5_BUCKETS // 2
    max_exact = half // 2
    n = jnp.abs(rel)
    nf = jnp.maximum(n, 1).astype(jnp.float32)
    large = max_exact + (jnp.log(nf / max_exact) / math.log(T5_MAX_DIST / max_exact)
                         * (half - max_exact)).astype(jnp.int32)
    large = jnp.minimum(large, half - 1)
    return jnp.where(rel > 0, half, 0) + jnp.where(n < max_exact, n, large)


def _band_bias(table_cols, n_kv, g, w, dil, geometry):
    tq, kw, shifts = geometry
    iq = jnp.arange(tq)[:, None]
    jk = jnp.arange(kw)[None, :]
    tiles = []
    for shift in shifts:
        rel = jk + shift - iq
        tb = table_cols.astype(F32)[_t5_bucket(rel * dil)]
        tb = jnp.where((jnp.abs(rel) <= w)[..., None], tb, NEG_INF)
        tiles.append(jnp.moveaxis(tb, -1, 0).reshape(n_kv, g * tq, kw))
    return jnp.stack(tiles)


def _na_bias(rpb, n_kv, g):
    col = jnp.arange(GRID_W)
    qc, kc = col[:, None], col[None, :]
    dc = jnp.clip(kc - qc + NA_COLS - 1, 0, 2 * NA_COLS - 2)
    cs = jnp.clip(qc - NA_COLS // 2, 0, GRID_W - NA_COLS)
    mask = (kc >= cs) & (kc < cs + NA_COLS)
    t = jnp.where(mask, rpb.astype(F32)[:, :, dc], NEG_INF)
    cls = jnp.stack([t[:, o:o + NA_ROWS] for o in range(NA_ROWS)])
    cls = cls.reshape(NA_ROWS, n_kv, g, NA_ROWS, GRID_W, GRID_W)
    cls = jnp.transpose(cls, (1, 0, 2, 4, 3, 5))
    return cls.reshape(n_kv, NA_ROWS, g * GRID_W, NA_ROWS * GRID_W)


def _rope_tables(t):
    n_pairs = HEAD_DIM // 2
    n_freq = n_pairs // 2
    pos = jnp.arange(t)
    row = (pos // GRID_W).astype(jnp.float32)
    col = (pos % GRID_W).astype(jnp.float32)
    omega = ROPE_THETA ** (-(jnp.arange(n_freq, dtype=jnp.float32) * 2.0 / n_pairs))
    ang = jnp.concatenate([row[:, None] * omega, col[:, None] * omega], axis=-1)
    cos, sin = jnp.cos(ang), jnp.sin(ang)
    return jnp.concatenate([cos, cos], axis=-1), jnp.concatenate([-sin, sin], axis=-1)


def _deinterleave_heads(w, n_heads):
    lead = w.shape[:-1]
    w = w.reshape(lead + (n_heads, HEAD_DIM // 2, 2))
    return jnp.swapaxes(w, -1, -2).reshape(lead + (n_heads * HEAD_DIM,))


def _to_sub(a, d):
    b, t = a.shape[:2]
    a = a.reshape((b, t // d, d) + a.shape[2:])
    return jnp.swapaxes(a, 1, 2).reshape((b * d, t // d) + a.shape[3:])


def _from_sub(a, b, d):
    n, u = a.shape[:2]
    a = a.reshape((b, d, u) + a.shape[2:])
    return jnp.swapaxes(a, 1, 2).reshape((b, u * d) + a.shape[3:])


AB_KINDS = ("scale",) * 4 + ("plain",) * 2 + ("qnorm",) * 4 + ("knorm", "plain")
CD_KINDS = ("scale",) * 4 + ("plain",) * 2 + ("scale",) * 6 + ("plain",) * 4


def kernel(x_prompt, x_sample, norm_ffn1, ffn1_w_in, ffn1_w_out, norm_mix, ab_w_in, ab_sink, ab_q_gain, ab_k_gain, ab_w_out, cd_w_in, cd_rpb, cd_w_out, norm_ffn2, ffn2_w_in, ffn2_w_out, t5_table, final_norm):
    nb_prompt = x_prompt.shape[0]
    x = jnp.concatenate([x_prompt, x_sample], axis=0)
    b, t, d = x.shape
    depth = norm_ffn1.shape[0]
    x = x.reshape(b * t, d)

    ffn1_w_in, ffn1_w_out = ffn1_w_in.astype(BF16), ffn1_w_out.astype(BF16)
    ffn2_w_in, ffn2_w_out = ffn2_w_in.astype(BF16), ffn2_w_out.astype(BF16)
    qa_end = (A_HEADS + 2 * A_KV) * HEAD_DIM
    qb_end = qa_end + B_HEADS * HEAD_DIM
    kb_end = qb_end + B_KV * HEAD_DIM
    ab_w_in = jnp.concatenate([
        ab_w_in[..., :qa_end],
        _deinterleave_heads(ab_w_in[..., qa_end:qb_end], B_HEADS),
        _deinterleave_heads(ab_w_in[..., qb_end:kb_end], B_KV),
        ab_w_in[..., kb_end:]], axis=-1).astype(BF16)
    ab_q_gain = _deinterleave_heads(ab_q_gain, 1)
    ab_k_gain = _deinterleave_heads(ab_k_gain, 1)
    ab_w_out, cd_w_in, cd_w_out = ab_w_out.astype(BF16), cd_w_in.astype(BF16), cd_w_out.astype(BF16)

    cos, sin = _rope_tables(t)
    ga, gc = A_HEADS // A_KV, C_HEADS // C_KV
    geo_a = _band_geometry(t, A_WINDOW)
    bias_a = _band_bias(t5_table[:, :A_HEADS], A_KV, ga, A_WINDOW, 1, geo_a)
    d_groups = []
    for gi, (win, dil) in enumerate(D_PAIRS):
        hs = win // (2 * dil)
        geo = _band_geometry(t // dil, hs)
        col0 = A_HEADS + gi * D_SLOTS
        d_groups.append((dil, hs, geo, _band_bias(t5_table[:, col0:col0 + D_SLOTS], D_SLOTS, 1, hs, dil, geo)))

    a_q, a_k, a_v = 0, A_HEADS * HEAD_DIM, (A_HEADS + A_KV) * HEAD_DIM
    b_q, b_k, b_v = qa_end, qb_end, kb_end
    c_q, c_k, c_v = 0, C_HEADS * HEAD_DIM, (C_HEADS + C_KV) * HEAD_DIM
    d_q = (C_HEADS + 2 * C_KV) * HEAD_DIM
    d_k = d_q + D_HEADS * HEAD_DIM
    d_v = d_k + D_SLOTS * HEAD_DIM
    dw = D_SLOTS * HEAD_DIM

    for l in range(depth):
        i = l // 2
        x = _ffn(x, norm_ffn1[l], ffn1_w_in[l], ffn1_w_out[l])
        if l % 2 == 0:
            qkv = _inproj(x, norm_mix[l], ab_w_in[i], AB_KINDS, t,
                          (ab_q_gain[i], ab_k_gain[i], cos, sin)).reshape(b, t, -1)
            o_a = _band_attention(qkv, a_q, qkv, a_k, qkv, a_v, bias_a, A_KV, ga, A_WINDOW, geo_a,
                                  sink=ab_sink[i])
            o_b = _dense_attention(qkv, b_q, b_k, b_v, B_KV, B_HEADS // B_KV)
            x = _outproj_ab(x, o_a.reshape(b * t, -1), o_b.reshape(b * t, -1), ab_w_out[i])
        else:
            qkv = _inproj(x, norm_mix[l], cd_w_in[i], CD_KINDS, t).reshape(b, t, -1)
            o_c = _na_attention(qkv, c_q, c_k, c_v, _na_bias(cd_rpb[i], C_KV, gc), C_KV, gc)
            outs, lses = [], []
            for gi, (dil, hs, geo, bias_d) in enumerate(d_groups):
                qd = qkv[:, :, d_q + gi * dw:d_q + (gi + 1) * dw]
                if dil == 1:
                    o_g, lse_g = _band_attention(qkv, d_q + gi * dw, qkv, d_k, qkv, d_v, bias_d,
                                                 D_SLOTS, 1, hs, geo, want_lse=True, out_dtype=F32)
                else:
                    kd, vd = qkv[:, :, d_k:d_k + dw], qkv[:, :, d_v:d_v + dw]
                    o_g, lse_g = _band_attention(_to_sub(qd, dil), 0, _to_sub(kd, dil), 0,
                                                 _to_sub(vd, dil), 0, bias_d, D_SLOTS, 1, hs, geo,
                                                 want_lse=True, out_dtype=F32)
                    o_g, lse_g = _from_sub(o_g, b, dil), _from_sub(lse_g, b, dil)
                outs.append(o_g.reshape(b * t, dw))
                lses.append(lse_g.reshape(b * t, dw))
            x = _outproj_cd(x, o_c.reshape(b * t, -1), outs, lses, cd_w_out[i])
        x = _ffn(x, norm_ffn2[l], ffn2_w_in[l], ffn2_w_out[l],
                 final_gain=final_norm if l == depth - 1 else None)

    y = x.reshape(b, t, d)
    return (y[:nb_prompt], y[nb_prompt:])
```

```python
import functools
import math

import jax
import jax.numpy as jnp
import numpy as np
from jax import lax
from jax.experimental import pallas as pl
from jax.experimental.pallas import tpu as pltpu

HEAD_DIM = 128
GRID_W = 64
A_HEADS, A_KV, A_WINDOW = 8, 2, 128
B_HEADS, B_KV = 8, 2
ROPE_THETA = 10000.0
C_HEADS, C_KV = 8, 2
NA_ROWS, NA_COLS = 8, 16
D_PAIRS = ((128, 1), (512, 4), (2048, 16))
D_SLOTS = 4
D_HEADS = D_SLOTS * len(D_PAIRS)
T5_BUCKETS = 32
T5_MAX_DIST = 2048
NORM_EPS = 1e-6
NEG_INF = -1e30
ATTN_SCALE = HEAD_DIM ** -0.5
LOG2_E = math.log2(math.e)

LANES = 128
MXU_COLS = 256
DENSE_KEY_CHUNK = 2 * MXU_COLS
MIB = 1024 * 1024
VMEM_LIMIT_BYTES = 56 * MIB

BF16 = jnp.bfloat16
F32 = jnp.float32


def _params(semantics):
    return pltpu.CompilerParams(dimension_semantics=semantics, vmem_limit_bytes=VMEM_LIMIT_BYTES)


def _rms(x, gain):
    ms = jnp.mean(x * x, axis=-1, keepdims=True)
    return x * lax.rsqrt(ms + NORM_EPS) * gain


def _ffn_body(*refs, n_ff, final_norm):
    if final_norm:
        x_ref, gain_ref, wg_ref, wu_ref, wo_ref, fgain_ref, o_ref, h_scr = refs
    else:
        x_ref, gain_ref, wg_ref, wu_ref, wo_ref, o_ref, h_scr = refs
    j = pl.program_id(1)

    @pl.when(j == 0)
    def _():
        x = x_ref[...]
        h_scr[...] = _rms(x, gain_ref[...]).astype(BF16)
        o_ref[...] = x

    h = h_scr[...]
    g = jnp.dot(h, wg_ref[...], preferred_element_type=F32)
    u = jnp.dot(h, wu_ref[...], preferred_element_type=F32)
    a = (g * jax.nn.sigmoid(g) * (0.5 * u)).astype(BF16)
    o_ref[...] += jnp.dot(a, wo_ref[...], preferred_element_type=F32)

    if final_norm:
        @pl.when(j == n_ff - 1)
        def _():
            o_ref[...] = _rms(o_ref[...], fgain_ref[...])


def _ffn(x, gain, w_in, w_out, final_gain=None, *, tm=512, tf=512):
    n, d = x.shape
    f = w_out.shape[0]
    n_ff = f // tf
    assert n % tm == 0 and f % tf == 0 and w_in.shape == (d, 2 * f)
    final_norm = final_gain is not None
    in_specs = [
        pl.BlockSpec((tm, d), lambda i, j: (i, 0)),
        pl.BlockSpec((1, d), lambda i, j: (0, 0)),
        pl.BlockSpec((d, tf), lambda i, j: (0, j)),
        pl.BlockSpec((d, tf), lambda i, j: (0, j + n_ff)),
        pl.BlockSpec((tf, d), lambda i, j: (j, 0)),
    ]
    args = [x, gain.reshape(1, d), w_in, w_in, w_out]
    if final_norm:
        in_specs.append(pl.BlockSpec((1, d), lambda i, j: (0, 0)))
        args.append(final_gain.reshape(1, d))
    return pl.pallas_call(
        functools.partial(_ffn_body, n_ff=n_ff, final_norm=final_norm),
        grid=(n // tm, n_ff),
        in_specs=in_specs,
        out_specs=pl.BlockSpec((tm, d), lambda i, j: (i, 0)),
        out_shape=jax.ShapeDtypeStruct((n, d), F32),
        scratch_shapes=[pltpu.VMEM((tm, d), BF16)],
        compiler_params=_params(("parallel", "arbitrary")),
        name="ffn",
    )(*args)


def _rope_norm(r, gain, cos, sin):
    y = _rms(r, gain)
    return y * cos + pltpu.roll(y, HEAD_DIM // 2, axis=1) * sin


def _inproj_body(*refs, kinds, rope):
    if rope:
        x_ref, gain_ref, w_ref, qg_ref, kg_ref, cos_ref, sin_ref, o_ref, kt_ref = refs
    else:
        x_ref, gain_ref, w_ref, o_ref = refs
    h = _rms(x_ref[...], gain_ref[...]).astype(BF16)
    for c, kind in enumerate(kinds):
        lo = c * MXU_COLS
        r = jnp.dot(h, w_ref[:, lo:lo + MXU_COLS], preferred_element_type=F32)
        if kind == "scale":
            r = r * ATTN_SCALE
        elif kind in ("qnorm", "knorm"):
            hg = qg_ref[...] if kind == "qnorm" else kg_ref[...]
            cos = cos_ref[...]
            sin = sin_ref[...]
            parts = []
            for hh in range(MXU_COLS // HEAD_DIM):
                y = _rope_norm(r[:, hh * HEAD_DIM:(hh + 1) * HEAD_DIM], hg, cos, sin)
                if kind == "qnorm":
                    y = y * (ATTN_SCALE * LOG2_E)
                parts.append(y)
            r = jnp.concatenate(parts, axis=1)
            if kind == "knorm":
                kt_ref[0] = r.T.astype(BF16)
        o_ref[:, lo:lo + MXU_COLS] = r.astype(BF16)


def _inproj(x, gain, w, kinds, seq_len, rope_args=None, *, tm=512):
    n, d = x.shape
    c = w.shape[1]
    assert n % tm == 0 and seq_len % tm == 0 and c == MXU_COLS * len(kinds)
    rope = rope_args is not None
    in_specs = [
        pl.BlockSpec((tm, d), lambda i: (i, 0)),
        pl.BlockSpec((1, d), lambda i: (0, 0)),
        pl.BlockSpec((d, c), lambda i: (0, 0), pipeline_mode=pl.Buffered(1)),
    ]
    args = [x, gain.reshape(1, d), w]
    if rope:
        q_gain, k_gain, cos, sin = rope_args
        tiles_per_seq = seq_len // tm
        in_specs += [
            pl.BlockSpec((1, HEAD_DIM), lambda i: (0, 0)),
            pl.BlockSpec((1, HEAD_DIM), lambda i: (0, 0)),
            pl.BlockSpec((tm, HEAD_DIM), lambda i: (i % tiles_per_seq, 0)),
            pl.BlockSpec((tm, HEAD_DIM), lambda i: (i % tiles_per_seq, 0)),
        ]
        args += [q_gain.reshape(1, HEAD_DIM), k_gain.reshape(1, HEAD_DIM), cos, sin]
    out_specs = pl.BlockSpec((tm, c), lambda i: (i, 0))
    out_shape = jax.ShapeDtypeStruct((n, c), BF16)
    if rope:
        assert kinds.count("knorm") == 1
        out_specs = [out_specs, pl.BlockSpec((1, MXU_COLS, tm),
                                             lambda i: (i // tiles_per_seq, 0, i % tiles_per_seq))]
        out_shape = [out_shape, jax.ShapeDtypeStruct((n // seq_len, MXU_COLS, seq_len), BF16)]
    return pl.pallas_call(
        functools.partial(_inproj_body, kinds=tuple(kinds), rope=rope),
        grid=(n // tm,),
        in_specs=in_specs,
        out_specs=out_specs,
        out_shape=out_shape,
        compiler_params=_params(("parallel",)),
        name="inproj",
    )(*args)


def _outproj_ab_body(x_ref, a_ref, b_ref, w_ref, o_ref):
    ka = a_ref.shape[1]
    acc = jnp.dot(a_ref[...], w_ref[0:ka, :], preferred_element_type=F32)
    acc = acc + jnp.dot(b_ref[...], w_ref[ka:, :], preferred_element_type=F32)
    o_ref[...] = x_ref[...] + acc


def _outproj_ab(x, oa, ob, w, *, tm=512):
    n, d = x.shape
    ka, kb = oa.shape[1], ob.shape[1]
    assert n % tm == 0 and w.shape == (ka + kb, d)
    return pl.pallas_call(
        _outproj_ab_body,
        grid=(n // tm,),
        in_specs=[
            pl.BlockSpec((tm, d), lambda i: (i, 0)),
            pl.BlockSpec((tm, ka), lambda i: (i, 0)),
            pl.BlockSpec((tm, kb), lambda i: (i, 0)),
            pl.BlockSpec((ka + kb, d), lambda i: (0, 0), pipeline_mode=pl.Buffered(1)),
        ],
        out_specs=pl.BlockSpec((tm, d), lambda i: (i, 0)),
        out_shape=jax.ShapeDtypeStruct((n, d), F32),
        compiler_params=_params(("parallel",)),
        name="outproj_ab",
    )(x, oa, ob, w)


def _outproj_cd_body(x_ref, c_ref, o0_ref, o1_ref, o2_ref, l0_ref, l1_ref, l2_ref, w_ref, o_ref):
    l0, l1, l2 = l0_ref[...], l1_ref[...], l2_ref[...]
    m = jnp.maximum(jnp.maximum(l0, l1), l2)
    e0, e1, e2 = jnp.exp(l0 - m), jnp.exp(l1 - m), jnp.exp(l2 - m)
    den = e0 + e1 + e2
    od = (e0 / den) * o0_ref[...] + (e1 / den) * o1_ref[...] + (e2 / den) * o2_ref[...]
    kc = c_ref.shape[1]
    acc = jnp.dot(c_ref[...], w_ref[0:kc, :], preferred_element_type=F32)
    acc = acc + jnp.dot(od.astype(BF16), w_ref[kc:, :], preferred_element_type=F32)
    o_ref[...] = x_ref[...] + acc


def _outproj_cd(x, oc, outs, lses, w, *, tm=512):
    n, d = x.shape
    kc, kd = oc.shape[1], outs[0].shape[1]
    assert n % tm == 0 and w.shape == (kc + kd, d)
    row = lambda i: (i, 0)
    return pl.pallas_call(
        _outproj_cd_body,
        grid=(n // tm,),
        in_specs=[pl.BlockSpec((tm, d), row), pl.BlockSpec((tm, kc), row)]
        + [pl.BlockSpec((tm, kd), row)] * 6
        + [pl.BlockSpec((kc + kd, d), lambda i: (0, 0), pipeline_mode=pl.Buffered(1))],
        out_specs=pl.BlockSpec((tm, d), row),
        out_shape=jax.ShapeDtypeStruct((n, d), F32),
        compiler_params=_params(("parallel",)),
        name="outproj_cd",
    )(x, oc, *outs, *lses, w)


def _stack_heads(q, g):
    return jnp.concatenate([q[:, i * HEAD_DIM:(i + 1) * HEAD_DIM] for i in range(g)], axis=0)


def _softmax_pv(s, v):
    m = jnp.max(s, axis=-1, keepdims=True)
    p = jnp.exp(s - m)
    l = jnp.sum(p, axis=-1, keepdims=True)
    o = jnp.dot(p.astype(BF16), v, preferred_element_type=F32) / l
    return o, m, l


def _qk(q, k):
    return lax.dot_general(q, k, (((1,), (1,)), ((), ())), preferred_element_type=F32)


def _dense_body(q_ref, kt_ref, v_ref, o_ref, s_scr, p_scr, l_scr, *, g):
    step = pl.program_id(2)
    tq = q_ref.shape[1] // 2

    @pl.when(step == 0)
    def _():
        s_scr[1] = jnp.zeros(s_scr.shape[1:], s_scr.dtype)
        p_scr[...] = jnp.zeros(p_scr.shape, p_scr.dtype)
        l_scr[...] = jnp.ones(l_scr.shape, l_scr.dtype)

    t = kt_ref.shape[2]
    kc = min(DENSE_KEY_CHUNK, t)
    for cur in range(2):
        prev = 1 - cur
        rows = slice(cur * tq, (cur + 1) * tq)
        q = _stack_heads(q_ref[0, rows, :], g)
        m = jnp.max(s_scr[prev], axis=-1, keepdims=True)
        o = jnp.zeros((g * tq, HEAD_DIM), F32)
        l = jnp.zeros((g * tq, 1), F32)
        for c in range(t // kc):
            keys = slice(c * kc, (c + 1) * kc)
            o = o + jnp.dot(p_scr[cur, :, keys], v_ref[0, keys, :], preferred_element_type=F32)
            p = jnp.exp2(s_scr[prev, :, keys] - m)
            l = l + jnp.sum(p, axis=-1, keepdims=True)
            p_scr[prev, :, keys] = p.astype(BF16)
            s_scr[cur, :, keys] = jnp.dot(q, kt_ref[0, :, keys], preferred_element_type=F32)
        o = o / l_scr[cur]
        l_scr[prev] = l
        for i in range(g):
            o_ref[0, rows, i * HEAD_DIM:(i + 1) * HEAD_DIM] = o[i * tq:(i + 1) * tq].astype(o_ref.dtype)


def _dense_attention(qkv, kt, q_col, v_col, n_kv, g, *, tq=128):
    b, t, _ = qkv.shape
    gw = g * HEAD_DIM
    n_pairs = t // (2 * tq)
    assert t % (2 * tq) == 0 and q_col % gw == 0 and v_col % HEAD_DIM == 0
    assert kt.shape == (b, n_kv * HEAD_DIM, t)
    qb, vb = q_col // gw, v_col // HEAD_DIM
    return pl.pallas_call(
        functools.partial(_dense_body, g=g),
        grid=(b, n_kv, n_pairs + 1),
        in_specs=[
            pl.BlockSpec((1, 2 * tq, gw), lambda n, h, k: (n, jnp.minimum(k, n_pairs - 1), qb + h)),
            pl.BlockSpec((1, HEAD_DIM, t), lambda n, h, k: (n, h, 0)),
            pl.BlockSpec((1, t, HEAD_DIM), lambda n, h, k: (n, 0, vb + h)),
        ],
        out_specs=pl.BlockSpec((1, 2 * tq, gw), lambda n, h, k: (n, jnp.maximum(k - 1, 0), h)),
        out_shape=jax.ShapeDtypeStruct((b, t, n_kv * gw), BF16),
        scratch_shapes=[pltpu.VMEM((2, g * tq, t), F32), pltpu.VMEM((2, g * tq, t), BF16),
                        pltpu.VMEM((2, g * tq, 1), F32)],
        compiler_params=_params(("parallel", "parallel", "arbitrary")),
        name="dense_attn",
    )(qkv, kt, qkv)


def _band_body(*refs, n_kv, g, w, seq, kw, has_sink, want_lse):
    refs = list(refs)
    q_ref, k_ref, v_ref, bias_ref = refs[:4]
    refs = refs[4:]
    sink_ref = refs.pop(0) if has_sink else None
    o_ref = refs.pop(0)
    lse_ref = refs.pop(0) if want_lse else None
    tq = q_ref.shape[1]
    gw = g * HEAD_DIM
    qi = pl.program_id(1)
    kstart = pl.multiple_of(jnp.clip(qi * tq - w, 0, seq - kw), 64)
    for h in range(n_kv):
        k = k_ref[0, pl.ds(kstart, kw), h * HEAD_DIM:(h + 1) * HEAD_DIM]
        v = v_ref[0, pl.ds(kstart, kw), h * HEAD_DIM:(h + 1) * HEAD_DIM]
        q = _stack_heads(q_ref[0, :, h * gw:(h + 1) * gw], g)
        s = _qk(q, k) + bias_ref[0, h]
        o, m, l = _softmax_pv(s, v)
        lse = m + jnp.log(l)
        for i in range(g):
            rows = slice(i * tq, (i + 1) * tq)
            cols = slice(h * gw + i * HEAD_DIM, h * gw + (i + 1) * HEAD_DIM)
            oi = o[rows]
            if has_sink:
                oi = oi * jax.nn.sigmoid(lse[rows] - sink_ref[h * g + i])
            o_ref[0, :, cols] = oi.astype(o_ref.dtype)
            if want_lse:
                lse_ref[0, :, cols] = jnp.broadcast_to(lse[rows], (tq, HEAD_DIM))


def _band_geometry(seq, w):
    if seq <= 4 * w:
        return seq, seq, (0,)
    tq = 2 * w if w < 128 else w
    tq = min(tq, 256)
    kw = tq + 2 * w
    assert seq % tq == 0 and seq >= kw
    return tq, kw, (0, -w, -2 * w)


def _band_attention(q_arr, q_col, k_arr, k_col, v_arr, v_col, bias, n_kv, g, w, geometry,
                    sink=None, want_lse=False, out_dtype=BF16):
    n, seq, _ = q_arr.shape
    tq, kw, shifts = geometry
    nq = seq // tq
    gw = g * HEAD_DIM
    qw, kvw = n_kv * gw, n_kv * HEAD_DIM
    assert q_col % qw == 0 and k_col % kvw == 0 and v_col % kvw == 0
    qb, kb, vb = q_col // qw, k_col // kvw, v_col // kvw
    if len(shifts) == 1:
        case = lambda i: 0
    else:
        case = lambda i: jnp.where(i == 0, 0, jnp.where(i == nq - 1, 2, 1))
    in_specs = [
        pl.BlockSpec((1, tq, qw), lambda b, i: (b, i, qb)),
        pl.BlockSpec((1, seq, kvw), lambda b, i: (b, 0, kb)),
        pl.BlockSpec((1, seq, kvw), lambda b, i: (b, 0, vb)),
        pl.BlockSpec((1, n_kv, g * tq, kw), lambda b, i: (case(i), 0, 0, 0)),
    ]
    args = [q_arr, k_arr, v_arr, bias]
    if sink is not None:
        in_specs.append(pl.BlockSpec(memory_space=pltpu.SMEM))
        args.append(sink.astype(F32))
    o_spec = pl.BlockSpec((1, tq, qw), lambda b, i: (b, i, 0))
    o_shape = jax.ShapeDtypeStruct((n, seq, qw), out_dtype)
    out_specs, out_shape = o_spec, o_shape
    if want_lse:
        out_specs = [o_spec, o_spec]
        out_shape = [o_shape, jax.ShapeDtypeStruct((n, seq, qw), F32)]
    return pl.pallas_call(
        functools.partial(_band_body, n_kv=n_kv, g=g, w=w, seq=seq, kw=kw,
                          has_sink=sink is not None, want_lse=want_lse),
        grid=(n, nq),
        in_specs=in_specs,
        out_specs=out_specs,
        out_shape=out_shape,
        compiler_params=_params(("parallel", "arbitrary")),
        name="band_attn",
    )(*args)


def _na_body(q_ref, k_ref, v_ref, bias_ref, o_ref, *, g, rows, rows_per_step):
    rb = pl.program_id(2)
    kwin = NA_ROWS * GRID_W

    def one_row(rr, carry):
        r = rb * rows_per_step + rr
        rs = jnp.clip(r - NA_ROWS // 2, 0, rows - NA_ROWS)
        off = rs - r + NA_ROWS - 1
        kstart = pl.multiple_of(rs * GRID_W, GRID_W)
        qstart = pl.multiple_of(rr * GRID_W, GRID_W)
        k = k_ref[0, pl.ds(kstart, kwin), :]
        v = v_ref[0, pl.ds(kstart, kwin), :]
        q = _stack_heads(q_ref[0, pl.ds(qstart, GRID_W), :], g)
        s = _qk(q, k) + bias_ref[0, off]
        o, _, _ = _softmax_pv(s, v)
        for i in range(g):
            o_ref[0, pl.ds(qstart, GRID_W), i * HEAD_DIM:(i + 1) * HEAD_DIM] = (
                o[i * GRID_W:(i + 1) * GRID_W].astype(o_ref.dtype))
        return carry

    lax.fori_loop(0, rows_per_step, one_row, 0)


def _na_attention(qkv, q_col, k_col, v_col, bias, n_kv, g, *, rows_per_step=8):
    b, t, _ = qkv.shape
    rows = t // GRID_W
    gw = g * HEAD_DIM
    assert rows % rows_per_step == 0 and rows >= NA_ROWS
    qb, kb, vb = q_col // gw, k_col // HEAD_DIM, v_col // HEAD_DIM
    tq = rows_per_step * GRID_W
    return pl.pallas_call(
        functools.partial(_na_body, g=g, rows=rows, rows_per_step=rows_per_step),
        grid=(b, n_kv, rows // rows_per_step),
        in_specs=[
            pl.BlockSpec((1, tq, gw), lambda n, h, i: (n, i, qb + h)),
            pl.BlockSpec((1, t, HEAD_DIM), lambda n, h, i: (n, 0, kb + h)),
            pl.BlockSpec((1, t, HEAD_DIM), lambda n, h, i: (n, 0, vb + h)),
            pl.BlockSpec((1, NA_ROWS, g * GRID_W, NA_ROWS * GRID_W), lambda n, h, i: (h, 0, 0, 0)),
        ],
        out_specs=pl.BlockSpec((1, tq, gw), lambda n, h, i: (n, i, h)),
        out_shape=jax.ShapeDtypeStruct((b, t, n_kv * gw), BF16),
        compiler_params=_params(("parallel", "parallel", "arbitrary")),
        name="na_attn",
    )(qkv, qkv, qkv, bias)


def _t5_bucket(rel):
    half = T5_BUCKETS // 2
    max_exact = half // 2
    n = jnp.abs(rel)
    nf = jnp.maximum(n, 1).astype(jnp.float32)
    large = max_exact + (jnp.log(nf / max_exact) / math.log(T5_MAX_DIST / max_exact)
                         * (half - max_exact)).astype(jnp.int32)
    large = jnp.minimum(large, half - 1)
    return jnp.where(rel > 0, half, 0) + jnp.where(n < max_exact, n, large)


def _band_bias(table_cols, n_kv, g, w, dil, geometry):
    tq, kw, shifts = geometry
    iq = jnp.arange(tq)[:, None]
    jk = jnp.arange(kw)[None, :]
    tiles = []
    for shift in shifts:
        rel = jk + shift - iq
        bucket = _t5_bucket(rel * dil)
        tb = jnp.zeros((tq, kw, table_cols.shape[1]), F32)
        for r in range(T5_BUCKETS):
            tb = jnp.where((bucket == r)[..., None], table_cols[r].astype(F32), tb)
        tb = jnp.where((jnp.abs(rel) <= w)[..., None], tb, NEG_INF)
        tiles.append(jnp.moveaxis(tb, -1, 0).reshape(n_kv, g * tq, kw))
    return jnp.stack(tiles)


def _na_bias(rpb, n_kv, g):
    col = jnp.arange(GRID_W)
    qc, kc = col[:, None], col[None, :]
    dc = jnp.clip(kc - qc + NA_COLS - 1, 0, 2 * NA_COLS - 2)
    cs = jnp.clip(qc - NA_COLS // 2, 0, GRID_W - NA_COLS)
    mask = (kc >= cs) & (kc < cs + NA_COLS)
    rpb = rpb.astype(F32)
    t = jnp.zeros(rpb.shape[:2] + dc.shape, F32)
    for c in range(2 * NA_COLS - 1):
        t = jnp.where(dc == c, rpb[:, :, c][:, :, None, None], t)
    t = jnp.where(mask, t, NEG_INF)
    cls = jnp.stack([t[:, o:o + NA_ROWS] for o in range(NA_ROWS)])
    cls = cls.reshape(NA_ROWS, n_kv, g, NA_ROWS, GRID_W, GRID_W)
    cls = jnp.transpose(cls, (1, 0, 2, 4, 3, 5))
    return cls.reshape(n_kv, NA_ROWS, g * GRID_W, NA_ROWS * GRID_W)


def _rope_tables(t):
    n_pairs = HEAD_DIM // 2
    n_freq = n_pairs // 2
    pos = jnp.arange(t)
    row = (pos // GRID_W).astype(jnp.float32)
    col = (pos % GRID_W).astype(jnp.float32)
    omega = ROPE_THETA ** (-(jnp.arange(n_freq, dtype=jnp.float32) * 2.0 / n_pairs))
    ang = jnp.concatenate([row[:, None] * omega, col[:, None] * omega], axis=-1)
    cos, sin = jnp.cos(ang), jnp.sin(ang)
    return jnp.concatenate([cos, cos], axis=-1), jnp.concatenate([-sin, sin], axis=-1)


def _deinterleave_heads(w, n_heads):
    lead = w.shape[:-1]
    w = w.reshape(lead + (n_heads, HEAD_DIM // 2, 2))
    return jnp.swapaxes(w, -1, -2).reshape(lead + (n_heads * HEAD_DIM,))


def _to_sub(a, d):
    b, t = a.shape[:2]
    a = a.reshape((b, t // d, d) + a.shape[2:])
    return jnp.swapaxes(a, 1, 2).reshape((b * d, t // d) + a.shape[3:])


def _from_sub(a, b, d):
    n, u = a.shape[:2]
    a = a.reshape((b, d, u) + a.shape[2:])
    return jnp.swapaxes(a, 1, 2).reshape((b, u * d) + a.shape[3:])


AB_KINDS = ("scale",) * 4 + ("qnorm",) * 4 + ("plain", "plain", "knorm", "plain")
CD_KINDS = ("scale",) * 4 + ("plain",) * 2 + ("scale",) * 6 + ("plain",) * 4


def kernel(x_prompt, x_sample, norm_ffn1, ffn1_w_in, ffn1_w_out, norm_mix, ab_w_in, ab_sink, ab_q_gain, ab_k_gain, ab_w_out, cd_w_in, cd_rpb, cd_w_out, norm_ffn2, ffn2_w_in, ffn2_w_out, t5_table, final_norm):
    nb_prompt = x_prompt.shape[0]
    x = jnp.concatenate([x_prompt, x_sample], axis=0)
    b, t, d = x.shape
    depth = norm_ffn1.shape[0]
    x = x.reshape(b * t, d)

    ffn1_w_in, ffn1_w_out = ffn1_w_in.astype(BF16), ffn1_w_out.astype(BF16)
    ffn2_w_in, ffn2_w_out = ffn2_w_in.astype(BF16), ffn2_w_out.astype(BF16)
    qa_w, kva_w = A_HEADS * HEAD_DIM, 2 * A_KV * HEAD_DIM
    qb_lo = qa_w + kva_w
    qb_hi = qb_lo + B_HEADS * HEAD_DIM
    kb_hi = qb_hi + B_KV * HEAD_DIM
    ab_w_in = jnp.concatenate([
        ab_w_in[..., :qa_w],
        _deinterleave_heads(ab_w_in[..., qb_lo:qb_hi], B_HEADS),
        ab_w_in[..., qa_w:qb_lo],
        _deinterleave_heads(ab_w_in[..., qb_hi:kb_hi], B_KV),
        ab_w_in[..., kb_hi:]], axis=-1).astype(BF16)
    ab_q_gain = _deinterleave_heads(ab_q_gain, 1)
    ab_k_gain = _deinterleave_heads(ab_k_gain, 1)
    ab_w_out, cd_w_in, cd_w_out = ab_w_out.astype(BF16), cd_w_in.astype(BF16), cd_w_out.astype(BF16)

    cos, sin = _rope_tables(t)
    ga, gc = A_HEADS // A_KV, C_HEADS // C_KV
    geo_a = _band_geometry(t, A_WINDOW)
    bias_a = _band_bias(t5_table[:, :A_HEADS], A_KV, ga, A_WINDOW, 1, geo_a)
    d_groups = []
    for gi, (win, dil) in enumerate(D_PAIRS):
        hs = win // (2 * dil)
        geo = _band_geometry(t // dil, hs)
        col0 = A_HEADS + gi * D_SLOTS
        d_groups.append((dil, hs, geo, _band_bias(t5_table[:, col0:col0 + D_SLOTS], D_SLOTS, 1, hs, dil, geo)))

    a_q, b_q = 0, qa_w
    a_k = qa_w + B_HEADS * HEAD_DIM
    a_v = a_k + A_KV * HEAD_DIM
    b_v = a_v + (A_KV + B_KV) * HEAD_DIM
    c_q, c_k, c_v = 0, C_HEADS * HEAD_DIM, (C_HEADS + C_KV) * HEAD_DIM
    d_q = (C_HEADS + 2 * C_KV) * HEAD_DIM
    d_k = d_q + D_HEADS * HEAD_DIM
    d_v = d_k + D_SLOTS * HEAD_DIM
    dw = D_SLOTS * HEAD_DIM

    for l in range(depth):
        i = l // 2
        x = _ffn(x, norm_ffn1[l], ffn1_w_in[l], ffn1_w_out[l])
        if l % 2 == 0:
            qkv, kbt = _inproj(x, norm_mix[l], ab_w_in[i], AB_KINDS, t,
                               (ab_q_gain[i], ab_k_gain[i], cos, sin))
            qkv = qkv.reshape(b, t, -1)
            o_a = _band_attention(qkv, a_q, qkv, a_k, qkv, a_v, bias_a, A_KV, ga, A_WINDOW, geo_a,
                                  sink=ab_sink[i])
            o_b = _dense_attention(qkv, kbt, b_q, b_v, B_KV, B_HEADS // B_KV)
            x = _outproj_ab(x, o_a.reshape(b * t, -1), o_b.reshape(b * t, -1), ab_w_out[i])
        else:
            qkv = _inproj(x, norm_mix[l], cd_w_in[i], CD_KINDS, t).reshape(b, t, -1)
            o_c = _na_attention(qkv, c_q, c_k, c_v, _na_bias(cd_rpb[i], C_KV, gc), C_KV, gc)
            outs, lses = [], []
            for gi, (dil, hs, geo, bias_d) in enumerate(d_groups):
                qd = qkv[:, :, d_q + gi * dw:d_q + (gi + 1) * dw]
                if dil == 1:
                    o_g, lse_g = _band_attention(qkv, d_q + gi * dw, qkv, d_k, qkv, d_v, bias_d,
                                                 D_SLOTS, 1, hs, geo, want_lse=True, out_dtype=F32)
                else:
                    kd, vd = qkv[:, :, d_k:d_k + dw], qkv[:, :, d_v:d_v + dw]
                    o_g, lse_g = _band_attention(_to_sub(qd, dil), 0, _to_sub(kd, dil), 0,
                                                 _to_sub(vd, dil), 0, bias_d, D_SLOTS, 1, hs, geo,
                                                 want_lse=True, out_dtype=F32)
                    o_g, lse_g = _from_sub(o_g, b, dil), _from_sub(lse_g, b, dil)
                outs.append(o_g.reshape(b * t, dw))
                lses.append(lse_g.reshape(b * t, dw))
            x = _outproj_cd(x, o_c.reshape(b * t, -1), outs, lses, cd_w_out[i])
        x = _ffn(x, norm_ffn2[l], ffn2_w_in[l], ffn2_w_out[l],
                 final_gain=final_norm if l == depth - 1 else None)

    y = x.reshape(b, t, d)
    return (y[:nb_prompt], y[nb_prompt:])
```

```python
import functools
import math

import jax
import jax.numpy as jnp
import numpy as np
from jax import lax
from jax.experimental import pallas as pl
from jax.experimental.pallas import tpu as pltpu

HEAD_DIM = 128
GRID_W = 64
A_HEADS, A_KV, A_WINDOW = 8, 2, 128
B_HEADS, B_KV = 8, 2
ROPE_THETA = 10000.0
C_HEADS, C_KV = 8, 2
NA_ROWS, NA_COLS = 8, 16
D_PAIRS = ((128, 1), (512, 4), (2048, 16))
D_SLOTS = 4
D_HEADS = D_SLOTS * len(D_PAIRS)
T5_BUCKETS = 32
T5_MAX_DIST = 2048
NORM_EPS = 1e-6
NEG_INF = -1e30
ATTN_SCALE = HEAD_DIM ** -0.5
LOG2_E = math.log2(math.e)

LANES = 128
SUBLANES = 8
MXU_COLS = 256
DENSE_KEY_CHUNK = 2 * MXU_COLS
BAND_TILES_PER_STEP = (4, 2, 1)
NA_ROW_UNROLL = 4
MIB = 1024 * 1024
VMEM_LIMIT_BYTES = 56 * MIB

BF16 = jnp.bfloat16
F32 = jnp.float32


def _params(semantics):
    return pltpu.CompilerParams(dimension_semantics=semantics, vmem_limit_bytes=VMEM_LIMIT_BYTES)


def _rms(x, gain):
    ms = jnp.mean(x * x, axis=-1, keepdims=True)
    return x * lax.rsqrt(ms + NORM_EPS) * gain


def _ffn_body(*refs, n_ff, n_in, n_out, tiles_a, final_norm):
    refs = list(refs)
    x_refs = [refs.pop(0) for _ in range(n_in)]
    gain_ref, wg_ref, wu_ref, wo_ref = [refs.pop(0) for _ in range(4)]
    fgain_ref = refs.pop(0) if final_norm else None
    o_refs = [refs.pop(0) for _ in range(n_out)]
    h_scr = refs.pop(0)
    acc_ref = refs.pop(0) if n_out == 2 else o_refs[0]
    i, j = pl.program_id(0), pl.program_id(1)

    def start(x_ref):
        x = x_ref[...]
        h_scr[...] = _rms(x, gain_ref[...]).astype(BF16)
        acc_ref[...] = x

    if n_in == 1:
        pl.when(j == 0)(lambda: start(x_refs[0]))
    else:
        pl.when((j == 0) & (i < tiles_a))(lambda: start(x_refs[0]))
        pl.when((j == 0) & (i >= tiles_a))(lambda: start(x_refs[1]))

    h = h_scr[...]
    g = jnp.dot(h, wg_ref[...], preferred_element_type=F32)
    u = jnp.dot(h, wu_ref[...], preferred_element_type=F32)
    a = (g * jax.nn.sigmoid(g) * (0.5 * u)).astype(BF16)
    acc_ref[...] += jnp.dot(a, wo_ref[...], preferred_element_type=F32)

    def finish(o_ref):
        y = acc_ref[...]
        o_ref[...] = _rms(y, fgain_ref[...]) if final_norm else y

    if n_out == 2:
        pl.when((j == n_ff - 1) & (i < tiles_a))(lambda: finish(o_refs[0]))
        pl.when((j == n_ff - 1) & (i >= tiles_a))(lambda: finish(o_refs[1]))
    elif final_norm:
        pl.when(j == n_ff - 1)(lambda: finish(o_refs[0]))


def _ffn(xs, gain, w_in, w_out, final_gain=None, out_rows=None, *, tm=512, tf=512):
    xs = tuple(xs) if isinstance(xs, (tuple, list)) else (xs,)
    d = xs[0].shape[1]
    n = sum(x.shape[0] for x in xs)
    f = w_out.shape[0]
    n_ff = f // tf
    assert all(x.shape[0] % tm == 0 for x in xs) and f % tf == 0 and w_in.shape == (d, 2 * f)
    n_in, n_out = len(xs), 1 if out_rows is None else 2
    rows_a = xs[0].shape[0] if n_in == 2 else (out_rows[0] if n_out == 2 else n)
    assert n_in == 1 or n_out == 1 or out_rows[0] == rows_a
    tiles_a = rows_a // tm
    final_norm = final_gain is not None
    seg_a = lambda i, j: (jnp.minimum(i, tiles_a - 1), 0)
    seg_b = lambda i, j: (jnp.maximum(i - tiles_a, 0), 0)
    whole = lambda i, j: (i, 0)
    in_specs = [pl.BlockSpec((tm, d), m) for m in ((whole,) if n_in == 1 else (seg_a, seg_b))]
    in_specs += [
        pl.BlockSpec((1, d), lambda i, j: (0, 0)),
        pl.BlockSpec((d, tf), lambda i, j: (0, j)),
        pl.BlockSpec((d, tf), lambda i, j: (0, j + n_ff)),
        pl.BlockSpec((tf, d), lambda i, j: (j, 0)),
    ]
    args = list(xs) + [gain.reshape(1, d), w_in, w_in, w_out]
    if final_norm:
        in_specs.append(pl.BlockSpec((1, d), lambda i, j: (0, 0)))
        args.append(final_gain.reshape(1, d))
    scratch = [pltpu.VMEM((tm, d), BF16)]
    if n_out == 1:
        out_specs = pl.BlockSpec((tm, d), whole)
        out_shape = jax.ShapeDtypeStruct((n, d), F32)
    else:
        assert out_rows[0] % tm == 0 and sum(out_rows) == n
        out_specs = [pl.BlockSpec((tm, d), seg_a), pl.BlockSpec((tm, d), seg_b)]
        out_shape = [jax.ShapeDtypeStruct((r, d), F32) for r in out_rows]
        scratch.append(pltpu.VMEM((tm, d), F32))
    row_sem = "parallel" if n_in == 1 and n_out == 1 else "arbitrary"
    return pl.pallas_call(
        functools.partial(_ffn_body, n_ff=n_ff, n_in=n_in, n_out=n_out, tiles_a=tiles_a,
                          final_norm=final_norm),
        grid=(n // tm, n_ff),
        in_specs=in_specs,
        out_specs=out_specs,
        out_shape=out_shape,
        scratch_shapes=scratch,
        compiler_params=_params((row_sem, "arbitrary")),
        name="ffn",
    )(*args)


def _rope_norm(r, gain, cos, sin):
    y = _rms(r, gain)
    return y * cos + pltpu.roll(y, HEAD_DIM // 2, axis=1) * sin


def _inproj_body(*refs, kinds, rope):
    if rope:
        x_ref, gain_ref, w_ref, qg_ref, kg_ref, cos_ref, sin_ref, o_ref, qt_ref, vt_ref = refs
    else:
        x_ref, gain_ref, w_ref, o_ref = refs
    h = _rms(x_ref[...], gain_ref[...]).astype(BF16)
    n_qt = 0
    for c, kind in enumerate(kinds):
        lo = c * MXU_COLS
        r = jnp.dot(h, w_ref[:, lo:lo + MXU_COLS], preferred_element_type=F32)
        if kind == "scale":
            r = r * ATTN_SCALE
        elif kind in ("qnorm", "knorm"):
            hg = qg_ref[...] if kind == "qnorm" else kg_ref[...]
            cos = cos_ref[...]
            sin = sin_ref[...]
            parts = []
            for hh in range(MXU_COLS // HEAD_DIM):
                y = _rope_norm(r[:, hh * HEAD_DIM:(hh + 1) * HEAD_DIM], hg, cos, sin)
                if kind == "qnorm":
                    y = y * (ATTN_SCALE * LOG2_E)
                parts.append(y)
            r = jnp.concatenate(parts, axis=1)
            if kind == "qnorm":
                qt_ref[0, n_qt * MXU_COLS:(n_qt + 1) * MXU_COLS, :] = r.T.astype(BF16)
                n_qt += 1
        elif kind == "plain_t":
            vt_ref[0] = r.T.astype(BF16)
        o_ref[:, lo:lo + MXU_COLS] = r.astype(BF16)


def _inproj(x, gain, w, kinds, seq_len, rope_args=None, *, tm=512):
    n, d = x.shape
    c = w.shape[1]
    assert n % tm == 0 and seq_len % tm == 0 and c == MXU_COLS * len(kinds)
    rope = rope_args is not None
    in_specs = [
        pl.BlockSpec((tm, d), lambda i: (i, 0)),
        pl.BlockSpec((1, d), lambda i: (0, 0)),
        pl.BlockSpec((d, c), lambda i: (0, 0), pipeline_mode=pl.Buffered(1)),
    ]
    args = [x, gain.reshape(1, d), w]
    if rope:
        q_gain, k_gain, cos, sin = rope_args
        tiles_per_seq = seq_len // tm
        in_specs += [
            pl.BlockSpec((1, HEAD_DIM), lambda i: (0, 0)),
            pl.BlockSpec((1, HEAD_DIM), lambda i: (0, 0)),
            pl.BlockSpec((tm, HEAD_DIM), lambda i: (i % tiles_per_seq, 0)),
            pl.BlockSpec((tm, HEAD_DIM), lambda i: (i % tiles_per_seq, 0)),
        ]
        args += [q_gain.reshape(1, HEAD_DIM), k_gain.reshape(1, HEAD_DIM), cos, sin]
    out_specs = pl.BlockSpec((tm, c), lambda i: (i, 0))
    out_shape = jax.ShapeDtypeStruct((n, c), BF16)
    if rope:
        assert kinds.count("plain_t") == 1
        qt_rows = MXU_COLS * kinds.count("qnorm")
        by_seq = lambda i: (i // tiles_per_seq, 0, i % tiles_per_seq)
        out_specs = [out_specs, pl.BlockSpec((1, qt_rows, tm), by_seq),
                     pl.BlockSpec((1, MXU_COLS, tm), by_seq)]
        out_shape = [out_shape, jax.ShapeDtypeStruct((n // seq_len, qt_rows, seq_len), BF16),
                     jax.ShapeDtypeStruct((n // seq_len, MXU_COLS, seq_len), BF16)]
    return pl.pallas_call(
        functools.partial(_inproj_body, kinds=tuple(kinds), rope=rope),
        grid=(n // tm,),
        in_specs=in_specs,
        out_specs=out_specs,
        out_shape=out_shape,
        compiler_params=_params(("parallel",)),
        name="inproj",
    )(*args)


def _outproj_ab_body(x_ref, a_ref, b_ref, w_ref, o_ref):
    ka = a_ref.shape[1]
    acc = jnp.dot(a_ref[...], w_ref[0:ka, :], preferred_element_type=F32)
    acc = acc + jnp.dot(b_ref[...], w_ref[ka:, :], preferred_element_type=F32)
    o_ref[...] = x_ref[...] + acc


def _outproj_ab(x, oa, ob, w, *, tm=512):
    n, d = x.shape
    ka, kb = oa.shape[1], ob.shape[1]
    assert n % tm == 0 and w.shape == (ka + kb, d)
    return pl.pallas_call(
        _outproj_ab_body,
        grid=(n // tm,),
        in_specs=[
            pl.BlockSpec((tm, d), lambda i: (i, 0)),
            pl.BlockSpec((tm, ka), lambda i: (i, 0)),
            pl.BlockSpec((tm, kb), lambda i: (i, 0)),
            pl.BlockSpec((ka + kb, d), lambda i: (0, 0), pipeline_mode=pl.Buffered(1)),
        ],
        out_specs=pl.BlockSpec((tm, d), lambda i: (i, 0)),
        out_shape=jax.ShapeDtypeStruct((n, d), F32),
        compiler_params=_params(("parallel",)),
        name="outproj_ab",
    )(x, oa, ob, w)


def _outproj_cd_body(x_ref, c_ref, o0_ref, o1_ref, o2_ref, l0_ref, l1_ref, l2_ref, w_ref, o_ref):
    l0, l1, l2 = l0_ref[...], l1_ref[...], l2_ref[...]
    m = jnp.maximum(jnp.maximum(l0, l1), l2)
    e0, e1, e2 = jnp.exp(l0 - m), jnp.exp(l1 - m), jnp.exp(l2 - m)
    den = e0 + e1 + e2
    od = (e0 / den) * o0_ref[...] + (e1 / den) * o1_ref[...] + (e2 / den) * o2_ref[...]
    kc = c_ref.shape[1]
    acc = jnp.dot(c_ref[...], w_ref[0:kc, :], preferred_element_type=F32)
    acc = acc + jnp.dot(od.astype(BF16), w_ref[kc:, :], preferred_element_type=F32)
    o_ref[...] = x_ref[...] + acc


def _outproj_cd(x, oc, outs, lses, w, *, tm=512):
    n, d = x.shape
    kc, kd = oc.shape[1], outs[0].shape[1]
    assert n % tm == 0 and w.shape == (kc + kd, d)
    row = lambda i: (i, 0)
    return pl.pallas_call(
        _outproj_cd_body,
        grid=(n // tm,),
        in_specs=[pl.BlockSpec((tm, d), row), pl.BlockSpec((tm, kc), row)]
        + [pl.BlockSpec((tm, kd), row)] * 6
        + [pl.BlockSpec((kc + kd, d), lambda i: (0, 0), pipeline_mode=pl.Buffered(1))],
        out_specs=pl.BlockSpec((tm, d), row),
        out_shape=jax.ShapeDtypeStruct((n, d), F32),
        compiler_params=_params(("parallel",)),
        name="outproj_cd",
    )(x, oc, *outs, *lses, w)


def _stack_heads(q, g):
    return jnp.concatenate([q[:, i * HEAD_DIM:(i + 1) * HEAD_DIM] for i in range(g)], axis=0)


def _softmax_pv(s, v):
    m = jnp.max(s, axis=-1, keepdims=True)
    p = jnp.exp(s - m)
    l = jnp.sum(p, axis=-1, keepdims=True)
    o = jnp.dot(p.astype(BF16), v, preferred_element_type=F32) / l
    return o, m, l


def _qk(q, k):
    return lax.dot_general(q, k, (((1,), (1,)), ((), ())), preferred_element_type=F32)


def _dense_body(qt_ref, k_ref, vt_ref, o_ref, s_scr, p_scr, m_scr, l_scr, *, g):
    step = pl.program_id(2)
    tq = qt_ref.shape[2] // 2
    t = k_ref.shape[1]
    gq = g * tq
    kc = min(DENSE_KEY_CHUNK, t)

    @pl.when(step == 0)
    def _():
        s_scr[1] = jnp.zeros(s_scr.shape[1:], s_scr.dtype)
        m_scr[1] = jnp.zeros(m_scr.shape[1:], m_scr.dtype)
        p_scr[...] = jnp.zeros(p_scr.shape, p_scr.dtype)
        l_scr[...] = jnp.ones(l_scr.shape, l_scr.dtype)

    def fold(x, op):
        return op(x.reshape(x.shape[0] // SUBLANES, SUBLANES, x.shape[1]), axis=0)

    for cur in range(2):
        prev = 1 - cur
        toks = slice(cur * tq, (cur + 1) * tq)
        qt = jnp.concatenate([qt_ref[0, i * HEAD_DIM:(i + 1) * HEAD_DIM, toks] for i in range(g)], axis=1)
        m = jnp.max(m_scr[prev], axis=0, keepdims=True)
        ot = jnp.zeros((HEAD_DIM, gq), F32)
        l_acc = jnp.zeros((SUBLANES, gq), F32)
        m_acc = None
        for c in range(t // kc):
            keys = slice(c * kc, (c + 1) * kc)
            p = jnp.exp2(s_scr[prev, keys, :] - m)
            l_acc = l_acc + fold(p, jnp.sum)
            p_scr[prev, keys, :] = p.astype(BF16)
        for c in range(t // kc):
            keys = slice(c * kc, (c + 1) * kc)
            ot = ot + jnp.dot(vt_ref[0, :, keys], p_scr[cur, keys, :], preferred_element_type=F32)
            s = jnp.dot(k_ref[0, keys, :], qt, preferred_element_type=F32)
            s_scr[cur, keys, :] = s
            m_chunk = fold(s, jnp.max)
            m_acc = m_chunk if m_acc is None else jnp.maximum(m_acc, m_chunk)
        ot = ot / jnp.sum(l_scr[cur], axis=0, keepdims=True)
        l_scr[prev] = l_acc
        m_scr[cur] = m_acc
        for i in range(g):
            o_ref[0, toks, i * HEAD_DIM:(i + 1) * HEAD_DIM] = ot[:, i * tq:(i + 1) * tq].T.astype(o_ref.dtype)


def _dense_attention(qt, qkv, vt, k_col, n_kv, g, *, tq=128):
    b, t, _ = qkv.shape
    gw = g * HEAD_DIM
    n_pairs = t // (2 * tq)
    assert t % (2 * tq) == 0 and k_col % HEAD_DIM == 0
    assert qt.shape == (b, n_kv * gw, t) and vt.shape == (b, n_kv * HEAD_DIM, t)
    kb = k_col // HEAD_DIM
    return pl.pallas_call(
        functools.partial(_dense_body, g=g),
        grid=(b, n_kv, n_pairs + 1),
        in_specs=[
            pl.BlockSpec((1, gw, 2 * tq), lambda n, h, k: (n, h, jnp.minimum(k, n_pairs - 1))),
            pl.BlockSpec((1, t, HEAD_DIM), lambda n, h, k: (n, 0, kb + h)),
            pl.BlockSpec((1, HEAD_DIM, t), lambda n, h, k: (n, h, 0)),
        ],
        out_specs=pl.BlockSpec((1, 2 * tq, gw), lambda n, h, k: (n, jnp.maximum(k - 1, 0), h)),
        out_shape=jax.ShapeDtypeStruct((b, t, n_kv * gw), BF16),
        scratch_shapes=[pltpu.VMEM((2, t, g * tq), F32), pltpu.VMEM((2, t, g * tq), BF16),
                        pltpu.VMEM((2, SUBLANES, g * tq), F32), pltpu.VMEM((2, SUBLANES, g * tq), F32)],
        compiler_params=_params(("parallel", "parallel", "arbitrary")),
        name="dense_attn",
    )(qt, qkv, vt)


def _band_body(*refs, n_kv, g, w, seq, tq, kw, n_cases, has_sink, want_lse):
    refs = list(refs)
    q_ref, k_ref, v_ref, bias_ref = refs[:4]
    refs = refs[4:]
    sink_ref = refs.pop(0) if has_sink else None
    o_ref = refs.pop(0)
    lse_ref = refs.pop(0) if want_lse else None
    gw = g * HEAD_DIM
    nq = seq // tq
    for sq, j in [(a, c) for a in range(q_ref.shape[0]) for c in range(q_ref.shape[1] // tq)]:
        qi = pl.program_id(1) * (q_ref.shape[1] // tq) + j
        kstart = pl.multiple_of(jnp.clip(qi * tq - w, 0, seq - kw), 64)
        case = 0 if n_cases == 1 else jnp.where(qi == 0, 0, jnp.where(qi == nq - 1, 2, 1))
        trows = slice(j * tq, (j + 1) * tq)
        for h in range(n_kv):
            k = k_ref[sq, pl.ds(kstart, kw), h * HEAD_DIM:(h + 1) * HEAD_DIM]
            v = v_ref[sq, pl.ds(kstart, kw), h * HEAD_DIM:(h + 1) * HEAD_DIM]
            q = _stack_heads(q_ref[sq, trows, h * gw:(h + 1) * gw], g)
            s = _qk(q, k) + bias_ref[case, h]
            o, m, l = _softmax_pv(s, v)
            lse = m + jnp.log(l)
            for i in range(g):
                rows = slice(i * tq, (i + 1) * tq)
                cols = slice(h * gw + i * HEAD_DIM, h * gw + (i + 1) * HEAD_DIM)
                oi = o[rows]
                if has_sink:
                    oi = oi * jax.nn.sigmoid(lse[rows] - sink_ref[h * g + i])
                o_ref[sq, trows, cols] = oi.astype(o_ref.dtype)
                if want_lse:
                    lse_ref[sq, trows, cols] = jnp.broadcast_to(lse[rows], (tq, HEAD_DIM))


def _band_geometry(seq, w):
    if seq <= 4 * w:
        return seq, seq, (0,)
    tq = 2 * w if w < 128 else w
    tq = min(tq, 256)
    kw = tq + 2 * w
    assert seq % tq == 0 and seq >= kw
    return tq, kw, (0, -w, -2 * w)


def _band_attention(q_arr, q_col, k_arr, k_col, v_arr, v_col, bias, n_kv, g, w, geometry,
                    sink=None, want_lse=False, out_dtype=BF16):
    n, seq, _ = q_arr.shape
    tq, kw, shifts = geometry
    nq = seq // tq
    gw = g * HEAD_DIM
    qw, kvw = n_kv * gw, n_kv * HEAD_DIM
    assert q_col % qw == 0 and k_col % kvw == 0 and v_col % kvw == 0
    qb, kb, vb = q_col // qw, k_col // kvw, v_col // kvw
    tps = next(c for c in BAND_TILES_PER_STEP if nq % c == 0)
    nq_steps = nq // tps
    sps = next(c for c in BAND_TILES_PER_STEP if n % c == 0) if nq == 1 else 1
    in_specs = [
        pl.BlockSpec((sps, tps * tq, qw), lambda b, i: (b, i, qb)),
        pl.BlockSpec((sps, seq, kvw), lambda b, i: (b, 0, kb)),
        pl.BlockSpec((sps, seq, kvw), lambda b, i: (b, 0, vb)),
        pl.BlockSpec((len(shifts), n_kv, g * tq, kw), lambda b, i: (0, 0, 0, 0)),
    ]
    args = [q_arr, k_arr, v_arr, bias]
    if sink is not None:
        in_specs.append(pl.BlockSpec(memory_space=pltpu.SMEM))
        args.append(sink.astype(F32))
    o_spec = pl.BlockSpec((sps, tps * tq, qw), lambda b, i: (b, i, 0))
    o_shape = jax.ShapeDtypeStruct((n, seq, qw), out_dtype)
    out_specs, out_shape = o_spec, o_shape
    if want_lse:
        out_specs = [o_spec, o_spec]
        out_shape = [o_shape, jax.ShapeDtypeStruct((n, seq, qw), F32)]
    return pl.pallas_call(
        functools.partial(_band_body, n_kv=n_kv, g=g, w=w, seq=seq, tq=tq, kw=kw, n_cases=len(shifts),
                          has_sink=sink is not None, want_lse=want_lse),
        grid=(n // sps, nq_steps),
        in_specs=in_specs,
        out_specs=out_specs,
        out_shape=out_shape,
        compiler_params=_params(("parallel", "arbitrary")),
        name="band_attn",
    )(*args)


def _na_body(q_ref, k_ref, v_ref, bias_ref, o_ref, *, g, rows, rows_per_step):
    rb = pl.program_id(2)
    kwin = NA_ROWS * GRID_W

    def one_row(rr, carry):
        r = rb * rows_per_step + rr
        rs = jnp.clip(r - NA_ROWS // 2, 0, rows - NA_ROWS)
        off = rs - r + NA_ROWS - 1
        kstart = pl.multiple_of(rs * GRID_W, GRID_W)
        qstart = pl.multiple_of(rr * GRID_W, GRID_W)
        k = k_ref[0, pl.ds(kstart, kwin), :]
        v = v_ref[0, pl.ds(kstart, kwin), :]
        q = _stack_heads(q_ref[0, pl.ds(qstart, GRID_W), :], g)
        s = _qk(q, k) + bias_ref[0, off]
        o, _, _ = _softmax_pv(s, v)
        for i in range(g):
            o_ref[0, pl.ds(qstart, GRID_W), i * HEAD_DIM:(i + 1) * HEAD_DIM] = (
                o[i * GRID_W:(i + 1) * GRID_W].astype(o_ref.dtype))
        return carry

    lax.fori_loop(0, rows_per_step, one_row, 0, unroll=NA_ROW_UNROLL)


def _na_attention(qkv, q_col, k_col, v_col, bias, n_kv, g, *, rows_per_step=8):
    b, t, _ = qkv.shape
    rows = t // GRID_W
    gw = g * HEAD_DIM
    assert rows % rows_per_step == 0 and rows >= NA_ROWS
    qb, kb, vb = q_col // gw, k_col // HEAD_DIM, v_col // HEAD_DIM
    tq = rows_per_step * GRID_W
    return pl.pallas_call(
        functools.partial(_na_body, g=g, rows=rows, rows_per_step=rows_per_step),
        grid=(b, n_kv, rows // rows_per_step),
        in_specs=[
            pl.BlockSpec((1, tq, gw), lambda n, h, i: (n, i, qb + h)),
            pl.BlockSpec((1, t, HEAD_DIM), lambda n, h, i: (n, 0, kb + h)),
            pl.BlockSpec((1, t, HEAD_DIM), lambda n, h, i: (n, 0, vb + h)),
            pl.BlockSpec((1, NA_ROWS, g * GRID_W, NA_ROWS * GRID_W), lambda n, h, i: (h, 0, 0, 0)),
        ],
        out_specs=pl.BlockSpec((1, tq, gw), lambda n, h, i: (n, i, h)),
        out_shape=jax.ShapeDtypeStruct((b, t, n_kv * gw), BF16),
        compiler_params=_params(("parallel", "parallel", "arbitrary")),
        name="na_attn",
    )(qkv, qkv, qkv, bias)


def _t5_bucket(rel):
    half = T5_BUCKETS // 2
    max_exact = half // 2
    n = jnp.abs(rel)
    nf = jnp.maximum(n, 1).astype(jnp.float32)
    large = max_exact + (jnp.log(nf / max_exact) / math.log(T5_MAX_DIST / max_exact)
                         * (half - max_exact)).astype(jnp.int32)
    large = jnp.minimum(large, half - 1)
    return jnp.where(rel > 0, half, 0) + jnp.where(n < max_exact, n, large)


def _band_bias(table_cols, n_kv, g, w, dil, geometry):
    tq, kw, shifts = geometry
    iq = jnp.arange(tq)[:, None]
    jk = jnp.arange(kw)[None, :]
    tiles = []
    for shift in shifts:
        rel = jk + shift - iq
        bucket = _t5_bucket(rel * dil)
        tb = jnp.zeros((tq, kw, table_cols.shape[1]), F32)
        for r in range(T5_BUCKETS):
            tb = jnp.where((bucket == r)[..., None], table_cols[r].astype(F32), tb)
        tb = jnp.where((jnp.abs(rel) <= w)[..., None], tb, NEG_INF)
        tiles.append(jnp.moveaxis(tb, -1, 0).reshape(n_kv, g * tq, kw))
    return jnp.stack(tiles)


def _na_bias(rpb, n_kv, g):
    col = jnp.arange(GRID_W)
    qc, kc = col[:, None], col[None, :]
    dc = jnp.clip(kc - qc + NA_COLS - 1, 0, 2 * NA_COLS - 2)
    cs = jnp.clip(qc - NA_COLS // 2, 0, GRID_W - NA_COLS)
    mask = (kc >= cs) & (kc < cs + NA_COLS)
    rpb = rpb.astype(F32)
    t = jnp.zeros(rpb.shape[:2] + dc.shape, F32)
    for c in range(2 * NA_COLS - 1):
        t = jnp.where(dc == c, rpb[:, :, c][:, :, None, None], t)
    t = jnp.where(mask, t, NEG_INF)
    cls = jnp.stack([t[:, o:o + NA_ROWS] for o in range(NA_ROWS)])
    cls = cls.reshape(NA_ROWS, n_kv, g, NA_ROWS, GRID_W, GRID_W)
    cls = jnp.transpose(cls, (1, 0, 2, 4, 3, 5))
    return cls.reshape(n_kv, NA_ROWS, g * GRID_W, NA_ROWS * GRID_W)


def _rope_tables(t):
    n_pairs = HEAD_DIM // 2
    n_freq = n_pairs // 2
    pos = jnp.arange(t)
    row = (pos // GRID_W).astype(jnp.float32)
    col = (pos % GRID_W).astype(jnp.float32)
    omega = ROPE_THETA ** (-(jnp.arange(n_freq, dtype=jnp.float32) * 2.0 / n_pairs))
    ang = jnp.concatenate([row[:, None] * omega, col[:, None] * omega], axis=-1)
    cos, sin = jnp.cos(ang), jnp.sin(ang)
    return jnp.concatenate([cos, cos], axis=-1), jnp.concatenate([-sin, sin], axis=-1)


def _deinterleave_heads(w, n_heads):
    lead = w.shape[:-1]
    w = w.reshape(lead + (n_heads, HEAD_DIM // 2, 2))
    return jnp.swapaxes(w, -1, -2).reshape(lead + (n_heads * HEAD_DIM,))


def _to_sub(a, d):
    b, t = a.shape[:2]
    a = a.reshape((b, t // d, d) + a.shape[2:])
    return jnp.swapaxes(a, 1, 2).reshape((b * d, t // d) + a.shape[3:])


def _from_sub(a, b, d):
    n, u = a.shape[:2]
    a = a.reshape((b, d, u) + a.shape[2:])
    return jnp.swapaxes(a, 1, 2).reshape((b, u * d) + a.shape[3:])


AB_KINDS = ("scale",) * 4 + ("qnorm",) * 4 + ("plain", "plain", "knorm", "plain_t")
CD_KINDS = ("scale",) * 4 + ("plain",) * 2 + ("scale",) * 6 + ("plain",) * 4


def kernel(x_prompt, x_sample, norm_ffn1, ffn1_w_in, ffn1_w_out, norm_mix, ab_w_in, ab_sink, ab_q_gain, ab_k_gain, ab_w_out, cd_w_in, cd_rpb, cd_w_out, norm_ffn2, ffn2_w_in, ffn2_w_out, t5_table, final_norm):
    (bp, t, d), bs = x_prompt.shape, x_sample.shape[0]
    assert x_sample.shape[1:] == (t, d)
    b = bp + bs
    depth = norm_ffn1.shape[0]
    x = (x_prompt.reshape(bp * t, d), x_sample.reshape(bs * t, d))

    ffn1_w_in, ffn1_w_out = ffn1_w_in.astype(BF16), ffn1_w_out.astype(BF16)
    ffn2_w_in, ffn2_w_out = ffn2_w_in.astype(BF16), ffn2_w_out.astype(BF16)
    qa_w, kva_w = A_HEADS * HEAD_DIM, 2 * A_KV * HEAD_DIM
    qb_lo = qa_w + kva_w
    qb_hi = qb_lo + B_HEADS * HEAD_DIM
    kb_hi = qb_hi + B_KV * HEAD_DIM
    ab_w_in = jnp.concatenate([
        ab_w_in[..., :qa_w],
        _deinterleave_heads(ab_w_in[..., qb_lo:qb_hi], B_HEADS),
        ab_w_in[..., qa_w:qb_lo],
        _deinterleave_heads(ab_w_in[..., qb_hi:kb_hi], B_KV),
        ab_w_in[..., kb_hi:]], axis=-1).astype(BF16)
    ab_q_gain = _deinterleave_heads(ab_q_gain, 1)
    ab_k_gain = _deinterleave_heads(ab_k_gain, 1)
    ab_w_out, cd_w_in, cd_w_out = ab_w_out.astype(BF16), cd_w_in.astype(BF16), cd_w_out.astype(BF16)

    cos, sin = _rope_tables(t)
    ga, gc = A_HEADS // A_KV, C_HEADS // C_KV
    geo_a = _band_geometry(t, A_WINDOW)
    bias_a = _band_bias(t5_table[:, :A_HEADS], A_KV, ga, A_WINDOW, 1, geo_a)
    d_groups = []
    for gi, (win, dil) in enumerate(D_PAIRS):
        hs = win // (2 * dil)
        geo = _band_geometry(t // dil, hs)
        col0 = A_HEADS + gi * D_SLOTS
        d_groups.append((dil, hs, geo, _band_bias(t5_table[:, col0:col0 + D_SLOTS], D_SLOTS, 1, hs, dil, geo)))

    a_q = 0
    a_k = qa_w + B_HEADS * HEAD_DIM
    a_v = a_k + A_KV * HEAD_DIM
    b_k = a_v + A_KV * HEAD_DIM
    c_q, c_k, c_v = 0, C_HEADS * HEAD_DIM, (C_HEADS + C_KV) * HEAD_DIM
    d_q = (C_HEADS + 2 * C_KV) * HEAD_DIM
    d_k = d_q + D_HEADS * HEAD_DIM
    d_v = d_k + D_SLOTS * HEAD_DIM
    dw = D_SLOTS * HEAD_DIM

    for l in range(depth):
        i = l // 2
        x = _ffn(x, norm_ffn1[l], ffn1_w_in[l], ffn1_w_out[l])
        if l % 2 == 0:
            qkv, qbt, vbt = _inproj(x, norm_mix[l], ab_w_in[i], AB_KINDS, t,
                                    (ab_q_gain[i], ab_k_gain[i], cos, sin))
            qkv = qkv.reshape(b, t, -1)
            o_a = _band_attention(qkv, a_q, qkv, a_k, qkv, a_v, bias_a, A_KV, ga, A_WINDOW, geo_a,
                                  sink=ab_sink[i])
            o_b = _dense_attention(qbt, qkv, vbt, b_k, B_KV, B_HEADS // B_KV)
            x = _outproj_ab(x, o_a.reshape(b * t, -1), o_b.reshape(b * t, -1), ab_w_out[i])
        else:
            qkv = _inproj(x, norm_mix[l], cd_w_in[i], CD_KINDS, t).reshape(b, t, -1)
            o_c = _na_attention(qkv, c_q, c_k, c_v, _na_bias(cd_rpb[i], C_KV, gc), C_KV, gc)
            outs, lses = [], []
            for gi, (dil, hs, geo, bias_d) in enumerate(d_groups):
                qd = qkv[:, :, d_q + gi * dw:d_q + (gi + 1) * dw]
                if dil == 1:
                    o_g, lse_g = _band_attention(qkv, d_q + gi * dw, qkv, d_k, qkv, d_v, bias_d,
                                                 D_SLOTS, 1, hs, geo, want_lse=True, out_dtype=F32)
                else:
                    kd, vd = qkv[:, :, d_k:d_k + dw], qkv[:, :, d_v:d_v + dw]
                    o_g, lse_g = _band_attention(_to_sub(qd, dil), 0, _to_sub(kd, dil), 0,
                                                 _to_sub(vd, dil), 0, bias_d, D_SLOTS, 1, hs, geo,
                                                 want_lse=True, out_dtype=F32)
                    o_g, lse_g = _from_sub(o_g, b, dil), _from_sub(lse_g, b, dil)
                outs.append(o_g.reshape(b * t, dw))
                lses.append(lse_g.reshape(b * t, dw))
            x = _outproj_cd(x, o_c.reshape(b * t, -1), outs, lses, cd_w_out[i])
        last = l == depth - 1
        x = _ffn(x, norm_ffn2[l], ffn2_w_in[l], ffn2_w_out[l], final_gain=final_norm if last else None,
                 out_rows=(bp * t, bs * t) if last else None)

    return (x[0].reshape(bp, t, d), x[1].reshape(bs, t, d))
```

```python
import functools
import math

import jax
import jax.numpy as jnp
import numpy as np
from jax import lax
from jax.experimental import pallas as pl
from jax.experimental.pallas import tpu as pltpu

HEAD_DIM = 128
GRID_W = 64
A_HEADS, A_KV, A_WINDOW = 8, 2, 128
B_HEADS, B_KV = 8, 2
ROPE_THETA = 10000.0
C_HEADS, C_KV = 8, 2
NA_ROWS, NA_COLS = 8, 16
D_PAIRS = ((128, 1), (512, 4), (2048, 16))
D_SLOTS = 4
D_HEADS = D_SLOTS * len(D_PAIRS)
T5_BUCKETS = 32
T5_MAX_DIST = 2048
NORM_EPS = 1e-6
NEG_INF = -1e30
ATTN_SCALE = HEAD_DIM ** -0.5
LOG2_E = math.log2(math.e)

LANES = 128
SUBLANES = 8
MXU_COLS = 256
DENSE_KEY_CHUNK = 2 * MXU_COLS
BAND_TILES_PER_STEP = (4, 2, 1)
NA_ROW_UNROLL = 4
MIB = 1024 * 1024
VMEM_LIMIT_BYTES = 56 * MIB

BF16 = jnp.bfloat16
F32 = jnp.float32


def _params(semantics):
    return pltpu.CompilerParams(dimension_semantics=semantics, vmem_limit_bytes=VMEM_LIMIT_BYTES)


def _rms(x, gain):
    ms = jnp.mean(x * x, axis=-1, keepdims=True)
    return x * lax.rsqrt(ms + NORM_EPS) * gain


def _ffn_body(*refs, n_ff, n_in, n_out, tiles_a, final_norm):
    refs = list(refs)
    x_refs = [refs.pop(0) for _ in range(n_in)]
    gain_ref, wg_ref, wu_ref, wo_ref = [refs.pop(0) for _ in range(4)]
    fgain_ref = refs.pop(0) if final_norm else None
    o_refs = [refs.pop(0) for _ in range(n_out)]
    h_scr = refs.pop(0)
    acc_ref = refs.pop(0) if n_out == 2 else o_refs[0]
    i, j = pl.program_id(0), pl.program_id(1)

    def start(x_ref):
        x = x_ref[...]
        h_scr[...] = _rms(x, gain_ref[...]).astype(BF16)
        acc_ref[...] = x

    if n_in == 1:
        pl.when(j == 0)(lambda: start(x_refs[0]))
    else:
        pl.when((j == 0) & (i < tiles_a))(lambda: start(x_refs[0]))
        pl.when((j == 0) & (i >= tiles_a))(lambda: start(x_refs[1]))

    h = h_scr[...]
    g = jnp.dot(h, wg_ref[...], preferred_element_type=F32)
    u = jnp.dot(h, wu_ref[...], preferred_element_type=F32)
    a = (g * jax.nn.sigmoid(g) * (0.5 * u)).astype(BF16)
    acc_ref[...] += jnp.dot(a, wo_ref[...], preferred_element_type=F32)

    def finish(o_ref):
        y = acc_ref[...]
        o_ref[...] = _rms(y, fgain_ref[...]) if final_norm else y

    if n_out == 2:
        pl.when((j == n_ff - 1) & (i < tiles_a))(lambda: finish(o_refs[0]))
        pl.when((j == n_ff - 1) & (i >= tiles_a))(lambda: finish(o_refs[1]))
    elif final_norm:
        pl.when(j == n_ff - 1)(lambda: finish(o_refs[0]))


def _ffn(xs, gain, w_in, w_out, final_gain=None, out_rows=None, *, tm=512, tf=512):
    xs = tuple(xs) if isinstance(xs, (tuple, list)) else (xs,)
    d = xs[0].shape[1]
    n = sum(x.shape[0] for x in xs)
    f = w_out.shape[0]
    n_ff = f // tf
    assert all(x.shape[0] % tm == 0 for x in xs) and f % tf == 0 and w_in.shape == (d, 2 * f)
    n_in, n_out = len(xs), 1 if out_rows is None else 2
    rows_a = xs[0].shape[0] if n_in == 2 else (out_rows[0] if n_out == 2 else n)
    assert n_in == 1 or n_out == 1 or out_rows[0] == rows_a
    tiles_a = rows_a // tm
    final_norm = final_gain is not None
    seg_a = lambda i, j: (jnp.minimum(i, tiles_a - 1), 0)
    seg_b = lambda i, j: (jnp.maximum(i - tiles_a, 0), 0)
    whole = lambda i, j: (i, 0)
    in_specs = [pl.BlockSpec((tm, d), m) for m in ((whole,) if n_in == 1 else (seg_a, seg_b))]
    in_specs += [
        pl.BlockSpec((1, d), lambda i, j: (0, 0)),
        pl.BlockSpec((d, tf), lambda i, j: (0, j)),
        pl.BlockSpec((d, tf), lambda i, j: (0, j + n_ff)),
        pl.BlockSpec((tf, d), lambda i, j: (j, 0)),
    ]
    args = list(xs) + [gain.reshape(1, d), w_in, w_in, w_out]
    if final_norm:
        in_specs.append(pl.BlockSpec((1, d), lambda i, j: (0, 0)))
        args.append(final_gain.reshape(1, d))
    scratch = [pltpu.VMEM((tm, d), BF16)]
    if n_out == 1:
        out_specs = pl.BlockSpec((tm, d), whole)
        out_shape = jax.ShapeDtypeStruct((n, d), F32)
    else:
        assert out_rows[0] % tm == 0 and sum(out_rows) == n
        out_specs = [pl.BlockSpec((tm, d), seg_a), pl.BlockSpec((tm, d), seg_b)]
        out_shape = [jax.ShapeDtypeStruct((r, d), F32) for r in out_rows]
        scratch.append(pltpu.VMEM((tm, d), F32))
    row_sem = "parallel" if n_in == 1 and n_out == 1 else "arbitrary"
    return pl.pallas_call(
        functools.partial(_ffn_body, n_ff=n_ff, n_in=n_in, n_out=n_out, tiles_a=tiles_a,
                          final_norm=final_norm),
        grid=(n // tm, n_ff),
        in_specs=in_specs,
        out_specs=out_specs,
        out_shape=out_shape,
        scratch_shapes=scratch,
        compiler_params=_params((row_sem, "arbitrary")),
        name="ffn",
    )(*args)


def _rope_norm(r, gain, cos, sin):
    y = _rms(r, gain)
    return y * cos + pltpu.roll(y, HEAD_DIM // 2, axis=1) * sin


def _inproj_body(*refs, kinds, rope, sub_plan, dils):
    if rope:
        x_ref, gain_ref, w_ref, qg_ref, kg_ref, cos_ref, sin_ref, o_ref, qt_ref, vt_ref = refs
    else:
        x_ref, gain_ref, w_ref, o_ref = refs[:4]
        sub_refs, slab_scr = refs[4:4 + len(dils)], refs[-1] if dils else None
    tm = x_ref.shape[0]
    h = _rms(x_ref[...], gain_ref[...]).astype(BF16)
    n_qt = 0
    for c, kind in enumerate(kinds):
        lo = c * MXU_COLS
        r = jnp.dot(h, w_ref[:, lo:lo + MXU_COLS], preferred_element_type=F32)
        if kind == "scale":
            r = r * ATTN_SCALE
        elif kind in ("qnorm", "knorm"):
            hg = qg_ref[...] if kind == "qnorm" else kg_ref[...]
            cos = cos_ref[...]
            sin = sin_ref[...]
            parts = []
            for hh in range(MXU_COLS // HEAD_DIM):
                y = _rope_norm(r[:, hh * HEAD_DIM:(hh + 1) * HEAD_DIM], hg, cos, sin)
                if kind == "qnorm":
                    y = y * (ATTN_SCALE * LOG2_E)
                parts.append(y)
            r = jnp.concatenate(parts, axis=1)
            if kind == "qnorm":
                qt_ref[0, n_qt * MXU_COLS:(n_qt + 1) * MXU_COLS, :] = r.T.astype(BF16)
                n_qt += 1
        elif kind == "plain_t":
            vt_ref[0] = r.T.astype(BF16)
        o_ref[:, lo:lo + MXU_COLS] = r.astype(BF16)
        if c in sub_plan:
            heads = range(MXU_COLS // HEAD_DIM)
            for hh in heads:
                slab_scr[hh] = r[:, hh * HEAD_DIM:(hh + 1) * HEAD_DIM]
            for si, col0 in sub_plan[c]:
                d = dils[si]
                for rho in range(d):
                    for hh in heads:
                        part = slab_scr[hh, pl.ds(rho, tm // d, stride=d), :]
                        cols = slice(col0 + hh * HEAD_DIM, col0 + (hh + 1) * HEAD_DIM)
                        sub_refs[si][0, rho, :, cols] = part.astype(BF16)


def _inproj(x, gain, w, kinds, seq_len, rope_args=None, sub_args=None, *, tm=512):
    n, d = x.shape
    c = w.shape[1]
    assert n % tm == 0 and seq_len % tm == 0 and c == MXU_COLS * len(kinds)
    rope = rope_args is not None
    assert not (rope and sub_args)
    dils, sub_width, sub_plan = sub_args if sub_args else ((), 0, {})
    scratch = [pltpu.VMEM((MXU_COLS // HEAD_DIM, tm, HEAD_DIM), F32)] if dils else []
    in_specs = [
        pl.BlockSpec((tm, d), lambda i: (i, 0)),
        pl.BlockSpec((1, d), lambda i: (0, 0)),
        pl.BlockSpec((d, c), lambda i: (0, 0), pipeline_mode=pl.Buffered(1)),
    ]
    args = [x, gain.reshape(1, d), w]
    if rope:
        q_gain, k_gain, cos, sin = rope_args
        tiles_per_seq = seq_len // tm
        in_specs += [
            pl.BlockSpec((1, HEAD_DIM), lambda i: (0, 0)),
            pl.BlockSpec((1, HEAD_DIM), lambda i: (0, 0)),
            pl.BlockSpec((tm, HEAD_DIM), lambda i: (i % tiles_per_seq, 0)),
            pl.BlockSpec((tm, HEAD_DIM), lambda i: (i % tiles_per_seq, 0)),
        ]
        args += [q_gain.reshape(1, HEAD_DIM), k_gain.reshape(1, HEAD_DIM), cos, sin]
    out_specs = pl.BlockSpec((tm, c), lambda i: (i, 0))
    out_shape = jax.ShapeDtypeStruct((n, c), BF16)
    if rope:
        assert kinds.count("plain_t") == 1
        qt_rows = MXU_COLS * kinds.count("qnorm")
        by_seq = lambda i: (i // tiles_per_seq, 0, i % tiles_per_seq)
        out_specs = [out_specs, pl.BlockSpec((1, qt_rows, tm), by_seq),
                     pl.BlockSpec((1, MXU_COLS, tm), by_seq)]
        out_shape = [out_shape, jax.ShapeDtypeStruct((n // seq_len, qt_rows, seq_len), BF16),
                     jax.ShapeDtypeStruct((n // seq_len, MXU_COLS, seq_len), BF16)]
    if dils:
        tiles_per_seq = seq_len // tm
        assert all(tm % (dl * 16) == 0 for dl in dils)
        out_specs = [out_specs] + [
            pl.BlockSpec((1, dl, tm // dl, sub_width), lambda i: (i // tiles_per_seq, 0, i % tiles_per_seq, 0))
            for dl in dils]
        out_shape = [out_shape] + [
            jax.ShapeDtypeStruct((n // seq_len, dl, seq_len // dl, sub_width), BF16) for dl in dils]
    return pl.pallas_call(
        functools.partial(_inproj_body, kinds=tuple(kinds), rope=rope, sub_plan=sub_plan, dils=tuple(dils)),
        grid=(n // tm,),
        in_specs=in_specs,
        out_specs=out_specs,
        out_shape=out_shape,
        scratch_shapes=scratch,
        compiler_params=_params(("parallel",)),
        name="inproj",
    )(*args)


def _outproj_ab_body(x_ref, a_ref, b_ref, w_ref, o_ref):
    ka = a_ref.shape[1]
    acc = jnp.dot(a_ref[...], w_ref[0:ka, :], preferred_element_type=F32)
    acc = acc + jnp.dot(b_ref[...], w_ref[ka:, :], preferred_element_type=F32)
    o_ref[...] = x_ref[...] + acc


def _outproj_ab(x, oa, ob, w, *, tm=512):
    n, d = x.shape
    ka, kb = oa.shape[1], ob.shape[1]
    assert n % tm == 0 and w.shape == (ka + kb, d)
    return pl.pallas_call(
        _outproj_ab_body,
        grid=(n // tm,),
        in_specs=[
            pl.BlockSpec((tm, d), lambda i: (i, 0)),
            pl.BlockSpec((tm, ka), lambda i: (i, 0)),
            pl.BlockSpec((tm, kb), lambda i: (i, 0)),
            pl.BlockSpec((ka + kb, d), lambda i: (0, 0), pipeline_mode=pl.Buffered(1)),
        ],
        out_specs=pl.BlockSpec((tm, d), lambda i: (i, 0)),
        out_shape=jax.ShapeDtypeStruct((n, d), F32),
        compiler_params=_params(("parallel",)),
        name="outproj_ab",
    )(x, oa, ob, w)


def _outproj_cd_body(*refs, dils, kd):
    ng = len(dils)
    x_ref, c_ref = refs[:2]
    o_refs, l_refs = refs[2:2 + ng], refs[2 + ng:2 + 2 * ng]
    w_ref, o_ref, row_scr = refs[2 + 2 * ng:]
    tm = x_ref.shape[0]
    slabs = kd // HEAD_DIM

    def token_rows(ref, d, slot):
        if d == 1:
            return ref[...]
        for rho in range(d):
            for s in range(slabs):
                lo = rho * kd + s * HEAD_DIM
                row_scr[slot, s, pl.ds(rho, tm // d, stride=d), :] = ref[0, :, lo:lo + HEAD_DIM]
        return jnp.concatenate([row_scr[slot, s] for s in range(slabs)], axis=1)

    outs = [token_rows(r, d, 2 * i) for i, (r, d) in enumerate(zip(o_refs, dils))]
    lses = [token_rows(r, d, 2 * i + 1) for i, (r, d) in enumerate(zip(l_refs, dils))]
    m = functools.reduce(jnp.maximum, lses)
    es = [jnp.exp(l - m) for l in lses]
    den = functools.reduce(lambda a, b: a + b, es)
    od = functools.reduce(lambda a, b: a + b, [(e / den) * o for e, o in zip(es, outs)])
    kc = c_ref.shape[1]
    acc = jnp.dot(c_ref[...], w_ref[0:kc, :], preferred_element_type=F32)
    acc = acc + jnp.dot(od.astype(BF16), w_ref[kc:, :], preferred_element_type=F32)
    o_ref[...] = x_ref[...] + acc


def _outproj_cd(x, oc, outs, lses, dils, seq_len, w, *, tm=512):
    n, d = x.shape
    kc = oc.shape[1]
    kd = w.shape[0] - kc
    assert n % tm == 0 and seq_len % tm == 0 and all(tm % (dl * SUBLANES) == 0 for dl in dils)
    tiles_per_seq = seq_len // tm
    row = lambda i: (i, 0)
    by_seq = lambda i: (i // tiles_per_seq, i % tiles_per_seq, 0)
    g_specs = [pl.BlockSpec((tm, kd), row) if dl == 1 else pl.BlockSpec((1, tm // dl, dl * kd), by_seq)
               for dl in dils]
    return pl.pallas_call(
        functools.partial(_outproj_cd_body, dils=tuple(dils), kd=kd),
        grid=(n // tm,),
        in_specs=[pl.BlockSpec((tm, d), row), pl.BlockSpec((tm, kc), row)] + g_specs + g_specs
        + [pl.BlockSpec((kc + kd, d), lambda i: (0, 0), pipeline_mode=pl.Buffered(1))],
        out_specs=pl.BlockSpec((tm, d), row),
        out_shape=jax.ShapeDtypeStruct((n, d), F32),
        scratch_shapes=[pltpu.VMEM((2 * len(dils), kd // HEAD_DIM, tm, HEAD_DIM), F32)],
        compiler_params=_params(("parallel",)),
        name="outproj_cd",
    )(x, oc, *outs, *lses, w)


def _stack_heads(q, g):
    return jnp.concatenate([q[:, i * HEAD_DIM:(i + 1) * HEAD_DIM] for i in range(g)], axis=0)


def _softmax_pv(s, v):
    m = jnp.max(s, axis=-1, keepdims=True)
    p = jnp.exp(s - m)
    l = jnp.sum(p, axis=-1, keepdims=True)
    o = jnp.dot(p.astype(BF16), v, preferred_element_type=F32) / l
    return o, m, l


def _qk(q, k):
    return lax.dot_general(q, k, (((1,), (1,)), ((), ())), preferred_element_type=F32)


def _dense_body(qt_ref, k_ref, vt_ref, o_ref, s_scr, p_scr, m_scr, l_scr, *, g):
    step = pl.program_id(2)
    tq = qt_ref.shape[2] // 2
    t = k_ref.shape[1]
    gq = g * tq
    kc = min(DENSE_KEY_CHUNK, t)

    @pl.when(step == 0)
    def _():
        s_scr[1] = jnp.zeros(s_scr.shape[1:], s_scr.dtype)
        m_scr[1] = jnp.zeros(m_scr.shape[1:], m_scr.dtype)
        p_scr[...] = jnp.zeros(p_scr.shape, p_scr.dtype)
        l_scr[...] = jnp.ones(l_scr.shape, l_scr.dtype)

    def fold(x, op):
        return op(x.reshape(x.shape[0] // SUBLANES, SUBLANES, x.shape[1]), axis=0)

    for cur in range(2):
        prev = 1 - cur
        toks = slice(cur * tq, (cur + 1) * tq)
        qt = jnp.concatenate([qt_ref[0, i * HEAD_DIM:(i + 1) * HEAD_DIM, toks] for i in range(g)], axis=1)
        m = jnp.max(m_scr[prev], axis=0, keepdims=True)
        ot = jnp.zeros((HEAD_DIM, gq), F32)
        l_acc = jnp.zeros((SUBLANES, gq), F32)
        m_acc = None
        for c in range(t // kc):
            keys = slice(c * kc, (c + 1) * kc)
            p = jnp.exp2(s_scr[prev, keys, :] - m)
            l_acc = l_acc + fold(p, jnp.sum)
            p_scr[prev, keys, :] = p.astype(BF16)
        for c in range(t // kc):
            keys = slice(c * kc, (c + 1) * kc)
            ot = ot + jnp.dot(vt_ref[0, :, keys], p_scr[cur, keys, :], preferred_element_type=F32)
            s = jnp.dot(k_ref[0, keys, :], qt, preferred_element_type=F32)
            s_scr[cur, keys, :] = s
            m_chunk = fold(s, jnp.max)
            m_acc = m_chunk if m_acc is None else jnp.maximum(m_acc, m_chunk)
        ot = ot / jnp.sum(l_scr[cur], axis=0, keepdims=True)
        l_scr[prev] = l_acc
        m_scr[cur] = m_acc
        for i in range(g):
            o_ref[0, toks, i * HEAD_DIM:(i + 1) * HEAD_DIM] = ot[:, i * tq:(i + 1) * tq].T.astype(o_ref.dtype)


def _dense_attention(qt, qkv, vt, k_col, n_kv, g, *, tq=128):
    b, t, _ = qkv.shape
    gw = g * HEAD_DIM
    n_pairs = t // (2 * tq)
    assert t % (2 * tq) == 0 and k_col % HEAD_DIM == 0
    assert qt.shape == (b, n_kv * gw, t) and vt.shape == (b, n_kv * HEAD_DIM, t)
    kb = k_col // HEAD_DIM
    return pl.pallas_call(
        functools.partial(_dense_body, g=g),
        grid=(b, n_kv, n_pairs + 1),
        in_specs=[
            pl.BlockSpec((1, gw, 2 * tq), lambda n, h, k: (n, h, jnp.minimum(k, n_pairs - 1))),
            pl.BlockSpec((1, t, HEAD_DIM), lambda n, h, k: (n, 0, kb + h)),
            pl.BlockSpec((1, HEAD_DIM, t), lambda n, h, k: (n, h, 0)),
        ],
        out_specs=pl.BlockSpec((1, 2 * tq, gw), lambda n, h, k: (n, jnp.maximum(k - 1, 0), h)),
        out_shape=jax.ShapeDtypeStruct((b, t, n_kv * gw), BF16),
        scratch_shapes=[pltpu.VMEM((2, t, g * tq), F32), pltpu.VMEM((2, t, g * tq), BF16),
                        pltpu.VMEM((2, SUBLANES, g * tq), F32), pltpu.VMEM((2, SUBLANES, g * tq), F32)],
        compiler_params=_params(("parallel", "parallel", "arbitrary")),
        name="dense_attn",
    )(qt, qkv, vt)


def _band_body(*refs, n_kv, g, w, seq, tq, kw, n_cases, has_sink, want_lse, side_by_side):
    refs = list(refs)
    q_ref, k_ref, v_ref, bias_ref = refs[:4]
    refs = refs[4:]
    sink_ref = refs.pop(0) if has_sink else None
    o_ref = refs.pop(0)
    lse_ref = refs.pop(0) if want_lse else None
    gw = g * HEAD_DIM
    qw = n_kv * gw
    nq = seq // tq
    for sq, j in [(a, c) for a in range(q_ref.shape[0]) for c in range(q_ref.shape[1] // tq)]:
        qi = pl.program_id(1) * (q_ref.shape[1] // tq) + j
        kstart = pl.multiple_of(jnp.clip(qi * tq - w, 0, seq - kw), 64)
        case = 0 if n_cases == 1 else jnp.where(qi == 0, 0, jnp.where(qi == nq - 1, 2, 1))
        trows = slice(j * tq, (j + 1) * tq)
        for h in range(n_kv):
            k = k_ref[sq, pl.ds(kstart, kw), h * HEAD_DIM:(h + 1) * HEAD_DIM]
            v = v_ref[sq, pl.ds(kstart, kw), h * HEAD_DIM:(h + 1) * HEAD_DIM]
            q = _stack_heads(q_ref[sq, trows, h * gw:(h + 1) * gw], g)
            s = _qk(q, k) + bias_ref[case, h]
            o, m, l = _softmax_pv(s, v)
            lse = m + jnp.log(l)
            for i in range(g):
                rows = slice(i * tq, (i + 1) * tq)
                cols = slice(h * gw + i * HEAD_DIM, h * gw + (i + 1) * HEAD_DIM)
                oi = o[rows]
                if has_sink:
                    oi = oi * jax.nn.sigmoid(lse[rows] - sink_ref[h * g + i])
                if side_by_side:
                    osq, cols = 0, slice(sq * qw + cols.start, sq * qw + cols.stop)
                else:
                    osq = sq
                o_ref[osq, trows, cols] = oi.astype(o_ref.dtype)
                if want_lse:
                    lse_ref[osq, trows, cols] = jnp.broadcast_to(lse[rows], (tq, HEAD_DIM))


def _band_geometry(seq, w):
    if seq <= 4 * w:
        return seq, seq, (0,)
    tq = 2 * w if w < 128 else w
    tq = min(tq, 256)
    kw = tq + 2 * w
    assert seq % tq == 0 and seq >= kw
    return tq, kw, (0, -w, -2 * w)


def _band_attention(q_arr, q_col, k_arr, k_col, v_arr, v_col, bias, n_kv, g, w, geometry,
                    sink=None, want_lse=False, out_dtype=BF16, dil=1):
    n, seq, _ = q_arr.shape
    tq, kw, shifts = geometry
    nq = seq // tq
    gw = g * HEAD_DIM
    qw, kvw = n_kv * gw, n_kv * HEAD_DIM
    assert q_col % qw == 0 and k_col % kvw == 0 and v_col % kvw == 0
    qb, kb, vb = q_col // qw, k_col // kvw, v_col // kvw
    tps = next(c for c in BAND_TILES_PER_STEP if nq % c == 0)
    nq_steps = nq // tps
    sps = next(c for c in BAND_TILES_PER_STEP if n % c == 0) if nq == 1 else 1
    in_specs = [
        pl.BlockSpec((sps, tps * tq, qw), lambda b, i: (b, i, qb)),
        pl.BlockSpec((sps, seq, kvw), lambda b, i: (b, 0, kb)),
        pl.BlockSpec((sps, seq, kvw), lambda b, i: (b, 0, vb)),
        pl.BlockSpec((len(shifts), n_kv, g * tq, kw), lambda b, i: (0, 0, 0, 0)),
    ]
    args = [q_arr, k_arr, v_arr, bias]
    if sink is not None:
        in_specs.append(pl.BlockSpec(memory_space=pltpu.SMEM))
        args.append(sink.astype(F32))
    if dil == 1:
        o_spec = pl.BlockSpec((sps, tps * tq, qw), lambda b, i: (b, i, 0))
        o_dims = (n, seq, qw)
    else:
        assert n % dil == 0 and dil % sps == 0
        o_spec = pl.BlockSpec((1, tps * tq, sps * qw),
                              lambda b, i: ((b * sps) // dil, i, ((b * sps) % dil) // sps))
        o_dims = (n // dil, seq, dil * qw)
    o_shape = jax.ShapeDtypeStruct(o_dims, out_dtype)
    out_specs, out_shape = o_spec, o_shape
    if want_lse:
        out_specs = [o_spec, o_spec]
        out_shape = [o_shape, jax.ShapeDtypeStruct(o_dims, F32)]
    return pl.pallas_call(
        functools.partial(_band_body, n_kv=n_kv, g=g, w=w, seq=seq, tq=tq, kw=kw, n_cases=len(shifts),
                          has_sink=sink is not None, want_lse=want_lse, side_by_side=dil > 1),
        grid=(n // sps, nq_steps),
        in_specs=in_specs,
        out_specs=out_specs,
        out_shape=out_shape,
        compiler_params=_params(("parallel", "arbitrary")),
        name="band_attn",
    )(*args)


def _na_body(q_ref, k_ref, v_ref, bias_ref, o_ref, *, g, rows, rows_per_step):
    rb = pl.program_id(2)
    kwin = NA_ROWS * GRID_W

    def one_row(rr, carry):
        r = rb * rows_per_step + rr
        rs = jnp.clip(r - NA_ROWS // 2, 0, rows - NA_ROWS)
        off = rs - r + NA_ROWS - 1
        kstart = pl.multiple_of(rs * GRID_W, GRID_W)
        qstart = pl.multiple_of(rr * GRID_W, GRID_W)
        k = k_ref[0, pl.ds(kstart, kwin), :]
        v = v_ref[0, pl.ds(kstart, kwin), :]
        q = _stack_heads(q_ref[0, pl.ds(qstart, GRID_W), :], g)
        s = _qk(q, k) + bias_ref[0, off]
        o, _, _ = _softmax_pv(s, v)
        for i in range(g):
            o_ref[0, pl.ds(qstart, GRID_W), i * HEAD_DIM:(i + 1) * HEAD_DIM] = (
                o[i * GRID_W:(i + 1) * GRID_W].astype(o_ref.dtype))
        return carry

    lax.fori_loop(0, rows_per_step, one_row, 0, unroll=NA_ROW_UNROLL)


def _na_attention(qkv, q_col, k_col, v_col, bias, n_kv, g, *, rows_per_step=8):
    b, t, _ = qkv.shape
    rows = t // GRID_W
    gw = g * HEAD_DIM
    assert rows % rows_per_step == 0 and rows >= NA_ROWS
    qb, kb, vb = q_col // gw, k_col // HEAD_DIM, v_col // HEAD_DIM
    tq = rows_per_step * GRID_W
    return pl.pallas_call(
        functools.partial(_na_body, g=g, rows=rows, rows_per_step=rows_per_step),
        grid=(b, n_kv, rows // rows_per_step),
        in_specs=[
            pl.BlockSpec((1, tq, gw), lambda n, h, i: (n, i, qb + h)),
            pl.BlockSpec((1, t, HEAD_DIM), lambda n, h, i: (n, 0, kb + h)),
            pl.BlockSpec((1, t, HEAD_DIM), lambda n, h, i: (n, 0, vb + h)),
            pl.BlockSpec((1, NA_ROWS, g * GRID_W, NA_ROWS * GRID_W), lambda n, h, i: (h, 0, 0, 0)),
        ],
        out_specs=pl.BlockSpec((1, tq, gw), lambda n, h, i: (n, i, h)),
        out_shape=jax.ShapeDtypeStruct((b, t, n_kv * gw), BF16),
        compiler_params=_params(("parallel", "parallel", "arbitrary")),
        name="na_attn",
    )(qkv, qkv, qkv, bias)


def _t5_bucket(rel):
    half = T5_BUCKETS // 2
    max_exact = half // 2
    n = jnp.abs(rel)
    nf = jnp.maximum(n, 1).astype(jnp.float32)
    large = max_exact + (jnp.log(nf / max_exact) / math.log(T5_MAX_DIST / max_exact)
                         * (half - max_exact)).astype(jnp.int32)
    large = jnp.minimum(large, half - 1)
    return jnp.where(rel > 0, half, 0) + jnp.where(n < max_exact, n, large)


def _band_bias(table_cols, n_kv, g, w, dil, geometry):
    tq, kw, shifts = geometry
    iq = jnp.arange(tq)[:, None]
    jk = jnp.arange(kw)[None, :]
    tiles = []
    for shift in shifts:
        rel = jk + shift - iq
        bucket = _t5_bucket(rel * dil)
        tb = jnp.zeros((tq, kw, table_cols.shape[1]), F32)
        for r in range(T5_BUCKETS):
            tb = jnp.where((bucket == r)[..., None], table_cols[r].astype(F32), tb)
        tb = jnp.where((jnp.abs(rel) <= w)[..., None], tb, NEG_INF)
        tiles.append(jnp.moveaxis(tb, -1, 0).reshape(n_kv, g * tq, kw))
    return jnp.stack(tiles)


def _na_bias(rpb, n_kv, g):
    col = jnp.arange(GRID_W)
    qc, kc = col[:, None], col[None, :]
    dc = jnp.clip(kc - qc + NA_COLS - 1, 0, 2 * NA_COLS - 2)
    cs = jnp.clip(qc - NA_COLS // 2, 0, GRID_W - NA_COLS)
    mask = (kc >= cs) & (kc < cs + NA_COLS)
    rpb = rpb.astype(F32)
    t = jnp.zeros(rpb.shape[:2] + dc.shape, F32)
    for c in range(2 * NA_COLS - 1):
        t = jnp.where(dc == c, rpb[:, :, c][:, :, None, None], t)
    t = jnp.where(mask, t, NEG_INF)
    cls = jnp.stack([t[:, o:o + NA_ROWS] for o in range(NA_ROWS)])
    cls = cls.reshape(NA_ROWS, n_kv, g, NA_ROWS, GRID_W, GRID_W)
    cls = jnp.transpose(cls, (1, 0, 2, 4, 3, 5))
    return cls.reshape(n_kv, NA_ROWS, g * GRID_W, NA_ROWS * GRID_W)


def _rope_tables(t):
    n_pairs = HEAD_DIM // 2
    n_freq = n_pairs // 2
    pos = jnp.arange(t)
    row = (pos // GRID_W).astype(jnp.float32)
    col = (pos % GRID_W).astype(jnp.float32)
    omega = ROPE_THETA ** (-(jnp.arange(n_freq, dtype=jnp.float32) * 2.0 / n_pairs))
    ang = jnp.concatenate([row[:, None] * omega, col[:, None] * omega], axis=-1)
    cos, sin = jnp.cos(ang), jnp.sin(ang)
    return jnp.concatenate([cos, cos], axis=-1), jnp.concatenate([-sin, sin], axis=-1)


def _deinterleave_heads(w, n_heads):
    lead = w.shape[:-1]
    w = w.reshape(lead + (n_heads, HEAD_DIM // 2, 2))
    return jnp.swapaxes(w, -1, -2).reshape(lead + (n_heads * HEAD_DIM,))


AB_KINDS = ("scale",) * 4 + ("qnorm",) * 4 + ("plain", "plain", "knorm", "plain_t")
CD_KINDS = ("scale",) * 4 + ("plain",) * 2 + ("scale",) * 6 + ("plain",) * 4


def kernel(x_prompt, x_sample, norm_ffn1, ffn1_w_in, ffn1_w_out, norm_mix, ab_w_in, ab_sink, ab_q_gain, ab_k_gain, ab_w_out, cd_w_in, cd_rpb, cd_w_out, norm_ffn2, ffn2_w_in, ffn2_w_out, t5_table, final_norm):
    (bp, t, d), bs = x_prompt.shape, x_sample.shape[0]
    assert x_sample.shape[1:] == (t, d)
    b = bp + bs
    depth = norm_ffn1.shape[0]
    x = (x_prompt.reshape(bp * t, d), x_sample.reshape(bs * t, d))

    ffn1_w_in, ffn1_w_out = ffn1_w_in.astype(BF16), ffn1_w_out.astype(BF16)
    ffn2_w_in, ffn2_w_out = ffn2_w_in.astype(BF16), ffn2_w_out.astype(BF16)
    qa_w, kva_w = A_HEADS * HEAD_DIM, 2 * A_KV * HEAD_DIM
    qb_lo = qa_w + kva_w
    qb_hi = qb_lo + B_HEADS * HEAD_DIM
    kb_hi = qb_hi + B_KV * HEAD_DIM
    ab_w_in = jnp.concatenate([
        ab_w_in[..., :qa_w],
        _deinterleave_heads(ab_w_in[..., qb_lo:qb_hi], B_HEADS),
        ab_w_in[..., qa_w:qb_lo],
        _deinterleave_heads(ab_w_in[..., qb_hi:kb_hi], B_KV),
        ab_w_in[..., kb_hi:]], axis=-1).astype(BF16)
    ab_q_gain = _deinterleave_heads(ab_q_gain, 1)
    ab_k_gain = _deinterleave_heads(ab_k_gain, 1)
    ab_w_out, cd_w_in, cd_w_out = ab_w_out.astype(BF16), cd_w_in.astype(BF16), cd_w_out.astype(BF16)

    cos, sin = _rope_tables(t)
    ga, gc = A_HEADS // A_KV, C_HEADS // C_KV
    geo_a = _band_geometry(t, A_WINDOW)
    bias_a = _band_bias(t5_table[:, :A_HEADS], A_KV, ga, A_WINDOW, 1, geo_a)
    d_groups = []
    for gi, (win, dil) in enumerate(D_PAIRS):
        hs = win // (2 * dil)
        geo = _band_geometry(t // dil, hs)
        col0 = A_HEADS + gi * D_SLOTS
        d_groups.append((dil, hs, geo, _band_bias(t5_table[:, col0:col0 + D_SLOTS], D_SLOTS, 1, hs, dil, geo)))

    a_q = 0
    a_k = qa_w + B_HEADS * HEAD_DIM
    a_v = a_k + A_KV * HEAD_DIM
    b_k = a_v + A_KV * HEAD_DIM
    c_q, c_k, c_v = 0, C_HEADS * HEAD_DIM, (C_HEADS + C_KV) * HEAD_DIM
    d_q = (C_HEADS + 2 * C_KV) * HEAD_DIM
    d_k = d_q + D_HEADS * HEAD_DIM
    d_v = d_k + D_SLOTS * HEAD_DIM
    dw = D_SLOTS * HEAD_DIM
    sub_dils = tuple(dil for _, dil in D_PAIRS if dil > 1)
    sub_plan = {}
    for gi, (_, dil) in enumerate(D_PAIRS):
        if dil > 1:
            si = sub_dils.index(dil)
            for src, dst in ((d_q + gi * dw, 0), (d_k, dw), (d_v, 2 * dw)):
                for off in range(0, dw, MXU_COLS):
                    sub_plan.setdefault((src + off) // MXU_COLS, []).append((si, dst + off))

    for l in range(depth):
        i = l // 2
        x = _ffn(x, norm_ffn1[l], ffn1_w_in[l], ffn1_w_out[l])
        if l % 2 == 0:
            qkv, qbt, vbt = _inproj(x, norm_mix[l], ab_w_in[i], AB_KINDS, t,
                                    (ab_q_gain[i], ab_k_gain[i], cos, sin))
            qkv = qkv.reshape(b, t, -1)
            o_a = _band_attention(qkv, a_q, qkv, a_k, qkv, a_v, bias_a, A_KV, ga, A_WINDOW, geo_a,
                                  sink=ab_sink[i])
            o_b = _dense_attention(qbt, qkv, vbt, b_k, B_KV, B_HEADS // B_KV)
            x = _outproj_ab(x, o_a.reshape(b * t, -1), o_b.reshape(b * t, -1), ab_w_out[i])
        else:
            qkv, *subs = _inproj(x, norm_mix[l], cd_w_in[i], CD_KINDS, t,
                                 sub_args=(sub_dils, 3 * dw, sub_plan))
            qkv = qkv.reshape(b, t, -1)
            o_c = _na_attention(qkv, c_q, c_k, c_v, _na_bias(cd_rpb[i], C_KV, gc), C_KV, gc)
            outs, lses = [], []
            for gi, (dil, hs, geo, bias_d) in enumerate(d_groups):
                if dil == 1:
                    o_g, lse_g = _band_attention(qkv, d_q + gi * dw, qkv, d_k, qkv, d_v, bias_d,
                                                 D_SLOTS, 1, hs, geo, want_lse=True, out_dtype=F32)
                    o_g, lse_g = o_g.reshape(b * t, dw), lse_g.reshape(b * t, dw)
                else:
                    sub = subs[sub_dils.index(dil)].reshape(b * dil, t // dil, 3 * dw)
                    o_g, lse_g = _band_attention(sub, 0, sub, dw, sub, 2 * dw, bias_d, D_SLOTS, 1, hs, geo,
                                                 want_lse=True, out_dtype=F32, dil=dil)
                outs.append(o_g)
                lses.append(lse_g)
            x = _outproj_cd(x, o_c.reshape(b * t, -1), outs, lses, [g[0] for g in d_groups], t, cd_w_out[i])
        last = l == depth - 1
        x = _ffn(x, norm_ffn2[l], ffn2_w_in[l], ffn2_w_out[l], final_gain=final_norm if last else None,
                 out_rows=(bp * t, bs * t) if last else None)

    return (x[0].reshape(bp, t, d), x[1].reshape(bs, t, d))
```

```python
import functools
import math

import jax
import jax.numpy as jnp
import numpy as np
from jax import lax
from jax.experimental import pallas as pl
from jax.experimental.pallas import tpu as pltpu

HEAD_DIM = 128
GRID_W = 64
A_HEADS, A_KV, A_WINDOW = 8, 2, 128
B_HEADS, B_KV = 8, 2
ROPE_THETA = 10000.0
C_HEADS, C_KV = 8, 2
NA_ROWS, NA_COLS = 8, 16
D_PAIRS = ((128, 1), (512, 4), (2048, 16))
D_SLOTS = 4
D_HEADS = D_SLOTS * len(D_PAIRS)
T5_BUCKETS = 32
T5_MAX_DIST = 2048
NORM_EPS = 1e-6
NEG_INF = -1e30
ATTN_SCALE = HEAD_DIM ** -0.5
LOG2_E = math.log2(math.e)

LANES = 128
SUBLANES = 8
MXU_COLS = 256
FFN_ROW_TILE = 1024
FFN_FF_TILE = 2 * MXU_COLS
DENSE_KEY_CHUNK = 2 * MXU_COLS
BAND_TILES_PER_STEP = (4, 2, 1)
NA_ROW_UNROLL = 4
MIB = 1024 * 1024
VMEM_LIMIT_BYTES = 56 * MIB

BF16 = jnp.bfloat16
F32 = jnp.float32


def _params(semantics):
    return pltpu.CompilerParams(dimension_semantics=semantics, vmem_limit_bytes=VMEM_LIMIT_BYTES)


def _resident_weight(shape, layer):
    return pl.BlockSpec((None,) + shape, lambda *_: (layer,) + (0,) * len(shape),
                        pipeline_mode=pl.Buffered(1))


def _rms(x, gain):
    ms = jnp.mean(x * x, axis=-1, keepdims=True)
    return x * lax.rsqrt(ms + NORM_EPS) * gain


def _ffn_body(*refs, n_ff, n_in, n_out, tiles_a, final_norm):
    refs = list(refs)
    x_refs = [refs.pop(0) for _ in range(n_in)]
    gain_ref, wgu_ref, wo_ref = [refs.pop(0) for _ in range(3)]
    fgain_ref = refs.pop(0) if final_norm else None
    o_refs = [refs.pop(0) for _ in range(n_out)]
    h_scr = refs.pop(0)
    acc_ref = refs.pop(0) if n_out == 2 else o_refs[0]
    i, j = pl.program_id(0), pl.program_id(1)

    def start(x_ref):
        x = x_ref[...]
        h_scr[...] = _rms(x, gain_ref[...]).astype(BF16)
        acc_ref[...] = x

    if n_in == 1:
        pl.when(j == 0)(lambda: start(x_refs[0]))
    else:
        pl.when((j == 0) & (i < tiles_a))(lambda: start(x_refs[0]))
        pl.when((j == 0) & (i >= tiles_a))(lambda: start(x_refs[1]))

    h = h_scr[...]
    tf = wo_ref.shape[0]
    gu = jnp.dot(h, wgu_ref[...], preferred_element_type=F32)
    g, u = gu[:, :tf], gu[:, tf:]
    a = (g * jax.nn.sigmoid(g) * (0.5 * u)).astype(BF16)
    acc_ref[...] += jnp.dot(a, wo_ref[...], preferred_element_type=F32)

    def finish(o_ref):
        y = acc_ref[...]
        o_ref[...] = _rms(y, fgain_ref[...]) if final_norm else y

    if n_out == 2:
        pl.when((j == n_ff - 1) & (i < tiles_a))(lambda: finish(o_refs[0]))
        pl.when((j == n_ff - 1) & (i >= tiles_a))(lambda: finish(o_refs[1]))
    elif final_norm:
        pl.when(j == n_ff - 1)(lambda: finish(o_refs[0]))


def _pack_gate_up(w_in, tf):
    layers, d, f2 = w_in.shape
    w = w_in.astype(BF16).reshape(layers, d, 2, f2 // (2 * tf), tf)
    return jnp.transpose(w, (0, 3, 1, 2, 4)).reshape(layers, f2 // (2 * tf), d, 2 * tf)


def _ffn(xs, gain, w_gu, w_out, layer, final_gain=None, out_rows=None, *, tm=FFN_ROW_TILE):
    xs = tuple(xs) if isinstance(xs, (tuple, list)) else (xs,)
    d = xs[0].shape[1]
    n = sum(x.shape[0] for x in xs)
    f = w_out.shape[1]
    n_ff = w_gu.shape[1]
    tf = f // n_ff
    if len(xs) == 2 or out_rows is not None:
        tm //= 2
    tm = math.gcd(tm, *(x.shape[0] for x in xs))
    assert w_gu.shape[2:] == (d, 2 * tf) and w_out.shape[2] == d
    n_in, n_out = len(xs), 1 if out_rows is None else 2
    rows_a = xs[0].shape[0] if n_in == 2 else (out_rows[0] if n_out == 2 else n)
    assert n_in == 1 or n_out == 1 or out_rows[0] == rows_a
    tiles_a = rows_a // tm
    final_norm = final_gain is not None
    seg_a = lambda i, j: (jnp.minimum(i, tiles_a - 1), 0)
    seg_b = lambda i, j: (jnp.maximum(i - tiles_a, 0), 0)
    whole = lambda i, j: (i, 0)
    in_specs = [pl.BlockSpec((tm, d), m) for m in ((whole,) if n_in == 1 else (seg_a, seg_b))]
    in_specs += [
        pl.BlockSpec((1, d), lambda i, j: (0, 0)),
        pl.BlockSpec((None, None, d, 2 * tf), lambda i, j: (layer, j, 0, 0)),
        pl.BlockSpec((None, tf, d), lambda i, j: (layer, j, 0)),
    ]
    args = list(xs) + [gain.reshape(1, d), w_gu, w_out]
    if final_norm:
        in_specs.append(pl.BlockSpec((1, d), lambda i, j: (0, 0)))
        args.append(final_gain.reshape(1, d))
    scratch = [pltpu.VMEM((tm, d), BF16)]
    if n_out == 1:
        out_specs = pl.BlockSpec((tm, d), whole)
        out_shape = jax.ShapeDtypeStruct((n, d), F32)
    else:
        assert out_rows[0] % tm == 0 and sum(out_rows) == n
        out_specs = [pl.BlockSpec((tm, d), seg_a), pl.BlockSpec((tm, d), seg_b)]
        out_shape = [jax.ShapeDtypeStruct((r, d), F32) for r in out_rows]
        scratch.append(pltpu.VMEM((tm, d), F32))
    row_sem = "parallel" if n_in == 1 and n_out == 1 else "arbitrary"
    return pl.pallas_call(
        functools.partial(_ffn_body, n_ff=n_ff, n_in=n_in, n_out=n_out, tiles_a=tiles_a,
                          final_norm=final_norm),
        grid=(n // tm, n_ff),
        in_specs=in_specs,
        out_specs=out_specs,
        out_shape=out_shape,
        scratch_shapes=scratch,
        compiler_params=_params((row_sem, "arbitrary")),
        name="ffn",
    )(*args)


def _rope_norm(r, gain, cos, sin):
    y = _rms(r, gain)
    return y * cos + pltpu.roll(y, HEAD_DIM // 2, axis=1) * sin


def _inproj_body(*refs, kinds, rope, sub_plan, dils):
    if rope:
        x_ref, gain_ref, w_ref, qg_ref, kg_ref, cos_ref, sin_ref, o_ref, qt_ref, vt_ref = refs
    else:
        x_ref, gain_ref, w_ref, o_ref = refs[:4]
        sub_refs, slab_scr = refs[4:4 + len(dils)], refs[-1] if dils else None
    tm = x_ref.shape[0]
    h = _rms(x_ref[...], gain_ref[...]).astype(BF16)
    n_qt = 0
    for c, kind in enumerate(kinds):
        lo = c * MXU_COLS
        r = jnp.dot(h, w_ref[:, lo:lo + MXU_COLS], preferred_element_type=F32)
        if kind == "scale":
            r = r * ATTN_SCALE
        elif kind in ("qnorm", "knorm"):
            hg = qg_ref[...] if kind == "qnorm" else kg_ref[...]
            cos = cos_ref[...]
            sin = sin_ref[...]
            parts = []
            for hh in range(MXU_COLS // HEAD_DIM):
                y = _rope_norm(r[:, hh * HEAD_DIM:(hh + 1) * HEAD_DIM], hg, cos, sin)
                if kind == "qnorm":
                    y = y * (ATTN_SCALE * LOG2_E)
                parts.append(y)
            r = jnp.concatenate(parts, axis=1)
            if kind == "qnorm":
                qt_ref[0, n_qt * MXU_COLS:(n_qt + 1) * MXU_COLS, :] = r.T.astype(BF16)
                n_qt += 1
        elif kind == "plain_t":
            vt_ref[0] = r.T.astype(BF16)
        o_ref[:, lo:lo + MXU_COLS] = r.astype(BF16)
        if c in sub_plan:
            heads = range(MXU_COLS // HEAD_DIM)
            for hh in heads:
                slab_scr[hh] = r[:, hh * HEAD_DIM:(hh + 1) * HEAD_DIM]
            for si, col0 in sub_plan[c]:
                d = dils[si]
                for rho in range(d):
                    for hh in heads:
                        part = slab_scr[hh, pl.ds(rho, tm // d, stride=d), :]
                        cols = slice(col0 + hh * HEAD_DIM, col0 + (hh + 1) * HEAD_DIM)
                        sub_refs[si][0, rho, :, cols] = part.astype(BF16)


def _inproj(x, gain, w, layer, kinds, seq_len, rope_args=None, sub_args=None, *, tm=512):
    n, d = x.shape
    c = w.shape[2]
    assert n % tm == 0 and seq_len % tm == 0 and c == MXU_COLS * len(kinds)
    rope = rope_args is not None
    assert not (rope and sub_args)
    dils, sub_width, sub_plan = sub_args if sub_args else ((), 0, {})
    scratch = [pltpu.VMEM((MXU_COLS // HEAD_DIM, tm, HEAD_DIM), F32)] if dils else []
    in_specs = [
        pl.BlockSpec((tm, d), lambda i: (i, 0)),
        pl.BlockSpec((1, d), lambda i: (0, 0)),
        _resident_weight((d, c), layer),
    ]
    args = [x, gain.reshape(1, d), w]
    if rope:
        q_gain, k_gain, cos, sin = rope_args
        tiles_per_seq = seq_len // tm
        in_specs += [
            pl.BlockSpec((1, HEAD_DIM), lambda i: (0, 0)),
            pl.BlockSpec((1, HEAD_DIM), lambda i: (0, 0)),
            pl.BlockSpec((tm, HEAD_DIM), lambda i: (i % tiles_per_seq, 0)),
            pl.BlockSpec((tm, HEAD_DIM), lambda i: (i % tiles_per_seq, 0)),
        ]
        args += [q_gain.reshape(1, HEAD_DIM), k_gain.reshape(1, HEAD_DIM), cos, sin]
    out_specs = pl.BlockSpec((tm, c), lambda i: (i, 0))
    out_shape = jax.ShapeDtypeStruct((n, c), BF16)
    if rope:
        assert kinds.count("plain_t") == 1
        qt_rows = MXU_COLS * kinds.count("qnorm")
        by_seq = lambda i: (i // tiles_per_seq, 0, i % tiles_per_seq)
        out_specs = [out_specs, pl.BlockSpec((1, qt_rows, tm), by_seq),
                     pl.BlockSpec((1, MXU_COLS, tm), by_seq)]
        out_shape = [out_shape, jax.ShapeDtypeStruct((n // seq_len, qt_rows, seq_len), BF16),
                     jax.ShapeDtypeStruct((n // seq_len, MXU_COLS, seq_len), BF16)]
    if dils:
        tiles_per_seq = seq_len // tm
        assert all(tm % (dl * 16) == 0 for dl in dils)
        out_specs = [out_specs] + [
            pl.BlockSpec((1, dl, tm // dl, sub_width), lambda i: (i // tiles_per_seq, 0, i % tiles_per_seq, 0))
            for dl in dils]
        out_shape = [out_shape] + [
            jax.ShapeDtypeStruct((n // seq_len, dl, seq_len // dl, sub_width), BF16) for dl in dils]
    return pl.pallas_call(
        functools.partial(_inproj_body, kinds=tuple(kinds), rope=rope, sub_plan=sub_plan, dils=tuple(dils)),
        grid=(n // tm,),
        in_specs=in_specs,
        out_specs=out_specs,
        out_shape=out_shape,
        scratch_shapes=scratch,
        compiler_params=_params(("parallel",)),
        name="inproj",
    )(*args)


def _outproj_ab_body(x_ref, a_ref, b_ref, w_ref, o_ref):
    ka = a_ref.shape[1]
    acc = jnp.dot(a_ref[...], w_ref[0:ka, :], preferred_element_type=F32)
    acc = acc + jnp.dot(b_ref[...], w_ref[ka:, :], preferred_element_type=F32)
    o_ref[...] = x_ref[...] + acc


def _outproj_ab(x, oa, ob, w, layer, *, tm=512):
    n, d = x.shape
    ka, kb = oa.shape[1], ob.shape[1]
    assert n % tm == 0 and w.shape[1:] == (ka + kb, d)
    return pl.pallas_call(
        _outproj_ab_body,
        grid=(n // tm,),
        in_specs=[
            pl.BlockSpec((tm, d), lambda i: (i, 0)),
            pl.BlockSpec((tm, ka), lambda i: (i, 0)),
            pl.BlockSpec((tm, kb), lambda i: (i, 0)),
            _resident_weight((ka + kb, d), layer),
        ],
        out_specs=pl.BlockSpec((tm, d), lambda i: (i, 0)),
        out_shape=jax.ShapeDtypeStruct((n, d), F32),
        compiler_params=_params(("parallel",)),
        name="outproj_ab",
    )(x, oa, ob, w)


def _outproj_cd_body(*refs, dils, kd):
    ng = len(dils)
    x_ref, c_ref = refs[:2]
    o_refs, l_refs = refs[2:2 + ng], refs[2 + ng:2 + 2 * ng]
    w_ref, o_ref, row_scr = refs[2 + 2 * ng:]
    tm = x_ref.shape[0]
    slabs = kd // HEAD_DIM

    def token_rows(ref, d, slot):
        if d == 1:
            return ref[...]
        for rho in range(d):
            for s in range(slabs):
                lo = rho * kd + s * HEAD_DIM
                row_scr[slot, s, pl.ds(rho, tm // d, stride=d), :] = ref[0, :, lo:lo + HEAD_DIM]
        return jnp.concatenate([row_scr[slot, s] for s in range(slabs)], axis=1)

    outs = [token_rows(r, d, 2 * i) for i, (r, d) in enumerate(zip(o_refs, dils))]
    lses = [token_rows(r, d, 2 * i + 1) for i, (r, d) in enumerate(zip(l_refs, dils))]
    m = functools.reduce(jnp.maximum, lses)
    es = [jnp.exp(l - m) for l in lses]
    den = functools.reduce(lambda a, b: a + b, es)
    od = functools.reduce(lambda a, b: a + b, [(e / den) * o for e, o in zip(es, outs)])
    kc = c_ref.shape[1]
    acc = jnp.dot(c_ref[...], w_ref[0:kc, :], preferred_element_type=F32)
    acc = acc + jnp.dot(od.astype(BF16), w_ref[kc:, :], preferred_element_type=F32)
    o_ref[...] = x_ref[...] + acc


def _outproj_cd(x, oc, outs, lses, dils, seq_len, w, layer, *, tm=512):
    n, d = x.shape
    kc = oc.shape[1]
    kd = w.shape[1] - kc
    assert n % tm == 0 and seq_len % tm == 0 and all(tm % (dl * SUBLANES) == 0 for dl in dils)
    tiles_per_seq = seq_len // tm
    row = lambda i: (i, 0)
    by_seq = lambda i: (i // tiles_per_seq, i % tiles_per_seq, 0)
    g_specs = [pl.BlockSpec((tm, kd), row) if dl == 1 else pl.BlockSpec((1, tm // dl, dl * kd), by_seq)
               for dl in dils]
    return pl.pallas_call(
        functools.partial(_outproj_cd_body, dils=tuple(dils), kd=kd),
        grid=(n // tm,),
        in_specs=[pl.BlockSpec((tm, d), row), pl.BlockSpec((tm, kc), row)] + g_specs + g_specs
        + [_resident_weight((kc + kd, d), layer)],
        out_specs=pl.BlockSpec((tm, d), row),
        out_shape=jax.ShapeDtypeStruct((n, d), F32),
        scratch_shapes=[pltpu.VMEM((2 * len(dils), kd // HEAD_DIM, tm, HEAD_DIM), F32)],
        compiler_params=_params(("parallel",)),
        name="outproj_cd",
    )(x, oc, *outs, *lses, w)


def _stack_heads(q, g):
    return jnp.concatenate([q[:, i * HEAD_DIM:(i + 1) * HEAD_DIM] for i in range(g)], axis=0)


def _softmax_pv(s, v):
    m = jnp.max(s, axis=-1, keepdims=True)
    p = jnp.exp(s - m)
    l = jnp.sum(p, axis=-1, keepdims=True)
    o = jnp.dot(p.astype(BF16), v, preferred_element_type=F32) / l
    return o, m, l


def _qk(q, k):
    return lax.dot_general(q, k, (((1,), (1,)), ((), ())), preferred_element_type=F32)


def _dense_body(qt_ref, k_ref, vt_ref, o_ref, s_scr, p_scr, m_scr, l_scr, *, g):
    step = pl.program_id(2)
    tq = qt_ref.shape[2] // 2
    t = k_ref.shape[1]
    gq = g * tq
    kc = min(DENSE_KEY_CHUNK, t)

    @pl.when(step == 0)
    def _():
        s_scr[1] = jnp.zeros(s_scr.shape[1:], s_scr.dtype)
        m_scr[1] = jnp.zeros(m_scr.shape[1:], m_scr.dtype)
        p_scr[...] = jnp.zeros(p_scr.shape, p_scr.dtype)
        l_scr[...] = jnp.ones(l_scr.shape, l_scr.dtype)

    def fold(x, op):
        return op(x.reshape(x.shape[0] // SUBLANES, SUBLANES, x.shape[1]), axis=0)

    for cur in range(2):
        prev = 1 - cur
        toks = slice(cur * tq, (cur + 1) * tq)
        qt = jnp.concatenate([qt_ref[0, i * HEAD_DIM:(i + 1) * HEAD_DIM, toks] for i in range(g)], axis=1)
        m = jnp.max(m_scr[prev], axis=0, keepdims=True)
        ot = jnp.zeros((HEAD_DIM, gq), F32)
        l_acc = jnp.zeros((SUBLANES, gq), F32)
        m_acc = None
        for c in range(t // kc):
            keys = slice(c * kc, (c + 1) * kc)
            p = jnp.exp2(s_scr[prev, keys, :] - m)
            l_acc = l_acc + fold(p, jnp.sum)
            p_scr[prev, keys, :] = p.astype(BF16)
        for c in range(t // kc):
            keys = slice(c * kc, (c + 1) * kc)
            ot = ot + jnp.dot(vt_ref[0, :, keys], p_scr[cur, keys, :], preferred_element_type=F32)
            s = jnp.dot(k_ref[0, keys, :], qt, preferred_element_type=F32)
            s_scr[cur, keys, :] = s
            m_chunk = fold(s, jnp.max)
            m_acc = m_chunk if m_acc is None else jnp.maximum(m_acc, m_chunk)
        ot = ot / jnp.sum(l_scr[cur], axis=0, keepdims=True)
        l_scr[prev] = l_acc
        m_scr[cur] = m_acc
        for i in range(g):
            o_ref[0, toks, i * HEAD_DIM:(i + 1) * HEAD_DIM] = ot[:, i * tq:(i + 1) * tq].T.astype(o_ref.dtype)


def _dense_attention(qt, qkv, vt, k_col, n_kv, g, *, tq=128):
    b, t, _ = qkv.shape
    gw = g * HEAD_DIM
    n_pairs = t // (2 * tq)
    assert t % (2 * tq) == 0 and k_col % HEAD_DIM == 0
    assert qt.shape == (b, n_kv * gw, t) and vt.shape == (b, n_kv * HEAD_DIM, t)
    kb = k_col // HEAD_DIM
    return pl.pallas_call(
        functools.partial(_dense_body, g=g),
        grid=(b, n_kv, n_pairs + 1),
        in_specs=[
            pl.BlockSpec((1, gw, 2 * tq), lambda n, h, k: (n, h, jnp.minimum(k, n_pairs - 1))),
            pl.BlockSpec((1, t, HEAD_DIM), lambda n, h, k: (n, 0, kb + h)),
            pl.BlockSpec((1, HEAD_DIM, t), lambda n, h, k: (n, h, 0)),
        ],
        out_specs=pl.BlockSpec((1, 2 * tq, gw), lambda n, h, k: (n, jnp.maximum(k - 1, 0), h)),
        out_shape=jax.ShapeDtypeStruct((b, t, n_kv * gw), BF16),
        scratch_shapes=[pltpu.VMEM((2, t, g * tq), F32), pltpu.VMEM((2, t, g * tq), BF16),
                        pltpu.VMEM((2, SUBLANES, g * tq), F32), pltpu.VMEM((2, SUBLANES, g * tq), F32)],
        compiler_params=_params(("parallel", "parallel", "arbitrary")),
        name="dense_attn",
    )(qt, qkv, vt)


def _band_body(*refs, n_kv, g, w, seq, tq, kw, n_cases, has_sink, want_lse, side_by_side):
    refs = list(refs)
    q_ref, k_ref, v_ref, bias_ref = refs[:4]
    refs = refs[4:]
    sink_ref = refs.pop(0) if has_sink else None
    o_ref = refs.pop(0)
    lse_ref = refs.pop(0) if want_lse else None
    gw = g * HEAD_DIM
    qw = n_kv * gw
    nq = seq // tq
    for sq, j in [(a, c) for a in range(q_ref.shape[0]) for c in range(q_ref.shape[1] // tq)]:
        qi = pl.program_id(1) * (q_ref.shape[1] // tq) + j
        kstart = pl.multiple_of(jnp.clip(qi * tq - w, 0, seq - kw), 64)
        case = 0 if n_cases == 1 else jnp.where(qi == 0, 0, jnp.where(qi == nq - 1, 2, 1))
        trows = slice(j * tq, (j + 1) * tq)
        for h in range(n_kv):
            k = k_ref[sq, pl.ds(kstart, kw), h * HEAD_DIM:(h + 1) * HEAD_DIM]
            v = v_ref[sq, pl.ds(kstart, kw), h * HEAD_DIM:(h + 1) * HEAD_DIM]
            q = _stack_heads(q_ref[sq, trows, h * gw:(h + 1) * gw], g)
            s = _qk(q, k) + bias_ref[case, h]
            o, m, l = _softmax_pv(s, v)
            lse = m + jnp.log(l)
            for i in range(g):
                rows = slice(i * tq, (i + 1) * tq)
                cols = slice(h * gw + i * HEAD_DIM, h * gw + (i + 1) * HEAD_DIM)
                oi = o[rows]
                if has_sink:
                    oi = oi * jax.nn.sigmoid(lse[rows] - sink_ref[h * g + i])
                if side_by_side:
                    osq, cols = 0, slice(sq * qw + cols.start, sq * qw + cols.stop)
                else:
                    osq = sq
                o_ref[osq, trows, cols] = oi.astype(o_ref.dtype)
                if want_lse:
                    lse_ref[osq, trows, cols] = jnp.broadcast_to(lse[rows], (tq, HEAD_DIM))


def _band_geometry(seq, w):
    if seq <= 4 * w:
        return seq, seq, (0,)
    tq = 2 * w if w < 128 else w
    tq = min(tq, 256)
    kw = tq + 2 * w
    assert seq % tq == 0 and seq >= kw
    return tq, kw, (0, -w, -2 * w)


def _band_attention(q_arr, q_col, k_arr, k_col, v_arr, v_col, bias, n_kv, g, w, geometry,
                    sink=None, want_lse=False, out_dtype=BF16, dil=1):
    n, seq, _ = q_arr.shape
    tq, kw, shifts = geometry
    nq = seq // tq
    gw = g * HEAD_DIM
    qw, kvw = n_kv * gw, n_kv * HEAD_DIM
    assert q_col % qw == 0 and k_col % kvw == 0 and v_col % kvw == 0
    qb, kb, vb = q_col // qw, k_col // kvw, v_col // kvw
    tps = next(c for c in BAND_TILES_PER_STEP if nq % c == 0)
    nq_steps = nq // tps
    sps = next(c for c in BAND_TILES_PER_STEP if n % c == 0) if nq == 1 else 1
    in_specs = [
        pl.BlockSpec((sps, tps * tq, qw), lambda b, i: (b, i, qb)),
        pl.BlockSpec((sps, seq, kvw), lambda b, i: (b, 0, kb)),
        pl.BlockSpec((sps, seq, kvw), lambda b, i: (b, 0, vb)),
        pl.BlockSpec((len(shifts), n_kv, g * tq, kw), lambda b, i: (0, 0, 0, 0)),
    ]
    args = [q_arr, k_arr, v_arr, bias]
    if sink is not None:
        in_specs.append(pl.BlockSpec(memory_space=pltpu.SMEM))
        args.append(sink.astype(F32))
    if dil == 1:
        o_spec = pl.BlockSpec((sps, tps * tq, qw), lambda b, i: (b, i, 0))
        o_dims = (n, seq, qw)
    else:
        assert n % dil == 0 and dil % sps == 0
        o_spec = pl.BlockSpec((1, tps * tq, sps * qw),
                              lambda b, i: ((b * sps) // dil, i, ((b * sps) % dil) // sps))
        o_dims = (n // dil, seq, dil * qw)
    o_shape = jax.ShapeDtypeStruct(o_dims, out_dtype)
    out_specs, out_shape = o_spec, o_shape
    if want_lse:
        out_specs = [o_spec, o_spec]
        out_shape = [o_shape, jax.ShapeDtypeStruct(o_dims, F32)]
    return pl.pallas_call(
        functools.partial(_band_body, n_kv=n_kv, g=g, w=w, seq=seq, tq=tq, kw=kw, n_cases=len(shifts),
                          has_sink=sink is not None, want_lse=want_lse, side_by_side=dil > 1),
        grid=(n // sps, nq_steps),
        in_specs=in_specs,
        out_specs=out_specs,
        out_shape=out_shape,
        compiler_params=_params(("parallel", "arbitrary")),
        name="band_attn",
    )(*args)


def _na_body(q_ref, k_ref, v_ref, bias_ref, o_ref, *, g, rows, rows_per_step):
    rb = pl.program_id(2)
    kwin = NA_ROWS * GRID_W

    def one_row(rr, carry):
        r = rb * rows_per_step + rr
        rs = jnp.clip(r - NA_ROWS // 2, 0, rows - NA_ROWS)
        off = rs - r + NA_ROWS - 1
        kstart = pl.multiple_of(rs * GRID_W, GRID_W)
        qstart = pl.multiple_of(rr * GRID_W, GRID_W)
        k = k_ref[0, pl.ds(kstart, kwin), :]
        v = v_ref[0, pl.ds(kstart, kwin), :]
        q = _stack_heads(q_ref[0, pl.ds(qstart, GRID_W), :], g)
        s = _qk(q, k) + bias_ref[0, off]
        o, _, _ = _softmax_pv(s, v)
        for i in range(g):
            o_ref[0, pl.ds(qstart, GRID_W), i * HEAD_DIM:(i + 1) * HEAD_DIM] = (
                o[i * GRID_W:(i + 1) * GRID_W].astype(o_ref.dtype))
        return carry

    lax.fori_loop(0, rows_per_step, one_row, 0, unroll=NA_ROW_UNROLL)


def _na_attention(qkv, q_col, k_col, v_col, bias, n_kv, g, *, rows_per_step=8):
    b, t, _ = qkv.shape
    rows = t // GRID_W
    gw = g * HEAD_DIM
    assert rows % rows_per_step == 0 and rows >= NA_ROWS
    qb, kb, vb = q_col // gw, k_col // HEAD_DIM, v_col // HEAD_DIM
    tq = rows_per_step * GRID_W
    return pl.pallas_call(
        functools.partial(_na_body, g=g, rows=rows, rows_per_step=rows_per_step),
        grid=(b, n_kv, rows // rows_per_step),
        in_specs=[
            pl.BlockSpec((1, tq, gw), lambda n, h, i: (n, i, qb + h)),
            pl.BlockSpec((1, t, HEAD_DIM), lambda n, h, i: (n, 0, kb + h)),
            pl.BlockSpec((1, t, HEAD_DIM), lambda n, h, i: (n, 0, vb + h)),
            pl.BlockSpec((1, NA_ROWS, g * GRID_W, NA_ROWS * GRID_W), lambda n, h, i: (h, 0, 0, 0)),
        ],
        out_specs=pl.BlockSpec((1, tq, gw), lambda n, h, i: (n, i, h)),
        out_shape=jax.ShapeDtypeStruct((b, t, n_kv * gw), BF16),
        compiler_params=_params(("parallel", "parallel", "arbitrary")),
        name="na_attn",
    )(qkv, qkv, qkv, bias)


def _t5_bucket(rel):
    half = T5_BUCKETS // 2
    max_exact = half // 2
    n = jnp.abs(rel)
    nf = jnp.maximum(n, 1).astype(jnp.float32)
    large = max_exact + (jnp.log(nf / max_exact) / math.log(T5_MAX_DIST / max_exact)
                         * (half - max_exact)).astype(jnp.int32)
    large = jnp.minimum(large, half - 1)
    return jnp.where(rel > 0, half, 0) + jnp.where(n < max_exact, n, large)


def _band_bias(table_cols, n_kv, g, w, dil, geometry):
    tq, kw, shifts = geometry
    iq = jnp.arange(tq)[:, None]
    jk = jnp.arange(kw)[None, :]
    tiles = []
    for shift in shifts:
        rel = jk + shift - iq
        bucket = _t5_bucket(rel * dil)
        tb = jnp.zeros((tq, kw, table_cols.shape[1]), F32)
        for r in range(T5_BUCKETS):
            tb = jnp.where((bucket == r)[..., None], table_cols[r].astype(F32), tb)
        tb = jnp.where((jnp.abs(rel) <= w)[..., None], tb, NEG_INF)
        tiles.append(jnp.moveaxis(tb, -1, 0).reshape(n_kv, g * tq, kw))
    return jnp.stack(tiles)


def _na_bias(rpb, n_kv, g):
    col = jnp.arange(GRID_W)
    qc, kc = col[:, None], col[None, :]
    dc = jnp.clip(kc - qc + NA_COLS - 1, 0, 2 * NA_COLS - 2)
    cs = jnp.clip(qc - NA_COLS // 2, 0, GRID_W - NA_COLS)
    mask = (kc >= cs) & (kc < cs + NA_COLS)
    rpb = rpb.astype(F32)
    t = jnp.zeros(rpb.shape[:2] + dc.shape, F32)
    for c in range(2 * NA_COLS - 1):
        t = jnp.where(dc == c, rpb[:, :, c][:, :, None, None], t)
    t = jnp.where(mask, t, NEG_INF)
    cls = jnp.stack([t[:, o:o + NA_ROWS] for o in range(NA_ROWS)])
    cls = cls.reshape(NA_ROWS, n_kv, g, NA_ROWS, GRID_W, GRID_W)
    cls = jnp.transpose(cls, (1, 0, 2, 4, 3, 5))
    return cls.reshape(n_kv, NA_ROWS, g * GRID_W, NA_ROWS * GRID_W)


def _rope_tables(t):
    n_pairs = HEAD_DIM // 2
    n_freq = n_pairs // 2
    pos = jnp.arange(t)
    row = (pos // GRID_W).astype(jnp.float32)
    col = (pos % GRID_W).astype(jnp.float32)
    omega = ROPE_THETA ** (-(jnp.arange(n_freq, dtype=jnp.float32) * 2.0 / n_pairs))
    ang = jnp.concatenate([row[:, None] * omega, col[:, None] * omega], axis=-1)
    cos, sin = jnp.cos(ang), jnp.sin(ang)
    return jnp.concatenate([cos, cos], axis=-1), jnp.concatenate([-sin, sin], axis=-1)


def _deinterleave_heads(w, n_heads):
    lead = w.shape[:-1]
    w = w.reshape(lead + (n_heads, HEAD_DIM // 2, 2))
    return jnp.swapaxes(w, -1, -2).reshape(lead + (n_heads * HEAD_DIM,))


AB_KINDS = ("scale",) * 4 + ("qnorm",) * 4 + ("plain", "plain", "knorm", "plain_t")
CD_KINDS = ("scale",) * 4 + ("plain",) * 2 + ("scale",) * 6 + ("plain",) * 4


def kernel(x_prompt, x_sample, norm_ffn1, ffn1_w_in, ffn1_w_out, norm_mix, ab_w_in, ab_sink, ab_q_gain, ab_k_gain, ab_w_out, cd_w_in, cd_rpb, cd_w_out, norm_ffn2, ffn2_w_in, ffn2_w_out, t5_table, final_norm):
    (bp, t, d), bs = x_prompt.shape, x_sample.shape[0]
    assert x_sample.shape[1:] == (t, d)
    b = bp + bs
    depth = norm_ffn1.shape[0]
    x = (x_prompt.reshape(bp * t, d), x_sample.reshape(bs * t, d))

    tf = math.gcd(FFN_FF_TILE, ffn1_w_out.shape[1])
    ffn1_w_in, ffn1_w_out = _pack_gate_up(ffn1_w_in, tf), ffn1_w_out.astype(BF16)
    ffn2_w_in, ffn2_w_out = _pack_gate_up(ffn2_w_in, tf), ffn2_w_out.astype(BF16)
    qa_w, kva_w = A_HEADS * HEAD_DIM, 2 * A_KV * HEAD_DIM
    qb_lo = qa_w + kva_w
    qb_hi = qb_lo + B_HEADS * HEAD_DIM
    kb_hi = qb_hi + B_KV * HEAD_DIM
    ab_w_in = jnp.concatenate([
        ab_w_in[..., :qa_w],
        _deinterleave_heads(ab_w_in[..., qb_lo:qb_hi], B_HEADS),
        ab_w_in[..., qa_w:qb_lo],
        _deinterleave_heads(ab_w_in[..., qb_hi:kb_hi], B_KV),
        ab_w_in[..., kb_hi:]], axis=-1).astype(BF16)
    ab_q_gain = _deinterleave_heads(ab_q_gain, 1)
    ab_k_gain = _deinterleave_heads(ab_k_gain, 1)
    ab_w_out, cd_w_in, cd_w_out = ab_w_out.astype(BF16), cd_w_in.astype(BF16), cd_w_out.astype(BF16)

    cos, sin = _rope_tables(t)
    ga, gc = A_HEADS // A_KV, C_HEADS // C_KV
    geo_a = _band_geometry(t, A_WINDOW)
    bias_a = _band_bias(t5_table[:, :A_HEADS], A_KV, ga, A_WINDOW, 1, geo_a)
    d_groups = []
    for gi, (win, dil) in enumerate(D_PAIRS):
        hs = win // (2 * dil)
        geo = _band_geometry(t // dil, hs)
        col0 = A_HEADS + gi * D_SLOTS
        d_groups.append((dil, hs, geo, _band_bias(t5_table[:, col0:col0 + D_SLOTS], D_SLOTS, 1, hs, dil, geo)))

    a_q = 0
    a_k = qa_w + B_HEADS * HEAD_DIM
    a_v = a_k + A_KV * HEAD_DIM
    b_k = a_v + A_KV * HEAD_DIM
    c_q, c_k, c_v = 0, C_HEADS * HEAD_DIM, (C_HEADS + C_KV) * HEAD_DIM
    d_q = (C_HEADS + 2 * C_KV) * HEAD_DIM
    d_k = d_q + D_HEADS * HEAD_DIM
    d_v = d_k + D_SLOTS * HEAD_DIM
    dw = D_SLOTS * HEAD_DIM
    sub_dils = tuple(dil for _, dil in D_PAIRS if dil > 1)
    sub_plan = {}
    for gi, (_, dil) in enumerate(D_PAIRS):
        if dil > 1:
            si = sub_dils.index(dil)
            for src, dst in ((d_q + gi * dw, 0), (d_k, dw), (d_v, 2 * dw)):
                for off in range(0, dw, MXU_COLS):
                    sub_plan.setdefault((src + off) // MXU_COLS, []).append((si, dst + off))

    for l in range(depth):
        i = l // 2
        x = _ffn(x, norm_ffn1[l], ffn1_w_in, ffn1_w_out, l)
        if l % 2 == 0:
            qkv, qbt, vbt = _inproj(x, norm_mix[l], ab_w_in, i, AB_KINDS, t,
                                    (ab_q_gain[i], ab_k_gain[i], cos, sin))
            qkv = qkv.reshape(b, t, -1)
            o_a = _band_attention(qkv, a_q, qkv, a_k, qkv, a_v, bias_a, A_KV, ga, A_WINDOW, geo_a,
                                  sink=ab_sink[i])
            o_b = _dense_attention(qbt, qkv, vbt, b_k, B_KV, B_HEADS // B_KV)
            x = _outproj_ab(x, o_a.reshape(b * t, -1), o_b.reshape(b * t, -1), ab_w_out, i)
        else:
            qkv, *subs = _inproj(x, norm_mix[l], cd_w_in, i, CD_KINDS, t,
                                 sub_args=(sub_dils, 3 * dw, sub_plan))
            qkv = qkv.reshape(b, t, -1)
            o_c = _na_attention(qkv, c_q, c_k, c_v, _na_bias(cd_rpb[i], C_KV, gc), C_KV, gc)
            outs, lses = [], []
            for gi, (dil, hs, geo, bias_d) in enumerate(d_groups):
                if dil == 1:
                    o_g, lse_g = _band_attention(qkv, d_q + gi * dw, qkv, d_k, qkv, d_v, bias_d,
                                                 D_SLOTS, 1, hs, geo, want_lse=True, out_dtype=F32)
                    o_g, lse_g = o_g.reshape(b * t, dw), lse_g.reshape(b * t, dw)
                else:
                    sub = subs[sub_dils.index(dil)].reshape(b * dil, t // dil, 3 * dw)
                    o_g, lse_g = _band_attention(sub, 0, sub, dw, sub, 2 * dw, bias_d, D_SLOTS, 1, hs, geo,
                                                 want_lse=True, out_dtype=F32, dil=dil)
                outs.append(o_g)
                lses.append(lse_g)
            x = _outproj_cd(x, o_c.reshape(b * t, -1), outs, lses, [g[0] for g in d_groups], t, cd_w_out, i)
        last = l == depth - 1
        x = _ffn(x, norm_ffn2[l], ffn2_w_in, ffn2_w_out, l, final_gain=final_norm if last else None,
                 out_rows=(bp * t, bs * t) if last else None)

    return (x[0].reshape(bp, t, d), x[1].reshape(bs, t, d))
```

```python
import functools
import math

import jax
import jax.numpy as jnp
import numpy as np
from jax import lax
from jax.experimental import pallas as pl
from jax.experimental.pallas import tpu as pltpu

HEAD_DIM = 128
GRID_W = 64
A_HEADS, A_KV, A_WINDOW = 8, 2, 128
B_HEADS, B_KV = 8, 2
ROPE_THETA = 10000.0
C_HEADS, C_KV = 8, 2
NA_ROWS, NA_COLS = 8, 16
D_PAIRS = ((128, 1), (512, 4), (2048, 16))
D_SLOTS = 4
D_HEADS = D_SLOTS * len(D_PAIRS)
T5_BUCKETS = 32
T5_MAX_DIST = 2048
NORM_EPS = 1e-6
NEG_INF = -1e30
ATTN_SCALE = HEAD_DIM ** -0.5
LOG2_E = math.log2(math.e)

LANES = 128
SUBLANES = 8
MXU_COLS = 256
FFN_ROW_TILE = 1024
FFN_FF_TILE = 2 * MXU_COLS
DENSE_KEY_CHUNK = 2 * MXU_COLS
BAND_TILES_PER_STEP = (4, 2, 1)
NA_ROW_UNROLL = 8
MIB = 1024 * 1024
VMEM_LIMIT_BYTES = 56 * MIB

BF16 = jnp.bfloat16
F32 = jnp.float32


def _params(semantics):
    return pltpu.CompilerParams(dimension_semantics=semantics, vmem_limit_bytes=VMEM_LIMIT_BYTES)


def _resident_weight(shape, layer):
    return pl.BlockSpec((None,) + shape, lambda *_: (layer,) + (0,) * len(shape),
                        pipeline_mode=pl.Buffered(1))


def _rms(x, gain):
    ms = jnp.mean(x * x, axis=-1, keepdims=True)
    return x * lax.rsqrt(ms + NORM_EPS) * gain


def _ffn_body(*refs, n_ff, n_in, n_out, tiles_a, final_norm):
    refs = list(refs)
    x_refs = [refs.pop(0) for _ in range(n_in)]
    gain_ref, wg_ref, wu_ref, wo_ref = [refs.pop(0) for _ in range(4)]
    fgain_ref = refs.pop(0) if final_norm else None
    o_refs = [refs.pop(0) for _ in range(n_out)]
    h_scr = refs.pop(0)
    acc_ref = refs.pop(0) if n_out == 2 else o_refs[0]
    i, j = pl.program_id(0), pl.program_id(1)

    def start(x_ref):
        x = x_ref[...]
        h_scr[...] = _rms(x, gain_ref[...]).astype(BF16)
        acc_ref[...] = x

    if n_in == 1:
        pl.when(j == 0)(lambda: start(x_refs[0]))
    else:
        pl.when((j == 0) & (i < tiles_a))(lambda: start(x_refs[0]))
        pl.when((j == 0) & (i >= tiles_a))(lambda: start(x_refs[1]))

    h = h_scr[...]
    g = jnp.dot(h, wg_ref[...], preferred_element_type=F32)
    u = jnp.dot(h, wu_ref[...], preferred_element_type=F32)
    a = (g * jax.nn.sigmoid(g) * (0.5 * u)).astype(BF16)
    acc_ref[...] += jnp.dot(a, wo_ref[...], preferred_element_type=F32)

    def finish(o_ref):
        y = acc_ref[...]
        o_ref[...] = _rms(y, fgain_ref[...]) if final_norm else y

    if n_out == 2:
        pl.when((j == n_ff - 1) & (i < tiles_a))(lambda: finish(o_refs[0]))
        pl.when((j == n_ff - 1) & (i >= tiles_a))(lambda: finish(o_refs[1]))
    elif final_norm:
        pl.when(j == n_ff - 1)(lambda: finish(o_refs[0]))


def _ffn(xs, gain, w_in, w_out, layer, final_gain=None, out_rows=None, *, tm=FFN_ROW_TILE):
    xs = tuple(xs) if isinstance(xs, (tuple, list)) else (xs,)
    d = xs[0].shape[1]
    n = sum(x.shape[0] for x in xs)
    f = w_out.shape[1]
    tf = math.gcd(FFN_FF_TILE, f)
    n_ff = f // tf
    if len(xs) == 2 or out_rows is not None:
        tm //= 2
    tm = math.gcd(tm, *(x.shape[0] for x in xs))
    assert w_in.shape[1:] == (d, 2 * f) and w_out.shape[2] == d
    n_in, n_out = len(xs), 1 if out_rows is None else 2
    rows_a = xs[0].shape[0] if n_in == 2 else (out_rows[0] if n_out == 2 else n)
    assert n_in == 1 or n_out == 1 or out_rows[0] == rows_a
    tiles_a = rows_a // tm
    final_norm = final_gain is not None
    seg_a = lambda i, j: (jnp.minimum(i, tiles_a - 1), 0)
    seg_b = lambda i, j: (jnp.maximum(i - tiles_a, 0), 0)
    whole = lambda i, j: (i, 0)
    in_specs = [pl.BlockSpec((tm, d), m) for m in ((whole,) if n_in == 1 else (seg_a, seg_b))]
    in_specs += [
        pl.BlockSpec((1, d), lambda i, j: (0, 0)),
        pl.BlockSpec((None, d, tf), lambda i, j: (layer, 0, j)),
        pl.BlockSpec((None, d, tf), lambda i, j: (layer, 0, j + n_ff)),
        pl.BlockSpec((None, tf, d), lambda i, j: (layer, j, 0)),
    ]
    args = list(xs) + [gain.reshape(1, d), w_in, w_in, w_out]
    if final_norm:
        in_specs.append(pl.BlockSpec((1, d), lambda i, j: (0, 0)))
        args.append(final_gain.reshape(1, d))
    scratch = [pltpu.VMEM((tm, d), BF16)]
    if n_out == 1:
        out_specs = pl.BlockSpec((tm, d), whole)
        out_shape = jax.ShapeDtypeStruct((n, d), F32)
    else:
        assert out_rows[0] % tm == 0 and sum(out_rows) == n
        out_specs = [pl.BlockSpec((tm, d), seg_a), pl.BlockSpec((tm, d), seg_b)]
        out_shape = [jax.ShapeDtypeStruct((r, d), F32) for r in out_rows]
        scratch.append(pltpu.VMEM((tm, d), F32))
    row_sem = "parallel" if n_in == 1 and n_out == 1 else "arbitrary"
    return pl.pallas_call(
        functools.partial(_ffn_body, n_ff=n_ff, n_in=n_in, n_out=n_out, tiles_a=tiles_a,
                          final_norm=final_norm),
        grid=(n // tm, n_ff),
        in_specs=in_specs,
        out_specs=out_specs,
        out_shape=out_shape,
        scratch_shapes=scratch,
        compiler_params=_params((row_sem, "arbitrary")),
        name="ffn",
    )(*args)


def _rope_norm(r, gain, cos, sin):
    y = _rms(r, gain)
    return y * cos + pltpu.roll(y, HEAD_DIM // 2, axis=1) * sin


def _inproj_body(*refs, kinds, rope, sub_plan, dils):
    if rope:
        x_ref, gain_ref, w_ref, qg_ref, kg_ref, cos_ref, sin_ref, o_ref, qt_ref, vt_ref = refs
    else:
        x_ref, gain_ref, w_ref, o_ref = refs[:4]
        sub_refs, slab_scr = refs[4:4 + len(dils)], refs[-1] if dils else None
    tm = x_ref.shape[0]
    h = _rms(x_ref[...], gain_ref[...]).astype(BF16)
    n_qt = 0
    for c, kind in enumerate(kinds):
        lo = c * MXU_COLS
        r = jnp.dot(h, w_ref[:, lo:lo + MXU_COLS], preferred_element_type=F32)
        if kind == "scale":
            r = r * ATTN_SCALE
        elif kind in ("qnorm", "knorm"):
            hg = qg_ref[...] if kind == "qnorm" else kg_ref[...]
            cos = cos_ref[...]
            sin = sin_ref[...]
            parts = []
            for hh in range(MXU_COLS // HEAD_DIM):
                y = _rope_norm(r[:, hh * HEAD_DIM:(hh + 1) * HEAD_DIM], hg, cos, sin)
                if kind == "qnorm":
                    y = y * (ATTN_SCALE * LOG2_E)
                parts.append(y)
            r = jnp.concatenate(parts, axis=1)
            if kind == "qnorm":
                qt_ref[0, n_qt * MXU_COLS:(n_qt + 1) * MXU_COLS, :] = r.T.astype(BF16)
                n_qt += 1
        elif kind == "plain_t":
            vt_ref[0] = r.T.astype(BF16)
        o_ref[:, lo:lo + MXU_COLS] = r.astype(BF16)
        if c in sub_plan:
            heads = range(MXU_COLS // HEAD_DIM)
            for hh in heads:
                slab_scr[hh] = r[:, hh * HEAD_DIM:(hh + 1) * HEAD_DIM]
            for si, col0 in sub_plan[c]:
                d = dils[si]
                for rho in range(d):
                    for hh in heads:
                        part = slab_scr[hh, pl.ds(rho, tm // d, stride=d), :]
                        cols = slice(col0 + hh * HEAD_DIM, col0 + (hh + 1) * HEAD_DIM)
                        sub_refs[si][0, rho, :, cols] = part.astype(BF16)


def _inproj(x, gain, w, layer, kinds, seq_len, rope_args=None, sub_args=None, *, tm=512):
    n, d = x.shape
    c = w.shape[2]
    assert n % tm == 0 and seq_len % tm == 0 and c == MXU_COLS * len(kinds)
    rope = rope_args is not None
    assert not (rope and sub_args)
    dils, sub_width, sub_plan = sub_args if sub_args else ((), 0, {})
    scratch = [pltpu.VMEM((MXU_COLS // HEAD_DIM, tm, HEAD_DIM), F32)] if dils else []
    in_specs = [
        pl.BlockSpec((tm, d), lambda i: (i, 0)),
        pl.BlockSpec((1, d), lambda i: (0, 0)),
        _resident_weight((d, c), layer),
    ]
    args = [x, gain.reshape(1, d), w]
    if rope:
        q_gain, k_gain, cos, sin = rope_args
        tiles_per_seq = seq_len // tm
        in_specs += [
            pl.BlockSpec((1, HEAD_DIM), lambda i: (0, 0)),
            pl.BlockSpec((1, HEAD_DIM), lambda i: (0, 0)),
            pl.BlockSpec((tm, HEAD_DIM), lambda i: (i % tiles_per_seq, 0)),
            pl.BlockSpec((tm, HEAD_DIM), lambda i: (i % tiles_per_seq, 0)),
        ]
        args += [q_gain.reshape(1, HEAD_DIM), k_gain.reshape(1, HEAD_DIM), cos, sin]
    out_specs = pl.BlockSpec((tm, c), lambda i: (i, 0))
    out_shape = jax.ShapeDtypeStruct((n, c), BF16)
    if rope:
        assert kinds.count("plain_t") == 1
        qt_rows = MXU_COLS * kinds.count("qnorm")
        by_seq = lambda i: (i // tiles_per_seq, 0, i % tiles_per_seq)
        out_specs = [out_specs, pl.BlockSpec((1, qt_rows, tm), by_seq),
                     pl.BlockSpec((1, MXU_COLS, tm), by_seq)]
        out_shape = [out_shape, jax.ShapeDtypeStruct((n // seq_len, qt_rows, seq_len), BF16),
                     jax.ShapeDtypeStruct((n // seq_len, MXU_COLS, seq_len), BF16)]
    if dils:
        tiles_per_seq = seq_len // tm
        assert all(tm % (dl * 16) == 0 for dl in dils)
        out_specs = [out_specs] + [
            pl.BlockSpec((1, dl, tm // dl, sub_width), lambda i: (i // tiles_per_seq, 0, i % tiles_per_seq, 0))
            for dl in dils]
        out_shape = [out_shape] + [
            jax.ShapeDtypeStruct((n // seq_len, dl, seq_len // dl, sub_width), BF16) for dl in dils]
    return pl.pallas_call(
        functools.partial(_inproj_body, kinds=tuple(kinds), rope=rope, sub_plan=sub_plan, dils=tuple(dils)),
        grid=(n // tm,),
        in_specs=in_specs,
        out_specs=out_specs,
        out_shape=out_shape,
        scratch_shapes=scratch,
        compiler_params=_params(("parallel",)),
        name="inproj",
    )(*args)


def _outproj_ab_body(x_ref, a_ref, b_ref, w_ref, o_ref):
    ka = a_ref.shape[1]
    acc = jnp.dot(a_ref[...], w_ref[0:ka, :], preferred_element_type=F32)
    acc = acc + jnp.dot(b_ref[...], w_ref[ka:, :], preferred_element_type=F32)
    o_ref[...] = x_ref[...] + acc


def _outproj_ab(x, oa, ob, w, layer, *, tm=512):
    n, d = x.shape
    ka, kb = oa.shape[1], ob.shape[1]
    assert n % tm == 0 and w.shape[1:] == (ka + kb, d)
    return pl.pallas_call(
        _outproj_ab_body,
        grid=(n // tm,),
        in_specs=[
            pl.BlockSpec((tm, d), lambda i: (i, 0)),
            pl.BlockSpec((tm, ka), lambda i: (i, 0)),
            pl.BlockSpec((tm, kb), lambda i: (i, 0)),
            _resident_weight((ka + kb, d), layer),
        ],
        out_specs=pl.BlockSpec((tm, d), lambda i: (i, 0)),
        out_shape=jax.ShapeDtypeStruct((n, d), F32),
        compiler_params=_params(("parallel",)),
        name="outproj_ab",
    )(x, oa, ob, w)


def _outproj_cd_body(*refs, dils, kd):
    ng = len(dils)
    x_ref, c_ref = refs[:2]
    o_refs, l_refs = refs[2:2 + ng], refs[2 + ng:2 + 2 * ng]
    w_ref, o_ref, row_scr = refs[2 + 2 * ng:]
    tm = x_ref.shape[0]
    slabs = kd // HEAD_DIM

    def token_rows(ref, d, slot):
        if d == 1:
            return ref[...]
        for rho in range(d):
            for s in range(slabs):
                lo = rho * kd + s * HEAD_DIM
                row_scr[slot, s, pl.ds(rho, tm // d, stride=d), :] = ref[0, :, lo:lo + HEAD_DIM]
        return jnp.concatenate([row_scr[slot, s] for s in range(slabs)], axis=1)

    outs = [token_rows(r, d, 2 * i) for i, (r, d) in enumerate(zip(o_refs, dils))]
    lses = [token_rows(r, d, 2 * i + 1) for i, (r, d) in enumerate(zip(l_refs, dils))]
    m = functools.reduce(jnp.maximum, lses)
    es = [jnp.exp(l - m) for l in lses]
    den = functools.reduce(lambda a, b: a + b, es)
    od = functools.reduce(lambda a, b: a + b, [(e / den) * o for e, o in zip(es, outs)])
    kc = c_ref.shape[1]
    acc = jnp.dot(c_ref[...], w_ref[0:kc, :], preferred_element_type=F32)
    acc = acc + jnp.dot(od.astype(BF16), w_ref[kc:, :], preferred_element_type=F32)
    o_ref[...] = x_ref[...] + acc


def _outproj_cd(x, oc, outs, lses, dils, seq_len, w, layer, *, tm=512):
    n, d = x.shape
    kc = oc.shape[1]
    kd = w.shape[1] - kc
    assert n % tm == 0 and seq_len % tm == 0 and all(tm % (dl * SUBLANES) == 0 for dl in dils)
    tiles_per_seq = seq_len // tm
    row = lambda i: (i, 0)
    by_seq = lambda i: (i // tiles_per_seq, i % tiles_per_seq, 0)
    g_specs = [pl.BlockSpec((tm, kd), row) if dl == 1 else pl.BlockSpec((1, tm // dl, dl * kd), by_seq)
               for dl in dils]
    return pl.pallas_call(
        functools.partial(_outproj_cd_body, dils=tuple(dils), kd=kd),
        grid=(n // tm,),
        in_specs=[pl.BlockSpec((tm, d), row), pl.BlockSpec((tm, kc), row)] + g_specs + g_specs
        + [_resident_weight((kc + kd, d), layer)],
        out_specs=pl.BlockSpec((tm, d), row),
        out_shape=jax.ShapeDtypeStruct((n, d), F32),
        scratch_shapes=[pltpu.VMEM((2 * len(dils), kd // HEAD_DIM, tm, HEAD_DIM), F32)],
        compiler_params=_params(("parallel",)),
        name="outproj_cd",
    )(x, oc, *outs, *lses, w)


def _stack_heads(q, g):
    return jnp.concatenate([q[:, i * HEAD_DIM:(i + 1) * HEAD_DIM] for i in range(g)], axis=0)


def _softmax_pv(s, v):
    m = jnp.max(s, axis=-1, keepdims=True)
    p = jnp.exp(s - m)
    l = jnp.sum(p, axis=-1, keepdims=True)
    o = jnp.dot(p.astype(BF16), v, preferred_element_type=F32) / l
    return o, m, l


def _qk(q, k):
    return lax.dot_general(q, k, (((1,), (1,)), ((), ())), preferred_element_type=F32)


def _dense_body(qt_ref, k_ref, vt_ref, o_ref, s_scr, p_scr, m_scr, l_scr, *, g):
    step = pl.program_id(2)
    tq = qt_ref.shape[2] // 2
    t = k_ref.shape[1]
    gq = g * tq
    kc = min(DENSE_KEY_CHUNK, t)

    @pl.when(step == 0)
    def _():
        s_scr[1] = jnp.zeros(s_scr.shape[1:], s_scr.dtype)
        m_scr[1] = jnp.zeros(m_scr.shape[1:], m_scr.dtype)
        p_scr[...] = jnp.zeros(p_scr.shape, p_scr.dtype)
        l_scr[...] = jnp.ones(l_scr.shape, l_scr.dtype)

    def fold(x, op):
        return op(x.reshape(x.shape[0] // SUBLANES, SUBLANES, x.shape[1]), axis=0)

    for cur in range(2):
        prev = 1 - cur
        toks = slice(cur * tq, (cur + 1) * tq)
        qt = jnp.concatenate([qt_ref[0, i * HEAD_DIM:(i + 1) * HEAD_DIM, toks] for i in range(g)], axis=1)
        m = jnp.max(m_scr[prev], axis=0, keepdims=True)
        ot = jnp.zeros((HEAD_DIM, gq), F32)
        l_acc = jnp.zeros((SUBLANES, gq), F32)
        m_acc = None
        for c in range(t // kc):
            keys = slice(c * kc, (c + 1) * kc)
            p = jnp.exp2(s_scr[prev, keys, :] - m)
            l_acc = l_acc + fold(p, jnp.sum)
            p_scr[prev, keys, :] = p.astype(BF16)
        for c in range(t // kc):
            keys = slice(c * kc, (c + 1) * kc)
            ot = ot + jnp.dot(vt_ref[0, :, keys], p_scr[cur, keys, :], preferred_element_type=F32)
            s = jnp.dot(k_ref[0, keys, :], qt, preferred_element_type=F32)
            s_scr[cur, keys, :] = s
            m_chunk = fold(s, jnp.max)
            m_acc = m_chunk if m_acc is None else jnp.maximum(m_acc, m_chunk)
        ot = ot / jnp.sum(l_scr[cur], axis=0, keepdims=True)
        l_scr[prev] = l_acc
        m_scr[cur] = m_acc
        for i in range(g):
            o_ref[0, toks, i * HEAD_DIM:(i + 1) * HEAD_DIM] = ot[:, i * tq:(i + 1) * tq].T.astype(o_ref.dtype)


def _dense_attention(qt, qkv, vt, k_col, n_kv, g, *, tq=128):
    b, t, _ = qkv.shape
    gw = g * HEAD_DIM
    n_pairs = t // (2 * tq)
    assert t % (2 * tq) == 0 and k_col % HEAD_DIM == 0
    assert qt.shape == (b, n_kv * gw, t) and vt.shape == (b, n_kv * HEAD_DIM, t)
    kb = k_col // HEAD_DIM
    return pl.pallas_call(
        functools.partial(_dense_body, g=g),
        grid=(b, n_kv, n_pairs + 1),
        in_specs=[
            pl.BlockSpec((1, gw, 2 * tq), lambda n, h, k: (n, h, jnp.minimum(k, n_pairs - 1))),
            pl.BlockSpec((1, t, HEAD_DIM), lambda n, h, k: (n, 0, kb + h)),
            pl.BlockSpec((1, HEAD_DIM, t), lambda n, h, k: (n, h, 0)),
        ],
        out_specs=pl.BlockSpec((1, 2 * tq, gw), lambda n, h, k: (n, jnp.maximum(k - 1, 0), h)),
        out_shape=jax.ShapeDtypeStruct((b, t, n_kv * gw), BF16),
        scratch_shapes=[pltpu.VMEM((2, t, g * tq), F32), pltpu.VMEM((2, t, g * tq), BF16),
                        pltpu.VMEM((2, SUBLANES, g * tq), F32), pltpu.VMEM((2, SUBLANES, g * tq), F32)],
        compiler_params=_params(("parallel", "parallel", "arbitrary")),
        name="dense_attn",
    )(qt, qkv, vt)


def _band_body(*refs, n_kv, g, w, seq, tq, kw, n_cases, has_sink, want_lse, side_by_side):
    refs = list(refs)
    q_ref, k_ref, v_ref, bias_ref = refs[:4]
    refs = refs[4:]
    sink_ref = refs.pop(0) if has_sink else None
    o_ref = refs.pop(0)
    lse_ref = refs.pop(0) if want_lse else None
    gw = g * HEAD_DIM
    qw = n_kv * gw
    nq = seq // tq
    for sq, j in [(a, c) for a in range(q_ref.shape[0]) for c in range(q_ref.shape[1] // tq)]:
        qi = pl.program_id(1) * (q_ref.shape[1] // tq) + j
        kstart = pl.multiple_of(jnp.clip(qi * tq - w, 0, seq - kw), 64)
        case = 0 if n_cases == 1 else jnp.where(qi == 0, 0, jnp.where(qi == nq - 1, 2, 1))
        trows = slice(j * tq, (j + 1) * tq)
        for h in range(n_kv):
            k = k_ref[sq, pl.ds(kstart, kw), h * HEAD_DIM:(h + 1) * HEAD_DIM]
            v = v_ref[sq, pl.ds(kstart, kw), h * HEAD_DIM:(h + 1) * HEAD_DIM]
            q = _stack_heads(q_ref[sq, trows, h * gw:(h + 1) * gw], g)
            s = _qk(q, k) + bias_ref[case, h]
            o, m, l = _softmax_pv(s, v)
            lse = m + jnp.log(l)
            for i in range(g):
                rows = slice(i * tq, (i + 1) * tq)
                cols = slice(h * gw + i * HEAD_DIM, h * gw + (i + 1) * HEAD_DIM)
                oi = o[rows]
                if has_sink:
                    oi = oi * jax.nn.sigmoid(lse[rows] - sink_ref[h * g + i])
                if side_by_side:
                    osq, cols = 0, slice(sq * qw + cols.start, sq * qw + cols.stop)
                else:
                    osq = sq
                o_ref[osq, trows, cols] = oi.astype(o_ref.dtype)
                if want_lse:
                    lse_ref[osq, trows, cols] = jnp.broadcast_to(lse[rows], (tq, HEAD_DIM))


def _band_geometry(seq, w):
    if seq <= 4 * w:
        return seq, seq, (0,)
    tq = 2 * w if w < 128 else w
    tq = min(tq, 256)
    kw = tq + 2 * w
    assert seq % tq == 0 and seq >= kw
    return tq, kw, (0, -w, -2 * w)


def _band_attention(q_arr, q_col, k_arr, k_col, v_arr, v_col, bias, n_kv, g, w, geometry,
                    sink=None, want_lse=False, out_dtype=BF16, dil=1):
    n, seq, _ = q_arr.shape
    tq, kw, shifts = geometry
    nq = seq // tq
    gw = g * HEAD_DIM
    qw, kvw = n_kv * gw, n_kv * HEAD_DIM
    assert q_col % qw == 0 and k_col % kvw == 0 and v_col % kvw == 0
    qb, kb, vb = q_col // qw, k_col // kvw, v_col // kvw
    tps = next(c for c in BAND_TILES_PER_STEP if nq % c == 0)
    nq_steps = nq // tps
    sps = next(c for c in BAND_TILES_PER_STEP if n % c == 0) if nq == 1 else 1
    in_specs = [
        pl.BlockSpec((sps, tps * tq, qw), lambda b, i: (b, i, qb)),
        pl.BlockSpec((sps, seq, kvw), lambda b, i: (b, 0, kb)),
        pl.BlockSpec((sps, seq, kvw), lambda b, i: (b, 0, vb)),
        pl.BlockSpec((len(shifts), n_kv, g * tq, kw), lambda b, i: (0, 0, 0, 0)),
    ]
    args = [q_arr, k_arr, v_arr, bias]
    if sink is not None:
        in_specs.append(pl.BlockSpec(memory_space=pltpu.SMEM))
        args.append(sink.astype(F32))
    if dil == 1:
        o_spec = pl.BlockSpec((sps, tps * tq, qw), lambda b, i: (b, i, 0))
        o_dims = (n, seq, qw)
    else:
        assert n % dil == 0 and dil % sps == 0
        o_spec = pl.BlockSpec((1, tps * tq, sps * qw),
                              lambda b, i: ((b * sps) // dil, i, ((b * sps) % dil) // sps))
        o_dims = (n // dil, seq, dil * qw)
    o_shape = jax.ShapeDtypeStruct(o_dims, out_dtype)
    out_specs, out_shape = o_spec, o_shape
    if want_lse:
        out_specs = [o_spec, o_spec]
        out_shape = [o_shape, jax.ShapeDtypeStruct(o_dims, F32)]
    return pl.pallas_call(
        functools.partial(_band_body, n_kv=n_kv, g=g, w=w, seq=seq, tq=tq, kw=kw, n_cases=len(shifts),
                          has_sink=sink is not None, want_lse=want_lse, side_by_side=dil > 1),
        grid=(n // sps, nq_steps),
        in_specs=in_specs,
        out_specs=out_specs,
        out_shape=out_shape,
        compiler_params=_params(("parallel", "arbitrary")),
        name="band_attn",
    )(*args)


def _na_body(q_ref, k_ref, v_ref, bias_ref, o_ref, *, g, rows, rows_per_step):
    rb = pl.program_id(2)
    kwin = NA_ROWS * GRID_W

    def one_row(rr, carry):
        r = rb * rows_per_step + rr
        rs = jnp.clip(r - NA_ROWS // 2, 0, rows - NA_ROWS)
        off = rs - r + NA_ROWS - 1
        kstart = pl.multiple_of(rs * GRID_W, GRID_W)
        qstart = pl.multiple_of(rr * GRID_W, GRID_W)
        k = k_ref[0, pl.ds(kstart, kwin), :]
        v = v_ref[0, pl.ds(kstart, kwin), :]
        q = _stack_heads(q_ref[0, pl.ds(qstart, GRID_W), :], g)
        s = _qk(q, k) + bias_ref[0, off]
        o, _, _ = _softmax_pv(s, v)
        for i in range(g):
            o_ref[0, pl.ds(qstart, GRID_W), i * HEAD_DIM:(i + 1) * HEAD_DIM] = (
                o[i * GRID_W:(i + 1) * GRID_W].astype(o_ref.dtype))
        return carry

    lax.fori_loop(0, rows_per_step, one_row, 0, unroll=NA_ROW_UNROLL)


def _na_attention(qkv, q_col, k_col, v_col, bias, n_kv, g, *, rows_per_step=8):
    b, t, _ = qkv.shape
    rows = t // GRID_W
    gw = g * HEAD_DIM
    assert rows % rows_per_step == 0 and rows >= NA_ROWS
    qb, kb, vb = q_col // gw, k_col // HEAD_DIM, v_col // HEAD_DIM
    tq = rows_per_step * GRID_W
    return pl.pallas_call(
        functools.partial(_na_body, g=g, rows=rows, rows_per_step=rows_per_step),
        grid=(b, n_kv, rows // rows_per_step),
        in_specs=[
            pl.BlockSpec((1, tq, gw), lambda n, h, i: (n, i, qb + h)),
            pl.BlockSpec((1, t, HEAD_DIM), lambda n, h, i: (n, 0, kb + h)),
            pl.BlockSpec((1, t, HEAD_DIM), lambda n, h, i: (n, 0, vb + h)),
            pl.BlockSpec((1, NA_ROWS, g * GRID_W, NA_ROWS * GRID_W), lambda n, h, i: (h, 0, 0, 0)),
        ],
        out_specs=pl.BlockSpec((1, tq, gw), lambda n, h, i: (n, i, h)),
        out_shape=jax.ShapeDtypeStruct((b, t, n_kv * gw), BF16),
        compiler_params=_params(("parallel", "parallel", "arbitrary")),
        name="na_attn",
    )(qkv, qkv, qkv, bias)


def _t5_bucket(rel):
    half = T5_BUCKETS // 2
    max_exact = half // 2
    n = jnp.abs(rel)
    nf = jnp.maximum(n, 1).astype(jnp.float32)
    large = max_exact + (jnp.log(nf / max_exact) / math.log(T5_MAX_DIST / max_exact)
                         * (half - max_exact)).astype(jnp.int32)
    large = jnp.minimum(large, half - 1)
    return jnp.where(rel > 0, half, 0) + jnp.where(n < max_exact, n, large)


def _band_bias(table_cols, n_kv, g, w, dil, geometry):
    tq, kw, shifts = geometry
    iq = jnp.arange(tq)[:, None]
    jk = jnp.arange(kw)[None, :]
    tiles = []
    for shift in shifts:
        rel = jk + shift - iq
        bucket = _t5_bucket(rel * dil)
        tb = jnp.zeros((tq, kw, table_cols.shape[1]), F32)
        for r in range(T5_BUCKETS):
            tb = jnp.where((bucket == r)[..., None], table_cols[r].astype(F32), tb)
        tb = jnp.where((jnp.abs(rel) <= w)[..., None], tb, NEG_INF)
        tiles.append(jnp.moveaxis(tb, -1, 0).reshape(n_kv, g * tq, kw))
    return jnp.stack(tiles)


def _na_bias(rpb, n_kv, g):
    col = jnp.arange(GRID_W)
    qc, kc = col[:, None], col[None, :]
    dc = jnp.clip(kc - qc + NA_COLS - 1, 0, 2 * NA_COLS - 2)
    cs = jnp.clip(qc - NA_COLS // 2, 0, GRID_W - NA_COLS)
    mask = (kc >= cs) & (kc < cs + NA_COLS)
    rpb = rpb.astype(F32)
    t = jnp.zeros(rpb.shape[:2] + dc.shape, F32)
    for c in range(2 * NA_COLS - 1):
        t = jnp.where(dc == c, rpb[:, :, c][:, :, None, None], t)
    t = jnp.where(mask, t, NEG_INF)
    cls = jnp.stack([t[:, o:o + NA_ROWS] for o in range(NA_ROWS)])
    cls = cls.reshape(NA_ROWS, n_kv, g, NA_ROWS, GRID_W, GRID_W)
    cls = jnp.transpose(cls, (1, 0, 2, 4, 3, 5))
    return cls.reshape(n_kv, NA_ROWS, g * GRID_W, NA_ROWS * GRID_W)


def _rope_tables(t):
    n_pairs = HEAD_DIM // 2
    n_freq = n_pairs // 2
    pos = jnp.arange(t)
    row = (pos // GRID_W).astype(jnp.float32)
    col = (pos % GRID_W).astype(jnp.float32)
    omega = ROPE_THETA ** (-(jnp.arange(n_freq, dtype=jnp.float32) * 2.0 / n_pairs))
    ang = jnp.concatenate([row[:, None] * omega, col[:, None] * omega], axis=-1)
    cos, sin = jnp.cos(ang), jnp.sin(ang)
    return jnp.concatenate([cos, cos], axis=-1), jnp.concatenate([-sin, sin], axis=-1)


def _deinterleave_heads(w, n_heads):
    lead = w.shape[:-1]
    w = w.reshape(lead + (n_heads, HEAD_DIM // 2, 2))
    return jnp.swapaxes(w, -1, -2).reshape(lead + (n_heads * HEAD_DIM,))


AB_KINDS = ("scale",) * 4 + ("qnorm",) * 4 + ("plain", "plain", "knorm", "plain_t")
CD_KINDS = ("scale",) * 4 + ("plain",) * 2 + ("scale",) * 6 + ("plain",) * 4


def kernel(x_prompt, x_sample, norm_ffn1, ffn1_w_in, ffn1_w_out, norm_mix, ab_w_in, ab_sink, ab_q_gain, ab_k_gain, ab_w_out, cd_w_in, cd_rpb, cd_w_out, norm_ffn2, ffn2_w_in, ffn2_w_out, t5_table, final_norm):
    (bp, t, d), bs = x_prompt.shape, x_sample.shape[0]
    assert x_sample.shape[1:] == (t, d)
    b = bp + bs
    depth = norm_ffn1.shape[0]
    x = (x_prompt.reshape(bp * t, d), x_sample.reshape(bs * t, d))

    ffn1_w_in, ffn1_w_out = ffn1_w_in.astype(BF16), ffn1_w_out.astype(BF16)
    ffn2_w_in, ffn2_w_out = ffn2_w_in.astype(BF16), ffn2_w_out.astype(BF16)
    qa_w, kva_w = A_HEADS * HEAD_DIM, 2 * A_KV * HEAD_DIM
    qb_lo = qa_w + kva_w
    qb_hi = qb_lo + B_HEADS * HEAD_DIM
    kb_hi = qb_hi + B_KV * HEAD_DIM
    ab_w_in = jnp.concatenate([
        ab_w_in[..., :qa_w],
        _deinterleave_heads(ab_w_in[..., qb_lo:qb_hi], B_HEADS),
        ab_w_in[..., qa_w:qb_lo],
        _deinterleave_heads(ab_w_in[..., qb_hi:kb_hi], B_KV),
        ab_w_in[..., kb_hi:]], axis=-1).astype(BF16)
    ab_q_gain = _deinterleave_heads(ab_q_gain, 1)
    ab_k_gain = _deinterleave_heads(ab_k_gain, 1)
    ab_w_out, cd_w_in, cd_w_out = ab_w_out.astype(BF16), cd_w_in.astype(BF16), cd_w_out.astype(BF16)

    cos, sin = _rope_tables(t)
    ga, gc = A_HEADS // A_KV, C_HEADS // C_KV
    geo_a = _band_geometry(t, A_WINDOW)
    bias_a = _band_bias(t5_table[:, :A_HEADS], A_KV, ga, A_WINDOW, 1, geo_a)
    d_groups = []
    for gi, (win, dil) in enumerate(D_PAIRS):
        hs = win // (2 * dil)
        geo = _band_geometry(t // dil, hs)
        col0 = A_HEADS + gi * D_SLOTS
        d_groups.append((dil, hs, geo, _band_bias(t5_table[:, col0:col0 + D_SLOTS], D_SLOTS, 1, hs, dil, geo)))

    a_q = 0
    a_k = qa_w + B_HEADS * HEAD_DIM
    a_v = a_k + A_KV * HEAD_DIM
    b_k = a_v + A_KV * HEAD_DIM
    c_q, c_k, c_v = 0, C_HEADS * HEAD_DIM, (C_HEADS + C_KV) * HEAD_DIM
    d_q = (C_HEADS + 2 * C_KV) * HEAD_DIM
    d_k = d_q + D_HEADS * HEAD_DIM
    d_v = d_k + D_SLOTS * HEAD_DIM
    dw = D_SLOTS * HEAD_DIM
    sub_dils = tuple(dil for _, dil in D_PAIRS if dil > 1)
    sub_plan = {}
    for gi, (_, dil) in enumerate(D_PAIRS):
        if dil > 1:
            si = sub_dils.index(dil)
            for src, dst in ((d_q + gi * dw, 0), (d_k, dw), (d_v, 2 * dw)):
                for off in range(0, dw, MXU_COLS):
                    sub_plan.setdefault((src + off) // MXU_COLS, []).append((si, dst + off))

    for l in range(depth):
        i = l // 2
        x = _ffn(x, norm_ffn1[l], ffn1_w_in, ffn1_w_out, l)
        if l % 2 == 0:
            qkv, qbt, vbt = _inproj(x, norm_mix[l], ab_w_in, i, AB_KINDS, t,
                                    (ab_q_gain[i], ab_k_gain[i], cos, sin))
            qkv = qkv.reshape(b, t, -1)
            o_a = _band_attention(qkv, a_q, qkv, a_k, qkv, a_v, bias_a, A_KV, ga, A_WINDOW, geo_a,
                                  sink=ab_sink[i])
            o_b = _dense_attention(qbt, qkv, vbt, b_k, B_KV, B_HEADS // B_KV)
            x = _outproj_ab(x, o_a.reshape(b * t, -1), o_b.reshape(b * t, -1), ab_w_out, i)
        else:
            qkv, *subs = _inproj(x, norm_mix[l], cd_w_in, i, CD_KINDS, t,
                                 sub_args=(sub_dils, 3 * dw, sub_plan))
            qkv = qkv.reshape(b, t, -1)
            o_c = _na_attention(qkv, c_q, c_k, c_v, _na_bias(cd_rpb[i], C_KV, gc), C_KV, gc)
            outs, lses = [], []
            for gi, (dil, hs, geo, bias_d) in enumerate(d_groups):
                if dil == 1:
                    o_g, lse_g = _band_attention(qkv, d_q + gi * dw, qkv, d_k, qkv, d_v, bias_d,
                                                 D_SLOTS, 1, hs, geo, want_lse=True, out_dtype=F32)
                    o_g, lse_g = o_g.reshape(b * t, dw), lse_g.reshape(b * t, dw)
                else:
                    sub = subs[sub_dils.index(dil)].reshape(b * dil, t // dil, 3 * dw)
                    o_g, lse_g = _band_attention(sub, 0, sub, dw, sub, 2 * dw, bias_d, D_SLOTS, 1, hs, geo,
                                                 want_lse=True, out_dtype=F32, dil=dil)
                outs.append(o_g)
                lses.append(lse_g)
            x = _outproj_cd(x, o_c.reshape(b * t, -1), outs, lses, [g[0] for g in d_groups], t, cd_w_out, i)
        last = l == depth - 1
        x = _ffn(x, norm_ffn2[l], ffn2_w_in, ffn2_w_out, l, final_gain=final_norm if last else None,
                 out_rows=(bp * t, bs * t) if last else None)

    return (x[0].reshape(bp, t, d), x[1].reshape(bs, t, d))
```

```python
import functools
import math

import jax
import jax.numpy as jnp
import numpy as np
from jax import lax
from jax.experimental import pallas as pl
from jax.experimental.pallas import tpu as pltpu

HEAD_DIM = 128
GRID_W = 64
A_HEADS, A_KV, A_WINDOW = 8, 2, 128
B_HEADS, B_KV = 8, 2
ROPE_THETA = 10000.0
C_HEADS, C_KV = 8, 2
NA_ROWS, NA_COLS = 8, 16
D_PAIRS = ((128, 1), (512, 4), (2048, 16))
D_SLOTS = 4
D_HEADS = D_SLOTS * len(D_PAIRS)
T5_BUCKETS = 32
T5_MAX_DIST = 2048
NORM_EPS = 1e-6
NEG_INF = -1e30
ATTN_SCALE = HEAD_DIM ** -0.5
LOG2_E = math.log2(math.e)

LANES = 128
SUBLANES = 8
MXU_COLS = 256
FFN_ROW_TILE = 1024
FFN_FF_TILE = 2 * MXU_COLS
DENSE_KEY_CHUNK = 2 * MXU_COLS
BAND_TILES_PER_STEP = (4, 2, 1)
NA_ROW_UNROLL = 8
MIB = 1024 * 1024
VMEM_LIMIT_BYTES = 56 * MIB

BF16 = jnp.bfloat16
F32 = jnp.float32


def _params(semantics):
    return pltpu.CompilerParams(dimension_semantics=semantics, vmem_limit_bytes=VMEM_LIMIT_BYTES)


def _resident_weight(shape, layer):
    return pl.BlockSpec((None,) + shape, lambda *_: (layer,) + (0,) * len(shape),
                        pipeline_mode=pl.Buffered(1))


def _rms(x, gain):
    ms = jnp.mean(x * x, axis=-1, keepdims=True)
    return x * lax.rsqrt(ms + NORM_EPS) * gain


def _ffn_body(*refs, n_ff, n_in, n_out, tiles_a, final_norm):
    refs = list(refs)
    x_refs = [refs.pop(0) for _ in range(n_in)]
    gain_ref, wg_ref, wu_ref, wo_ref = [refs.pop(0) for _ in range(4)]
    fgain_ref = refs.pop(0) if final_norm else None
    o_refs = [refs.pop(0) for _ in range(n_out)]
    h_scr = refs.pop(0)
    acc_ref = refs.pop(0) if n_out == 2 else o_refs[0]
    i, j = pl.program_id(0), pl.program_id(1)

    def start(x_ref):
        x = x_ref[...]
        h_scr[...] = _rms(x, gain_ref[...]).astype(BF16)
        acc_ref[...] = x

    if n_in == 1:
        pl.when(j == 0)(lambda: start(x_refs[0]))
    else:
        pl.when((j == 0) & (i < tiles_a))(lambda: start(x_refs[0]))
        pl.when((j == 0) & (i >= tiles_a))(lambda: start(x_refs[1]))

    h = h_scr[...]
    g = jnp.dot(h, wg_ref[...], preferred_element_type=F32)
    u = jnp.dot(h, wu_ref[...], preferred_element_type=F32)
    a = (g * jax.nn.sigmoid(g) * (0.5 * u)).astype(BF16)
    acc_ref[...] += jnp.dot(a, wo_ref[...], preferred_element_type=F32)

    def finish(o_ref):
        y = acc_ref[...]
        o_ref[...] = _rms(y, fgain_ref[...]) if final_norm else y

    if n_out == 2:
        pl.when((j == n_ff - 1) & (i < tiles_a))(lambda: finish(o_refs[0]))
        pl.when((j == n_ff - 1) & (i >= tiles_a))(lambda: finish(o_refs[1]))
    elif final_norm:
        pl.when(j == n_ff - 1)(lambda: finish(o_refs[0]))


def _ffn(xs, gain, w_in, w_out, layer, final_gain=None, out_rows=None, *, tm=FFN_ROW_TILE):
    xs = tuple(xs) if isinstance(xs, (tuple, list)) else (xs,)
    d = xs[0].shape[1]
    n = sum(x.shape[0] for x in xs)
    f = w_out.shape[1]
    tf = math.gcd(FFN_FF_TILE, f)
    n_ff = f // tf
    if len(xs) == 2 or out_rows is not None:
        tm //= 2
    tm = math.gcd(tm, *(x.shape[0] for x in xs))
    assert w_in.shape[1:] == (d, 2 * f) and w_out.shape[2] == d
    n_in, n_out = len(xs), 1 if out_rows is None else 2
    rows_a = xs[0].shape[0] if n_in == 2 else (out_rows[0] if n_out == 2 else n)
    assert n_in == 1 or n_out == 1 or out_rows[0] == rows_a
    tiles_a = rows_a // tm
    final_norm = final_gain is not None
    seg_a = lambda i, j: (jnp.minimum(i, tiles_a - 1), 0)
    seg_b = lambda i, j: (jnp.maximum(i - tiles_a, 0), 0)
    whole = lambda i, j: (i, 0)
    in_specs = [pl.BlockSpec((tm, d), m) for m in ((whole,) if n_in == 1 else (seg_a, seg_b))]
    in_specs += [
        pl.BlockSpec((1, d), lambda i, j: (0, 0)),
        pl.BlockSpec((None, d, tf), lambda i, j: (layer, 0, j)),
        pl.BlockSpec((None, d, tf), lambda i, j: (layer, 0, j + n_ff)),
        pl.BlockSpec((None, tf, d), lambda i, j: (layer, j, 0)),
    ]
    args = list(xs) + [gain.reshape(1, d), w_in, w_in, w_out]
    if final_norm:
        in_specs.append(pl.BlockSpec((1, d), lambda i, j: (0, 0)))
        args.append(final_gain.reshape(1, d))
    scratch = [pltpu.VMEM((tm, d), BF16)]
    if n_out == 1:
        out_specs = pl.BlockSpec((tm, d), whole)
        out_shape = jax.ShapeDtypeStruct((n, d), F32)
    else:
        assert out_rows[0] % tm == 0 and sum(out_rows) == n
        out_specs = [pl.BlockSpec((tm, d), seg_a), pl.BlockSpec((tm, d), seg_b)]
        out_shape = [jax.ShapeDtypeStruct((r, d), F32) for r in out_rows]
        scratch.append(pltpu.VMEM((tm, d), F32))
    row_sem = "parallel" if n_in == 1 and n_out == 1 else "arbitrary"
    return pl.pallas_call(
        functools.partial(_ffn_body, n_ff=n_ff, n_in=n_in, n_out=n_out, tiles_a=tiles_a,
                          final_norm=final_norm),
        grid=(n // tm, n_ff),
        in_specs=in_specs,
        out_specs=out_specs,
        out_shape=out_shape,
        scratch_shapes=scratch,
        compiler_params=_params((row_sem, "arbitrary")),
        name="ffn",
    )(*args)


def _rope_norm(r, gain, cos, sin):
    y = _rms(r, gain)
    return y * cos + pltpu.roll(y, HEAD_DIM // 2, axis=1) * sin


def _inproj_body(*refs, kinds, rope, sub_plan, dils):
    if rope:
        x_ref, gain_ref, w_ref, qg_ref, kg_ref, cos_ref, sin_ref, o_ref, qt_ref, vt_ref = refs
    else:
        x_ref, gain_ref, w_ref, o_ref = refs[:4]
        sub_refs, slab_scr = refs[4:4 + len(dils)], refs[-1] if dils else None
    tm = x_ref.shape[0]
    h = _rms(x_ref[...], gain_ref[...]).astype(BF16)
    n_qt = 0
    for c, kind in enumerate(kinds):
        lo = c * MXU_COLS
        r = jnp.dot(h, w_ref[:, lo:lo + MXU_COLS], preferred_element_type=F32)
        if kind == "scale":
            r = r * ATTN_SCALE
        elif kind in ("qnorm", "knorm"):
            hg = qg_ref[...] if kind == "qnorm" else kg_ref[...]
            cos = cos_ref[...]
            sin = sin_ref[...]
            parts = []
            for hh in range(MXU_COLS // HEAD_DIM):
                y = _rope_norm(r[:, hh * HEAD_DIM:(hh + 1) * HEAD_DIM], hg, cos, sin)
                if kind == "qnorm":
                    y = y * (ATTN_SCALE * LOG2_E)
                parts.append(y)
            r = jnp.concatenate(parts, axis=1)
            if kind == "qnorm":
                qt_ref[0, n_qt * MXU_COLS:(n_qt + 1) * MXU_COLS, :] = r.T.astype(BF16)
                n_qt += 1
        elif kind == "plain_t":
            vt_ref[0] = r.T.astype(BF16)
        o_ref[:, lo:lo + MXU_COLS] = r.astype(BF16)
        if c in sub_plan:
            heads = range(MXU_COLS // HEAD_DIM)
            for hh in heads:
                slab_scr[hh] = r[:, hh * HEAD_DIM:(hh + 1) * HEAD_DIM]
            for si, col0 in sub_plan[c]:
                d = dils[si]
                for rho in range(d):
                    for hh in heads:
                        part = slab_scr[hh, pl.ds(rho, tm // d, stride=d), :]
                        cols = slice(col0 + hh * HEAD_DIM, col0 + (hh + 1) * HEAD_DIM)
                        sub_refs[si][0, rho, :, cols] = part.astype(BF16)


def _inproj(x, gain, w, layer, kinds, seq_len, rope_args=None, sub_args=None, *, tm=512):
    n, d = x.shape
    c = w.shape[2]
    assert n % tm == 0 and seq_len % tm == 0 and c == MXU_COLS * len(kinds)
    rope = rope_args is not None
    assert not (rope and sub_args)
    dils, sub_width, sub_plan = sub_args if sub_args else ((), 0, {})
    scratch = [pltpu.VMEM((MXU_COLS // HEAD_DIM, tm, HEAD_DIM), F32)] if dils else []
    in_specs = [
        pl.BlockSpec((tm, d), lambda i: (i, 0)),
        pl.BlockSpec((1, d), lambda i: (0, 0)),
        _resident_weight((d, c), layer),
    ]
    args = [x, gain.reshape(1, d), w]
    if rope:
        q_gain, k_gain, cos, sin = rope_args
        tiles_per_seq = seq_len // tm
        in_specs += [
            pl.BlockSpec((1, HEAD_DIM), lambda i: (0, 0)),
            pl.BlockSpec((1, HEAD_DIM), lambda i: (0, 0)),
            pl.BlockSpec((tm, HEAD_DIM), lambda i: (i % tiles_per_seq, 0)),
            pl.BlockSpec((tm, HEAD_DIM), lambda i: (i % tiles_per_seq, 0)),
        ]
        args += [q_gain.reshape(1, HEAD_DIM), k_gain.reshape(1, HEAD_DIM), cos, sin]
    out_specs = pl.BlockSpec((tm, c), lambda i: (i, 0))
    out_shape = jax.ShapeDtypeStruct((n, c), BF16)
    if rope:
        assert kinds.count("plain_t") == 1
        qt_rows = MXU_COLS * kinds.count("qnorm")
        by_seq = lambda i: (i // tiles_per_seq, 0, i % tiles_per_seq)
        out_specs = [out_specs, pl.BlockSpec((1, qt_rows, tm), by_seq),
                     pl.BlockSpec((1, MXU_COLS, tm), by_seq)]
        out_shape = [out_shape, jax.ShapeDtypeStruct((n // seq_len, qt_rows, seq_len), BF16),
                     jax.ShapeDtypeStruct((n // seq_len, MXU_COLS, seq_len), BF16)]
    if dils:
        tiles_per_seq = seq_len // tm
        assert all(tm % (dl * 16) == 0 for dl in dils)
        out_specs = [out_specs] + [
            pl.BlockSpec((1, dl, tm // dl, sub_width), lambda i: (i // tiles_per_seq, 0, i % tiles_per_seq, 0))
            for dl in dils]
        out_shape = [out_shape] + [
            jax.ShapeDtypeStruct((n // seq_len, dl, seq_len // dl, sub_width), BF16) for dl in dils]
    return pl.pallas_call(
        functools.partial(_inproj_body, kinds=tuple(kinds), rope=rope, sub_plan=sub_plan, dils=tuple(dils)),
        grid=(n // tm,),
        in_specs=in_specs,
        out_specs=out_specs,
        out_shape=out_shape,
        scratch_shapes=scratch,
        compiler_params=_params(("parallel",)),
        name="inproj",
    )(*args)


def _outproj_ab_body(x_ref, a_ref, b_ref, w_ref, o_ref):
    ka = a_ref.shape[1]
    acc = jnp.dot(a_ref[...], w_ref[0:ka, :], preferred_element_type=F32)
    acc = acc + jnp.dot(b_ref[...], w_ref[ka:, :], preferred_element_type=F32)
    o_ref[...] = x_ref[...] + acc


def _outproj_ab(x, oa, ob, w, layer, *, tm=512):
    n, d = x.shape
    ka, kb = oa.shape[1], ob.shape[1]
    assert n % tm == 0 and w.shape[1:] == (ka + kb, d)
    return pl.pallas_call(
        _outproj_ab_body,
        grid=(n // tm,),
        in_specs=[
            pl.BlockSpec((tm, d), lambda i: (i, 0)),
            pl.BlockSpec((tm, ka), lambda i: (i, 0)),
            pl.BlockSpec((tm, kb), lambda i: (i, 0)),
            _resident_weight((ka + kb, d), layer),
        ],
        out_specs=pl.BlockSpec((tm, d), lambda i: (i, 0)),
        out_shape=jax.ShapeDtypeStruct((n, d), F32),
        compiler_params=_params(("parallel",)),
        name="outproj_ab",
    )(x, oa, ob, w)


def _outproj_cd_body(*refs, dils, kd):
    ng = len(dils)
    x_ref, c_ref = refs[:2]
    o_refs, l_refs = refs[2:2 + ng], refs[2 + ng:2 + 2 * ng]
    w_ref, o_ref, row_scr = refs[2 + 2 * ng:]
    tm = x_ref.shape[0]
    slabs = kd // HEAD_DIM

    def token_rows(ref, d, slot):
        if d == 1:
            return ref[...]
        for rho in range(d):
            for s in range(slabs):
                lo = rho * kd + s * HEAD_DIM
                row_scr[slot, s, pl.ds(rho, tm // d, stride=d), :] = ref[0, :, lo:lo + HEAD_DIM]
        return jnp.concatenate([row_scr[slot, s] for s in range(slabs)], axis=1)

    outs = [token_rows(r, d, 2 * i) for i, (r, d) in enumerate(zip(o_refs, dils))]
    lses = [token_rows(r, d, 2 * i + 1) for i, (r, d) in enumerate(zip(l_refs, dils))]
    m = functools.reduce(jnp.maximum, lses)
    es = [jnp.exp(l - m) for l in lses]
    den = functools.reduce(lambda a, b: a + b, es)
    od = functools.reduce(lambda a, b: a + b, [(e / den) * o for e, o in zip(es, outs)])
    kc = c_ref.shape[1]
    acc = jnp.dot(c_ref[...], w_ref[0:kc, :], preferred_element_type=F32)
    acc = acc + jnp.dot(od.astype(BF16), w_ref[kc:, :], preferred_element_type=F32)
    o_ref[...] = x_ref[...] + acc


def _outproj_cd(x, oc, outs, lses, dils, seq_len, w, layer, *, tm=512):
    n, d = x.shape
    kc = oc.shape[1]
    kd = w.shape[1] - kc
    assert n % tm == 0 and seq_len % tm == 0 and all(tm % (dl * SUBLANES) == 0 for dl in dils)
    tiles_per_seq = seq_len // tm
    row = lambda i: (i, 0)
    by_seq = lambda i: (i // tiles_per_seq, i % tiles_per_seq, 0)
    g_specs = [pl.BlockSpec((tm, kd), row) if dl == 1 else pl.BlockSpec((1, tm // dl, dl * kd), by_seq)
               for dl in dils]
    return pl.pallas_call(
        functools.partial(_outproj_cd_body, dils=tuple(dils), kd=kd),
        grid=(n // tm,),
        in_specs=[pl.BlockSpec((tm, d), row), pl.BlockSpec((tm, kc), row)] + g_specs + g_specs
        + [_resident_weight((kc + kd, d), layer)],
        out_specs=pl.BlockSpec((tm, d), row),
        out_shape=jax.ShapeDtypeStruct((n, d), F32),
        scratch_shapes=[pltpu.VMEM((2 * len(dils), kd // HEAD_DIM, tm, HEAD_DIM), F32)],
        compiler_params=_params(("parallel",)),
        name="outproj_cd",
    )(x, oc, *outs, *lses, w)


def _stack_heads(q, g):
    return jnp.concatenate([q[:, i * HEAD_DIM:(i + 1) * HEAD_DIM] for i in range(g)], axis=0)


def _softmax_pv(s, v):
    m = jnp.max(s, axis=-1, keepdims=True)
    p = jnp.exp(s - m)
    l = jnp.sum(p, axis=-1, keepdims=True)
    o = jnp.dot(p.astype(BF16), v, preferred_element_type=F32) / l
    return o, m, l


def _qk(q, k):
    return lax.dot_general(q, k, (((1,), (1,)), ((), ())), preferred_element_type=F32)


def _dense_body(qt_ref, k_ref, vt_ref, o_ref, s_scr, p_scr, m_scr, l_scr, *, g):
    step = pl.program_id(0)
    tq = qt_ref.shape[2] // 2
    t = k_ref.shape[1]
    gq = g * tq
    kc = min(DENSE_KEY_CHUNK, t)

    @pl.when(step == 0)
    def _():
        s_scr[1] = jnp.zeros(s_scr.shape[1:], s_scr.dtype)
        m_scr[1] = jnp.zeros(m_scr.shape[1:], m_scr.dtype)
        p_scr[...] = jnp.zeros(p_scr.shape, p_scr.dtype)
        l_scr[...] = jnp.ones(l_scr.shape, l_scr.dtype)

    def fold(x, op):
        return op(x.reshape(x.shape[0] // SUBLANES, SUBLANES, x.shape[1]), axis=0)

    for cur in range(2):
        prev = 1 - cur
        toks = slice(cur * tq, (cur + 1) * tq)
        qt = jnp.concatenate([qt_ref[0, i * HEAD_DIM:(i + 1) * HEAD_DIM, toks] for i in range(g)], axis=1)
        m = jnp.max(m_scr[prev], axis=0, keepdims=True)
        ot = jnp.zeros((HEAD_DIM, gq), F32)
        l_acc = jnp.zeros((SUBLANES, gq), F32)
        m_acc = None
        for c in range(t // kc):
            keys = slice(c * kc, (c + 1) * kc)
            p = jnp.exp2(s_scr[prev, keys, :] - m)
            l_acc = l_acc + fold(p, jnp.sum)
            p_scr[prev, keys, :] = p.astype(BF16)
        for c in range(t // kc):
            keys = slice(c * kc, (c + 1) * kc)
            ot = ot + jnp.dot(vt_ref[0, :, keys], p_scr[cur, keys, :], preferred_element_type=F32)
            s = jnp.dot(k_ref[0, keys, :], qt, preferred_element_type=F32)
            s_scr[cur, keys, :] = s
            m_chunk = fold(s, jnp.max)
            m_acc = m_chunk if m_acc is None else jnp.maximum(m_acc, m_chunk)
        ot = ot / jnp.sum(l_scr[cur], axis=0, keepdims=True)
        l_scr[prev] = l_acc
        m_scr[cur] = m_acc
        for i in range(g):
            o_ref[0, toks, i * HEAD_DIM:(i + 1) * HEAD_DIM] = ot[:, i * tq:(i + 1) * tq].T.astype(o_ref.dtype)


def _dense_attention(qt, qkv, vt, k_col, n_kv, g, *, tq=128):
    b, t, _ = qkv.shape
    gw = g * HEAD_DIM
    n_pairs = t // (2 * tq)
    assert t % (2 * tq) == 0 and k_col % HEAD_DIM == 0
    assert qt.shape == (b, n_kv * gw, t) and vt.shape == (b, n_kv * HEAD_DIM, t)
    kb = k_col // HEAD_DIM
    total = b * n_kv * n_pairs

    def where(f):
        return f // (n_kv * n_pairs), (f // n_pairs) % n_kv, f % n_pairs

    def q_map(f):
        n, h, k = where(jnp.minimum(f, total - 1))
        return n, h, k

    def k_map(f):
        n, h, _ = where(jnp.minimum(f, total - 1))
        return n, 0, kb + h

    def v_map(f):
        n, h, _ = where(jnp.maximum(f - 1, 0))
        return n, h, 0

    def o_map(f):
        n, h, k = where(jnp.maximum(f - 1, 0))
        return n, k, h

    return pl.pallas_call(
        functools.partial(_dense_body, g=g),
        grid=(total + 1,),
        in_specs=[
            pl.BlockSpec((1, gw, 2 * tq), q_map),
            pl.BlockSpec((1, t, HEAD_DIM), k_map),
            pl.BlockSpec((1, HEAD_DIM, t), v_map),
        ],
        out_specs=pl.BlockSpec((1, 2 * tq, gw), o_map),
        out_shape=jax.ShapeDtypeStruct((b, t, n_kv * gw), BF16),
        scratch_shapes=[pltpu.VMEM((2, t, g * tq), F32), pltpu.VMEM((2, t, g * tq), BF16),
                        pltpu.VMEM((2, SUBLANES, g * tq), F32), pltpu.VMEM((2, SUBLANES, g * tq), F32)],
        compiler_params=_params(("arbitrary",)),
        name="dense_attn",
    )(qt, qkv, vt)


def _band_body(*refs, n_kv, g, w, seq, tq, kw, n_cases, has_sink, want_lse, side_by_side):
    refs = list(refs)
    q_ref, k_ref, v_ref, bias_ref = refs[:4]
    refs = refs[4:]
    sink_ref = refs.pop(0) if has_sink else None
    o_ref = refs.pop(0)
    lse_ref = refs.pop(0) if want_lse else None
    gw = g * HEAD_DIM
    qw = n_kv * gw
    nq = seq // tq
    for sq, j in [(a, c) for a in range(q_ref.shape[0]) for c in range(q_ref.shape[1] // tq)]:
        qi = pl.program_id(1) * (q_ref.shape[1] // tq) + j
        kstart = pl.multiple_of(jnp.clip(qi * tq - w, 0, seq - kw), 64)
        case = 0 if n_cases == 1 else jnp.where(qi == 0, 0, jnp.where(qi == nq - 1, 2, 1))
        trows = slice(j * tq, (j + 1) * tq)
        for h in range(n_kv):
            k = k_ref[sq, pl.ds(kstart, kw), h * HEAD_DIM:(h + 1) * HEAD_DIM]
            v = v_ref[sq, pl.ds(kstart, kw), h * HEAD_DIM:(h + 1) * HEAD_DIM]
            q = _stack_heads(q_ref[sq, trows, h * gw:(h + 1) * gw], g)
            s = _qk(q, k) + bias_ref[case, h]
            o, m, l = _softmax_pv(s, v)
            lse = m + jnp.log(l)
            for i in range(g):
                rows = slice(i * tq, (i + 1) * tq)
                cols = slice(h * gw + i * HEAD_DIM, h * gw + (i + 1) * HEAD_DIM)
                oi = o[rows]
                if has_sink:
                    oi = oi * jax.nn.sigmoid(lse[rows] - sink_ref[h * g + i])
                if side_by_side:
                    osq, cols = 0, slice(sq * qw + cols.start, sq * qw + cols.stop)
                else:
                    osq = sq
                o_ref[osq, trows, cols] = oi.astype(o_ref.dtype)
                if want_lse:
                    lse_ref[osq, trows, cols] = jnp.broadcast_to(lse[rows], (tq, HEAD_DIM))


def _band_geometry(seq, w):
    if seq <= 4 * w:
        return seq, seq, (0,)
    tq = 2 * w if w < 128 else w
    tq = min(tq, 256)
    kw = tq + 2 * w
    assert seq % tq == 0 and seq >= kw
    return tq, kw, (0, -w, -2 * w)


def _band_attention(q_arr, q_col, k_arr, k_col, v_arr, v_col, bias, n_kv, g, w, geometry,
                    sink=None, want_lse=False, out_dtype=BF16, dil=1):
    n, seq, _ = q_arr.shape
    tq, kw, shifts = geometry
    nq = seq // tq
    gw = g * HEAD_DIM
    qw, kvw = n_kv * gw, n_kv * HEAD_DIM
    assert q_col % qw == 0 and k_col % kvw == 0 and v_col % kvw == 0
    qb, kb, vb = q_col // qw, k_col // kvw, v_col // kvw
    tps = next(c for c in BAND_TILES_PER_STEP if nq % c == 0)
    nq_steps = nq // tps
    sps = next(c for c in BAND_TILES_PER_STEP if n % c == 0) if nq == 1 else 1
    in_specs = [
        pl.BlockSpec((sps, tps * tq, qw), lambda b, i: (b, i, qb)),
        pl.BlockSpec((sps, seq, kvw), lambda b, i: (b, 0, kb)),
        pl.BlockSpec((sps, seq, kvw), lambda b, i: (b, 0, vb)),
        pl.BlockSpec((len(shifts), n_kv, g * tq, kw), lambda b, i: (0, 0, 0, 0)),
    ]
    args = [q_arr, k_arr, v_arr, bias]
    if sink is not None:
        in_specs.append(pl.BlockSpec(memory_space=pltpu.SMEM))
        args.append(sink.astype(F32))
    if dil == 1:
        o_spec = pl.BlockSpec((sps, tps * tq, qw), lambda b, i: (b, i, 0))
        o_dims = (n, seq, qw)
    else:
        assert n % dil == 0 and dil % sps == 0
        o_spec = pl.BlockSpec((1, tps * tq, sps * qw),
                              lambda b, i: ((b * sps) // dil, i, ((b * sps) % dil) // sps))
        o_dims = (n // dil, seq, dil * qw)
    o_shape = jax.ShapeDtypeStruct(o_dims, out_dtype)
    out_specs, out_shape = o_spec, o_shape
    if want_lse:
        out_specs = [o_spec, o_spec]
        out_shape = [o_shape, jax.ShapeDtypeStruct(o_dims, F32)]
    return pl.pallas_call(
        functools.partial(_band_body, n_kv=n_kv, g=g, w=w, seq=seq, tq=tq, kw=kw, n_cases=len(shifts),
                          has_sink=sink is not None, want_lse=want_lse, side_by_side=dil > 1),
        grid=(n // sps, nq_steps),
        in_specs=in_specs,
        out_specs=out_specs,
        out_shape=out_shape,
        compiler_params=_params(("parallel", "arbitrary")),
        name="band_attn",
    )(*args)


def _na_body(q_ref, k_ref, v_ref, bias_ref, o_ref, *, g, rows, rows_per_step):
    rb = pl.program_id(2)
    kwin = NA_ROWS * GRID_W

    def one_row(rr, carry):
        r = rb * rows_per_step + rr
        rs = jnp.clip(r - NA_ROWS // 2, 0, rows - NA_ROWS)
        off = rs - r + NA_ROWS - 1
        kstart = pl.multiple_of(rs * GRID_W, GRID_W)
        qstart = pl.multiple_of(rr * GRID_W, GRID_W)
        k = k_ref[0, pl.ds(kstart, kwin), :]
        v = v_ref[0, pl.ds(kstart, kwin), :]
        q = _stack_heads(q_ref[0, pl.ds(qstart, GRID_W), :], g)
        s = _qk(q, k) + bias_ref[0, off]
        o, _, _ = _softmax_pv(s, v)
        for i in range(g):
            o_ref[0, pl.ds(qstart, GRID_W), i * HEAD_DIM:(i + 1) * HEAD_DIM] = (
                o[i * GRID_W:(i + 1) * GRID_W].astype(o_ref.dtype))
        return carry

    lax.fori_loop(0, rows_per_step, one_row, 0, unroll=NA_ROW_UNROLL)


def _na_attention(qkv, q_col, k_col, v_col, bias, n_kv, g, *, rows_per_step=8):
    b, t, _ = qkv.shape
    rows = t // GRID_W
    gw = g * HEAD_DIM
    assert rows % rows_per_step == 0 and rows >= NA_ROWS
    qb, kb, vb = q_col // gw, k_col // HEAD_DIM, v_col // HEAD_DIM
    tq = rows_per_step * GRID_W
    return pl.pallas_call(
        functools.partial(_na_body, g=g, rows=rows, rows_per_step=rows_per_step),
        grid=(b, n_kv, rows // rows_per_step),
        in_specs=[
            pl.BlockSpec((1, tq, gw), lambda n, h, i: (n, i, qb + h)),
            pl.BlockSpec((1, t, HEAD_DIM), lambda n, h, i: (n, 0, kb + h)),
            pl.BlockSpec((1, t, HEAD_DIM), lambda n, h, i: (n, 0, vb + h)),
            pl.BlockSpec((1, NA_ROWS, g * GRID_W, NA_ROWS * GRID_W), lambda n, h, i: (h, 0, 0, 0)),
        ],
        out_specs=pl.BlockSpec((1, tq, gw), lambda n, h, i: (n, i, h)),
        out_shape=jax.ShapeDtypeStruct((b, t, n_kv * gw), BF16),
        compiler_params=_params(("parallel", "parallel", "arbitrary")),
        name="na_attn",
    )(qkv, qkv, qkv, bias)


def _t5_bucket(rel):
    half = T5_BUCKETS // 2
    max_exact = half // 2
    n = jnp.abs(rel)
    nf = jnp.maximum(n, 1).astype(jnp.float32)
    large = max_exact + (jnp.log(nf / max_exact) / math.log(T5_MAX_DIST / max_exact)
                         * (half - max_exact)).astype(jnp.int32)
    large = jnp.minimum(large, half - 1)
    return jnp.where(rel > 0, half, 0) + jnp.where(n < max_exact, n, large)


def _band_bias(table_cols, n_kv, g, w, dil, geometry):
    tq, kw, shifts = geometry
    iq = jnp.arange(tq)[:, None]
    jk = jnp.arange(kw)[None, :]
    tiles = []
    for shift in shifts:
        rel = jk + shift - iq
        bucket = _t5_bucket(rel * dil)
        tb = jnp.zeros((tq, kw, table_cols.shape[1]), F32)
        for r in range(T5_BUCKETS):
            tb = jnp.where((bucket == r)[..., None], table_cols[r].astype(F32), tb)
        tb = jnp.where((jnp.abs(rel) <= w)[..., None], tb, NEG_INF)
        tiles.append(jnp.moveaxis(tb, -1, 0).reshape(n_kv, g * tq, kw))
    return jnp.stack(tiles)


def _na_bias(rpb, n_kv, g):
    col = jnp.arange(GRID_W)
    qc, kc = col[:, None], col[None, :]
    dc = jnp.clip(kc - qc + NA_COLS - 1, 0, 2 * NA_COLS - 2)
    cs = jnp.clip(qc - NA_COLS // 2, 0, GRID_W - NA_COLS)
    mask = (kc >= cs) & (kc < cs + NA_COLS)
    rpb = rpb.astype(F32)
    t = jnp.zeros(rpb.shape[:2] + dc.shape, F32)
    for c in range(2 * NA_COLS - 1):
        t = jnp.where(dc == c, rpb[:, :, c][:, :, None, None], t)
    t = jnp.where(mask, t, NEG_INF)
    cls = jnp.stack([t[:, o:o + NA_ROWS] for o in range(NA_ROWS)])
    cls = cls.reshape(NA_ROWS, n_kv, g, NA_ROWS, GRID_W, GRID_W)
    cls = jnp.transpose(cls, (1, 0, 2, 4, 3, 5))
    return cls.reshape(n_kv, NA_ROWS, g * GRID_W, NA_ROWS * GRID_W)


def _rope_tables(t):
    n_pairs = HEAD_DIM // 2
    n_freq = n_pairs // 2
    pos = jnp.arange(t)
    row = (pos // GRID_W).astype(jnp.float32)
    col = (pos % GRID_W).astype(jnp.float32)
    omega = ROPE_THETA ** (-(jnp.arange(n_freq, dtype=jnp.float32) * 2.0 / n_pairs))
    ang = jnp.concatenate([row[:, None] * omega, col[:, None] * omega], axis=-1)
    cos, sin = jnp.cos(ang), jnp.sin(ang)
    return jnp.concatenate([cos, cos], axis=-1), jnp.concatenate([-sin, sin], axis=-1)


def _deinterleave_heads(w, n_heads):
    lead = w.shape[:-1]
    w = w.reshape(lead + (n_heads, HEAD_DIM // 2, 2))
    return jnp.swapaxes(w, -1, -2).reshape(lead + (n_heads * HEAD_DIM,))


AB_KINDS = ("scale",) * 4 + ("qnorm",) * 4 + ("plain", "plain", "knorm", "plain_t")
CD_KINDS = ("scale",) * 4 + ("plain",) * 2 + ("scale",) * 6 + ("plain",) * 4


def kernel(x_prompt, x_sample, norm_ffn1, ffn1_w_in, ffn1_w_out, norm_mix, ab_w_in, ab_sink, ab_q_gain, ab_k_gain, ab_w_out, cd_w_in, cd_rpb, cd_w_out, norm_ffn2, ffn2_w_in, ffn2_w_out, t5_table, final_norm):
    (bp, t, d), bs = x_prompt.shape, x_sample.shape[0]
    assert x_sample.shape[1:] == (t, d)
    b = bp + bs
    depth = norm_ffn1.shape[0]
    x = (x_prompt.reshape(bp * t, d), x_sample.reshape(bs * t, d))

    ffn1_w_in, ffn1_w_out = ffn1_w_in.astype(BF16), ffn1_w_out.astype(BF16)
    ffn2_w_in, ffn2_w_out = ffn2_w_in.astype(BF16), ffn2_w_out.astype(BF16)
    qa_w, kva_w = A_HEADS * HEAD_DIM, 2 * A_KV * HEAD_DIM
    qb_lo = qa_w + kva_w
    qb_hi = qb_lo + B_HEADS * HEAD_DIM
    kb_hi = qb_hi + B_KV * HEAD_DIM
    ab_w_in = jnp.concatenate([
        ab_w_in[..., :qa_w],
        _deinterleave_heads(ab_w_in[..., qb_lo:qb_hi], B_HEADS),
        ab_w_in[..., qa_w:qb_lo],
        _deinterleave_heads(ab_w_in[..., qb_hi:kb_hi], B_KV),
        ab_w_in[..., kb_hi:]], axis=-1).astype(BF16)
    ab_q_gain = _deinterleave_heads(ab_q_gain, 1)
    ab_k_gain = _deinterleave_heads(ab_k_gain, 1)
    ab_w_out, cd_w_in, cd_w_out = ab_w_out.astype(BF16), cd_w_in.astype(BF16), cd_w_out.astype(BF16)

    cos, sin = _rope_tables(t)
    ga, gc = A_HEADS // A_KV, C_HEADS // C_KV
    geo_a = _band_geometry(t, A_WINDOW)
    bias_a = _band_bias(t5_table[:, :A_HEADS], A_KV, ga, A_WINDOW, 1, geo_a)
    d_groups = []
    for gi, (win, dil) in enumerate(D_PAIRS):
        hs = win // (2 * dil)
        geo = _band_geometry(t // dil, hs)
        col0 = A_HEADS + gi * D_SLOTS
        d_groups.append((dil, hs, geo, _band_bias(t5_table[:, col0:col0 + D_SLOTS], D_SLOTS, 1, hs, dil, geo)))

    a_q = 0
    a_k = qa_w + B_HEADS * HEAD_DIM
    a_v = a_k + A_KV * HEAD_DIM
    b_k = a_v + A_KV * HEAD_DIM
    c_q, c_k, c_v = 0, C_HEADS * HEAD_DIM, (C_HEADS + C_KV) * HEAD_DIM
    d_q = (C_HEADS + 2 * C_KV) * HEAD_DIM
    d_k = d_q + D_HEADS * HEAD_DIM
    d_v = d_k + D_SLOTS * HEAD_DIM
    dw = D_SLOTS * HEAD_DIM
    sub_dils = tuple(dil for _, dil in D_PAIRS if dil > 1)
    sub_plan = {}
    for gi, (_, dil) in enumerate(D_PAIRS):
        if dil > 1:
            si = sub_dils.index(dil)
            for src, dst in ((d_q + gi * dw, 0), (d_k, dw), (d_v, 2 * dw)):
                for off in range(0, dw, MXU_COLS):
                    sub_plan.setdefault((src + off) // MXU_COLS, []).append((si, dst + off))

    for l in range(depth):
        i = l // 2
        x = _ffn(x, norm_ffn1[l], ffn1_w_in, ffn1_w_out, l)
        if l % 2 == 0:
            qkv, qbt, vbt = _inproj(x, norm_mix[l], ab_w_in, i, AB_KINDS, t,
                                    (ab_q_gain[i], ab_k_gain[i], cos, sin))
            qkv = qkv.reshape(b, t, -1)
            o_a = _band_attention(qkv, a_q, qkv, a_k, qkv, a_v, bias_a, A_KV, ga, A_WINDOW, geo_a,
                                  sink=ab_sink[i])
            o_b = _dense_attention(qbt, qkv, vbt, b_k, B_KV, B_HEADS // B_KV)
            x = _outproj_ab(x, o_a.reshape(b * t, -1), o_b.reshape(b * t, -1), ab_w_out, i)
        else:
            qkv, *subs = _inproj(x, norm_mix[l], cd_w_in, i, CD_KINDS, t,
                                 sub_args=(sub_dils, 3 * dw, sub_plan))
            qkv = qkv.reshape(b, t, -1)
            o_c = _na_attention(qkv, c_q, c_k, c_v, _na_bias(cd_rpb[i], C_KV, gc), C_KV, gc)
            outs, lses = [], []
            for gi, (dil, hs, geo, bias_d) in enumerate(d_groups):
                if dil == 1:
                    o_g, lse_g = _band_attention(qkv, d_q + gi * dw, qkv, d_k, qkv, d_v, bias_d,
                                                 D_SLOTS, 1, hs, geo, want_lse=True, out_dtype=F32)
                    o_g, lse_g = o_g.reshape(b * t, dw), lse_g.reshape(b * t, dw)
                else:
                    sub = subs[sub_dils.index(dil)].reshape(b * dil, t // dil, 3 * dw)
                    o_g, lse_g = _band_attention(sub, 0, sub, dw, sub, 2 * dw, bias_d, D_SLOTS, 1, hs, geo,
                                                 want_lse=True, out_dtype=F32, dil=dil)
                outs.append(o_g)
                lses.append(lse_g)
            x = _outproj_cd(x, o_c.reshape(b * t, -1), outs, lses, [g[0] for g in d_groups], t, cd_w_out, i)
        last = l == depth - 1
        x = _ffn(x, norm_ffn2[l], ffn2_w_in, ffn2_w_out, l, final_gain=final_norm if last else None,
                 out_rows=(bp * t, bs * t) if last else None)

    return (x[0].reshape(bp, t, d), x[1].reshape(bs, t, d))
```

```python
import functools
import math

import jax
import jax.numpy as jnp
import numpy as np
from jax import lax
from jax.experimental import pallas as pl
from jax.experimental.pallas import tpu as pltpu

HEAD_DIM = 128
GRID_W = 64
A_HEADS, A_KV, A_WINDOW = 8, 2, 128
B_HEADS, B_KV = 8, 2
ROPE_THETA = 10000.0
C_HEADS, C_KV = 8, 2
NA_ROWS, NA_COLS = 8, 16
D_PAIRS = ((128, 1), (512, 4), (2048, 16))
D_SLOTS = 4
D_HEADS = D_SLOTS * len(D_PAIRS)
T5_BUCKETS = 32
T5_MAX_DIST = 2048
NORM_EPS = 1e-6
NEG_INF = -1e30
ATTN_SCALE = HEAD_DIM ** -0.5
LOG2_E = math.log2(math.e)

LANES = 128
SUBLANES = 8
MXU_COLS = 256
FFN_ROW_TILE = 1024
FFN_FF_TILE = 2 * MXU_COLS
DENSE_KEY_CHUNK = 2 * MXU_COLS
BAND_TILES_PER_STEP = (4, 2, 1)
NA_ROW_UNROLL = 8
MIB = 1024 * 1024
VMEM_LIMIT_BYTES = 56 * MIB

BF16 = jnp.bfloat16
F32 = jnp.float32


def _params(semantics):
    return pltpu.CompilerParams(dimension_semantics=semantics, vmem_limit_bytes=VMEM_LIMIT_BYTES)


def _resident_weight(shape, layer):
    return pl.BlockSpec((None,) + shape, lambda *_: (layer,) + (0,) * len(shape),
                        pipeline_mode=pl.Buffered(1))


def _rms(x, gain):
    ms = jnp.mean(x * x, axis=-1, keepdims=True)
    return x * lax.rsqrt(ms + NORM_EPS) * gain


def _ffn_body(*refs, n_ff, n_in, n_out, tiles_a, final_norm):
    refs = list(refs)
    x_refs = [refs.pop(0) for _ in range(n_in)]
    gain_ref, wg_ref, wu_ref, wo_ref = [refs.pop(0) for _ in range(4)]
    fgain_ref = refs.pop(0) if final_norm else None
    o_refs = [refs.pop(0) for _ in range(n_out)]
    h_scr = refs.pop(0)
    acc_ref = refs.pop(0) if n_out == 2 else o_refs[0]
    i, j = pl.program_id(0), pl.program_id(1)

    def start(x_ref):
        x = x_ref[...]
        h_scr[...] = _rms(x, gain_ref[...]).astype(BF16)
        acc_ref[...] = x

    if n_in == 1:
        pl.when(j == 0)(lambda: start(x_refs[0]))
    else:
        pl.when((j == 0) & (i < tiles_a))(lambda: start(x_refs[0]))
        pl.when((j == 0) & (i >= tiles_a))(lambda: start(x_refs[1]))

    h = h_scr[...]
    g = jnp.dot(h, wg_ref[...], preferred_element_type=F32)
    u = jnp.dot(h, wu_ref[...], preferred_element_type=F32)
    a = (g * jax.nn.sigmoid(g) * (0.5 * u)).astype(BF16)
    acc_ref[...] += jnp.dot(a, wo_ref[...], preferred_element_type=F32)

    def finish(o_ref):
        y = acc_ref[...]
        o_ref[...] = _rms(y, fgain_ref[...]) if final_norm else y

    if n_out == 2:
        pl.when((j == n_ff - 1) & (i < tiles_a))(lambda: finish(o_refs[0]))
        pl.when((j == n_ff - 1) & (i >= tiles_a))(lambda: finish(o_refs[1]))
    elif final_norm:
        pl.when(j == n_ff - 1)(lambda: finish(o_refs[0]))


def _ffn_lookahead_body(x_ref, gain_ref, wg_ref, wu_ref, wo_ref, o_ref, h_scr, *, n_ff):
    i, j = pl.program_id(0), pl.program_id(1)

    def norm_into_h():
        h_scr[...] = _rms(x_ref[...], gain_ref[...]).astype(BF16)

    def step(first, last):
        h = h_scr[...]
        g = jnp.dot(h, wg_ref[...], preferred_element_type=F32)
        u = jnp.dot(h, wu_ref[...], preferred_element_type=F32)
        a = (g * jax.nn.sigmoid(g) * (0.5 * u)).astype(BF16)
        y = jnp.dot(a, wo_ref[...], preferred_element_type=F32)
        if first:
            o_ref[...] = x_ref[...] + y
        else:
            o_ref[...] += y
        if last:
            norm_into_h()

    pl.when((i == 0) & (j == 0))(norm_into_h)
    pl.when(j == 0)(lambda: step(True, False))
    pl.when((j > 0) & (j < n_ff - 1))(lambda: step(False, False))
    pl.when(j == n_ff - 1)(lambda: step(False, True))


def _ffn(xs, gain, w_in, w_out, layer, final_gain=None, out_rows=None, *, tm=FFN_ROW_TILE):
    xs = tuple(xs) if isinstance(xs, (tuple, list)) else (xs,)
    d = xs[0].shape[1]
    n = sum(x.shape[0] for x in xs)
    f = w_out.shape[1]
    tf = math.gcd(FFN_FF_TILE, f)
    n_ff = f // tf
    if len(xs) == 2 or out_rows is not None:
        tm //= 2
    tm = math.gcd(tm, *(x.shape[0] for x in xs))
    assert w_in.shape[1:] == (d, 2 * f) and w_out.shape[2] == d
    n_in, n_out = len(xs), 1 if out_rows is None else 2
    rows_a = xs[0].shape[0] if n_in == 2 else (out_rows[0] if n_out == 2 else n)
    assert n_in == 1 or n_out == 1 or out_rows[0] == rows_a
    tiles_a = rows_a // tm
    final_norm = final_gain is not None
    seg_a = lambda i, j: (jnp.minimum(i, tiles_a - 1), 0)
    seg_b = lambda i, j: (jnp.maximum(i - tiles_a, 0), 0)
    whole = lambda i, j: (i, 0)
    lookahead = n_in == 1 and n_out == 1 and not final_norm and n_ff >= 3
    if lookahead:
        x_maps = (lambda i, j: (jnp.minimum(i + (j + 1) // n_ff, n // tm - 1), 0),)
    else:
        x_maps = (whole,) if n_in == 1 else (seg_a, seg_b)
    in_specs = [pl.BlockSpec((tm, d), m) for m in x_maps]
    in_specs += [
        pl.BlockSpec((1, d), lambda i, j: (0, 0)),
        pl.BlockSpec((None, d, tf), lambda i, j: (layer, 0, j)),
        pl.BlockSpec((None, d, tf), lambda i, j: (layer, 0, j + n_ff)),
        pl.BlockSpec((None, tf, d), lambda i, j: (layer, j, 0)),
    ]
    args = list(xs) + [gain.reshape(1, d), w_in, w_in, w_out]
    if final_norm:
        in_specs.append(pl.BlockSpec((1, d), lambda i, j: (0, 0)))
        args.append(final_gain.reshape(1, d))
    scratch = [pltpu.VMEM((tm, d), BF16)]
    if n_out == 1:
        out_specs = pl.BlockSpec((tm, d), whole)
        out_shape = jax.ShapeDtypeStruct((n, d), F32)
    else:
        assert out_rows[0] % tm == 0 and sum(out_rows) == n
        out_specs = [pl.BlockSpec((tm, d), seg_a), pl.BlockSpec((tm, d), seg_b)]
        out_shape = [jax.ShapeDtypeStruct((r, d), F32) for r in out_rows]
        scratch.append(pltpu.VMEM((tm, d), F32))
    row_sem = "parallel" if n_in == 1 and n_out == 1 and not lookahead else "arbitrary"
    if lookahead:
        body = functools.partial(_ffn_lookahead_body, n_ff=n_ff)
    else:
        body = functools.partial(_ffn_body, n_ff=n_ff, n_in=n_in, n_out=n_out, tiles_a=tiles_a,
                                 final_norm=final_norm)
    return pl.pallas_call(
        body,
        grid=(n // tm, n_ff),
        in_specs=in_specs,
        out_specs=out_specs,
        out_shape=out_shape,
        scratch_shapes=scratch,
        compiler_params=_params((row_sem, "arbitrary")),
        name="ffn",
    )(*args)


def _rope_norm(r, gain, cos, sin):
    y = _rms(r, gain)
    return y * cos + pltpu.roll(y, HEAD_DIM // 2, axis=1) * sin


def _inproj_body(*refs, kinds, rope, sub_plan, dils):
    if rope:
        x_ref, gain_ref, w_ref, qg_ref, kg_ref, cos_ref, sin_ref, o_ref, qt_ref, vt_ref = refs
    else:
        x_ref, gain_ref, w_ref, o_ref = refs[:4]
        sub_refs, slab_scr = refs[4:4 + len(dils)], refs[-1] if dils else None
    tm = x_ref.shape[0]
    h = _rms(x_ref[...], gain_ref[...]).astype(BF16)
    n_qt = 0
    for c, kind in enumerate(kinds):
        lo = c * MXU_COLS
        r = jnp.dot(h, w_ref[:, lo:lo + MXU_COLS], preferred_element_type=F32)
        if kind == "scale":
            r = r * ATTN_SCALE
        elif kind in ("qnorm", "knorm"):
            hg = qg_ref[...] if kind == "qnorm" else kg_ref[...]
            cos = cos_ref[...]
            sin = sin_ref[...]
            parts = []
            for hh in range(MXU_COLS // HEAD_DIM):
                y = _rope_norm(r[:, hh * HEAD_DIM:(hh + 1) * HEAD_DIM], hg, cos, sin)
                if kind == "qnorm":
                    y = y * (ATTN_SCALE * LOG2_E)
                parts.append(y)
            r = jnp.concatenate(parts, axis=1)
            if kind == "qnorm":
                qt_ref[0, n_qt * MXU_COLS:(n_qt + 1) * MXU_COLS, :] = r.T.astype(BF16)
                n_qt += 1
        elif kind == "plain_t":
            vt_ref[0] = r.T.astype(BF16)
        o_ref[:, lo:lo + MXU_COLS] = r.astype(BF16)
        if c in sub_plan:
            heads = range(MXU_COLS // HEAD_DIM)
            for hh in heads:
                slab_scr[hh] = r[:, hh * HEAD_DIM:(hh + 1) * HEAD_DIM]
            for si, col0 in sub_plan[c]:
                d = dils[si]
                for rho in range(d):
                    for hh in heads:
                        part = slab_scr[hh, pl.ds(rho, tm // d, stride=d), :]
                        cols = slice(col0 + hh * HEAD_DIM, col0 + (hh + 1) * HEAD_DIM)
                        sub_refs[si][0, rho, :, cols] = part.astype(BF16)


def _inproj(x, gain, w, layer, kinds, seq_len, rope_args=None, sub_args=None, *, tm=512):
    n, d = x.shape
    c = w.shape[2]
    assert n % tm == 0 and seq_len % tm == 0 and c == MXU_COLS * len(kinds)
    rope = rope_args is not None
    assert not (rope and sub_args)
    dils, sub_width, sub_plan = sub_args if sub_args else ((), 0, {})
    scratch = [pltpu.VMEM((MXU_COLS // HEAD_DIM, tm, HEAD_DIM), F32)] if dils else []
    in_specs = [
        pl.BlockSpec((tm, d), lambda i: (i, 0)),
        pl.BlockSpec((1, d), lambda i: (0, 0)),
        _resident_weight((d, c), layer),
    ]
    args = [x, gain.reshape(1, d), w]
    if rope:
        q_gain, k_gain, cos, sin = rope_args
        tiles_per_seq = seq_len // tm
        in_specs += [
            pl.BlockSpec((1, HEAD_DIM), lambda i: (0, 0)),
            pl.BlockSpec((1, HEAD_DIM), lambda i: (0, 0)),
            pl.BlockSpec((tm, HEAD_DIM), lambda i: (i % tiles_per_seq, 0)),
            pl.BlockSpec((tm, HEAD_DIM), lambda i: (i % tiles_per_seq, 0)),
        ]
        args += [q_gain.reshape(1, HEAD_DIM), k_gain.reshape(1, HEAD_DIM), cos, sin]
    out_specs = pl.BlockSpec((tm, c), lambda i: (i, 0))
    out_shape = jax.ShapeDtypeStruct((n, c), BF16)
    if rope:
        assert kinds.count("plain_t") == 1
        qt_rows = MXU_COLS * kinds.count("qnorm")
        by_seq = lambda i: (i // tiles_per_seq, 0, i % tiles_per_seq)
        out_specs = [out_specs, pl.BlockSpec((1, qt_rows, tm), by_seq),
                     pl.BlockSpec((1, MXU_COLS, tm), by_seq)]
        out_shape = [out_shape, jax.ShapeDtypeStruct((n // seq_len, qt_rows, seq_len), BF16),
                     jax.ShapeDtypeStruct((n // seq_len, MXU_COLS, seq_len), BF16)]
    if dils:
        tiles_per_seq = seq_len // tm
        assert all(tm % (dl * 16) == 0 for dl in dils)
        out_specs = [out_specs] + [
            pl.BlockSpec((1, dl, tm // dl, sub_width), lambda i: (i // tiles_per_seq, 0, i % tiles_per_seq, 0))
            for dl in dils]
        out_shape = [out_shape] + [
            jax.ShapeDtypeStruct((n // seq_len, dl, seq_len // dl, sub_width), BF16) for dl in dils]
    return pl.pallas_call(
        functools.partial(_inproj_body, kinds=tuple(kinds), rope=rope, sub_plan=sub_plan, dils=tuple(dils)),
        grid=(n // tm,),
        in_specs=in_specs,
        out_specs=out_specs,
        out_shape=out_shape,
        scratch_shapes=scratch,
        compiler_params=_params(("parallel",)),
        name="inproj",
    )(*args)


def _outproj_ab_body(x_ref, a_ref, b_ref, w_ref, o_ref):
    ka = a_ref.shape[1]
    acc = jnp.dot(a_ref[...], w_ref[0:ka, :], preferred_element_type=F32)
    acc = acc + jnp.dot(b_ref[...], w_ref[ka:, :], preferred_element_type=F32)
    o_ref[...] = x_ref[...] + acc


def _outproj_ab(x, oa, ob, w, layer, *, tm=512):
    n, d = x.shape
    ka, kb = oa.shape[1], ob.shape[1]
    assert n % tm == 0 and w.shape[1:] == (ka + kb, d)
    return pl.pallas_call(
        _outproj_ab_body,
        grid=(n // tm,),
        in_specs=[
            pl.BlockSpec((tm, d), lambda i: (i, 0)),
            pl.BlockSpec((tm, ka), lambda i: (i, 0)),
            pl.BlockSpec((tm, kb), lambda i: (i, 0)),
            _resident_weight((ka + kb, d), layer),
        ],
        out_specs=pl.BlockSpec((tm, d), lambda i: (i, 0)),
        out_shape=jax.ShapeDtypeStruct((n, d), F32),
        compiler_params=_params(("parallel",)),
        name="outproj_ab",
    )(x, oa, ob, w)


def _outproj_cd_body(*refs, dils, kd):
    ng = len(dils)
    x_ref, c_ref = refs[:2]
    o_refs, l_refs = refs[2:2 + ng], refs[2 + ng:2 + 2 * ng]
    w_ref, o_ref, row_scr = refs[2 + 2 * ng:]
    tm = x_ref.shape[0]
    slabs = kd // HEAD_DIM

    def token_rows(ref, d, slot):
        if d == 1:
            return ref[...]
        for rho in range(d):
            for s in range(slabs):
                lo = rho * kd + s * HEAD_DIM
                row_scr[slot, s, pl.ds(rho, tm // d, stride=d), :] = ref[0, :, lo:lo + HEAD_DIM]
        return jnp.concatenate([row_scr[slot, s] for s in range(slabs)], axis=1)

    outs = [token_rows(r, d, 2 * i) for i, (r, d) in enumerate(zip(o_refs, dils))]
    lses = [token_rows(r, d, 2 * i + 1) for i, (r, d) in enumerate(zip(l_refs, dils))]
    m = functools.reduce(jnp.maximum, lses)
    es = [jnp.exp(l - m) for l in lses]
    den = functools.reduce(lambda a, b: a + b, es)
    od = functools.reduce(lambda a, b: a + b, [(e / den) * o for e, o in zip(es, outs)])
    kc = c_ref.shape[1]
    acc = jnp.dot(c_ref[...], w_ref[0:kc, :], preferred_element_type=F32)
    acc = acc + jnp.dot(od.astype(BF16), w_ref[kc:, :], preferred_element_type=F32)
    o_ref[...] = x_ref[...] + acc


def _outproj_cd(x, oc, outs, lses, dils, seq_len, w, layer, *, tm=512):
    n, d = x.shape
    kc = oc.shape[1]
    kd = w.shape[1] - kc
    assert n % tm == 0 and seq_len % tm == 0 and all(tm % (dl * SUBLANES) == 0 for dl in dils)
    tiles_per_seq = seq_len // tm
    row = lambda i: (i, 0)
    by_seq = lambda i: (i // tiles_per_seq, i % tiles_per_seq, 0)
    g_specs = [pl.BlockSpec((tm, kd), row) if dl == 1 else pl.BlockSpec((1, tm // dl, dl * kd), by_seq)
               for dl in dils]
    return pl.pallas_call(
        functools.partial(_outproj_cd_body, dils=tuple(dils), kd=kd),
        grid=(n // tm,),
        in_specs=[pl.BlockSpec((tm, d), row), pl.BlockSpec((tm, kc), row)] + g_specs + g_specs
        + [_resident_weight((kc + kd, d), layer)],
        out_specs=pl.BlockSpec((tm, d), row),
        out_shape=jax.ShapeDtypeStruct((n, d), F32),
        scratch_shapes=[pltpu.VMEM((2 * len(dils), kd // HEAD_DIM, tm, HEAD_DIM), F32)],
        compiler_params=_params(("parallel",)),
        name="outproj_cd",
    )(x, oc, *outs, *lses, w)


def _stack_heads(q, g):
    return jnp.concatenate([q[:, i * HEAD_DIM:(i + 1) * HEAD_DIM] for i in range(g)], axis=0)


def _softmax_pv(s, v):
    m = jnp.max(s, axis=-1, keepdims=True)
    p = jnp.exp(s - m)
    l = jnp.sum(p, axis=-1, keepdims=True)
    o = jnp.dot(p.astype(BF16), v, preferred_element_type=F32) / l
    return o, m, l


def _qk(q, k):
    return lax.dot_general(q, k, (((1,), (1,)), ((), ())), preferred_element_type=F32)


def _dense_body(qt_ref, k_ref, vt_ref, o_ref, s_scr, p_scr, m_scr, l_scr, *, g):
    step = pl.program_id(0)
    tq = qt_ref.shape[2] // 2
    t = k_ref.shape[1]
    gq = g * tq
    kc = min(DENSE_KEY_CHUNK, t)

    @pl.when(step == 0)
    def _():
        s_scr[1] = jnp.zeros(s_scr.shape[1:], s_scr.dtype)
        m_scr[1] = jnp.zeros(m_scr.shape[1:], m_scr.dtype)
        p_scr[...] = jnp.zeros(p_scr.shape, p_scr.dtype)
        l_scr[...] = jnp.ones(l_scr.shape, l_scr.dtype)

    def fold(x, op):
        return op(x.reshape(x.shape[0] // SUBLANES, SUBLANES, x.shape[1]), axis=0)

    for cur in range(2):
        prev = 1 - cur
        toks = slice(cur * tq, (cur + 1) * tq)
        qt = jnp.concatenate([qt_ref[0, i * HEAD_DIM:(i + 1) * HEAD_DIM, toks] for i in range(g)], axis=1)
        m = jnp.max(m_scr[prev], axis=0, keepdims=True)
        ot = jnp.zeros((HEAD_DIM, gq), F32)
        l_acc = jnp.zeros((SUBLANES, gq), F32)
        m_acc = None
        for c in range(t // kc):
            keys = slice(c * kc, (c + 1) * kc)
            p = jnp.exp2(s_scr[prev, keys, :] - m)
            l_acc = l_acc + fold(p, jnp.sum)
            p_scr[prev, keys, :] = p.astype(BF16)
        for c in range(t // kc):
            keys = slice(c * kc, (c + 1) * kc)
            ot = ot + jnp.dot(vt_ref[0, :, keys], p_scr[cur, keys, :], preferred_element_type=F32)
            s = jnp.dot(k_ref[0, keys, :], qt, preferred_element_type=F32)
            s_scr[cur, keys, :] = s
            m_chunk = fold(s, jnp.max)
            m_acc = m_chunk if m_acc is None else jnp.maximum(m_acc, m_chunk)
        ot = ot / jnp.sum(l_scr[cur], axis=0, keepdims=True)
        l_scr[prev] = l_acc
        m_scr[cur] = m_acc
        for i in range(g):
            o_ref[0, toks, i * HEAD_DIM:(i + 1) * HEAD_DIM] = ot[:, i * tq:(i + 1) * tq].T.astype(o_ref.dtype)


def _dense_attention(qt, qkv, vt, k_col, n_kv, g, *, tq=128):
    b, t, _ = qkv.shape
    gw = g * HEAD_DIM
    n_pairs = t // (2 * tq)
    assert t % (2 * tq) == 0 and k_col % HEAD_DIM == 0
    assert qt.shape == (b, n_kv * gw, t) and vt.shape == (b, n_kv * HEAD_DIM, t)
    kb = k_col // HEAD_DIM
    total = b * n_kv * n_pairs

    def where(f):
        return f // (n_kv * n_pairs), (f // n_pairs) % n_kv, f % n_pairs

    def q_map(f):
        n, h, k = where(jnp.minimum(f, total - 1))
        return n, h, k

    def k_map(f):
        n, h, _ = where(jnp.minimum(f, total - 1))
        return n, 0, kb + h

    def v_map(f):
        n, h, _ = where(jnp.maximum(f - 1, 0))
        return n, h, 0

    def o_map(f):
        n, h, k = where(jnp.maximum(f - 1, 0))
        return n, k, h

    return pl.pallas_call(
        functools.partial(_dense_body, g=g),
        grid=(total + 1,),
        in_specs=[
            pl.BlockSpec((1, gw, 2 * tq), q_map),
            pl.BlockSpec((1, t, HEAD_DIM), k_map),
            pl.BlockSpec((1, HEAD_DIM, t), v_map),
        ],
        out_specs=pl.BlockSpec((1, 2 * tq, gw), o_map),
        out_shape=jax.ShapeDtypeStruct((b, t, n_kv * gw), BF16),
        scratch_shapes=[pltpu.VMEM((2, t, g * tq), F32), pltpu.VMEM((2, t, g * tq), BF16),
                        pltpu.VMEM((2, SUBLANES, g * tq), F32), pltpu.VMEM((2, SUBLANES, g * tq), F32)],
        compiler_params=_params(("arbitrary",)),
        name="dense_attn",
    )(qt, qkv, vt)


def _band_body(*refs, n_kv, g, w, seq, tq, kw, n_cases, has_sink, want_lse, side_by_side):
    refs = list(refs)
    q_ref, k_ref, v_ref, bias_ref = refs[:4]
    refs = refs[4:]
    sink_ref = refs.pop(0) if has_sink else None
    o_ref = refs.pop(0)
    lse_ref = refs.pop(0) if want_lse else None
    gw = g * HEAD_DIM
    qw = n_kv * gw
    nq = seq // tq
    for sq, j in [(a, c) for a in range(q_ref.shape[0]) for c in range(q_ref.shape[1] // tq)]:
        qi = pl.program_id(1) * (q_ref.shape[1] // tq) + j
        kstart = pl.multiple_of(jnp.clip(qi * tq - w, 0, seq - kw), 64)
        case = 0 if n_cases == 1 else jnp.where(qi == 0, 0, jnp.where(qi == nq - 1, 2, 1))
        trows = slice(j * tq, (j + 1) * tq)
        for h in range(n_kv):
            k = k_ref[sq, pl.ds(kstart, kw), h * HEAD_DIM:(h + 1) * HEAD_DIM]
            v = v_ref[sq, pl.ds(kstart, kw), h * HEAD_DIM:(h + 1) * HEAD_DIM]
            q = _stack_heads(q_ref[sq, trows, h * gw:(h + 1) * gw], g)
            s = _qk(q, k) + bias_ref[case, h]
            o, m, l = _softmax_pv(s, v)
            lse = m + jnp.log(l)
            for i in range(g):
                rows = slice(i * tq, (i + 1) * tq)
                cols = slice(h * gw + i * HEAD_DIM, h * gw + (i + 1) * HEAD_DIM)
                oi = o[rows]
                if has_sink:
                    oi = oi * jax.nn.sigmoid(lse[rows] - sink_ref[h * g + i])
                if side_by_side:
                    osq, cols = 0, slice(sq * qw + cols.start, sq * qw + cols.stop)
                else:
                    osq = sq
                o_ref[osq, trows, cols] = oi.astype(o_ref.dtype)
                if want_lse:
                    lse_ref[osq, trows, cols] = jnp.broadcast_to(lse[rows], (tq, HEAD_DIM))


def _band_geometry(seq, w):
    if seq <= 4 * w:
        return seq, seq, (0,)
    tq = 2 * w if w < 128 else w
    tq = min(tq, 256)
    kw = tq + 2 * w
    assert seq % tq == 0 and seq >= kw
    return tq, kw, (0, -w, -2 * w)


def _band_attention(q_arr, q_col, k_arr, k_col, v_arr, v_col, bias, n_kv, g, w, geometry,
                    sink=None, want_lse=False, out_dtype=BF16, dil=1):
    n, seq, _ = q_arr.shape
    tq, kw, shifts = geometry
    nq = seq // tq
    gw = g * HEAD_DIM
    qw, kvw = n_kv * gw, n_kv * HEAD_DIM
    assert q_col % qw == 0 and k_col % kvw == 0 and v_col % kvw == 0
    qb, kb, vb = q_col // qw, k_col // kvw, v_col // kvw
    tps = next(c for c in BAND_TILES_PER_STEP if nq % c == 0)
    nq_steps = nq // tps
    sps = next(c for c in BAND_TILES_PER_STEP if n % c == 0) if nq == 1 else 1
    in_specs = [
        pl.BlockSpec((sps, tps * tq, qw), lambda b, i: (b, i, qb)),
        pl.BlockSpec((sps, seq, kvw), lambda b, i: (b, 0, kb)),
        pl.BlockSpec((sps, seq, kvw), lambda b, i: (b, 0, vb)),
        pl.BlockSpec((len(shifts), n_kv, g * tq, kw), lambda b, i: (0, 0, 0, 0)),
    ]
    args = [q_arr, k_arr, v_arr, bias]
    if sink is not None:
        in_specs.append(pl.BlockSpec(memory_space=pltpu.SMEM))
        args.append(sink.astype(F32))
    if dil == 1:
        o_spec = pl.BlockSpec((sps, tps * tq, qw), lambda b, i: (b, i, 0))
        o_dims = (n, seq, qw)
    else:
        assert n % dil == 0 and dil % sps == 0
        o_spec = pl.BlockSpec((1, tps * tq, sps * qw),
                              lambda b, i: ((b * sps) // dil, i, ((b * sps) % dil) // sps))
        o_dims = (n // dil, seq, dil * qw)
    o_shape = jax.ShapeDtypeStruct(o_dims, out_dtype)
    out_specs, out_shape = o_spec, o_shape
    if want_lse:
        out_specs = [o_spec, o_spec]
        out_shape = [o_shape, jax.ShapeDtypeStruct(o_dims, F32)]
    return pl.pallas_call(
        functools.partial(_band_body, n_kv=n_kv, g=g, w=w, seq=seq, tq=tq, kw=kw, n_cases=len(shifts),
                          has_sink=sink is not None, want_lse=want_lse, side_by_side=dil > 1),
        grid=(n // sps, nq_steps),
        in_specs=in_specs,
        out_specs=out_specs,
        out_shape=out_shape,
        compiler_params=_params(("parallel", "arbitrary")),
        name="band_attn",
    )(*args)


def _na_body(q_ref, k_ref, v_ref, bias_ref, o_ref, *, g, rows, rows_per_step):
    rb = pl.program_id(2)
    kwin = NA_ROWS * GRID_W

    def one_row(rr, carry):
        r = rb * rows_per_step + rr
        rs = jnp.clip(r - NA_ROWS // 2, 0, rows - NA_ROWS)
        off = rs - r + NA_ROWS - 1
        kstart = pl.multiple_of(rs * GRID_W, GRID_W)
        qstart = pl.multiple_of(rr * GRID_W, GRID_W)
        k = k_ref[0, pl.ds(kstart, kwin), :]
        v = v_ref[0, pl.ds(kstart, kwin), :]
        q = _stack_heads(q_ref[0, pl.ds(qstart, GRID_W), :], g)
        s = _qk(q, k) + bias_ref[0, off]
        o, _, _ = _softmax_pv(s, v)
        for i in range(g):
            o_ref[0, pl.ds(qstart, GRID_W), i * HEAD_DIM:(i + 1) * HEAD_DIM] = (
                o[i * GRID_W:(i + 1) * GRID_W].astype(o_ref.dtype))
        return carry

    lax.fori_loop(0, rows_per_step, one_row, 0, unroll=NA_ROW_UNROLL)


def _na_attention(qkv, q_col, k_col, v_col, bias, n_kv, g, *, rows_per_step=8):
    b, t, _ = qkv.shape
    rows = t // GRID_W
    gw = g * HEAD_DIM
    assert rows % rows_per_step == 0 and rows >= NA_ROWS
    qb, kb, vb = q_col // gw, k_col // HEAD_DIM, v_col // HEAD_DIM
    tq = rows_per_step * GRID_W
    return pl.pallas_call(
        functools.partial(_na_body, g=g, rows=rows, rows_per_step=rows_per_step),
        grid=(b, n_kv, rows // rows_per_step),
        in_specs=[
            pl.BlockSpec((1, tq, gw), lambda n, h, i: (n, i, qb + h)),
            pl.BlockSpec((1, t, HEAD_DIM), lambda n, h, i: (n, 0, kb + h)),
            pl.BlockSpec((1, t, HEAD_DIM), lambda n, h, i: (n, 0, vb + h)),
            pl.BlockSpec((1, NA_ROWS, g * GRID_W, NA_ROWS * GRID_W), lambda n, h, i: (h, 0, 0, 0)),
        ],
        out_specs=pl.BlockSpec((1, tq, gw), lambda n, h, i: (n, i, h)),
        out_shape=jax.ShapeDtypeStruct((b, t, n_kv * gw), BF16),
        compiler_params=_params(("parallel", "parallel", "arbitrary")),
        name="na_attn",
    )(qkv, qkv, qkv, bias)


def _t5_bucket(rel):
    half = T5_BUCKETS // 2
    max_exact = half // 2
    n = jnp.abs(rel)
    nf = jnp.maximum(n, 1).astype(jnp.float32)
    large = max_exact + (jnp.log(nf / max_exact) / math.log(T5_MAX_DIST / max_exact)
                         * (half - max_exact)).astype(jnp.int32)
    large = jnp.minimum(large, half - 1)
    return jnp.where(rel > 0, half, 0) + jnp.where(n < max_exact, n, large)


def _band_bias(table_cols, n_kv, g, w, dil, geometry):
    tq, kw, shifts = geometry
    iq = jnp.arange(tq)[:, None]
    jk = jnp.arange(kw)[None, :]
    tiles = []
    for shift in shifts:
        rel = jk + shift - iq
        bucket = _t5_bucket(rel * dil)
        tb = jnp.zeros((tq, kw, table_cols.shape[1]), F32)
        for r in range(T5_BUCKETS):
            tb = jnp.where((bucket == r)[..., None], table_cols[r].astype(F32), tb)
        tb = jnp.where((jnp.abs(rel) <= w)[..., None], tb, NEG_INF)
        tiles.append(jnp.moveaxis(tb, -1, 0).reshape(n_kv, g * tq, kw))
    return jnp.stack(tiles)


def _na_bias(rpb, n_kv, g):
    col = jnp.arange(GRID_W)
    qc, kc = col[:, None], col[None, :]
    dc = jnp.clip(kc - qc + NA_COLS - 1, 0, 2 * NA_COLS - 2)
    cs = jnp.clip(qc - NA_COLS // 2, 0, GRID_W - NA_COLS)
    mask = (kc >= cs) & (kc < cs + NA_COLS)
    rpb = rpb.astype(F32)
    t = jnp.zeros(rpb.shape[:2] + dc.shape, F32)
    for c in range(2 * NA_COLS - 1):
        t = jnp.where(dc == c, rpb[:, :, c][:, :, None, None], t)
    t = jnp.where(mask, t, NEG_INF)
    cls = jnp.stack([t[:, o:o + NA_ROWS] for o in range(NA_ROWS)])
    cls = cls.reshape(NA_ROWS, n_kv, g, NA_ROWS, GRID_W, GRID_W)
    cls = jnp.transpose(cls, (1, 0, 2, 4, 3, 5))
    return cls.reshape(n_kv, NA_ROWS, g * GRID_W, NA_ROWS * GRID_W)


def _rope_tables(t):
    n_pairs = HEAD_DIM // 2
    n_freq = n_pairs // 2
    pos = jnp.arange(t)
    row = (pos // GRID_W).astype(jnp.float32)
    col = (pos % GRID_W).astype(jnp.float32)
    omega = ROPE_THETA ** (-(jnp.arange(n_freq, dtype=jnp.float32) * 2.0 / n_pairs))
    ang = jnp.concatenate([row[:, None] * omega, col[:, None] * omega], axis=-1)
    cos, sin = jnp.cos(ang), jnp.sin(ang)
    return jnp.concatenate([cos, cos], axis=-1), jnp.concatenate([-sin, sin], axis=-1)


def _deinterleave_heads(w, n_heads):
    lead = w.shape[:-1]
    w = w.reshape(lead + (n_heads, HEAD_DIM // 2, 2))
    return jnp.swapaxes(w, -1, -2).reshape(lead + (n_heads * HEAD_DIM,))


AB_KINDS = ("scale",) * 4 + ("qnorm",) * 4 + ("plain", "plain", "knorm", "plain_t")
CD_KINDS = ("scale",) * 4 + ("plain",) * 2 + ("scale",) * 6 + ("plain",) * 4


def kernel(x_prompt, x_sample, norm_ffn1, ffn1_w_in, ffn1_w_out, norm_mix, ab_w_in, ab_sink, ab_q_gain, ab_k_gain, ab_w_out, cd_w_in, cd_rpb, cd_w_out, norm_ffn2, ffn2_w_in, ffn2_w_out, t5_table, final_norm):
    (bp, t, d), bs = x_prompt.shape, x_sample.shape[0]
    assert x_sample.shape[1:] == (t, d)
    b = bp + bs
    depth = norm_ffn1.shape[0]
    x = (x_prompt.reshape(bp * t, d), x_sample.reshape(bs * t, d))

    ffn1_w_in, ffn1_w_out = ffn1_w_in.astype(BF16), ffn1_w_out.astype(BF16)
    ffn2_w_in, ffn2_w_out = ffn2_w_in.astype(BF16), ffn2_w_out.astype(BF16)
    qa_w, kva_w = A_HEADS * HEAD_DIM, 2 * A_KV * HEAD_DIM
    qb_lo = qa_w + kva_w
    qb_hi = qb_lo + B_HEADS * HEAD_DIM
    kb_hi = qb_hi + B_KV * HEAD_DIM
    ab_w_in = jnp.concatenate([
        ab_w_in[..., :qa_w],
        _deinterleave_heads(ab_w_in[..., qb_lo:qb_hi], B_HEADS),
        ab_w_in[..., qa_w:qb_lo],
        _deinterleave_heads(ab_w_in[..., qb_hi:kb_hi], B_KV),
        ab_w_in[..., kb_hi:]], axis=-1).astype(BF16)
    ab_q_gain = _deinterleave_heads(ab_q_gain, 1)
    ab_k_gain = _deinterleave_heads(ab_k_gain, 1)
    ab_w_out, cd_w_in, cd_w_out = ab_w_out.astype(BF16), cd_w_in.astype(BF16), cd_w_out.astype(BF16)

    cos, sin = _rope_tables(t)
    ga, gc = A_HEADS // A_KV, C_HEADS // C_KV
    geo_a = _band_geometry(t, A_WINDOW)
    bias_a = _band_bias(t5_table[:, :A_HEADS], A_KV, ga, A_WINDOW, 1, geo_a)
    d_groups = []
    for gi, (win, dil) in enumerate(D_PAIRS):
        hs = win // (2 * dil)
        geo = _band_geometry(t // dil, hs)
        col0 = A_HEADS + gi * D_SLOTS
        d_groups.append((dil, hs, geo, _band_bias(t5_table[:, col0:col0 + D_SLOTS], D_SLOTS, 1, hs, dil, geo)))

    a_q = 0
    a_k = qa_w + B_HEADS * HEAD_DIM
    a_v = a_k + A_KV * HEAD_DIM
    b_k = a_v + A_KV * HEAD_DIM
    c_q, c_k, c_v = 0, C_HEADS * HEAD_DIM, (C_HEADS + C_KV) * HEAD_DIM
    d_q = (C_HEADS + 2 * C_KV) * HEAD_DIM
    d_k = d_q + D_HEADS * HEAD_DIM
    d_v = d_k + D_SLOTS * HEAD_DIM
    dw = D_SLOTS * HEAD_DIM
    sub_dils = tuple(dil for _, dil in D_PAIRS if dil > 1)
    sub_plan = {}
    for gi, (_, dil) in enumerate(D_PAIRS):
        if dil > 1:
            si = sub_dils.index(dil)
            for src, dst in ((d_q + gi * dw, 0), (d_k, dw), (d_v, 2 * dw)):
                for off in range(0, dw, MXU_COLS):
                    sub_plan.setdefault((src + off) // MXU_COLS, []).append((si, dst + off))

    for l in range(depth):
        i = l // 2
        x = _ffn(x, norm_ffn1[l], ffn1_w_in, ffn1_w_out, l)
        if l % 2 == 0:
            qkv, qbt, vbt = _inproj(x, norm_mix[l], ab_w_in, i, AB_KINDS, t,
                                    (ab_q_gain[i], ab_k_gain[i], cos, sin))
            qkv = qkv.reshape(b, t, -1)
            o_a = _band_attention(qkv, a_q, qkv, a_k, qkv, a_v, bias_a, A_KV, ga, A_WINDOW, geo_a,
                                  sink=ab_sink[i])
            o_b = _dense_attention(qbt, qkv, vbt, b_k, B_KV, B_HEADS // B_KV)
            x = _outproj_ab(x, o_a.reshape(b * t, -1), o_b.reshape(b * t, -1), ab_w_out, i)
        else:
            qkv, *subs = _inproj(x, norm_mix[l], cd_w_in, i, CD_KINDS, t,
                                 sub_args=(sub_dils, 3 * dw, sub_plan))
            qkv = qkv.reshape(b, t, -1)
            o_c = _na_attention(qkv, c_q, c_k, c_v, _na_bias(cd_rpb[i], C_KV, gc), C_KV, gc)
            outs, lses = [], []
            for gi, (dil, hs, geo, bias_d) in enumerate(d_groups):
                if dil == 1:
                    o_g, lse_g = _band_attention(qkv, d_q + gi * dw, qkv, d_k, qkv, d_v, bias_d,
                                                 D_SLOTS, 1, hs, geo, want_lse=True, out_dtype=F32)
                    o_g, lse_g = o_g.reshape(b * t, dw), lse_g.reshape(b * t, dw)
                else:
                    sub = subs[sub_dils.index(dil)].reshape(b * dil, t // dil, 3 * dw)
                    o_g, lse_g = _band_attention(sub, 0, sub, dw, sub, 2 * dw, bias_d, D_SLOTS, 1, hs, geo,
                                                 want_lse=True, out_dtype=F32, dil=dil)
                outs.append(o_g)
                lses.append(lse_g)
            x = _outproj_cd(x, o_c.reshape(b * t, -1), outs, lses, [g[0] for g in d_groups], t, cd_w_out, i)
        last = l == depth - 1
        x = _ffn(x, norm_ffn2[l], ffn2_w_in, ffn2_w_out, l, final_gain=final_norm if last else None,
                 out_rows=(bp * t, bs * t) if last else None)

    return (x[0].reshape(bp, t, d), x[1].reshape(bs, t, d))
```

```python
import functools
import math

import jax
import jax.numpy as jnp
import numpy as np
from jax import lax
from jax.experimental import pallas as pl
from jax.experimental.pallas import tpu as pltpu

HEAD_DIM = 128
GRID_W = 64
A_HEADS, A_KV, A_WINDOW = 8, 2, 128
B_HEADS, B_KV = 8, 2
ROPE_THETA = 10000.0
C_HEADS, C_KV = 8, 2
NA_ROWS, NA_COLS = 8, 16
D_PAIRS = ((128, 1), (512, 4), (2048, 16))
D_SLOTS = 4
D_HEADS = D_SLOTS * len(D_PAIRS)
T5_BUCKETS = 32
T5_MAX_DIST = 2048
NORM_EPS = 1e-6
NEG_INF = -1e30
ATTN_SCALE = HEAD_DIM ** -0.5
LOG2_E = math.log2(math.e)

LANES = 128
SUBLANES = 8
MXU_COLS = 256
FFN_ROW_TILE = 1024
FFN_FF_TILE = 2 * MXU_COLS
DENSE_KEY_CHUNK = 2 * MXU_COLS
BAND_TILES_PER_STEP = (4, 2, 1)
NA_ROW_UNROLL = 8
MIB = 1024 * 1024
VMEM_LIMIT_BYTES = 56 * MIB

BF16 = jnp.bfloat16
F32 = jnp.float32


def _params(semantics):
    return pltpu.CompilerParams(dimension_semantics=semantics, vmem_limit_bytes=VMEM_LIMIT_BYTES)


def _resident_weight(shape, layer):
    return pl.BlockSpec((None,) + shape, lambda *_: (layer,) + (0,) * len(shape),
                        pipeline_mode=pl.Buffered(1))


def _rms(x, gain):
    ms = jnp.mean(x * x, axis=-1, keepdims=True)
    return x * lax.rsqrt(ms + NORM_EPS) * gain


def _ffn_body(*refs, n_ff, n_in, n_out, tiles_a, final_norm):
    refs = list(refs)
    x_refs = [refs.pop(0) for _ in range(n_in)]
    gain_ref, wg_ref, wu_ref, wo_ref = [refs.pop(0) for _ in range(4)]
    fgain_ref = refs.pop(0) if final_norm else None
    o_refs = [refs.pop(0) for _ in range(n_out)]
    h_scr = refs.pop(0)
    acc_ref = refs.pop(0) if n_out == 2 else o_refs[0]
    i, j = pl.program_id(0), pl.program_id(1)

    def start(x_ref):
        x = x_ref[...]
        h_scr[...] = _rms(x, gain_ref[...]).astype(BF16)
        acc_ref[...] = x

    if n_in == 1:
        pl.when(j == 0)(lambda: start(x_refs[0]))
    else:
        pl.when((j == 0) & (i < tiles_a))(lambda: start(x_refs[0]))
        pl.when((j == 0) & (i >= tiles_a))(lambda: start(x_refs[1]))

    h = h_scr[...]
    g = jnp.dot(h, wg_ref[...], preferred_element_type=F32)
    u = jnp.dot(h, wu_ref[...], preferred_element_type=F32)
    a = (g * jax.nn.sigmoid(g) * (0.5 * u)).astype(BF16)
    acc_ref[...] += jnp.dot(a, wo_ref[...], preferred_element_type=F32)

    def finish(o_ref):
        y = acc_ref[...]
        o_ref[...] = _rms(y, fgain_ref[...]) if final_norm else y

    if n_out == 2:
        pl.when((j == n_ff - 1) & (i < tiles_a))(lambda: finish(o_refs[0]))
        pl.when((j == n_ff - 1) & (i >= tiles_a))(lambda: finish(o_refs[1]))
    elif final_norm:
        pl.when(j == n_ff - 1)(lambda: finish(o_refs[0]))


def _ffn(xs, gain, w_in, w_out, layer, final_gain=None, out_rows=None, *, tm=FFN_ROW_TILE):
    xs = tuple(xs) if isinstance(xs, (tuple, list)) else (xs,)
    d = xs[0].shape[1]
    n = sum(x.shape[0] for x in xs)
    f = w_out.shape[1]
    tf = math.gcd(FFN_FF_TILE, f)
    n_ff = f // tf
    if len(xs) == 2 or out_rows is not None:
        tm //= 2
    tm = math.gcd(tm, *(x.shape[0] for x in xs))
    assert w_in.shape[1:] == (d, 2 * f) and w_out.shape[2] == d
    n_in, n_out = len(xs), 1 if out_rows is None else 2
    rows_a = xs[0].shape[0] if n_in == 2 else (out_rows[0] if n_out == 2 else n)
    assert n_in == 1 or n_out == 1 or out_rows[0] == rows_a
    tiles_a = rows_a // tm
    final_norm = final_gain is not None
    seg_a = lambda i, j: (jnp.minimum(i, tiles_a - 1), 0)
    seg_b = lambda i, j: (jnp.maximum(i - tiles_a, 0), 0)
    whole = lambda i, j: (i, 0)
    in_specs = [pl.BlockSpec((tm, d), m) for m in ((whole,) if n_in == 1 else (seg_a, seg_b))]
    in_specs += [
        pl.BlockSpec((1, d), lambda i, j: (0, 0)),
        pl.BlockSpec((None, d, tf), lambda i, j: (layer, 0, j)),
        pl.BlockSpec((None, d, tf), lambda i, j: (layer, 0, j + n_ff)),
        pl.BlockSpec((None, tf, d), lambda i, j: (layer, j, 0)),
    ]
    args = list(xs) + [gain.reshape(1, d), w_in, w_in, w_out]
    if final_norm:
        in_specs.append(pl.BlockSpec((1, d), lambda i, j: (0, 0)))
        args.append(final_gain.reshape(1, d))
    scratch = [pltpu.VMEM((tm, d), BF16)]
    if n_out == 1:
        out_specs = pl.BlockSpec((tm, d), whole)
        out_shape = jax.ShapeDtypeStruct((n, d), F32)
    else:
        assert out_rows[0] % tm == 0 and sum(out_rows) == n
        out_specs = [pl.BlockSpec((tm, d), seg_a), pl.BlockSpec((tm, d), seg_b)]
        out_shape = [jax.ShapeDtypeStruct((r, d), F32) for r in out_rows]
        scratch.append(pltpu.VMEM((tm, d), F32))
    row_sem = "parallel" if n_in == 1 and n_out == 1 else "arbitrary"
    return pl.pallas_call(
        functools.partial(_ffn_body, n_ff=n_ff, n_in=n_in, n_out=n_out, tiles_a=tiles_a,
                          final_norm=final_norm),
        grid=(n // tm, n_ff),
        in_specs=in_specs,
        out_specs=out_specs,
        out_shape=out_shape,
        scratch_shapes=scratch,
        compiler_params=_params((row_sem, "arbitrary")),
        name="ffn",
    )(*args)


def _rope_norm(r, gain, cos, sin):
    y = _rms(r, gain)
    return y * cos + pltpu.roll(y, HEAD_DIM // 2, axis=1) * sin


def _inproj_body(*refs, kinds, rope, sub_plan, dils):
    if rope:
        x_ref, gain_ref, w_ref, qg_ref, kg_ref, cos_ref, sin_ref, o_ref, qt_ref, vt_ref = refs
    else:
        x_ref, gain_ref, w_ref, o_ref = refs[:4]
        sub_refs, slab_scr = refs[4:4 + len(dils)], refs[-1] if dils else None
    tm = x_ref.shape[0]
    h = _rms(x_ref[...], gain_ref[...]).astype(BF16)
    n_qt = 0
    for c, kind in enumerate(kinds):
        lo = c * MXU_COLS
        r = jnp.dot(h, w_ref[:, lo:lo + MXU_COLS], preferred_element_type=F32)
        if kind == "scale":
            r = r * ATTN_SCALE
        elif kind in ("qnorm", "knorm"):
            hg = qg_ref[...] if kind == "qnorm" else kg_ref[...]
            cos = cos_ref[...]
            sin = sin_ref[...]
            parts = []
            for hh in range(MXU_COLS // HEAD_DIM):
                y = _rope_norm(r[:, hh * HEAD_DIM:(hh + 1) * HEAD_DIM], hg, cos, sin)
                if kind == "qnorm":
                    y = y * (ATTN_SCALE * LOG2_E)
                parts.append(y)
            r = jnp.concatenate(parts, axis=1)
            if kind == "qnorm":
                qt_ref[0, n_qt * MXU_COLS:(n_qt + 1) * MXU_COLS, :] = r.T.astype(BF16)
                n_qt += 1
        elif kind == "plain_t":
            vt_ref[0] = r.T.astype(BF16)
        o_ref[:, lo:lo + MXU_COLS] = r.astype(BF16)
        if c in sub_plan:
            heads = range(MXU_COLS // HEAD_DIM)
            for hh in heads:
                slab_scr[hh] = r[:, hh * HEAD_DIM:(hh + 1) * HEAD_DIM]
            for si, col0 in sub_plan[c]:
                d = dils[si]
                for rho in range(d):
                    for hh in heads:
                        part = slab_scr[hh, pl.ds(rho, tm // d, stride=d), :]
                        cols = slice(col0 + hh * HEAD_DIM, col0 + (hh + 1) * HEAD_DIM)
                        sub_refs[si][0, rho, :, cols] = part.astype(BF16)


def _inproj(x, gain, w, layer, kinds, seq_len, rope_args=None, sub_args=None, *, tm=512):
    n, d = x.shape
    c = w.shape[2]
    assert n % tm == 0 and seq_len % tm == 0 and c == MXU_COLS * len(kinds)
    rope = rope_args is not None
    assert not (rope and sub_args)
    dils, sub_width, sub_plan = sub_args if sub_args else ((), 0, {})
    scratch = [pltpu.VMEM((MXU_COLS // HEAD_DIM, tm, HEAD_DIM), F32)] if dils else []
    in_specs = [
        pl.BlockSpec((tm, d), lambda i: (i, 0)),
        pl.BlockSpec((1, d), lambda i: (0, 0)),
        _resident_weight((d, c), layer),
    ]
    args = [x, gain.reshape(1, d), w]
    if rope:
        q_gain, k_gain, cos, sin = rope_args
        tiles_per_seq = seq_len // tm
        in_specs += [
            pl.BlockSpec((1, HEAD_DIM), lambda i: (0, 0)),
            pl.BlockSpec((1, HEAD_DIM), lambda i: (0, 0)),
            pl.BlockSpec((tm, HEAD_DIM), lambda i: (i % tiles_per_seq, 0)),
            pl.BlockSpec((tm, HEAD_DIM), lambda i: (i % tiles_per_seq, 0)),
        ]
        args += [q_gain.reshape(1, HEAD_DIM), k_gain.reshape(1, HEAD_DIM), cos, sin]
    out_specs = pl.BlockSpec((tm, c), lambda i: (i, 0))
    out_shape = jax.ShapeDtypeStruct((n, c), BF16)
    if rope:
        assert kinds.count("plain_t") == 1
        qt_rows = MXU_COLS * kinds.count("qnorm")
        by_seq = lambda i: (i // tiles_per_seq, 0, i % tiles_per_seq)
        out_specs = [out_specs, pl.BlockSpec((1, qt_rows, tm), by_seq),
                     pl.BlockSpec((1, MXU_COLS, tm), by_seq)]
        out_shape = [out_shape, jax.ShapeDtypeStruct((n // seq_len, qt_rows, seq_len), BF16),
                     jax.ShapeDtypeStruct((n // seq_len, MXU_COLS, seq_len), BF16)]
    if dils:
        tiles_per_seq = seq_len // tm
        assert all(tm % (dl * 16) == 0 for dl in dils)
        out_specs = [out_specs] + [
            pl.BlockSpec((1, dl, tm // dl, sub_width), lambda i: (i // tiles_per_seq, 0, i % tiles_per_seq, 0))
            for dl in dils]
        out_shape = [out_shape] + [
            jax.ShapeDtypeStruct((n // seq_len, dl, seq_len // dl, sub_width), BF16) for dl in dils]
    return pl.pallas_call(
        functools.partial(_inproj_body, kinds=tuple(kinds), rope=rope, sub_plan=sub_plan, dils=tuple(dils)),
        grid=(n // tm,),
        in_specs=in_specs,
        out_specs=out_specs,
        out_shape=out_shape,
        scratch_shapes=scratch,
        compiler_params=_params(("parallel",)),
        name="inproj",
    )(*args)


def _outproj_ab_body(x_ref, a_ref, b_ref, w_ref, o_ref):
    ka = a_ref.shape[1]
    acc = jnp.dot(a_ref[...], w_ref[0:ka, :], preferred_element_type=F32)
    acc = acc + jnp.dot(b_ref[...], w_ref[ka:, :], preferred_element_type=F32)
    o_ref[...] = x_ref[...] + acc


def _outproj_ab(x, oa, ob, w, layer, *, tm=512):
    n, d = x.shape
    ka, kb = oa.shape[1], ob.shape[1]
    assert n % tm == 0 and w.shape[1:] == (ka + kb, d)
    return pl.pallas_call(
        _outproj_ab_body,
        grid=(n // tm,),
        in_specs=[
            pl.BlockSpec((tm, d), lambda i: (i, 0)),
            pl.BlockSpec((tm, ka), lambda i: (i, 0)),
            pl.BlockSpec((tm, kb), lambda i: (i, 0)),
            _resident_weight((ka + kb, d), layer),
        ],
        out_specs=pl.BlockSpec((tm, d), lambda i: (i, 0)),
        out_shape=jax.ShapeDtypeStruct((n, d), F32),
        compiler_params=_params(("parallel",)),
        name="outproj_ab",
    )(x, oa, ob, w)


def _outproj_cd_body(*refs, dils, kd):
    ng = len(dils)
    x_ref, c_ref = refs[:2]
    o_refs, l_refs = refs[2:2 + ng], refs[2 + ng:2 + 2 * ng]
    w_ref, o_ref, row_scr = refs[2 + 2 * ng:]
    tm = x_ref.shape[0]
    heads = kd // HEAD_DIM

    def token_rows(ref, d, slot, width):
        if d == 1:
            return ref[...]
        for rho in range(d):
            for s in range(width // LANES):
                lo = rho * width + s * LANES
                row_scr[slot, s, pl.ds(rho, tm // d, stride=d), :] = ref[0, :, lo:lo + LANES]
        return jnp.concatenate([row_scr[slot, s] for s in range(width // LANES)], axis=1)

    outs = [token_rows(r, d, 2 * i, kd) for i, (r, d) in enumerate(zip(o_refs, dils))]
    lses = [token_rows(r, d, 2 * i + 1, LANES) for i, (r, d) in enumerate(zip(l_refs, dils))]
    m = functools.reduce(jnp.maximum, lses)
    es = [jnp.exp(l - m) for l in lses]
    den = functools.reduce(lambda a, b: a + b, es)
    wts = [e / den for e in es]
    parts = []
    for hd in range(heads):
        lane = hd * (LANES // heads)
        cols = slice(hd * HEAD_DIM, (hd + 1) * HEAD_DIM)
        terms = [jnp.broadcast_to(wt[:, lane:lane + 1], (tm, HEAD_DIM)) * o[:, cols] for wt, o in zip(wts, outs)]
        parts.append(functools.reduce(lambda a, b: a + b, terms))
    od = jnp.concatenate(parts, axis=1)
    kc = c_ref.shape[1]
    acc = jnp.dot(c_ref[...], w_ref[0:kc, :], preferred_element_type=F32)
    acc = acc + jnp.dot(od.astype(BF16), w_ref[kc:, :], preferred_element_type=F32)
    o_ref[...] = x_ref[...] + acc


def _outproj_cd(x, oc, outs, lses, dils, seq_len, w, layer, *, tm=512):
    n, d = x.shape
    kc = oc.shape[1]
    kd = w.shape[1] - kc
    assert n % tm == 0 and seq_len % tm == 0 and all(tm % (dl * SUBLANES) == 0 for dl in dils)
    tiles_per_seq = seq_len // tm
    row = lambda i: (i, 0)
    by_seq = lambda i: (i // tiles_per_seq, i % tiles_per_seq, 0)

    def g_specs(width):
        return [pl.BlockSpec((tm, width), row) if dl == 1 else pl.BlockSpec((1, tm // dl, dl * width), by_seq)
                for dl in dils]

    return pl.pallas_call(
        functools.partial(_outproj_cd_body, dils=tuple(dils), kd=kd),
        grid=(n // tm,),
        in_specs=[pl.BlockSpec((tm, d), row), pl.BlockSpec((tm, kc), row)] + g_specs(kd) + g_specs(LANES)
        + [_resident_weight((kc + kd, d), layer)],
        out_specs=pl.BlockSpec((tm, d), row),
        out_shape=jax.ShapeDtypeStruct((n, d), F32),
        scratch_shapes=[pltpu.VMEM((2 * len(dils), kd // HEAD_DIM, tm, HEAD_DIM), F32)],
        compiler_params=_params(("parallel",)),
        name="outproj_cd",
    )(x, oc, *outs, *lses, w)


def _stack_heads(q, g):
    return jnp.concatenate([q[:, i * HEAD_DIM:(i + 1) * HEAD_DIM] for i in range(g)], axis=0)


def _softmax_pv(s, v):
    m = jnp.max(s, axis=-1, keepdims=True)
    p = jnp.exp(s - m)
    l = jnp.sum(p, axis=-1, keepdims=True)
    o = jnp.dot(p.astype(BF16), v, preferred_element_type=F32) / l
    return o, m, l


def _qk(q, k):
    return lax.dot_general(q, k, (((1,), (1,)), ((), ())), preferred_element_type=F32)


def _dense_body(qt_ref, k_ref, vt_ref, o_ref, s_scr, p_scr, m_scr, l_scr, *, g):
    step = pl.program_id(0)
    tq = qt_ref.shape[2] // 2
    t = k_ref.shape[1]
    gq = g * tq
    kc = min(DENSE_KEY_CHUNK, t)

    @pl.when(step == 0)
    def _():
        s_scr[1] = jnp.zeros(s_scr.shape[1:], s_scr.dtype)
        m_scr[1] = jnp.zeros(m_scr.shape[1:], m_scr.dtype)
        p_scr[...] = jnp.zeros(p_scr.shape, p_scr.dtype)
        l_scr[...] = jnp.ones(l_scr.shape, l_scr.dtype)

    def fold(x, op):
        return op(x.reshape(x.shape[0] // SUBLANES, SUBLANES, x.shape[1]), axis=0)

    for cur in range(2):
        prev = 1 - cur
        toks = slice(cur * tq, (cur + 1) * tq)
        qt = jnp.concatenate([qt_ref[0, i * HEAD_DIM:(i + 1) * HEAD_DIM, toks] for i in range(g)], axis=1)
        m = jnp.max(m_scr[prev], axis=0, keepdims=True)
        ot = jnp.zeros((HEAD_DIM, gq), F32)
        l_acc = jnp.zeros((SUBLANES, gq), F32)
        m_acc = None
        for c in range(t // kc):
            keys = slice(c * kc, (c + 1) * kc)
            p = jnp.exp2(s_scr[prev, keys, :] - m)
            l_acc = l_acc + fold(p, jnp.sum)
            p_scr[prev, keys, :] = p.astype(BF16)
        for c in range(t // kc):
            keys = slice(c * kc, (c + 1) * kc)
            ot = ot + jnp.dot(vt_ref[0, :, keys], p_scr[cur, keys, :], preferred_element_type=F32)
            s = jnp.dot(k_ref[0, keys, :], qt, preferred_element_type=F32)
            s_scr[cur, keys, :] = s
            m_chunk = fold(s, jnp.max)
            m_acc = m_chunk if m_acc is None else jnp.maximum(m_acc, m_chunk)
        ot = ot / jnp.sum(l_scr[cur], axis=0, keepdims=True)
        l_scr[prev] = l_acc
        m_scr[cur] = m_acc
        for i in range(g):
            o_ref[0, toks, i * HEAD_DIM:(i + 1) * HEAD_DIM] = ot[:, i * tq:(i + 1) * tq].T.astype(o_ref.dtype)


def _dense_attention(qt, qkv, vt, k_col, n_kv, g, *, tq=128):
    b, t, _ = qkv.shape
    gw = g * HEAD_DIM
    n_pairs = t // (2 * tq)
    assert t % (2 * tq) == 0 and k_col % HEAD_DIM == 0
    assert qt.shape == (b, n_kv * gw, t) and vt.shape == (b, n_kv * HEAD_DIM, t)
    kb = k_col // HEAD_DIM
    total = b * n_kv * n_pairs

    def where(f):
        return f // (n_kv * n_pairs), (f // n_pairs) % n_kv, f % n_pairs

    def q_map(f):
        n, h, k = where(jnp.minimum(f, total - 1))
        return n, h, k

    def k_map(f):
        n, h, _ = where(jnp.minimum(f, total - 1))
        return n, 0, kb + h

    def v_map(f):
        n, h, _ = where(jnp.maximum(f - 1, 0))
        return n, h, 0

    def o_map(f):
        n, h, k = where(jnp.maximum(f - 1, 0))
        return n, k, h

    return pl.pallas_call(
        functools.partial(_dense_body, g=g),
        grid=(total + 1,),
        in_specs=[
            pl.BlockSpec((1, gw, 2 * tq), q_map),
            pl.BlockSpec((1, t, HEAD_DIM), k_map),
            pl.BlockSpec((1, HEAD_DIM, t), v_map),
        ],
        out_specs=pl.BlockSpec((1, 2 * tq, gw), o_map),
        out_shape=jax.ShapeDtypeStruct((b, t, n_kv * gw), BF16),
        scratch_shapes=[pltpu.VMEM((2, t, g * tq), F32), pltpu.VMEM((2, t, g * tq), BF16),
                        pltpu.VMEM((2, SUBLANES, g * tq), F32), pltpu.VMEM((2, SUBLANES, g * tq), F32)],
        compiler_params=_params(("arbitrary",)),
        name="dense_attn",
    )(qt, qkv, vt)


def _band_body(*refs, n_kv, g, w, seq, tq, kw, n_cases, has_sink, want_lse, side_by_side):
    refs = list(refs)
    q_ref, k_ref, v_ref, bias_ref = refs[:4]
    refs = refs[4:]
    sink_ref = refs.pop(0) if has_sink else None
    o_ref = refs.pop(0)
    lse_ref = refs.pop(0) if want_lse else None
    gw = g * HEAD_DIM
    qw = n_kv * gw
    nq = seq // tq
    for sq, j in [(a, c) for a in range(q_ref.shape[0]) for c in range(q_ref.shape[1] // tq)]:
        qi = pl.program_id(1) * (q_ref.shape[1] // tq) + j
        kstart = pl.multiple_of(jnp.clip(qi * tq - w, 0, seq - kw), 64)
        case = 0 if n_cases == 1 else jnp.where(qi == 0, 0, jnp.where(qi == nq - 1, 2, 1))
        trows = slice(j * tq, (j + 1) * tq)
        osq, ocol0, lcol0 = (0, sq * qw, sq * LANES) if side_by_side else (sq, 0, 0)
        lse_tile = jnp.zeros((tq, LANES), F32)
        head_of_lane = lax.broadcasted_iota(jnp.int32, (tq, LANES), 1) // (LANES // (n_kv * g))
        for h in range(n_kv):
            k = k_ref[sq, pl.ds(kstart, kw), h * HEAD_DIM:(h + 1) * HEAD_DIM]
            v = v_ref[sq, pl.ds(kstart, kw), h * HEAD_DIM:(h + 1) * HEAD_DIM]
            q = _stack_heads(q_ref[sq, trows, h * gw:(h + 1) * gw], g)
            s = _qk(q, k) + bias_ref[case, h]
            o, m, l = _softmax_pv(s, v)
            lse = m + jnp.log(l)
            for i in range(g):
                rows = slice(i * tq, (i + 1) * tq)
                cols = slice(h * gw + i * HEAD_DIM, h * gw + (i + 1) * HEAD_DIM)
                oi = o[rows]
                if has_sink:
                    oi = oi * jax.nn.sigmoid(lse[rows] - sink_ref[h * g + i])
                o_ref[osq, trows, ocol0 + cols.start:ocol0 + cols.stop] = oi.astype(o_ref.dtype)
                if want_lse:
                    lse_tile = jnp.where(head_of_lane == h * g + i, lse[rows], lse_tile)
        if want_lse:
            lse_ref[osq, trows, lcol0:lcol0 + LANES] = lse_tile


def _band_geometry(seq, w):
    if seq <= 4 * w:
        return seq, seq, (0,)
    tq = 2 * w if w < 128 else w
    tq = min(tq, 256)
    kw = tq + 2 * w
    assert seq % tq == 0 and seq >= kw
    return tq, kw, (0, -w, -2 * w)


def _band_attention(q_arr, q_col, k_arr, k_col, v_arr, v_col, bias, n_kv, g, w, geometry,
                    sink=None, want_lse=False, out_dtype=BF16, dil=1):
    n, seq, _ = q_arr.shape
    tq, kw, shifts = geometry
    nq = seq // tq
    gw = g * HEAD_DIM
    qw, kvw = n_kv * gw, n_kv * HEAD_DIM
    assert q_col % qw == 0 and k_col % kvw == 0 and v_col % kvw == 0
    qb, kb, vb = q_col // qw, k_col // kvw, v_col // kvw
    tps = next(c for c in BAND_TILES_PER_STEP if nq % c == 0)
    nq_steps = nq // tps
    sps = next(c for c in BAND_TILES_PER_STEP if n % c == 0) if nq == 1 else 1
    in_specs = [
        pl.BlockSpec((sps, tps * tq, qw), lambda b, i: (b, i, qb)),
        pl.BlockSpec((sps, seq, kvw), lambda b, i: (b, 0, kb)),
        pl.BlockSpec((sps, seq, kvw), lambda b, i: (b, 0, vb)),
        pl.BlockSpec((len(shifts), n_kv, g * tq, kw), lambda b, i: (0, 0, 0, 0)),
    ]
    args = [q_arr, k_arr, v_arr, bias]
    if sink is not None:
        in_specs.append(pl.BlockSpec(memory_space=pltpu.SMEM))
        args.append(sink.astype(F32))
    def out_like(width, dtype):
        if dil == 1:
            return (pl.BlockSpec((sps, tps * tq, width), lambda b, i: (b, i, 0)),
                    jax.ShapeDtypeStruct((n, seq, width), dtype))
        return (pl.BlockSpec((1, tps * tq, sps * width),
                             lambda b, i: ((b * sps) // dil, i, ((b * sps) % dil) // sps)),
                jax.ShapeDtypeStruct((n // dil, seq, dil * width), dtype))

    assert dil == 1 or (n % dil == 0 and dil % sps == 0)
    out_specs, out_shape = out_like(qw, out_dtype)
    if want_lse:
        assert LANES % (n_kv * g) == 0
        l_spec, l_shape = out_like(LANES, F32)
        out_specs, out_shape = [out_specs, l_spec], [out_shape, l_shape]
    return pl.pallas_call(
        functools.partial(_band_body, n_kv=n_kv, g=g, w=w, seq=seq, tq=tq, kw=kw, n_cases=len(shifts),
                          has_sink=sink is not None, want_lse=want_lse, side_by_side=dil > 1),
        grid=(n // sps, nq_steps),
        in_specs=in_specs,
        out_specs=out_specs,
        out_shape=out_shape,
        compiler_params=_params(("parallel", "arbitrary")),
        name="band_attn",
    )(*args)


def _na_body(q_ref, k_ref, v_ref, bias_ref, o_ref, *, g, rows, rows_per_step):
    rb = pl.program_id(2)
    kwin = NA_ROWS * GRID_W

    def one_row(rr, carry):
        r = rb * rows_per_step + rr
        rs = jnp.clip(r - NA_ROWS // 2, 0, rows - NA_ROWS)
        off = rs - r + NA_ROWS - 1
        kstart = pl.multiple_of(rs * GRID_W, GRID_W)
        qstart = pl.multiple_of(rr * GRID_W, GRID_W)
        k = k_ref[0, pl.ds(kstart, kwin), :]
        v = v_ref[0, pl.ds(kstart, kwin), :]
        q = _stack_heads(q_ref[0, pl.ds(qstart, GRID_W), :], g)
        s = _qk(q, k) + bias_ref[0, off]
        o, _, _ = _softmax_pv(s, v)
        for i in range(g):
            o_ref[0, pl.ds(qstart, GRID_W), i * HEAD_DIM:(i + 1) * HEAD_DIM] = (
                o[i * GRID_W:(i + 1) * GRID_W].astype(o_ref.dtype))
        return carry

    lax.fori_loop(0, rows_per_step, one_row, 0, unroll=NA_ROW_UNROLL)


def _na_attention(qkv, q_col, k_col, v_col, bias, n_kv, g, *, rows_per_step=8):
    b, t, _ = qkv.shape
    rows = t // GRID_W
    gw = g * HEAD_DIM
    assert rows % rows_per_step == 0 and rows >= NA_ROWS
    qb, kb, vb = q_col // gw, k_col // HEAD_DIM, v_col // HEAD_DIM
    tq = rows_per_step * GRID_W
    return pl.pallas_call(
        functools.partial(_na_body, g=g, rows=rows, rows_per_step=rows_per_step),
        grid=(b, n_kv, rows // rows_per_step),
        in_specs=[
            pl.BlockSpec((1, tq, gw), lambda n, h, i: (n, i, qb + h)),
            pl.BlockSpec((1, t, HEAD_DIM), lambda n, h, i: (n, 0, kb + h)),
            pl.BlockSpec((1, t, HEAD_DIM), lambda n, h, i: (n, 0, vb + h)),
            pl.BlockSpec((1, NA_ROWS, g * GRID_W, NA_ROWS * GRID_W), lambda n, h, i: (h, 0, 0, 0)),
        ],
        out_specs=pl.BlockSpec((1, tq, gw), lambda n, h, i: (n, i, h)),
        out_shape=jax.ShapeDtypeStruct((b, t, n_kv * gw), BF16),
        compiler_params=_params(("parallel", "parallel", "arbitrary")),
        name="na_attn",
    )(qkv, qkv, qkv, bias)


def _t5_bucket(rel):
    half = T5_BUCKETS // 2
    max_exact = half // 2
    n = jnp.abs(rel)
    nf = jnp.maximum(n, 1).astype(jnp.float32)
    large = max_exact + (jnp.log(nf / max_exact) / math.log(T5_MAX_DIST / max_exact)
                         * (half - max_exact)).astype(jnp.int32)
    large = jnp.minimum(large, half - 1)
    return jnp.where(rel > 0, half, 0) + jnp.where(n < max_exact, n, large)


def _band_bias(table_cols, n_kv, g, w, dil, geometry):
    tq, kw, shifts = geometry
    iq = jnp.arange(tq)[:, None]
    jk = jnp.arange(kw)[None, :]
    tiles = []
    for shift in shifts:
        rel = jk + shift - iq
        bucket = _t5_bucket(rel * dil)
        tb = jnp.zeros((tq, kw, table_cols.shape[1]), F32)
        for r in range(T5_BUCKETS):
            tb = jnp.where((bucket == r)[..., None], table_cols[r].astype(F32), tb)
        tb = jnp.where((jnp.abs(rel) <= w)[..., None], tb, NEG_INF)
        tiles.append(jnp.moveaxis(tb, -1, 0).reshape(n_kv, g * tq, kw))
    return jnp.stack(tiles)


def _na_bias(rpb, n_kv, g):
    col = jnp.arange(GRID_W)
    qc, kc = col[:, None], col[None, :]
    dc = jnp.clip(kc - qc + NA_COLS - 1, 0, 2 * NA_COLS - 2)
    cs = jnp.clip(qc - NA_COLS // 2, 0, GRID_W - NA_COLS)
    mask = (kc >= cs) & (kc < cs + NA_COLS)
    rpb = rpb.astype(F32)
    t = jnp.zeros(rpb.shape[:2] + dc.shape, F32)
    for c in range(2 * NA_COLS - 1):
        t = jnp.where(dc == c, rpb[:, :, c][:, :, None, None], t)
    t = jnp.where(mask, t, NEG_INF)
    cls = jnp.stack([t[:, o:o + NA_ROWS] for o in range(NA_ROWS)])
    cls = cls.reshape(NA_ROWS, n_kv, g, NA_ROWS, GRID_W, GRID_W)
    cls = jnp.transpose(cls, (1, 0, 2, 4, 3, 5))
    return cls.reshape(n_kv, NA_ROWS, g * GRID_W, NA_ROWS * GRID_W)


def _rope_tables(t):
    n_pairs = HEAD_DIM // 2
    n_freq = n_pairs // 2
    pos = jnp.arange(t)
    row = (pos // GRID_W).astype(jnp.float32)
    col = (pos % GRID_W).astype(jnp.float32)
    omega = ROPE_THETA ** (-(jnp.arange(n_freq, dtype=jnp.float32) * 2.0 / n_pairs))
    ang = jnp.concatenate([row[:, None] * omega, col[:, None] * omega], axis=-1)
    cos, sin = jnp.cos(ang), jnp.sin(ang)
    return jnp.concatenate([cos, cos], axis=-1), jnp.concatenate([-sin, sin], axis=-1)


def _deinterleave_heads(w, n_heads):
    lead = w.shape[:-1]
    w = w.reshape(lead + (n_heads, HEAD_DIM // 2, 2))
    return jnp.swapaxes(w, -1, -2).reshape(lead + (n_heads * HEAD_DIM,))


AB_KINDS = ("scale",) * 4 + ("qnorm",) * 4 + ("plain", "plain", "knorm", "plain_t")
CD_KINDS = ("scale",) * 4 + ("plain",) * 2 + ("scale",) * 6 + ("plain",) * 4


def kernel(x_prompt, x_sample, norm_ffn1, ffn1_w_in, ffn1_w_out, norm_mix, ab_w_in, ab_sink, ab_q_gain, ab_k_gain, ab_w_out, cd_w_in, cd_rpb, cd_w_out, norm_ffn2, ffn2_w_in, ffn2_w_out, t5_table, final_norm):
    (bp, t, d), bs = x_prompt.shape, x_sample.shape[0]
    assert x_sample.shape[1:] == (t, d)
    b = bp + bs
    depth = norm_ffn1.shape[0]
    x = (x_prompt.reshape(bp * t, d), x_sample.reshape(bs * t, d))

    ffn1_w_in, ffn1_w_out = ffn1_w_in.astype(BF16), ffn1_w_out.astype(BF16)
    ffn2_w_in, ffn2_w_out = ffn2_w_in.astype(BF16), ffn2_w_out.astype(BF16)
    qa_w, kva_w = A_HEADS * HEAD_DIM, 2 * A_KV * HEAD_DIM
    qb_lo = qa_w + kva_w
    qb_hi = qb_lo + B_HEADS * HEAD_DIM
    kb_hi = qb_hi + B_KV * HEAD_DIM
    ab_w_in = jnp.concatenate([
        ab_w_in[..., :qa_w],
        _deinterleave_heads(ab_w_in[..., qb_lo:qb_hi], B_HEADS),
        ab_w_in[..., qa_w:qb_lo],
        _deinterleave_heads(ab_w_in[..., qb_hi:kb_hi], B_KV),
        ab_w_in[..., kb_hi:]], axis=-1).astype(BF16)
    ab_q_gain = _deinterleave_heads(ab_q_gain, 1)
    ab_k_gain = _deinterleave_heads(ab_k_gain, 1)
    ab_w_out, cd_w_in, cd_w_out = ab_w_out.astype(BF16), cd_w_in.astype(BF16), cd_w_out.astype(BF16)

    cos, sin = _rope_tables(t)
    ga, gc = A_HEADS // A_KV, C_HEADS // C_KV
    geo_a = _band_geometry(t, A_WINDOW)
    bias_a = _band_bias(t5_table[:, :A_HEADS], A_KV, ga, A_WINDOW, 1, geo_a)
    d_groups = []
    for gi, (win, dil) in enumerate(D_PAIRS):
        hs = win // (2 * dil)
        geo = _band_geometry(t // dil, hs)
        col0 = A_HEADS + gi * D_SLOTS
        d_groups.append((dil, hs, geo, _band_bias(t5_table[:, col0:col0 + D_SLOTS], D_SLOTS, 1, hs, dil, geo)))

    a_q = 0
    a_k = qa_w + B_HEADS * HEAD_DIM
    a_v = a_k + A_KV * HEAD_DIM
    b_k = a_v + A_KV * HEAD_DIM
    c_q, c_k, c_v = 0, C_HEADS * HEAD_DIM, (C_HEADS + C_KV) * HEAD_DIM
    d_q = (C_HEADS + 2 * C_KV) * HEAD_DIM
    d_k = d_q + D_HEADS * HEAD_DIM
    d_v = d_k + D_SLOTS * HEAD_DIM
    dw = D_SLOTS * HEAD_DIM
    sub_dils = tuple(dil for _, dil in D_PAIRS if dil > 1)
    sub_plan = {}
    for gi, (_, dil) in enumerate(D_PAIRS):
        if dil > 1:
            si = sub_dils.index(dil)
            for src, dst in ((d_q + gi * dw, 0), (d_k, dw), (d_v, 2 * dw)):
                for off in range(0, dw, MXU_COLS):
                    sub_plan.setdefault((src + off) // MXU_COLS, []).append((si, dst + off))

    for l in range(depth):
        i = l // 2
        x = _ffn(x, norm_ffn1[l], ffn1_w_in, ffn1_w_out, l)
        if l % 2 == 0:
            qkv, qbt, vbt = _inproj(x, norm_mix[l], ab_w_in, i, AB_KINDS, t,
                                    (ab_q_gain[i], ab_k_gain[i], cos, sin))
            qkv = qkv.reshape(b, t, -1)
            o_a = _band_attention(qkv, a_q, qkv, a_k, qkv, a_v, bias_a, A_KV, ga, A_WINDOW, geo_a,
                                  sink=ab_sink[i])
            o_b = _dense_attention(qbt, qkv, vbt, b_k, B_KV, B_HEADS // B_KV)
            x = _outproj_ab(x, o_a.reshape(b * t, -1), o_b.reshape(b * t, -1), ab_w_out, i)
        else:
            qkv, *subs = _inproj(x, norm_mix[l], cd_w_in, i, CD_KINDS, t,
                                 sub_args=(sub_dils, 3 * dw, sub_plan))
            qkv = qkv.reshape(b, t, -1)
            o_c = _na_attention(qkv, c_q, c_k, c_v, _na_bias(cd_rpb[i], C_KV, gc), C_KV, gc)
            outs, lses = [], []
            for gi, (dil, hs, geo, bias_d) in enumerate(d_groups):
                if dil == 1:
                    o_g, lse_g = _band_attention(qkv, d_q + gi * dw, qkv, d_k, qkv, d_v, bias_d,
                                                 D_SLOTS, 1, hs, geo, want_lse=True, out_dtype=F32)
                    o_g, lse_g = o_g.reshape(b * t, dw), lse_g.reshape(b * t, LANES)
                else:
                    sub = subs[sub_dils.index(dil)].reshape(b * dil, t // dil, 3 * dw)
                    o_g, lse_g = _band_attention(sub, 0, sub, dw, sub, 2 * dw, bias_d, D_SLOTS, 1, hs, geo,
                                                 want_lse=True, out_dtype=F32, dil=dil)
                outs.append(o_g)
                lses.append(lse_g)
            x = _outproj_cd(x, o_c.reshape(b * t, -1), outs, lses, [g[0] for g in d_groups], t, cd_w_out, i)
        last = l == depth - 1
        x = _ffn(x, norm_ffn2[l], ffn2_w_in, ffn2_w_out, l, final_gain=final_norm if last else None,
                 out_rows=(bp * t, bs * t) if last else None)

    return (x[0].reshape(bp, t, d), x[1].reshape(bs, t, d))
```

```python
import functools
import math

import jax
import jax.numpy as jnp
import numpy as np
from jax import lax
from jax.experimental import pallas as pl
from jax.experimental.pallas import tpu as pltpu

HEAD_DIM = 128
GRID_W = 64
A_HEADS, A_KV, A_WINDOW = 8, 2, 128
B_HEADS, B_KV = 8, 2
ROPE_THETA = 10000.0
C_HEADS, C_KV = 8, 2
NA_ROWS, NA_COLS = 8, 16
D_PAIRS = ((128, 1), (512, 4), (2048, 16))
D_SLOTS = 4
D_HEADS = D_SLOTS * len(D_PAIRS)
T5_BUCKETS = 32
T5_MAX_DIST = 2048
NORM_EPS = 1e-6
NEG_INF = -1e30
ATTN_SCALE = HEAD_DIM ** -0.5
LOG2_E = math.log2(math.e)

LANES = 128
SUBLANES = 8
MXU_COLS = 256
FFN_ROW_TILE = 1024
FFN_FF_TILE = 2 * MXU_COLS
DENSE_KEY_CHUNK = 2 * MXU_COLS
CHEAP_ROW_STRIDE = 4
BAND_TILES_PER_STEP = (4, 2, 1)
NA_ROW_UNROLL = 8
MIB = 1024 * 1024
VMEM_LIMIT_BYTES = 56 * MIB

BF16 = jnp.bfloat16
F32 = jnp.float32


def _params(semantics):
    return pltpu.CompilerParams(dimension_semantics=semantics, vmem_limit_bytes=VMEM_LIMIT_BYTES)


def _resident_weight(shape, layer):
    return pl.BlockSpec((None,) + shape, lambda *_: (layer,) + (0,) * len(shape),
                        pipeline_mode=pl.Buffered(1))


def _rms(x, gain):
    ms = jnp.mean(x * x, axis=-1, keepdims=True)
    return x * lax.rsqrt(ms + NORM_EPS) * gain


def _ffn_body(*refs, n_ff, n_in, n_out, tiles_a, final_norm):
    refs = list(refs)
    x_refs = [refs.pop(0) for _ in range(n_in)]
    gain_ref, wg_ref, wu_ref, wo_ref = [refs.pop(0) for _ in range(4)]
    fgain_ref = refs.pop(0) if final_norm else None
    o_refs = [refs.pop(0) for _ in range(n_out)]
    h_scr = refs.pop(0)
    acc_ref = refs.pop(0) if n_out == 2 else o_refs[0]
    i, j = pl.program_id(0), pl.program_id(1)

    def start(x_ref):
        x = x_ref[...]
        h_scr[...] = _rms(x, gain_ref[...]).astype(BF16)
        acc_ref[...] = x

    if n_in == 1:
        pl.when(j == 0)(lambda: start(x_refs[0]))
    else:
        pl.when((j == 0) & (i < tiles_a))(lambda: start(x_refs[0]))
        pl.when((j == 0) & (i >= tiles_a))(lambda: start(x_refs[1]))

    h = h_scr[...]
    g = jnp.dot(h, wg_ref[...], preferred_element_type=F32)
    u = jnp.dot(h, wu_ref[...], preferred_element_type=F32)
    a = (g * jax.nn.sigmoid(g) * (0.5 * u)).astype(BF16)
    acc_ref[...] += jnp.dot(a, wo_ref[...], preferred_element_type=F32)

    def finish(o_ref):
        y = acc_ref[...]
        o_ref[...] = _rms(y, fgain_ref[...]) if final_norm else y

    if n_out == 2:
        pl.when((j == n_ff - 1) & (i < tiles_a))(lambda: finish(o_refs[0]))
        pl.when((j == n_ff - 1) & (i >= tiles_a))(lambda: finish(o_refs[1]))
    elif final_norm:
        pl.when(j == n_ff - 1)(lambda: finish(o_refs[0]))


def _ffn(xs, gain, w_in, w_out, layer, final_gain=None, out_rows=None, *, tm=FFN_ROW_TILE):
    xs = tuple(xs) if isinstance(xs, (tuple, list)) else (xs,)
    d = xs[0].shape[1]
    n = sum(x.shape[0] for x in xs)
    f = w_out.shape[1]
    tf = math.gcd(FFN_FF_TILE, f)
    n_ff = f // tf
    if out_rows is not None:
        tm //= 2
    tm = math.gcd(tm, *(x.shape[0] for x in xs))
    assert w_in.shape[1:] == (d, 2 * f) and w_out.shape[2] == d
    n_in, n_out = len(xs), 1 if out_rows is None else 2
    rows_a = xs[0].shape[0] if n_in == 2 else (out_rows[0] if n_out == 2 else n)
    assert n_in == 1 or n_out == 1 or out_rows[0] == rows_a
    tiles_a = rows_a // tm
    final_norm = final_gain is not None
    seg_a = lambda i, j: (jnp.minimum(i, tiles_a - 1), 0)
    seg_b = lambda i, j: (jnp.maximum(i - tiles_a, 0), 0)
    whole = lambda i, j: (i, 0)
    if n_in == 1:
        in_specs = [pl.BlockSpec((tm, d), whole)]
    else:
        in_specs = [pl.BlockSpec((tm, d), m, pipeline_mode=pl.Buffered(1)) for m in (seg_a, seg_b)]
    in_specs += [
        pl.BlockSpec((1, d), lambda i, j: (0, 0)),
        pl.BlockSpec((None, d, tf), lambda i, j: (layer, 0, j)),
        pl.BlockSpec((None, d, tf), lambda i, j: (layer, 0, j + n_ff)),
        pl.BlockSpec((None, tf, d), lambda i, j: (layer, j, 0)),
    ]
    args = list(xs) + [gain.reshape(1, d), w_in, w_in, w_out]
    if final_norm:
        in_specs.append(pl.BlockSpec((1, d), lambda i, j: (0, 0)))
        args.append(final_gain.reshape(1, d))
    scratch = [pltpu.VMEM((tm, d), BF16)]
    if n_out == 1:
        out_specs = pl.BlockSpec((tm, d), whole)
        out_shape = jax.ShapeDtypeStruct((n, d), F32)
    else:
        assert out_rows[0] % tm == 0 and sum(out_rows) == n
        out_specs = [pl.BlockSpec((tm, d), seg_a), pl.BlockSpec((tm, d), seg_b)]
        out_shape = [jax.ShapeDtypeStruct((r, d), F32) for r in out_rows]
        scratch.append(pltpu.VMEM((tm, d), F32))
    row_sem = "parallel" if n_in == 1 and n_out == 1 else "arbitrary"
    return pl.pallas_call(
        functools.partial(_ffn_body, n_ff=n_ff, n_in=n_in, n_out=n_out, tiles_a=tiles_a,
                          final_norm=final_norm),
        grid=(n // tm, n_ff),
        in_specs=in_specs,
        out_specs=out_specs,
        out_shape=out_shape,
        scratch_shapes=scratch,
        compiler_params=_params((row_sem, "arbitrary")),
        name="ffn",
    )(*args)


def _rope_norm(r, gain, cos, sin):
    y = _rms(r, gain)
    return y * cos + pltpu.roll(y, HEAD_DIM // 2, axis=1) * sin


def _inproj_body(*refs, kinds, rope, sub_plan, dils):
    if rope:
        x_ref, gain_ref, w_ref, qg_ref, kg_ref, cos_ref, sin_ref, o_ref, qt_ref, vt_ref = refs
    else:
        x_ref, gain_ref, w_ref, o_ref = refs[:4]
        sub_refs, slab_scr = refs[4:4 + len(dils)], refs[-1] if dils else None
    tm = x_ref.shape[0]
    h = _rms(x_ref[...], gain_ref[...]).astype(BF16)
    n_qt = 0
    for c, kind in enumerate(kinds):
        lo = c * MXU_COLS
        r = jnp.dot(h, w_ref[:, lo:lo + MXU_COLS], preferred_element_type=F32)
        if kind == "scale":
            r = r * ATTN_SCALE
        elif kind in ("qnorm", "knorm"):
            hg = qg_ref[...] if kind == "qnorm" else kg_ref[...]
            cos = cos_ref[...]
            sin = sin_ref[...]
            parts = []
            for hh in range(MXU_COLS // HEAD_DIM):
                y = _rope_norm(r[:, hh * HEAD_DIM:(hh + 1) * HEAD_DIM], hg, cos, sin)
                if kind == "qnorm":
                    y = y * (ATTN_SCALE * LOG2_E)
                parts.append(y)
            r = jnp.concatenate(parts, axis=1)
            if kind == "qnorm":
                qt_ref[0, n_qt * MXU_COLS:(n_qt + 1) * MXU_COLS, :] = r.T.astype(BF16)
                n_qt += 1
        elif kind == "plain_t":
            vt_ref[0] = r.T.astype(BF16)
        o_ref[:, lo:lo + MXU_COLS] = r.astype(BF16)
        if c in sub_plan:
            heads = range(MXU_COLS // HEAD_DIM)
            q_rows = tm // CHEAP_ROW_STRIDE
            for hh in heads:
                slab_scr[0, hh] = r[:, hh * HEAD_DIM:(hh + 1) * HEAD_DIM]
            if any(dils[si] > CHEAP_ROW_STRIDE for si, _ in sub_plan[c]):
                for hh in heads:
                    for r4 in range(CHEAP_ROW_STRIDE):
                        slab_scr[1, hh, r4 * q_rows:(r4 + 1) * q_rows, :] = (
                            slab_scr[0, hh, pl.ds(r4, q_rows, stride=CHEAP_ROW_STRIDE), :])
            for si, col0 in sub_plan[c]:
                d = dils[si]
                for rho in range(d):
                    for hh in heads:
                        if d <= CHEAP_ROW_STRIDE:
                            part = slab_scr[0, hh, pl.ds(rho, tm // d, stride=d), :]
                        else:
                            r4, r2 = rho % CHEAP_ROW_STRIDE, rho // CHEAP_ROW_STRIDE
                            part = slab_scr[1, hh, pl.ds(r4 * q_rows + r2, tm // d,
                                                         stride=d // CHEAP_ROW_STRIDE), :]
                        cols = slice(col0 + hh * HEAD_DIM, col0 + (hh + 1) * HEAD_DIM)
                        sub_refs[si][0, rho, :, cols] = part.astype(BF16)


def _inproj(x, gain, w, layer, kinds, seq_len, rope_args=None, sub_args=None, *, tm=512):
    n, d = x.shape
    c = w.shape[2]
    assert n % tm == 0 and seq_len % tm == 0 and c == MXU_COLS * len(kinds)
    rope = rope_args is not None
    assert not (rope and sub_args)
    dils, sub_width, sub_plan = sub_args if sub_args else ((), 0, {})
    scratch = [pltpu.VMEM((2, MXU_COLS // HEAD_DIM, tm, HEAD_DIM), F32)] if dils else []
    assert all(dl <= CHEAP_ROW_STRIDE or (dl % CHEAP_ROW_STRIDE == 0 and dl <= CHEAP_ROW_STRIDE ** 2)
               for dl in dils)
    in_specs = [
        pl.BlockSpec((tm, d), lambda i: (i, 0)),
        pl.BlockSpec((1, d), lambda i: (0, 0)),
        _resident_weight((d, c), layer),
    ]
    args = [x, gain.reshape(1, d), w]
    if rope:
        q_gain, k_gain, cos, sin = rope_args
        tiles_per_seq = seq_len // tm
        in_specs += [
            pl.BlockSpec((1, HEAD_DIM), lambda i: (0, 0)),
            pl.BlockSpec((1, HEAD_DIM), lambda i: (0, 0)),
            pl.BlockSpec((tm, HEAD_DIM), lambda i: (i % tiles_per_seq, 0)),
            pl.BlockSpec((tm, HEAD_DIM), lambda i: (i % tiles_per_seq, 0)),
        ]
        args += [q_gain.reshape(1, HEAD_DIM), k_gain.reshape(1, HEAD_DIM), cos, sin]
    out_specs = pl.BlockSpec((tm, c), lambda i: (i, 0))
    out_shape = jax.ShapeDtypeStruct((n, c), BF16)
    if rope:
        assert kinds.count("plain_t") == 1
        qt_rows = MXU_COLS * kinds.count("qnorm")
        by_seq = lambda i: (i // tiles_per_seq, 0, i % tiles_per_seq)
        out_specs = [out_specs, pl.BlockSpec((1, qt_rows, tm), by_seq),
                     pl.BlockSpec((1, MXU_COLS, tm), by_seq)]
        out_shape = [out_shape, jax.ShapeDtypeStruct((n // seq_len, qt_rows, seq_len), BF16),
                     jax.ShapeDtypeStruct((n // seq_len, MXU_COLS, seq_len), BF16)]
    if dils:
        tiles_per_seq = seq_len // tm
        assert all(tm % (dl * 16) == 0 for dl in dils)
        out_specs = [out_specs] + [
            pl.BlockSpec((1, dl, tm // dl, sub_width), lambda i: (i // tiles_per_seq, 0, i % tiles_per_seq, 0))
            for dl in dils]
        out_shape = [out_shape] + [
            jax.ShapeDtypeStruct((n // seq_len, dl, seq_len // dl, sub_width), BF16) for dl in dils]
    return pl.pallas_call(
        functools.partial(_inproj_body, kinds=tuple(kinds), rope=rope, sub_plan=sub_plan, dils=tuple(dils)),
        grid=(n // tm,),
        in_specs=in_specs,
        out_specs=out_specs,
        out_shape=out_shape,
        scratch_shapes=scratch,
        compiler_params=_params(("parallel",)),
        name="inproj",
    )(*args)


def _outproj_ab_body(x_ref, a_ref, b_ref, w_ref, o_ref):
    ka = a_ref.shape[1]
    acc = jnp.dot(a_ref[...], w_ref[0:ka, :], preferred_element_type=F32)
    acc = acc + jnp.dot(b_ref[...], w_ref[ka:, :], preferred_element_type=F32)
    o_ref[...] = x_ref[...] + acc


def _outproj_ab(x, oa, ob, w, layer, *, tm=512):
    n, d = x.shape
    ka, kb = oa.shape[1], ob.shape[1]
    assert n % tm == 0 and w.shape[1:] == (ka + kb, d)
    return pl.pallas_call(
        _outproj_ab_body,
        grid=(n // tm,),
        in_specs=[
            pl.BlockSpec((tm, d), lambda i: (i, 0)),
            pl.BlockSpec((tm, ka), lambda i: (i, 0)),
            pl.BlockSpec((tm, kb), lambda i: (i, 0)),
            _resident_weight((ka + kb, d), layer),
        ],
        out_specs=pl.BlockSpec((tm, d), lambda i: (i, 0)),
        out_shape=jax.ShapeDtypeStruct((n, d), F32),
        compiler_params=_params(("parallel",)),
        name="outproj_ab",
    )(x, oa, ob, w)


def _outproj_cd_body(*refs, dils, kd):
    ng = len(dils)
    x_ref, c_ref = refs[:2]
    o_refs, l_refs = refs[2:2 + ng], refs[2 + ng:2 + 2 * ng]
    w_ref, o_ref, row_scr = refs[2 + 2 * ng:]
    tm = x_ref.shape[0]
    heads = kd // HEAD_DIM

    def token_rows(ref, d, slot, width):
        if d == 1:
            return ref[...]
        for rho in range(d):
            for s in range(width // LANES):
                lo = rho * width + s * LANES
                row_scr[slot, s, pl.ds(rho, tm // d, stride=d), :] = ref[0, :, lo:lo + LANES]
        return jnp.concatenate([row_scr[slot, s] for s in range(width // LANES)], axis=1)

    outs = [token_rows(r, d, 2 * i, kd) for i, (r, d) in enumerate(zip(o_refs, dils))]
    lses = [token_rows(r, d, 2 * i + 1, LANES) for i, (r, d) in enumerate(zip(l_refs, dils))]
    m = functools.reduce(jnp.maximum, lses)
    es = [jnp.exp(l - m) for l in lses]
    den = functools.reduce(lambda a, b: a + b, es)
    wts = [e / den for e in es]
    parts = []
    for hd in range(heads):
        lane = hd * (LANES // heads)
        cols = slice(hd * HEAD_DIM, (hd + 1) * HEAD_DIM)
        terms = [jnp.broadcast_to(wt[:, lane:lane + 1], (tm, HEAD_DIM)) * o[:, cols] for wt, o in zip(wts, outs)]
        parts.append(functools.reduce(lambda a, b: a + b, terms))
    od = jnp.concatenate(parts, axis=1)
    kc = c_ref.shape[1]
    acc = jnp.dot(c_ref[...], w_ref[0:kc, :], preferred_element_type=F32)
    acc = acc + jnp.dot(od.astype(BF16), w_ref[kc:, :], preferred_element_type=F32)
    o_ref[...] = x_ref[...] + acc


def _outproj_cd(x, oc, outs, lses, dils, seq_len, w, layer, *, tm=512):
    n, d = x.shape
    kc = oc.shape[1]
    kd = w.shape[1] - kc
    assert n % tm == 0 and seq_len % tm == 0 and all(tm % (dl * SUBLANES) == 0 for dl in dils)
    tiles_per_seq = seq_len // tm
    row = lambda i: (i, 0)
    by_seq = lambda i: (i // tiles_per_seq, i % tiles_per_seq, 0)

    def g_specs(width):
        return [pl.BlockSpec((tm, width), row) if dl == 1 else pl.BlockSpec((1, tm // dl, dl * width), by_seq)
                for dl in dils]

    return pl.pallas_call(
        functools.partial(_outproj_cd_body, dils=tuple(dils), kd=kd),
        grid=(n // tm,),
        in_specs=[pl.BlockSpec((tm, d), row), pl.BlockSpec((tm, kc), row)] + g_specs(kd) + g_specs(LANES)
        + [_resident_weight((kc + kd, d), layer)],
        out_specs=pl.BlockSpec((tm, d), row),
        out_shape=jax.ShapeDtypeStruct((n, d), F32),
        scratch_shapes=[pltpu.VMEM((2 * len(dils), kd // HEAD_DIM, tm, HEAD_DIM), F32)],
        compiler_params=_params(("parallel",)),
        name="outproj_cd",
    )(x, oc, *outs, *lses, w)


def _stack_heads(q, g):
    return jnp.concatenate([q[:, i * HEAD_DIM:(i + 1) * HEAD_DIM] for i in range(g)], axis=0)


def _softmax_pv(s, v):
    m = jnp.max(s, axis=-1, keepdims=True)
    p = jnp.exp(s - m)
    l = jnp.sum(p, axis=-1, keepdims=True)
    o = jnp.dot(p.astype(BF16), v, preferred_element_type=F32) / l
    return o, m, l


def _qk(q, k):
    return lax.dot_general(q, k, (((1,), (1,)), ((), ())), preferred_element_type=F32)


def _dense_body(qt_ref, k_ref, vt_ref, o_ref, s_scr, p_scr, m_scr, l_scr, *, g):
    step = pl.program_id(0)
    tq = qt_ref.shape[2] // 2
    t = k_ref.shape[1]
    gq = g * tq
    kc = min(DENSE_KEY_CHUNK, t)

    @pl.when(step == 0)
    def _():
        s_scr[1] = jnp.zeros(s_scr.shape[1:], s_scr.dtype)
        m_scr[1] = jnp.zeros(m_scr.shape[1:], m_scr.dtype)
        p_scr[...] = jnp.zeros(p_scr.shape, p_scr.dtype)
        l_scr[...] = jnp.ones(l_scr.shape, l_scr.dtype)

    def fold(x, op):
        return op(x.reshape(x.shape[0] // SUBLANES, SUBLANES, x.shape[1]), axis=0)

    for cur in range(2):
        prev = 1 - cur
        toks = slice(cur * tq, (cur + 1) * tq)
        qt = jnp.concatenate([qt_ref[0, i * HEAD_DIM:(i + 1) * HEAD_DIM, toks] for i in range(g)], axis=1)
        m = jnp.max(m_scr[prev], axis=0, keepdims=True)
        ot = jnp.zeros((HEAD_DIM, gq), F32)
        l_acc = jnp.zeros((SUBLANES, gq), F32)
        m_acc = None
        for c in range(t // kc):
            keys = slice(c * kc, (c + 1) * kc)
            p = jnp.exp2(s_scr[prev, keys, :] - m)
            l_acc = l_acc + fold(p, jnp.sum)
            p_scr[prev, keys, :] = p.astype(BF16)
        for c in range(t // kc):
            keys = slice(c * kc, (c + 1) * kc)
            ot = ot + jnp.dot(vt_ref[0, :, keys], p_scr[cur, keys, :], preferred_element_type=F32)
            s = jnp.dot(k_ref[0, keys, :], qt, preferred_element_type=F32)
            s_scr[cur, keys, :] = s
            m_chunk = fold(s, jnp.max)
            m_acc = m_chunk if m_acc is None else jnp.maximum(m_acc, m_chunk)
        ot = ot / jnp.sum(l_scr[cur], axis=0, keepdims=True)
        l_scr[prev] = l_acc
        m_scr[cur] = m_acc
        for i in range(g):
            o_ref[0, toks, i * HEAD_DIM:(i + 1) * HEAD_DIM] = ot[:, i * tq:(i + 1) * tq].T.astype(o_ref.dtype)


def _dense_attention(qt, qkv, vt, k_col, n_kv, g, *, tq=128):
    b, t, _ = qkv.shape
    gw = g * HEAD_DIM
    n_pairs = t // (2 * tq)
    assert t % (2 * tq) == 0 and k_col % HEAD_DIM == 0
    assert qt.shape == (b, n_kv * gw, t) and vt.shape == (b, n_kv * HEAD_DIM, t)
    kb = k_col // HEAD_DIM
    total = b * n_kv * n_pairs

    def where(f):
        return f // (n_kv * n_pairs), (f // n_pairs) % n_kv, f % n_pairs

    def q_map(f):
        n, h, k = where(jnp.minimum(f, total - 1))
        return n, h, k

    def k_map(f):
        n, h, _ = where(jnp.minimum(f, total - 1))
        return n, 0, kb + h

    def v_map(f):
        n, h, _ = where(jnp.maximum(f - 1, 0))
        return n, h, 0

    def o_map(f):
        n, h, k = where(jnp.maximum(f - 1, 0))
        return n, k, h

    return pl.pallas_call(
        functools.partial(_dense_body, g=g),
        grid=(total + 1,),
        in_specs=[
            pl.BlockSpec((1, gw, 2 * tq), q_map),
            pl.BlockSpec((1, t, HEAD_DIM), k_map),
            pl.BlockSpec((1, HEAD_DIM, t), v_map),
        ],
        out_specs=pl.BlockSpec((1, 2 * tq, gw), o_map),
        out_shape=jax.ShapeDtypeStruct((b, t, n_kv * gw), BF16),
        scratch_shapes=[pltpu.VMEM((2, t, g * tq), F32), pltpu.VMEM((2, t, g * tq), BF16),
                        pltpu.VMEM((2, SUBLANES, g * tq), F32), pltpu.VMEM((2, SUBLANES, g * tq), F32)],
        compiler_params=_params(("arbitrary",)),
        name="dense_attn",
    )(qt, qkv, vt)


def _band_body(*refs, n_kv, g, w, seq, tq, kw, n_cases, has_sink, want_lse, side_by_side):
    refs = list(refs)
    q_ref, k_ref, v_ref, bias_ref = refs[:4]
    refs = refs[4:]
    sink_ref = refs.pop(0) if has_sink else None
    o_ref = refs.pop(0)
    lse_ref = refs.pop(0) if want_lse else None
    gw = g * HEAD_DIM
    qw = n_kv * gw
    nq = seq // tq
    for sq, j in [(a, c) for a in range(q_ref.shape[0]) for c in range(q_ref.shape[1] // tq)]:
        qi = pl.program_id(1) * (q_ref.shape[1] // tq) + j
        kstart = pl.multiple_of(jnp.clip(qi * tq - w, 0, seq - kw), 64)
        case = 0 if n_cases == 1 else jnp.where(qi == 0, 0, jnp.where(qi == nq - 1, 2, 1))
        trows = slice(j * tq, (j + 1) * tq)
        osq, ocol0, lcol0 = (0, sq * qw, sq * LANES) if side_by_side else (sq, 0, 0)
        lse_tile = jnp.zeros((tq, LANES), F32)
        head_of_lane = lax.broadcasted_iota(jnp.int32, (tq, LANES), 1) // (LANES // (n_kv * g))
        for h in range(n_kv):
            k = k_ref[sq, pl.ds(kstart, kw), h * HEAD_DIM:(h + 1) * HEAD_DIM]
            v = v_ref[sq, pl.ds(kstart, kw), h * HEAD_DIM:(h + 1) * HEAD_DIM]
            q = _stack_heads(q_ref[sq, trows, h * gw:(h + 1) * gw], g)
            s = _qk(q, k) + bias_ref[case, h]
            o, m, l = _softmax_pv(s, v)
            lse = m + jnp.log(l)
            for i in range(g):
                rows = slice(i * tq, (i + 1) * tq)
                cols = slice(h * gw + i * HEAD_DIM, h * gw + (i + 1) * HEAD_DIM)
                oi = o[rows]
                if has_sink:
                    oi = oi * jax.nn.sigmoid(lse[rows] - sink_ref[h * g + i])
                o_ref[osq, trows, ocol0 + cols.start:ocol0 + cols.stop] = oi.astype(o_ref.dtype)
                if want_lse:
                    lse_tile = jnp.where(head_of_lane == h * g + i, lse[rows], lse_tile)
        if want_lse:
            lse_ref[osq, trows, lcol0:lcol0 + LANES] = lse_tile


def _band_geometry(seq, w):
    if seq <= 4 * w:
        return seq, seq, (0,)
    tq = 2 * w if w < 128 else w
    tq = min(tq, 256)
    kw = tq + 2 * w
    assert seq % tq == 0 and seq >= kw
    return tq, kw, (0, -w, -2 * w)


def _band_attention(q_arr, q_col, k_arr, k_col, v_arr, v_col, bias, n_kv, g, w, geometry,
                    sink=None, want_lse=False, out_dtype=BF16, dil=1):
    n, seq, _ = q_arr.shape
    tq, kw, shifts = geometry
    nq = seq // tq
    gw = g * HEAD_DIM
    qw, kvw = n_kv * gw, n_kv * HEAD_DIM
    assert q_col % qw == 0 and k_col % kvw == 0 and v_col % kvw == 0
    qb, kb, vb = q_col // qw, k_col // kvw, v_col // kvw
    tps = next(c for c in BAND_TILES_PER_STEP if nq % c == 0)
    nq_steps = nq // tps
    sps = next(c for c in BAND_TILES_PER_STEP if n % c == 0) if nq == 1 else 1
    in_specs = [
        pl.BlockSpec((sps, tps * tq, qw), lambda b, i: (b, i, qb)),
        pl.BlockSpec((sps, seq, kvw), lambda b, i: (b, 0, kb)),
        pl.BlockSpec((sps, seq, kvw), lambda b, i: (b, 0, vb)),
        pl.BlockSpec((len(shifts), n_kv, g * tq, kw), lambda b, i: (0, 0, 0, 0)),
    ]
    args = [q_arr, k_arr, v_arr, bias]
    if sink is not None:
        in_specs.append(pl.BlockSpec(memory_space=pltpu.SMEM))
        args.append(sink.astype(F32))
    def out_like(width, dtype):
        if dil == 1:
            return (pl.BlockSpec((sps, tps * tq, width), lambda b, i: (b, i, 0)),
                    jax.ShapeDtypeStruct((n, seq, width), dtype))
        return (pl.BlockSpec((1, tps * tq, sps * width),
                             lambda b, i: ((b * sps) // dil, i, ((b * sps) % dil) // sps)),
                jax.ShapeDtypeStruct((n // dil, seq, dil * width), dtype))

    assert dil == 1 or (n % dil == 0 and dil % sps == 0)
    out_specs, out_shape = out_like(qw, out_dtype)
    if want_lse:
        assert LANES % (n_kv * g) == 0
        l_spec, l_shape = out_like(LANES, F32)
        out_specs, out_shape = [out_specs, l_spec], [out_shape, l_shape]
    return pl.pallas_call(
        functools.partial(_band_body, n_kv=n_kv, g=g, w=w, seq=seq, tq=tq, kw=kw, n_cases=len(shifts),
                          has_sink=sink is not None, want_lse=want_lse, side_by_side=dil > 1),
        grid=(n // sps, nq_steps),
        in_specs=in_specs,
        out_specs=out_specs,
        out_shape=out_shape,
        compiler_params=_params(("parallel", "arbitrary")),
        name="band_attn",
    )(*args)


def _na_body(q_ref, k_ref, v_ref, bias_ref, o_ref, *, g, rows, rows_per_step):
    rb = pl.program_id(2)
    kwin = NA_ROWS * GRID_W

    def one_row(rr, carry):
        r = rb * rows_per_step + rr
        rs = jnp.clip(r - NA_ROWS // 2, 0, rows - NA_ROWS)
        off = rs - r + NA_ROWS - 1
        kstart = pl.multiple_of(rs * GRID_W, GRID_W)
        qstart = pl.multiple_of(rr * GRID_W, GRID_W)
        k = k_ref[0, pl.ds(kstart, kwin), :]
        v = v_ref[0, pl.ds(kstart, kwin), :]
        q = _stack_heads(q_ref[0, pl.ds(qstart, GRID_W), :], g)
        s = _qk(q, k) + bias_ref[0, off]
        o, _, _ = _softmax_pv(s, v)
        for i in range(g):
            o_ref[0, pl.ds(qstart, GRID_W), i * HEAD_DIM:(i + 1) * HEAD_DIM] = (
                o[i * GRID_W:(i + 1) * GRID_W].astype(o_ref.dtype))
        return carry

    lax.fori_loop(0, rows_per_step, one_row, 0, unroll=NA_ROW_UNROLL)


def _na_attention(qkv, q_col, k_col, v_col, bias, n_kv, g, *, rows_per_step=8):
    b, t, _ = qkv.shape
    rows = t // GRID_W
    gw = g * HEAD_DIM
    assert rows % rows_per_step == 0 and rows >= NA_ROWS
    qb, kb, vb = q_col // gw, k_col // HEAD_DIM, v_col // HEAD_DIM
    tq = rows_per_step * GRID_W
    return pl.pallas_call(
        functools.partial(_na_body, g=g, rows=rows, rows_per_step=rows_per_step),
        grid=(b, n_kv, rows // rows_per_step),
        in_specs=[
            pl.BlockSpec((1, tq, gw), lambda n, h, i: (n, i, qb + h)),
            pl.BlockSpec((1, t, HEAD_DIM), lambda n, h, i: (n, 0, kb + h)),
            pl.BlockSpec((1, t, HEAD_DIM), lambda n, h, i: (n, 0, vb + h)),
            pl.BlockSpec((1, NA_ROWS, g * GRID_W, NA_ROWS * GRID_W), lambda n, h, i: (h, 0, 0, 0)),
        ],
        out_specs=pl.BlockSpec((1, tq, gw), lambda n, h, i: (n, i, h)),
        out_shape=jax.ShapeDtypeStruct((b, t, n_kv * gw), BF16),
        compiler_params=_params(("parallel", "parallel", "arbitrary")),
        name="na_attn",
    )(qkv, qkv, qkv, bias)


def _t5_bucket(rel):
    half = T5_BUCKETS // 2
    max_exact = half // 2
    n = jnp.abs(rel)
    nf = jnp.maximum(n, 1).astype(jnp.float32)
    large = max_exact + (jnp.log(nf / max_exact) / math.log(T5_MAX_DIST / max_exact)
                         * (half - max_exact)).astype(jnp.int32)
    large = jnp.minimum(large, half - 1)
    return jnp.where(rel > 0, half, 0) + jnp.where(n < max_exact, n, large)


def _band_bias(table_cols, n_kv, g, w, dil, geometry):
    tq, kw, shifts = geometry
    iq = jnp.arange(tq)[:, None]
    jk = jnp.arange(kw)[None, :]
    tiles = []
    for shift in shifts:
        rel = jk + shift - iq
        bucket = _t5_bucket(rel * dil)
        tb = jnp.zeros((tq, kw, table_cols.shape[1]), F32)
        for r in range(T5_BUCKETS):
            tb = jnp.where((bucket == r)[..., None], table_cols[r].astype(F32), tb)
        tb = jnp.where((jnp.abs(rel) <= w)[..., None], tb, NEG_INF)
        tiles.append(jnp.moveaxis(tb, -1, 0).reshape(n_kv, g * tq, kw))
    return jnp.stack(tiles)


def _na_bias(rpb, n_kv, g):
    col = jnp.arange(GRID_W)
    qc, kc = col[:, None], col[None, :]
    dc = jnp.clip(kc - qc + NA_COLS - 1, 0, 2 * NA_COLS - 2)
    cs = jnp.clip(qc - NA_COLS // 2, 0, GRID_W - NA_COLS)
    mask = (kc >= cs) & (kc < cs + NA_COLS)
    rpb = rpb.astype(F32)
    t = jnp.zeros(rpb.shape[:2] + dc.shape, F32)
    for c in range(2 * NA_COLS - 1):
        t = jnp.where(dc == c, rpb[:, :, c][:, :, None, None], t)
    t = jnp.where(mask, t, NEG_INF)
    cls = jnp.stack([t[:, o:o + NA_ROWS] for o in range(NA_ROWS)])
    cls = cls.reshape(NA_ROWS, n_kv, g, NA_ROWS, GRID_W, GRID_W)
    cls = jnp.transpose(cls, (1, 0, 2, 4, 3, 5))
    return cls.reshape(n_kv, NA_ROWS, g * GRID_W, NA_ROWS * GRID_W)


def _rope_tables(t):
    n_pairs = HEAD_DIM // 2
    n_freq = n_pairs // 2
    pos = jnp.arange(t)
    row = (pos // GRID_W).astype(jnp.float32)
    col = (pos % GRID_W).astype(jnp.float32)
    omega = ROPE_THETA ** (-(jnp.arange(n_freq, dtype=jnp.float32) * 2.0 / n_pairs))
    ang = jnp.concatenate([row[:, None] * omega, col[:, None] * omega], axis=-1)
    cos, sin = jnp.cos(ang), jnp.sin(ang)
    return jnp.concatenate([cos, cos], axis=-1), jnp.concatenate([-sin, sin], axis=-1)


def _deinterleave_heads(w, n_heads):
    lead = w.shape[:-1]
    w = w.reshape(lead + (n_heads, HEAD_DIM // 2, 2))
    return jnp.swapaxes(w, -1, -2).reshape(lead + (n_heads * HEAD_DIM,))


AB_KINDS = ("scale",) * 4 + ("qnorm",) * 4 + ("plain", "plain", "knorm", "plain_t")
CD_KINDS = ("scale",) * 4 + ("plain",) * 2 + ("scale",) * 6 + ("plain",) * 4


def kernel(x_prompt, x_sample, norm_ffn1, ffn1_w_in, ffn1_w_out, norm_mix, ab_w_in, ab_sink, ab_q_gain, ab_k_gain, ab_w_out, cd_w_in, cd_rpb, cd_w_out, norm_ffn2, ffn2_w_in, ffn2_w_out, t5_table, final_norm):
    (bp, t, d), bs = x_prompt.shape, x_sample.shape[0]
    assert x_sample.shape[1:] == (t, d)
    b = bp + bs
    depth = norm_ffn1.shape[0]
    x = (x_prompt.reshape(bp * t, d), x_sample.reshape(bs * t, d))

    ffn1_w_in, ffn1_w_out = ffn1_w_in.astype(BF16), ffn1_w_out.astype(BF16)
    ffn2_w_in, ffn2_w_out = ffn2_w_in.astype(BF16), ffn2_w_out.astype(BF16)
    qa_w, kva_w = A_HEADS * HEAD_DIM, 2 * A_KV * HEAD_DIM
    qb_lo = qa_w + kva_w
    qb_hi = qb_lo + B_HEADS * HEAD_DIM
    kb_hi = qb_hi + B_KV * HEAD_DIM
    ab_w_in = jnp.concatenate([
        ab_w_in[..., :qa_w],
        _deinterleave_heads(ab_w_in[..., qb_lo:qb_hi], B_HEADS),
        ab_w_in[..., qa_w:qb_lo],
        _deinterleave_heads(ab_w_in[..., qb_hi:kb_hi], B_KV),
        ab_w_in[..., kb_hi:]], axis=-1).astype(BF16)
    ab_q_gain = _deinterleave_heads(ab_q_gain, 1)
    ab_k_gain = _deinterleave_heads(ab_k_gain, 1)
    ab_w_out, cd_w_in, cd_w_out = ab_w_out.astype(BF16), cd_w_in.astype(BF16), cd_w_out.astype(BF16)

    cos, sin = _rope_tables(t)
    ga, gc = A_HEADS // A_KV, C_HEADS // C_KV
    geo_a = _band_geometry(t, A_WINDOW)
    bias_a = _band_bias(t5_table[:, :A_HEADS], A_KV, ga, A_WINDOW, 1, geo_a)
    d_groups = []
    for gi, (win, dil) in enumerate(D_PAIRS):
        hs = win // (2 * dil)
        geo = _band_geometry(t // dil, hs)
        col0 = A_HEADS + gi * D_SLOTS
        d_groups.append((dil, hs, geo, _band_bias(t5_table[:, col0:col0 + D_SLOTS], D_SLOTS, 1, hs, dil, geo)))

    a_q = 0
    a_k = qa_w + B_HEADS * HEAD_DIM
    a_v = a_k + A_KV * HEAD_DIM
    b_k = a_v + A_KV * HEAD_DIM
    c_q, c_k, c_v = 0, C_HEADS * HEAD_DIM, (C_HEADS + C_KV) * HEAD_DIM
    d_q = (C_HEADS + 2 * C_KV) * HEAD_DIM
    d_k = d_q + D_HEADS * HEAD_DIM
    d_v = d_k + D_SLOTS * HEAD_DIM
    dw = D_SLOTS * HEAD_DIM
    sub_dils = tuple(dil for _, dil in D_PAIRS if dil > 1)
    sub_plan = {}
    for gi, (_, dil) in enumerate(D_PAIRS):
        if dil > 1:
            si = sub_dils.index(dil)
            for src, dst in ((d_q + gi * dw, 0), (d_k, dw), (d_v, 2 * dw)):
                for off in range(0, dw, MXU_COLS):
                    sub_plan.setdefault((src + off) // MXU_COLS, []).append((si, dst + off))

    for l in range(depth):
        i = l // 2
        x = _ffn(x, norm_ffn1[l], ffn1_w_in, ffn1_w_out, l)
        if l % 2 == 0:
            qkv, qbt, vbt = _inproj(x, norm_mix[l], ab_w_in, i, AB_KINDS, t,
                                    (ab_q_gain[i], ab_k_gain[i], cos, sin))
            qkv = qkv.reshape(b, t, -1)
            o_a = _band_attention(qkv, a_q, qkv, a_k, qkv, a_v, bias_a, A_KV, ga, A_WINDOW, geo_a,
                                  sink=ab_sink[i])
            o_b = _dense_attention(qbt, qkv, vbt, b_k, B_KV, B_HEADS // B_KV)
            x = _outproj_ab(x, o_a.reshape(b * t, -1), o_b.reshape(b * t, -1), ab_w_out, i)
        else:
            qkv, *subs = _inproj(x, norm_mix[l], cd_w_in, i, CD_KINDS, t,
                                 sub_args=(sub_dils, 3 * dw, sub_plan))
            qkv = qkv.reshape(b, t, -1)
            o_c = _na_attention(qkv, c_q, c_k, c_v, _na_bias(cd_rpb[i], C_KV, gc), C_KV, gc)
            outs, lses = [], []
            for gi, (dil, hs, geo, bias_d) in enumerate(d_groups):
                if dil == 1:
                    o_g, lse_g = _band_attention(qkv, d_q + gi * dw, qkv, d_k, qkv, d_v, bias_d,
                                                 D_SLOTS, 1, hs, geo, want_lse=True, out_dtype=F32)
                    o_g, lse_g = o_g.reshape(b * t, dw), lse_g.reshape(b * t, LANES)
                else:
                    sub = subs[sub_dils.index(dil)].reshape(b * dil, t // dil, 3 * dw)
                    o_g, lse_g = _band_attention(sub, 0, sub, dw, sub, 2 * dw, bias_d, D_SLOTS, 1, hs, geo,
                                                 want_lse=True, out_dtype=F32, dil=dil)
                outs.append(o_g)
                lses.append(lse_g)
            x = _outproj_cd(x, o_c.reshape(b * t, -1), outs, lses, [g[0] for g in d_groups], t, cd_w_out, i)
        last = l == depth - 1
        x = _ffn(x, norm_ffn2[l], ffn2_w_in, ffn2_w_out, l, final_gain=final_norm if last else None,
                 out_rows=(bp * t, bs * t) if last else None)

    return (x[0].reshape(bp, t, d), x[1].reshape(bs, t, d))
```

```python
import functools
import math

import jax
import jax.numpy as jnp
from jax import lax
from jax.experimental import pallas as pl
from jax.experimental.pallas import tpu as pltpu

HEAD_DIM = 128
GRID_W = 64
A_HEADS, A_KV, A_WINDOW = 8, 2, 128
B_HEADS, B_KV = 8, 2
ROPE_THETA = 10000.0
C_HEADS, C_KV = 8, 2
NA_ROWS, NA_COLS = 8, 16
D_PAIRS = ((128, 1), (512, 4), (2048, 16))
D_SLOTS = 4
D_HEADS = D_SLOTS * len(D_PAIRS)
T5_BUCKETS = 32
T5_MAX_DIST = 2048
NORM_EPS = 1e-6
NEG_INF = -1e30
ATTN_SCALE = HEAD_DIM ** -0.5
LOG2_E = math.log2(math.e)

LANES = 128
SUBLANES = 8
MXU_COLS = 256
FFN_ROW_TILE = 1024
FFN_FF_TILE = 2 * MXU_COLS
DENSE_KEY_CHUNK = 2 * MXU_COLS
CHEAP_ROW_STRIDE = 4
BAND_TILES_PER_STEP = (4, 2, 1)
NA_ROW_UNROLL = 8
MIB = 1024 * 1024
VMEM_LIMIT_BYTES = 56 * MIB

BF16 = jnp.bfloat16
F32 = jnp.float32


def _params(semantics):
    return pltpu.CompilerParams(dimension_semantics=semantics, vmem_limit_bytes=VMEM_LIMIT_BYTES)


def _resident_weight(shape, layer):
    return pl.BlockSpec((None,) + shape, lambda *_: (layer,) + (0,) * len(shape),
                        pipeline_mode=pl.Buffered(1))


def _rms(x, gain):
    ms = jnp.mean(x * x, axis=-1, keepdims=True)
    return x * lax.rsqrt(ms + NORM_EPS) * gain


def _ffn_body(*refs, n_ff, n_in, n_out, tiles_a, final_norm):
    refs = list(refs)
    x_refs = [refs.pop(0) for _ in range(n_in)]
    gain_ref, wg_ref, wu_ref, wo_ref = [refs.pop(0) for _ in range(4)]
    fgain_ref = refs.pop(0) if final_norm else None
    o_refs = [refs.pop(0) for _ in range(n_out)]
    h_scr = refs.pop(0)
    acc_ref = refs.pop(0) if n_out == 2 else o_refs[0]
    i, j = pl.program_id(0), pl.program_id(1)

    def start(x_ref):
        x = x_ref[...]
        h_scr[...] = _rms(x, gain_ref[...]).astype(BF16)
        acc_ref[...] = x

    if n_in == 1:
        pl.when(j == 0)(lambda: start(x_refs[0]))
    else:
        pl.when((j == 0) & (i < tiles_a))(lambda: start(x_refs[0]))
        pl.when((j == 0) & (i >= tiles_a))(lambda: start(x_refs[1]))

    h = h_scr[...]
    g = jnp.dot(h, wg_ref[...], preferred_element_type=F32)
    u = jnp.dot(h, wu_ref[...], preferred_element_type=F32)
    a = (g * jax.nn.sigmoid(g) * (0.5 * u)).astype(BF16)
    acc_ref[...] += jnp.dot(a, wo_ref[...], preferred_element_type=F32)

    def finish(o_ref):
        y = acc_ref[...]
        o_ref[...] = _rms(y, fgain_ref[...]) if final_norm else y

    if n_out == 2:
        pl.when((j == n_ff - 1) & (i < tiles_a))(lambda: finish(o_refs[0]))
        pl.when((j == n_ff - 1) & (i >= tiles_a))(lambda: finish(o_refs[1]))
    elif final_norm:
        pl.when(j == n_ff - 1)(lambda: finish(o_refs[0]))


def _ffn(xs, gain, w_in, w_out, layer, final_gain=None, out_rows=None, *, tm=FFN_ROW_TILE):
    xs = tuple(xs) if isinstance(xs, (tuple, list)) else (xs,)
    d = xs[0].shape[1]
    n = sum(x.shape[0] for x in xs)
    f = w_out.shape[1]
    tf = math.gcd(FFN_FF_TILE, f)
    n_ff = f // tf
    if len(xs) == 2 or out_rows is not None:
        tm //= 2
    tm = math.gcd(tm, *(x.shape[0] for x in xs))
    assert w_in.shape[1:] == (d, 2 * f) and w_out.shape[2] == d
    n_in, n_out = len(xs), 1 if out_rows is None else 2
    rows_a = xs[0].shape[0] if n_in == 2 else (out_rows[0] if n_out == 2 else n)
    assert n_in == 1 or n_out == 1 or out_rows[0] == rows_a
    tiles_a = rows_a // tm
    final_norm = final_gain is not None
    seg_a = lambda i, j: (jnp.minimum(i, tiles_a - 1), 0)
    seg_b = lambda i, j: (jnp.maximum(i - tiles_a, 0), 0)
    whole = lambda i, j: (i, 0)
    in_specs = [pl.BlockSpec((tm, d), m) for m in ((whole,) if n_in == 1 else (seg_a, seg_b))]
    in_specs += [
        pl.BlockSpec((1, d), lambda i, j: (0, 0)),
        pl.BlockSpec((None, d, tf), lambda i, j: (layer, 0, j)),
        pl.BlockSpec((None, d, tf), lambda i, j: (layer, 0, j + n_ff)),
        pl.BlockSpec((None, tf, d), lambda i, j: (layer, j, 0)),
    ]
    args = list(xs) + [gain.reshape(1, d), w_in, w_in, w_out]
    if final_norm:
        in_specs.append(pl.BlockSpec((1, d), lambda i, j: (0, 0)))
        args.append(final_gain.reshape(1, d))
    scratch = [pltpu.VMEM((tm, d), BF16)]
    if n_out == 1:
        out_specs = pl.BlockSpec((tm, d), whole)
        out_shape = jax.ShapeDtypeStruct((n, d), F32)
    else:
        assert out_rows[0] % tm == 0 and sum(out_rows) == n
        out_specs = [pl.BlockSpec((tm, d), seg_a), pl.BlockSpec((tm, d), seg_b)]
        out_shape = [jax.ShapeDtypeStruct((r, d), F32) for r in out_rows]
        scratch.append(pltpu.VMEM((tm, d), F32))
    row_sem = "parallel" if n_in == 1 and n_out == 1 else "arbitrary"
    return pl.pallas_call(
        functools.partial(_ffn_body, n_ff=n_ff, n_in=n_in, n_out=n_out, tiles_a=tiles_a,
                          final_norm=final_norm),
        grid=(n // tm, n_ff),
        in_specs=in_specs,
        out_specs=out_specs,
        out_shape=out_shape,
        scratch_shapes=scratch,
        compiler_params=_params((row_sem, "arbitrary")),
        name="ffn",
    )(*args)


def _rope_norm(r, gain, cos, sin):
    y = _rms(r, gain)
    return y * cos + pltpu.roll(y, HEAD_DIM // 2, axis=1) * sin


def _inproj_body(*refs, kinds, rope, sub_plan, dils):
    if rope:
        x_ref, gain_ref, w_ref, qg_ref, kg_ref, cos_ref, sin_ref, o_ref, qt_ref, vt_ref = refs
    else:
        x_ref, gain_ref, w_ref, o_ref = refs[:4]
        sub_refs, slab_scr = refs[4:4 + len(dils)], refs[-1] if dils else None
    tm = x_ref.shape[0]
    h = _rms(x_ref[...], gain_ref[...]).astype(BF16)
    n_qt = 0
    for c, kind in enumerate(kinds):
        lo = c * MXU_COLS
        r = jnp.dot(h, w_ref[:, lo:lo + MXU_COLS], preferred_element_type=F32)
        if kind == "scale":
            r = r * ATTN_SCALE
        elif kind in ("qnorm", "knorm"):
            hg = qg_ref[...] if kind == "qnorm" else kg_ref[...]
            cos = cos_ref[...]
            sin = sin_ref[...]
            parts = []
            for hh in range(MXU_COLS // HEAD_DIM):
                y = _rope_norm(r[:, hh * HEAD_DIM:(hh + 1) * HEAD_DIM], hg, cos, sin)
                if kind == "qnorm":
                    y = y * (ATTN_SCALE * LOG2_E)
                parts.append(y)
            r = jnp.concatenate(parts, axis=1)
            if kind == "qnorm":
                qt_ref[0, n_qt * MXU_COLS:(n_qt + 1) * MXU_COLS, :] = r.T.astype(BF16)
                n_qt += 1
        elif kind == "plain_t":
            vt_ref[0] = r.T.astype(BF16)
        o_ref[:, lo:lo + MXU_COLS] = r.astype(BF16)
        if c in sub_plan:
            heads = range(MXU_COLS // HEAD_DIM)
            q_rows = tm // CHEAP_ROW_STRIDE
            for hh in heads:
                slab_scr[0, hh] = r[:, hh * HEAD_DIM:(hh + 1) * HEAD_DIM]
            if any(dils[si] > CHEAP_ROW_STRIDE for si, _ in sub_plan[c]):
                for hh in heads:
                    for r4 in range(CHEAP_ROW_STRIDE):
                        slab_scr[1, hh, r4 * q_rows:(r4 + 1) * q_rows, :] = (
                            slab_scr[0, hh, pl.ds(r4, q_rows, stride=CHEAP_ROW_STRIDE), :])
            for si, col0 in sub_plan[c]:
                d = dils[si]
                for rho in range(d):
                    for hh in heads:
                        if d <= CHEAP_ROW_STRIDE:
                            part = slab_scr[0, hh, pl.ds(rho, tm // d, stride=d), :]
                        else:
                            r4, r2 = rho % CHEAP_ROW_STRIDE, rho // CHEAP_ROW_STRIDE
                            part = slab_scr[1, hh, pl.ds(r4 * q_rows + r2, tm // d,
                                                         stride=d // CHEAP_ROW_STRIDE), :]
                        cols = slice(col0 + hh * HEAD_DIM, col0 + (hh + 1) * HEAD_DIM)
                        sub_refs[si][0, rho, :, cols] = part.astype(BF16)


def _inproj(x, gain, w, layer, kinds, seq_len, rope_args=None, sub_args=None, *, tm=512):
    n, d = x.shape
    c = w.shape[2]
    assert n % tm == 0 and seq_len % tm == 0 and c == MXU_COLS * len(kinds)
    rope = rope_args is not None
    assert not (rope and sub_args)
    dils, sub_width, sub_plan = sub_args if sub_args else ((), 0, {})
    scratch = [pltpu.VMEM((2, MXU_COLS // HEAD_DIM, tm, HEAD_DIM), F32)] if dils else []
    assert all(dl <= CHEAP_ROW_STRIDE or (dl % CHEAP_ROW_STRIDE == 0 and dl <= CHEAP_ROW_STRIDE ** 2)
               for dl in dils)
    in_specs = [
        pl.BlockSpec((tm, d), lambda i: (i, 0)),
        pl.BlockSpec((1, d), lambda i: (0, 0)),
        _resident_weight((d, c), layer),
    ]
    args = [x, gain.reshape(1, d), w]
    if rope:
        q_gain, k_gain, cos, sin = rope_args
        tiles_per_seq = seq_len // tm
        in_specs += [
            pl.BlockSpec((1, HEAD_DIM), lambda i: (0, 0)),
            pl.BlockSpec((1, HEAD_DIM), lambda i: (0, 0)),
            pl.BlockSpec((tm, HEAD_DIM), lambda i: (i % tiles_per_seq, 0)),
            pl.BlockSpec((tm, HEAD_DIM), lambda i: (i % tiles_per_seq, 0)),
        ]
        args += [q_gain.reshape(1, HEAD_DIM), k_gain.reshape(1, HEAD_DIM), cos, sin]
    out_specs = pl.BlockSpec((tm, c), lambda i: (i, 0))
    out_shape = jax.ShapeDtypeStruct((n, c), BF16)
    if rope:
        assert kinds.count("plain_t") == 1
        qt_rows = MXU_COLS * kinds.count("qnorm")
        by_seq = lambda i: (i // tiles_per_seq, 0, i % tiles_per_seq)
        out_specs = [out_specs, pl.BlockSpec((1, qt_rows, tm), by_seq),
                     pl.BlockSpec((1, MXU_COLS, tm), by_seq)]
        out_shape = [out_shape, jax.ShapeDtypeStruct((n // seq_len, qt_rows, seq_len), BF16),
                     jax.ShapeDtypeStruct((n // seq_len, MXU_COLS, seq_len), BF16)]
    if dils:
        tiles_per_seq = seq_len // tm
        assert all(tm % (dl * 16) == 0 for dl in dils)
        out_specs = [out_specs] + [
            pl.BlockSpec((1, dl, tm // dl, sub_width), lambda i: (i // tiles_per_seq, 0, i % tiles_per_seq, 0))
            for dl in dils]
        out_shape = [out_shape] + [
            jax.ShapeDtypeStruct((n // seq_len, dl, seq_len // dl, sub_width), BF16) for dl in dils]
    return pl.pallas_call(
        functools.partial(_inproj_body, kinds=tuple(kinds), rope=rope, sub_plan=sub_plan, dils=tuple(dils)),
        grid=(n // tm,),
        in_specs=in_specs,
        out_specs=out_specs,
        out_shape=out_shape,
        scratch_shapes=scratch,
        compiler_params=_params(("parallel",)),
        name="inproj",
    )(*args)


def _outproj_ab_body(x_ref, a_ref, b_ref, w_ref, o_ref):
    ka = a_ref.shape[1]
    acc = jnp.dot(a_ref[...], w_ref[0:ka, :], preferred_element_type=F32)
    acc = acc + jnp.dot(b_ref[...], w_ref[ka:, :], preferred_element_type=F32)
    o_ref[...] = x_ref[...] + acc


def _outproj_ab(x, oa, ob, w, layer, *, tm=512):
    n, d = x.shape
    ka, kb = oa.shape[1], ob.shape[1]
    assert n % tm == 0 and w.shape[1:] == (ka + kb, d)
    return pl.pallas_call(
        _outproj_ab_body,
        grid=(n // tm,),
        in_specs=[
            pl.BlockSpec((tm, d), lambda i: (i, 0)),
            pl.BlockSpec((tm, ka), lambda i: (i, 0)),
            pl.BlockSpec((tm, kb), lambda i: (i, 0)),
            _resident_weight((ka + kb, d), layer),
        ],
        out_specs=pl.BlockSpec((tm, d), lambda i: (i, 0)),
        out_shape=jax.ShapeDtypeStruct((n, d), F32),
        compiler_params=_params(("parallel",)),
        name="outproj_ab",
    )(x, oa, ob, w)


def _outproj_cd_body(*refs, dils, kd):
    ng = len(dils)
    x_ref, c_ref = refs[:2]
    o_refs, l_refs = refs[2:2 + ng], refs[2 + ng:2 + 2 * ng]
    w_ref, o_ref, row_scr = refs[2 + 2 * ng:]
    tm = x_ref.shape[0]
    heads = kd // HEAD_DIM

    def token_rows(ref, d, slot, width):
        if d == 1:
            return ref[...]
        for rho in range(d):
            for s in range(width // LANES):
                lo = rho * width + s * LANES
                row_scr[slot, s, pl.ds(rho, tm // d, stride=d), :] = ref[0, :, lo:lo + LANES]
        return jnp.concatenate([row_scr[slot, s] for s in range(width // LANES)], axis=1)

    outs = [token_rows(r, d, 2 * i, kd) for i, (r, d) in enumerate(zip(o_refs, dils))]
    lses = [token_rows(r, d, 2 * i + 1, LANES) for i, (r, d) in enumerate(zip(l_refs, dils))]
    m = functools.reduce(jnp.maximum, lses)
    es = [jnp.exp(l - m) for l in lses]
    den = functools.reduce(lambda a, b: a + b, es)
    wts = [e / den for e in es]
    parts = []
    for hd in range(heads):
        lane = hd * (LANES // heads)
        cols = slice(hd * HEAD_DIM, (hd + 1) * HEAD_DIM)
        terms = [jnp.broadcast_to(wt[:, lane:lane + 1], (tm, HEAD_DIM)) * o[:, cols] for wt, o in zip(wts, outs)]
        parts.append(functools.reduce(lambda a, b: a + b, terms))
    od = jnp.concatenate(parts, axis=1)
    kc = c_ref.shape[1]
    acc = jnp.dot(c_ref[...], w_ref[0:kc, :], preferred_element_type=F32)
    acc = acc + jnp.dot(od.astype(BF16), w_ref[kc:, :], preferred_element_type=F32)
    o_ref[...] = x_ref[...] + acc


def _outproj_cd(x, oc, outs, lses, dils, seq_len, w, layer, *, tm=512):
    n, d = x.shape
    kc = oc.shape[1]
    kd = w.shape[1] - kc
    assert n % tm == 0 and seq_len % tm == 0 and all(tm % (dl * SUBLANES) == 0 for dl in dils)
    tiles_per_seq = seq_len // tm
    row = lambda i: (i, 0)
    by_seq = lambda i: (i // tiles_per_seq, i % tiles_per_seq, 0)

    def g_specs(width):
        return [pl.BlockSpec((tm, width), row) if dl == 1 else pl.BlockSpec((1, tm // dl, dl * width), by_seq)
                for dl in dils]

    return pl.pallas_call(
        functools.partial(_outproj_cd_body, dils=tuple(dils), kd=kd),
        grid=(n // tm,),
        in_specs=[pl.BlockSpec((tm, d), row), pl.BlockSpec((tm, kc), row)] + g_specs(kd) + g_specs(LANES)
        + [_resident_weight((kc + kd, d), layer)],
        out_specs=pl.BlockSpec((tm, d), row),
        out_shape=jax.ShapeDtypeStruct((n, d), F32),
        scratch_shapes=[pltpu.VMEM((2 * len(dils), kd // HEAD_DIM, tm, HEAD_DIM), F32)],
        compiler_params=_params(("parallel",)),
        name="outproj_cd",
    )(x, oc, *outs, *lses, w)


def _stack_heads(q, g):
    return jnp.concatenate([q[:, i * HEAD_DIM:(i + 1) * HEAD_DIM] for i in range(g)], axis=0)


def _softmax_pv(s, v):
    m = jnp.max(s, axis=-1, keepdims=True)
    p = jnp.exp(s - m)
    l = jnp.sum(p, axis=-1, keepdims=True)
    o = jnp.dot(p.astype(BF16), v, preferred_element_type=F32) / l
    return o, m, l


def _qk(q, k):
    return lax.dot_general(q, k, (((1,), (1,)), ((), ())), preferred_element_type=F32)


def _dense_body(qt_ref, k_ref, vt_ref, o_ref, s_scr, p_scr, m_scr, l_scr, *, g):
    step = pl.program_id(0)
    tq = qt_ref.shape[2] // 2
    t = k_ref.shape[1]
    gq = g * tq
    kc = min(DENSE_KEY_CHUNK, t)

    @pl.when(step == 0)
    def _():
        s_scr[1] = jnp.zeros(s_scr.shape[1:], s_scr.dtype)
        m_scr[1] = jnp.zeros(m_scr.shape[1:], m_scr.dtype)
        p_scr[...] = jnp.zeros(p_scr.shape, p_scr.dtype)
        l_scr[...] = jnp.ones(l_scr.shape, l_scr.dtype)

    def fold(x, op):
        return op(x.reshape(x.shape[0] // SUBLANES, SUBLANES, x.shape[1]), axis=0)

    for cur in range(2):
        prev = 1 - cur
        toks = slice(cur * tq, (cur + 1) * tq)
        qt = jnp.concatenate([qt_ref[0, i * HEAD_DIM:(i + 1) * HEAD_DIM, toks] for i in range(g)], axis=1)
        m = jnp.max(m_scr[prev], axis=0, keepdims=True)
        ot = jnp.zeros((HEAD_DIM, gq), F32)
        l_acc = jnp.zeros((SUBLANES, gq), F32)
        m_acc = None
        for c in range(t // kc):
            keys = slice(c * kc, (c + 1) * kc)
            p = jnp.exp2(s_scr[prev, keys, :] - m)
            l_acc = l_acc + fold(p, jnp.sum)
            p_scr[prev, keys, :] = p.astype(BF16)
        for c in range(t // kc):
            keys = slice(c * kc, (c + 1) * kc)
            ot = ot + jnp.dot(vt_ref[0, :, keys], p_scr[cur, keys, :], preferred_element_type=F32)
            s = jnp.dot(k_ref[0, keys, :], qt, preferred_element_type=F32)
            s_scr[cur, keys, :] = s
            m_chunk = fold(s, jnp.max)
            m_acc = m_chunk if m_acc is None else jnp.maximum(m_acc, m_chunk)
        ot = ot / jnp.sum(l_scr[cur], axis=0, keepdims=True)
        l_scr[prev] = l_acc
        m_scr[cur] = m_acc
        for i in range(g):
            o_ref[0, toks, i * HEAD_DIM:(i + 1) * HEAD_DIM] = ot[:, i * tq:(i + 1) * tq].T.astype(o_ref.dtype)


def _dense_attention(qt, qkv, vt, k_col, n_kv, g, *, tq=128):
    b, t, _ = qkv.shape
    gw = g * HEAD_DIM
    n_pairs = t // (2 * tq)
    assert t % (2 * tq) == 0 and k_col % HEAD_DIM == 0
    assert qt.shape == (b, n_kv * gw, t) and vt.shape == (b, n_kv * HEAD_DIM, t)
    kb = k_col // HEAD_DIM
    total = b * n_kv * n_pairs

    def where(f):
        return f // (n_kv * n_pairs), (f // n_pairs) % n_kv, f % n_pairs

    def q_map(f):
        n, h, k = where(jnp.minimum(f, total - 1))
        return n, h, k

    def k_map(f):
        n, h, _ = where(jnp.minimum(f, total - 1))
        return n, 0, kb + h

    def v_map(f):
        n, h, _ = where(jnp.maximum(f - 1, 0))
        return n, h, 0

    def o_map(f):
        n, h, k = where(jnp.maximum(f - 1, 0))
        return n, k, h

    return pl.pallas_call(
        functools.partial(_dense_body, g=g),
        grid=(total + 1,),
        in_specs=[
            pl.BlockSpec((1, gw, 2 * tq), q_map),
            pl.BlockSpec((1, t, HEAD_DIM), k_map),
            pl.BlockSpec((1, HEAD_DIM, t), v_map),
        ],
        out_specs=pl.BlockSpec((1, 2 * tq, gw), o_map),
        out_shape=jax.ShapeDtypeStruct((b, t, n_kv * gw), BF16),
        scratch_shapes=[pltpu.VMEM((2, t, g * tq), F32), pltpu.VMEM((2, t, g * tq), BF16),
                        pltpu.VMEM((2, SUBLANES, g * tq), F32), pltpu.VMEM((2, SUBLANES, g * tq), F32)],
        compiler_params=_params(("arbitrary",)),
        name="dense_attn",
    )(qt, qkv, vt)


def _band_body(*refs, n_kv, g, w, seq, tq, kw, n_cases, has_sink, want_lse, side_by_side):
    refs = list(refs)
    q_ref, k_ref, v_ref, bias_ref = refs[:4]
    refs = refs[4:]
    sink_ref = refs.pop(0) if has_sink else None
    o_ref = refs.pop(0)
    lse_ref = refs.pop(0) if want_lse else None
    gw = g * HEAD_DIM
    qw = n_kv * gw
    nq = seq // tq
    for sq, j in [(a, c) for a in range(q_ref.shape[0]) for c in range(q_ref.shape[1] // tq)]:
        qi = pl.program_id(1) * (q_ref.shape[1] // tq) + j
        kstart = pl.multiple_of(jnp.clip(qi * tq - w, 0, seq - kw), 64)
        case = 0 if n_cases == 1 else jnp.where(qi == 0, 0, jnp.where(qi == nq - 1, 2, 1))
        trows = slice(j * tq, (j + 1) * tq)
        osq, ocol0, lcol0 = (0, sq * qw, sq * LANES) if side_by_side else (sq, 0, 0)
        lse_tile = jnp.zeros((tq, LANES), F32)
        head_of_lane = lax.broadcasted_iota(jnp.int32, (tq, LANES), 1) // (LANES // (n_kv * g))
        for h in range(n_kv):
            k = k_ref[sq, pl.ds(kstart, kw), h * HEAD_DIM:(h + 1) * HEAD_DIM]
            v = v_ref[sq, pl.ds(kstart, kw), h * HEAD_DIM:(h + 1) * HEAD_DIM]
            q = _stack_heads(q_ref[sq, trows, h * gw:(h + 1) * gw], g)
            s = _qk(q, k) + bias_ref[case, h]
            o, m, l = _softmax_pv(s, v)
            lse = m + jnp.log(l)
            for i in range(g):
                rows = slice(i * tq, (i + 1) * tq)
                cols = slice(h * gw + i * HEAD_DIM, h * gw + (i + 1) * HEAD_DIM)
                oi = o[rows]
                if has_sink:
                    oi = oi * jax.nn.sigmoid(lse[rows] - sink_ref[h * g + i])
                o_ref[osq, trows, ocol0 + cols.start:ocol0 + cols.stop] = oi.astype(o_ref.dtype)
                if want_lse:
                    lse_tile = jnp.where(head_of_lane == h * g + i, lse[rows], lse_tile)
        if want_lse:
            lse_ref[osq, trows, lcol0:lcol0 + LANES] = lse_tile


def _band_geometry(seq, w):
    if seq <= 4 * w:
        return seq, seq, (0,)
    tq = 2 * w if w < 128 else w
    tq = min(tq, 256)
    kw = tq + 2 * w
    assert seq % tq == 0 and seq >= kw
    return tq, kw, (0, -w, -2 * w)


def _band_attention(q_arr, q_col, k_arr, k_col, v_arr, v_col, bias, n_kv, g, w, geometry,
                    sink=None, want_lse=False, out_dtype=BF16, dil=1):
    n, seq, _ = q_arr.shape
    tq, kw, shifts = geometry
    nq = seq // tq
    gw = g * HEAD_DIM
    qw, kvw = n_kv * gw, n_kv * HEAD_DIM
    assert q_col % qw == 0 and k_col % kvw == 0 and v_col % kvw == 0
    qb, kb, vb = q_col // qw, k_col // kvw, v_col // kvw
    tps = next(c for c in BAND_TILES_PER_STEP if nq % c == 0)
    nq_steps = nq // tps
    sps = next(c for c in BAND_TILES_PER_STEP if n % c == 0) if nq == 1 else 1
    in_specs = [
        pl.BlockSpec((sps, tps * tq, qw), lambda b, i: (b, i, qb)),
        pl.BlockSpec((sps, seq, kvw), lambda b, i: (b, 0, kb)),
        pl.BlockSpec((sps, seq, kvw), lambda b, i: (b, 0, vb)),
        pl.BlockSpec((len(shifts), n_kv, g * tq, kw), lambda b, i: (0, 0, 0, 0)),
    ]
    args = [q_arr, k_arr, v_arr, bias]
    if sink is not None:
        in_specs.append(pl.BlockSpec(memory_space=pltpu.SMEM))
        args.append(sink.astype(F32))
    def out_like(width, dtype):
        if dil == 1:
            return (pl.BlockSpec((sps, tps * tq, width), lambda b, i: (b, i, 0)),
                    jax.ShapeDtypeStruct((n, seq, width), dtype))
        return (pl.BlockSpec((1, tps * tq, sps * width),
                             lambda b, i: ((b * sps) // dil, i, ((b * sps) % dil) // sps)),
                jax.ShapeDtypeStruct((n // dil, seq, dil * width), dtype))

    assert dil == 1 or (n % dil == 0 and dil % sps == 0)
    out_specs, out_shape = out_like(qw, out_dtype)
    if want_lse:
        assert LANES % (n_kv * g) == 0
        l_spec, l_shape = out_like(LANES, F32)
        out_specs, out_shape = [out_specs, l_spec], [out_shape, l_shape]
    return pl.pallas_call(
        functools.partial(_band_body, n_kv=n_kv, g=g, w=w, seq=seq, tq=tq, kw=kw, n_cases=len(shifts),
                          has_sink=sink is not None, want_lse=want_lse, side_by_side=dil > 1),
        grid=(n // sps, nq_steps),
        in_specs=in_specs,
        out_specs=out_specs,
        out_shape=out_shape,
        compiler_params=_params(("parallel", "arbitrary")),
        name="band_attn",
    )(*args)


def _na_body(q_ref, k_ref, v_ref, bias_ref, o_ref, *, g, rows, rows_per_step):
    rb = pl.program_id(2)
    kwin = NA_ROWS * GRID_W

    def one_row(rr, carry):
        r = rb * rows_per_step + rr
        rs = jnp.clip(r - NA_ROWS // 2, 0, rows - NA_ROWS)
        off = rs - r + NA_ROWS - 1
        kstart = pl.multiple_of(rs * GRID_W, GRID_W)
        qstart = pl.multiple_of(rr * GRID_W, GRID_W)
        k = k_ref[0, pl.ds(kstart, kwin), :]
        v = v_ref[0, pl.ds(kstart, kwin), :]
        q = _stack_heads(q_ref[0, pl.ds(qstart, GRID_W), :], g)
        s = _qk(q, k) + bias_ref[0, off]
        o, _, _ = _softmax_pv(s, v)
        for i in range(g):
            o_ref[0, pl.ds(qstart, GRID_W), i * HEAD_DIM:(i + 1) * HEAD_DIM] = (
                o[i * GRID_W:(i + 1) * GRID_W].astype(o_ref.dtype))
        return carry

    lax.fori_loop(0, rows_per_step, one_row, 0, unroll=NA_ROW_UNROLL)


def _na_attention(qkv, q_col, k_col, v_col, bias, n_kv, g, *, rows_per_step=8):
    b, t, _ = qkv.shape
    rows = t // GRID_W
    gw = g * HEAD_DIM
    assert rows % rows_per_step == 0 and rows >= NA_ROWS
    qb, kb, vb = q_col // gw, k_col // HEAD_DIM, v_col // HEAD_DIM
    tq = rows_per_step * GRID_W
    return pl.pallas_call(
        functools.partial(_na_body, g=g, rows=rows, rows_per_step=rows_per_step),
        grid=(b, n_kv, rows // rows_per_step),
        in_specs=[
            pl.BlockSpec((1, tq, gw), lambda n, h, i: (n, i, qb + h)),
            pl.BlockSpec((1, t, HEAD_DIM), lambda n, h, i: (n, 0, kb + h)),
            pl.BlockSpec((1, t, HEAD_DIM), lambda n, h, i: (n, 0, vb + h)),
            pl.BlockSpec((1, NA_ROWS, g * GRID_W, NA_ROWS * GRID_W), lambda n, h, i: (h, 0, 0, 0)),
        ],
        out_specs=pl.BlockSpec((1, tq, gw), lambda n, h, i: (n, i, h)),
        out_shape=jax.ShapeDtypeStruct((b, t, n_kv * gw), BF16),
        compiler_params=_params(("parallel", "parallel", "arbitrary")),
        name="na_attn",
    )(qkv, qkv, qkv, bias)


def _t5_bucket(rel):
    half = T5_BUCKETS // 2
    max_exact = half // 2
    n = jnp.abs(rel)
    nf = jnp.maximum(n, 1).astype(jnp.float32)
    large = max_exact + (jnp.log(nf / max_exact) / math.log(T5_MAX_DIST / max_exact)
                         * (half - max_exact)).astype(jnp.int32)
    large = jnp.minimum(large, half - 1)
    return jnp.where(rel > 0, half, 0) + jnp.where(n < max_exact, n, large)


def _band_bias(table_cols, n_kv, g, w, dil, geometry):
    tq, kw, shifts = geometry
    iq = jnp.arange(tq)[:, None]
    jk = jnp.arange(kw)[None, :]
    tiles = []
    for shift in shifts:
        rel = jk + shift - iq
        bucket = _t5_bucket(rel * dil)
        tb = jnp.zeros((tq, kw, table_cols.shape[1]), F32)
        for r in range(T5_BUCKETS):
            tb = jnp.where((bucket == r)[..., None], table_cols[r].astype(F32), tb)
        tb = jnp.where((jnp.abs(rel) <= w)[..., None], tb, NEG_INF)
        tiles.append(jnp.moveaxis(tb, -1, 0).reshape(n_kv, g * tq, kw))
    return jnp.stack(tiles)


def _na_bias(rpb, n_kv, g):
    col = jnp.arange(GRID_W)
    qc, kc = col[:, None], col[None, :]
    dc = jnp.clip(kc - qc + NA_COLS - 1, 0, 2 * NA_COLS - 2)
    cs = jnp.clip(qc - NA_COLS // 2, 0, GRID_W - NA_COLS)
    mask = (kc >= cs) & (kc < cs + NA_COLS)
    rpb = rpb.astype(F32)
    t = jnp.zeros(rpb.shape[:2] + dc.shape, F32)
    for c in range(2 * NA_COLS - 1):
        t = jnp.where(dc == c, rpb[:, :, c][:, :, None, None], t)
    t = jnp.where(mask, t, NEG_INF)
    cls = jnp.stack([t[:, o:o + NA_ROWS] for o in range(NA_ROWS)])
    cls = cls.reshape(NA_ROWS, n_kv, g, NA_ROWS, GRID_W, GRID_W)
    cls = jnp.transpose(cls, (1, 0, 2, 4, 3, 5))
    return cls.reshape(n_kv, NA_ROWS, g * GRID_W, NA_ROWS * GRID_W)


def _rope_tables(t):
    n_pairs = HEAD_DIM // 2
    n_freq = n_pairs // 2
    pos = jnp.arange(t)
    row = (pos // GRID_W).astype(jnp.float32)
    col = (pos % GRID_W).astype(jnp.float32)
    omega = ROPE_THETA ** (-(jnp.arange(n_freq, dtype=jnp.float32) * 2.0 / n_pairs))
    ang = jnp.concatenate([row[:, None] * omega, col[:, None] * omega], axis=-1)
    cos, sin = jnp.cos(ang), jnp.sin(ang)
    return jnp.concatenate([cos, cos], axis=-1), jnp.concatenate([-sin, sin], axis=-1)


def _deinterleave_heads(w, n_heads):
    lead = w.shape[:-1]
    w = w.reshape(lead + (n_heads, HEAD_DIM // 2, 2))
    return jnp.swapaxes(w, -1, -2).reshape(lead + (n_heads * HEAD_DIM,))


AB_KINDS = ("scale",) * 4 + ("qnorm",) * 4 + ("plain", "plain", "knorm", "plain_t")
CD_KINDS = ("scale",) * 4 + ("plain",) * 2 + ("scale",) * 6 + ("plain",) * 4


def kernel(x_prompt, x_sample, norm_ffn1, ffn1_w_in, ffn1_w_out, norm_mix, ab_w_in, ab_sink, ab_q_gain, ab_k_gain, ab_w_out, cd_w_in, cd_rpb, cd_w_out, norm_ffn2, ffn2_w_in, ffn2_w_out, t5_table, final_norm):
    (bp, t, d), bs = x_prompt.shape, x_sample.shape[0]
    assert x_sample.shape[1:] == (t, d)
    b = bp + bs
    depth = norm_ffn1.shape[0]
    x = (x_prompt.reshape(bp * t, d), x_sample.reshape(bs * t, d))

    ffn1_w_in, ffn1_w_out = ffn1_w_in.astype(BF16), ffn1_w_out.astype(BF16)
    ffn2_w_in, ffn2_w_out = ffn2_w_in.astype(BF16), ffn2_w_out.astype(BF16)
    qa_w, kva_w = A_HEADS * HEAD_DIM, 2 * A_KV * HEAD_DIM
    qb_lo = qa_w + kva_w
    qb_hi = qb_lo + B_HEADS * HEAD_DIM
    kb_hi = qb_hi + B_KV * HEAD_DIM
    ab_w_in = jnp.concatenate([
        ab_w_in[..., :qa_w],
        _deinterleave_heads(ab_w_in[..., qb_lo:qb_hi], B_HEADS),
        ab_w_in[..., qa_w:qb_lo],
        _deinterleave_heads(ab_w_in[..., qb_hi:kb_hi], B_KV),
        ab_w_in[..., kb_hi:]], axis=-1).astype(BF16)
    ab_q_gain = _deinterleave_heads(ab_q_gain, 1)
    ab_k_gain = _deinterleave_heads(ab_k_gain, 1)
    ab_w_out, cd_w_in, cd_w_out = ab_w_out.astype(BF16), cd_w_in.astype(BF16), cd_w_out.astype(BF16)

    cos, sin = _rope_tables(t)
    ga, gc = A_HEADS // A_KV, C_HEADS // C_KV
    geo_a = _band_geometry(t, A_WINDOW)
    bias_a = _band_bias(t5_table[:, :A_HEADS], A_KV, ga, A_WINDOW, 1, geo_a)
    d_groups = []
    for gi, (win, dil) in enumerate(D_PAIRS):
        hs = win // (2 * dil)
        geo = _band_geometry(t // dil, hs)
        col0 = A_HEADS + gi * D_SLOTS
        d_groups.append((dil, hs, geo, _band_bias(t5_table[:, col0:col0 + D_SLOTS], D_SLOTS, 1, hs, dil, geo)))

    a_q = 0
    a_k = qa_w + B_HEADS * HEAD_DIM
    a_v = a_k + A_KV * HEAD_DIM
    b_k = a_v + A_KV * HEAD_DIM
    c_q, c_k, c_v = 0, C_HEADS * HEAD_DIM, (C_HEADS + C_KV) * HEAD_DIM
    d_q = (C_HEADS + 2 * C_KV) * HEAD_DIM
    d_k = d_q + D_HEADS * HEAD_DIM
    d_v = d_k + D_SLOTS * HEAD_DIM
    dw = D_SLOTS * HEAD_DIM
    sub_dils = tuple(dil for _, dil in D_PAIRS if dil > 1)
    sub_plan = {}
    for gi, (_, dil) in enumerate(D_PAIRS):
        if dil > 1:
            si = sub_dils.index(dil)
            for src, dst in ((d_q + gi * dw, 0), (d_k, dw), (d_v, 2 * dw)):
                for off in range(0, dw, MXU_COLS):
                    sub_plan.setdefault((src + off) // MXU_COLS, []).append((si, dst + off))

    for l in range(depth):
        i = l // 2
        x = _ffn(x, norm_ffn1[l], ffn1_w_in, ffn1_w_out, l)
        if l % 2 == 0:
            qkv, qbt, vbt = _inproj(x, norm_mix[l], ab_w_in, i, AB_KINDS, t,
                                    (ab_q_gain[i], ab_k_gain[i], cos, sin))
            qkv = qkv.reshape(b, t, -1)
            o_a = _band_attention(qkv, a_q, qkv, a_k, qkv, a_v, bias_a, A_KV, ga, A_WINDOW, geo_a,
                                  sink=ab_sink[i])
            o_b = _dense_attention(qbt, qkv, vbt, b_k, B_KV, B_HEADS // B_KV)
            x = _outproj_ab(x, o_a.reshape(b * t, -1), o_b.reshape(b * t, -1), ab_w_out, i)
        else:
            qkv, *subs = _inproj(x, norm_mix[l], cd_w_in, i, CD_KINDS, t,
                                 sub_args=(sub_dils, 3 * dw, sub_plan))
            qkv = qkv.reshape(b, t, -1)
            o_c = _na_attention(qkv, c_q, c_k, c_v, _na_bias(cd_rpb[i], C_KV, gc), C_KV, gc)
            outs, lses = [], []
            for gi, (dil, hs, geo, bias_d) in enumerate(d_groups):
                if dil == 1:
                    o_g, lse_g = _band_attention(qkv, d_q + gi * dw, qkv, d_k, qkv, d_v, bias_d,
                                                 D_SLOTS, 1, hs, geo, want_lse=True, out_dtype=F32)
                    o_g, lse_g = o_g.reshape(b * t, dw), lse_g.reshape(b * t, LANES)
                else:
                    sub = subs[sub_dils.index(dil)].reshape(b * dil, t // dil, 3 * dw)
                    o_g, lse_g = _band_attention(sub, 0, sub, dw, sub, 2 * dw, bias_d, D_SLOTS, 1, hs, geo,
                                                 want_lse=True, out_dtype=F32, dil=dil)
                outs.append(o_g)
                lses.append(lse_g)
            x = _outproj_cd(x, o_c.reshape(b * t, -1), outs, lses, [g[0] for g in d_groups], t, cd_w_out, i)
        last = l == depth - 1
        x = _ffn(x, norm_ffn2[l], ffn2_w_in, ffn2_w_out, l, final_gain=final_norm if last else None,
                 out_rows=(bp * t, bs * t) if last else None)

    return (x[0].reshape(bp, t, d), x[1].reshape(bs, t, d))
```

```python
import functools
import math

import jax
import jax.numpy as jnp
from jax import lax
from jax.experimental import pallas as pl
from jax.experimental.pallas import tpu as pltpu

HEAD_DIM = 128
GRID_W = 64
A_HEADS, A_KV, A_WINDOW = 8, 2, 128
B_HEADS, B_KV = 8, 2
ROPE_THETA = 10000.0
C_HEADS, C_KV = 8, 2
NA_ROWS, NA_COLS = 8, 16
D_PAIRS = ((128, 1), (512, 4), (2048, 16))
D_SLOTS = 4
D_HEADS = D_SLOTS * len(D_PAIRS)
T5_BUCKETS = 32
T5_MAX_DIST = 2048
NORM_EPS = 1e-6
NEG_INF = -1e30
ATTN_SCALE = HEAD_DIM ** -0.5
LOG2_E = math.log2(math.e)

LANES = 128
SUBLANES = 8
MXU_COLS = 256
FFN_ROW_TILE = 1024
FFN_FF_TILE = 2 * MXU_COLS
DENSE_KEY_CHUNK = MXU_COLS
CHEAP_ROW_STRIDE = 4
BAND_TILES_PER_STEP = (8, 4, 2, 1)
NA_ROWS_PER_STEP = 16
MIB = 1024 * 1024
VMEM_LIMIT_BYTES = 56 * MIB

BF16 = jnp.bfloat16
F32 = jnp.float32


def _params(semantics):
    return pltpu.CompilerParams(dimension_semantics=semantics, vmem_limit_bytes=VMEM_LIMIT_BYTES)


def _resident_weight(shape, layer):
    return pl.BlockSpec((None,) + shape, lambda *_: (layer,) + (0,) * len(shape),
                        pipeline_mode=pl.Buffered(1))


def _rms(x, gain):
    ms = jnp.mean(x * x, axis=-1, keepdims=True)
    return x * lax.rsqrt(ms + NORM_EPS) * gain


def _ffn_body(*refs, n_ff, n_in, n_out, tiles_a, final_norm):
    refs = list(refs)
    x_refs = [refs.pop(0) for _ in range(n_in)]
    gain_ref, wg_ref, wu_ref, wo_ref = [refs.pop(0) for _ in range(4)]
    fgain_ref = refs.pop(0) if final_norm else None
    o_refs = [refs.pop(0) for _ in range(n_out)]
    h_scr = refs.pop(0)
    acc_ref = refs.pop(0) if n_out == 2 else o_refs[0]
    i, j = pl.program_id(0), pl.program_id(1)

    def start(x_ref):
        x = x_ref[...]
        h_scr[...] = _rms(x, gain_ref[...]).astype(BF16)
        acc_ref[...] = x

    if n_in == 1:
        pl.when(j == 0)(lambda: start(x_refs[0]))
    else:
        pl.when((j == 0) & (i < tiles_a))(lambda: start(x_refs[0]))
        pl.when((j == 0) & (i >= tiles_a))(lambda: start(x_refs[1]))

    h = h_scr[...]
    g = jnp.dot(h, wg_ref[...], preferred_element_type=F32)
    u = jnp.dot(h, wu_ref[...], preferred_element_type=F32)
    a = (g * jax.nn.sigmoid(g) * (0.5 * u)).astype(BF16)
    acc_ref[...] += jnp.dot(a, wo_ref[...], preferred_element_type=F32)

    def finish(o_ref):
        y = acc_ref[...]
        o_ref[...] = _rms(y, fgain_ref[...]) if final_norm else y

    if n_out == 2:
        pl.when((j == n_ff - 1) & (i < tiles_a))(lambda: finish(o_refs[0]))
        pl.when((j == n_ff - 1) & (i >= tiles_a))(lambda: finish(o_refs[1]))
    elif final_norm:
        pl.when(j == n_ff - 1)(lambda: finish(o_refs[0]))


def _ffn(xs, gain, w_in, w_out, layer, final_gain=None, out_rows=None, *, tm=FFN_ROW_TILE):
    xs = tuple(xs) if isinstance(xs, (tuple, list)) else (xs,)
    d = xs[0].shape[1]
    n = sum(x.shape[0] for x in xs)
    f = w_out.shape[1]
    tf = math.gcd(FFN_FF_TILE, f)
    n_ff = f // tf
    if len(xs) == 2 or out_rows is not None:
        tm //= 2
    tm = math.gcd(tm, *(x.shape[0] for x in xs))
    assert w_in.shape[1:] == (d, 2 * f) and w_out.shape[2] == d
    n_in, n_out = len(xs), 1 if out_rows is None else 2
    rows_a = xs[0].shape[0] if n_in == 2 else (out_rows[0] if n_out == 2 else n)
    assert n_in == 1 or n_out == 1 or out_rows[0] == rows_a
    tiles_a = rows_a // tm
    final_norm = final_gain is not None
    seg_a = lambda i, j: (jnp.minimum(i, tiles_a - 1), 0)
    seg_b = lambda i, j: (jnp.maximum(i - tiles_a, 0), 0)
    whole = lambda i, j: (i, 0)
    in_specs = [pl.BlockSpec((tm, d), m) for m in ((whole,) if n_in == 1 else (seg_a, seg_b))]
    in_specs += [
        pl.BlockSpec((1, d), lambda i, j: (0, 0)),
        pl.BlockSpec((None, d, tf), lambda i, j: (layer, 0, j)),
        pl.BlockSpec((None, d, tf), lambda i, j: (layer, 0, j + n_ff)),
        pl.BlockSpec((None, tf, d), lambda i, j: (layer, j, 0)),
    ]
    args = list(xs) + [gain.reshape(1, d), w_in, w_in, w_out]
    if final_norm:
        in_specs.append(pl.BlockSpec((1, d), lambda i, j: (0, 0)))
        args.append(final_gain.reshape(1, d))
    scratch = [pltpu.VMEM((tm, d), BF16)]
    if n_out == 1:
        out_specs = pl.BlockSpec((tm, d), whole)
        out_shape = jax.ShapeDtypeStruct((n, d), F32)
    else:
        assert out_rows[0] % tm == 0 and sum(out_rows) == n
        out_specs = [pl.BlockSpec((tm, d), seg_a), pl.BlockSpec((tm, d), seg_b)]
        out_shape = [jax.ShapeDtypeStruct((r, d), F32) for r in out_rows]
        scratch.append(pltpu.VMEM((tm, d), F32))
    row_sem = "parallel" if n_in == 1 and n_out == 1 else "arbitrary"
    return pl.pallas_call(
        functools.partial(_ffn_body, n_ff=n_ff, n_in=n_in, n_out=n_out, tiles_a=tiles_a,
                          final_norm=final_norm),
        grid=(n // tm, n_ff),
        in_specs=in_specs,
        out_specs=out_specs,
        out_shape=out_shape,
        scratch_shapes=scratch,
        compiler_params=_params((row_sem, "arbitrary")),
        name="ffn",
    )(*args)


def _rope_norm(r, gain, cos, sin):
    y = _rms(r, gain)
    return y * cos + pltpu.roll(y, HEAD_DIM // 2, axis=1) * sin


def _inproj_body(*refs, kinds, rope, sub_plan, dils):
    if rope:
        x_ref, gain_ref, w_ref, qg_ref, kg_ref, cos_ref, sin_ref, o_ref, qt_ref, vt_ref = refs
    else:
        x_ref, gain_ref, w_ref, o_ref = refs[:4]
        sub_refs, slab_scr = refs[4:4 + len(dils)], refs[-1] if dils else None
    tm = x_ref.shape[0]
    h = _rms(x_ref[...], gain_ref[...]).astype(BF16)
    n_qt = 0
    for c, kind in enumerate(kinds):
        lo = c * MXU_COLS
        r = jnp.dot(h, w_ref[:, lo:lo + MXU_COLS], preferred_element_type=F32)
        if kind == "scale":
            r = r * ATTN_SCALE
        elif kind in ("qnorm", "knorm"):
            hg = qg_ref[...] if kind == "qnorm" else kg_ref[...]
            cos = cos_ref[...]
            sin = sin_ref[...]
            parts = []
            for hh in range(MXU_COLS // HEAD_DIM):
                y = _rope_norm(r[:, hh * HEAD_DIM:(hh + 1) * HEAD_DIM], hg, cos, sin)
                if kind == "qnorm":
                    y = y * (ATTN_SCALE * LOG2_E)
                parts.append(y)
            r = jnp.concatenate(parts, axis=1)
            if kind == "qnorm":
                qt_ref[0, n_qt * MXU_COLS:(n_qt + 1) * MXU_COLS, :] = r.T.astype(BF16)
                n_qt += 1
        elif kind == "plain_t":
            vt_ref[0] = r.T.astype(BF16)
        o_ref[:, lo:lo + MXU_COLS] = r.astype(BF16)
        if c in sub_plan:
            heads = range(MXU_COLS // HEAD_DIM)
            q_rows = tm // CHEAP_ROW_STRIDE
            for hh in heads:
                slab_scr[0, hh] = r[:, hh * HEAD_DIM:(hh + 1) * HEAD_DIM]
            if any(dils[si] > CHEAP_ROW_STRIDE for si, _ in sub_plan[c]):
                for hh in heads:
                    for r4 in range(CHEAP_ROW_STRIDE):
                        slab_scr[1, hh, r4 * q_rows:(r4 + 1) * q_rows, :] = (
                            slab_scr[0, hh, pl.ds(r4, q_rows, stride=CHEAP_ROW_STRIDE), :])
            for si, col0 in sub_plan[c]:
                d = dils[si]
                for rho in range(d):
                    for hh in heads:
                        if d <= CHEAP_ROW_STRIDE:
                            part = slab_scr[0, hh, pl.ds(rho, tm // d, stride=d), :]
                        else:
                            r4, r2 = rho % CHEAP_ROW_STRIDE, rho // CHEAP_ROW_STRIDE
                            part = slab_scr[1, hh, pl.ds(r4 * q_rows + r2, tm // d,
                                                         stride=d // CHEAP_ROW_STRIDE), :]
                        cols = slice(col0 + hh * HEAD_DIM, col0 + (hh + 1) * HEAD_DIM)
                        sub_refs[si][0, rho, :, cols] = part.astype(BF16)


def _inproj(x, gain, w, layer, kinds, seq_len, rope_args=None, sub_args=None, *, tm=512):
    n, d = x.shape
    c = w.shape[2]
    assert n % tm == 0 and seq_len % tm == 0 and c == MXU_COLS * len(kinds)
    rope = rope_args is not None
    assert not (rope and sub_args)
    dils, sub_width, sub_plan = sub_args if sub_args else ((), 0, {})
    scratch = [pltpu.VMEM((2, MXU_COLS // HEAD_DIM, tm, HEAD_DIM), F32)] if dils else []
    assert all(dl <= CHEAP_ROW_STRIDE or (dl % CHEAP_ROW_STRIDE == 0 and dl <= CHEAP_ROW_STRIDE ** 2)
               for dl in dils)
    in_specs = [
        pl.BlockSpec((tm, d), lambda i: (i, 0)),
        pl.BlockSpec((1, d), lambda i: (0, 0)),
        _resident_weight((d, c), layer),
    ]
    args = [x, gain.reshape(1, d), w]
    if rope:
        q_gain, k_gain, cos, sin = rope_args
        tiles_per_seq = seq_len // tm
        in_specs += [
            pl.BlockSpec((1, HEAD_DIM), lambda i: (0, 0)),
            pl.BlockSpec((1, HEAD_DIM), lambda i: (0, 0)),
            pl.BlockSpec((tm, HEAD_DIM), lambda i: (i % tiles_per_seq, 0)),
            pl.BlockSpec((tm, HEAD_DIM), lambda i: (i % tiles_per_seq, 0)),
        ]
        args += [q_gain.reshape(1, HEAD_DIM), k_gain.reshape(1, HEAD_DIM), cos, sin]
    out_specs = pl.BlockSpec((tm, c), lambda i: (i, 0))
    out_shape = jax.ShapeDtypeStruct((n, c), BF16)
    if rope:
        assert kinds.count("plain_t") == 1
        qt_rows = MXU_COLS * kinds.count("qnorm")
        by_seq = lambda i: (i // tiles_per_seq, 0, i % tiles_per_seq)
        out_specs = [out_specs, pl.BlockSpec((1, qt_rows, tm), by_seq),
                     pl.BlockSpec((1, MXU_COLS, tm), by_seq)]
        out_shape = [out_shape, jax.ShapeDtypeStruct((n // seq_len, qt_rows, seq_len), BF16),
                     jax.ShapeDtypeStruct((n // seq_len, MXU_COLS, seq_len), BF16)]
    if dils:
        tiles_per_seq = seq_len // tm
        assert all(tm % (dl * 16) == 0 for dl in dils)
        out_specs = [out_specs] + [
            pl.BlockSpec((1, dl, tm // dl, sub_width), lambda i: (i // tiles_per_seq, 0, i % tiles_per_seq, 0))
            for dl in dils]
        out_shape = [out_shape] + [
            jax.ShapeDtypeStruct((n // seq_len, dl, seq_len // dl, sub_width), BF16) for dl in dils]
    return pl.pallas_call(
        functools.partial(_inproj_body, kinds=tuple(kinds), rope=rope, sub_plan=sub_plan, dils=tuple(dils)),
        grid=(n // tm,),
        in_specs=in_specs,
        out_specs=out_specs,
        out_shape=out_shape,
        scratch_shapes=scratch,
        compiler_params=_params(("parallel",)),
        name="inproj",
    )(*args)


def _outproj_ab_body(x_ref, a_ref, b_ref, w_ref, o_ref):
    ka = a_ref.shape[1]
    acc = jnp.dot(a_ref[...], w_ref[0:ka, :], preferred_element_type=F32)
    acc = acc + jnp.dot(b_ref[...], w_ref[ka:, :], preferred_element_type=F32)
    o_ref[...] = x_ref[...] + acc


def _outproj_ab(x, oa, ob, w, layer, *, tm=512):
    n, d = x.shape
    ka, kb = oa.shape[1], ob.shape[1]
    assert n % tm == 0 and w.shape[1:] == (ka + kb, d)
    return pl.pallas_call(
        _outproj_ab_body,
        grid=(n // tm,),
        in_specs=[
            pl.BlockSpec((tm, d), lambda i: (i, 0)),
            pl.BlockSpec((tm, ka), lambda i: (i, 0)),
            pl.BlockSpec((tm, kb), lambda i: (i, 0)),
            _resident_weight((ka + kb, d), layer),
        ],
        out_specs=pl.BlockSpec((tm, d), lambda i: (i, 0)),
        out_shape=jax.ShapeDtypeStruct((n, d), F32),
        compiler_params=_params(("parallel",)),
        name="outproj_ab",
    )(x, oa, ob, w)


def _outproj_cd_body(*refs, dils, kd):
    ng = len(dils)
    x_ref, c_ref = refs[:2]
    o_refs, l_refs = refs[2:2 + ng], refs[2 + ng:2 + 2 * ng]
    w_ref, o_ref, row_scr = refs[2 + 2 * ng:]
    tm = x_ref.shape[0]
    heads = kd // HEAD_DIM

    def token_rows(ref, d, slot, width):
        if d == 1:
            return ref[...]
        for rho in range(d):
            for s in range(width // LANES):
                lo = rho * width + s * LANES
                row_scr[slot, s, pl.ds(rho, tm // d, stride=d), :] = ref[0, :, lo:lo + LANES]
        return jnp.concatenate([row_scr[slot, s] for s in range(width // LANES)], axis=1)

    outs = [token_rows(r, d, 2 * i, kd) for i, (r, d) in enumerate(zip(o_refs, dils))]
    lses = [token_rows(r, d, 2 * i + 1, LANES) for i, (r, d) in enumerate(zip(l_refs, dils))]
    m = functools.reduce(jnp.maximum, lses)
    es = [jnp.exp(l - m) for l in lses]
    den = functools.reduce(lambda a, b: a + b, es)
    wts = [e / den for e in es]
    parts = []
    for hd in range(heads):
        lane = hd * (LANES // heads)
        cols = slice(hd * HEAD_DIM, (hd + 1) * HEAD_DIM)
        terms = [jnp.broadcast_to(wt[:, lane:lane + 1], (tm, HEAD_DIM)) * o[:, cols] for wt, o in zip(wts, outs)]
        parts.append(functools.reduce(lambda a, b: a + b, terms))
    od = jnp.concatenate(parts, axis=1)
    kc = c_ref.shape[1]
    acc = jnp.dot(c_ref[...], w_ref[0:kc, :], preferred_element_type=F32)
    acc = acc + jnp.dot(od.astype(BF16), w_ref[kc:, :], preferred_element_type=F32)
    o_ref[...] = x_ref[...] + acc


def _outproj_cd(x, oc, outs, lses, dils, seq_len, w, layer, *, tm=512):
    n, d = x.shape
    kc = oc.shape[1]
    kd = w.shape[1] - kc
    assert n % tm == 0 and seq_len % tm == 0 and all(tm % (dl * SUBLANES) == 0 for dl in dils)
    tiles_per_seq = seq_len // tm
    row = lambda i: (i, 0)
    by_seq = lambda i: (i // tiles_per_seq, i % tiles_per_seq, 0)

    def g_specs(width):
        return [pl.BlockSpec((tm, width), row) if dl == 1 else pl.BlockSpec((1, tm // dl, dl * width), by_seq)
                for dl in dils]

    return pl.pallas_call(
        functools.partial(_outproj_cd_body, dils=tuple(dils), kd=kd),
        grid=(n // tm,),
        in_specs=[pl.BlockSpec((tm, d), row), pl.BlockSpec((tm, kc), row)] + g_specs(kd) + g_specs(LANES)
        + [_resident_weight((kc + kd, d), layer)],
        out_specs=pl.BlockSpec((tm, d), row),
        out_shape=jax.ShapeDtypeStruct((n, d), F32),
        scratch_shapes=[pltpu.VMEM((2 * len(dils), kd // HEAD_DIM, tm, HEAD_DIM), F32)],
        compiler_params=_params(("parallel",)),
        name="outproj_cd",
    )(x, oc, *outs, *lses, w)


def _stack_heads(q, g):
    return jnp.concatenate([q[:, i * HEAD_DIM:(i + 1) * HEAD_DIM] for i in range(g)], axis=0)


def _softmax_pv(s, v):
    m = jnp.max(s, axis=-1, keepdims=True)
    p = jnp.exp(s - m)
    l = jnp.sum(p, axis=-1, keepdims=True)
    o = jnp.dot(p.astype(BF16), v, preferred_element_type=F32) / l
    return o, m, l


def _qk(q, k):
    return lax.dot_general(q, k, (((1,), (1,)), ((), ())), preferred_element_type=F32)


def _dense_body(qt_ref, k_ref, vt_ref, o_ref, s_scr, p_scr, m_scr, l_scr, *, g):
    step = pl.program_id(0)
    tq = qt_ref.shape[2] // 2
    t = k_ref.shape[1]
    gq = g * tq
    kc = min(DENSE_KEY_CHUNK, t)

    @pl.when(step == 0)
    def _():
        s_scr[1] = jnp.zeros(s_scr.shape[1:], s_scr.dtype)
        m_scr[1] = jnp.zeros(m_scr.shape[1:], m_scr.dtype)
        p_scr[...] = jnp.zeros(p_scr.shape, p_scr.dtype)
        l_scr[...] = jnp.ones(l_scr.shape, l_scr.dtype)

    def fold(x, op):
        return op(x.reshape(x.shape[0] // SUBLANES, SUBLANES, x.shape[1]), axis=0)

    for cur in range(2):
        prev = 1 - cur
        toks = slice(cur * tq, (cur + 1) * tq)
        qt = jnp.concatenate([qt_ref[0, i * HEAD_DIM:(i + 1) * HEAD_DIM, toks] for i in range(g)], axis=1)
        m = jnp.max(m_scr[prev], axis=0, keepdims=True)
        ot = jnp.zeros((HEAD_DIM, gq), F32)
        l_acc = jnp.zeros((SUBLANES, gq), F32)
        m_acc = None
        for c in range(t // kc):
            keys = slice(c * kc, (c + 1) * kc)
            p = jnp.exp2(s_scr[prev, keys, :] - m)
            l_acc = l_acc + fold(p, jnp.sum)
            p_scr[prev, keys, :] = p.astype(BF16)
        for c in range(t // kc):
            keys = slice(c * kc, (c + 1) * kc)
            ot = ot + jnp.dot(vt_ref[0, :, keys], p_scr[cur, keys, :], preferred_element_type=F32)
            s = jnp.dot(k_ref[0, keys, :], qt, preferred_element_type=F32)
            s_scr[cur, keys, :] = s
            m_chunk = fold(s, jnp.max)
            m_acc = m_chunk if m_acc is None else jnp.maximum(m_acc, m_chunk)
        ot = ot / jnp.sum(l_scr[cur], axis=0, keepdims=True)
        l_scr[prev] = l_acc
        m_scr[cur] = m_acc
        for i in range(g):
            o_ref[0, toks, i * HEAD_DIM:(i + 1) * HEAD_DIM] = ot[:, i * tq:(i + 1) * tq].T.astype(o_ref.dtype)


def _dense_attention(qt, qkv, vt, k_col, n_kv, g, *, tq=128):
    b, t, _ = qkv.shape
    gw = g * HEAD_DIM
    n_pairs = t // (2 * tq)
    assert t % (2 * tq) == 0 and k_col % HEAD_DIM == 0
    assert qt.shape == (b, n_kv * gw, t) and vt.shape == (b, n_kv * HEAD_DIM, t)
    kb = k_col // HEAD_DIM
    total = b * n_kv * n_pairs

    def where(f):
        return f // (n_kv * n_pairs), (f // n_pairs) % n_kv, f % n_pairs

    def q_map(f):
        n, h, k = where(jnp.minimum(f, total - 1))
        return n, h, k

    def k_map(f):
        n, h, _ = where(jnp.minimum(f, total - 1))
        return n, 0, kb + h

    def v_map(f):
        n, h, _ = where(jnp.maximum(f - 1, 0))
        return n, h, 0

    def o_map(f):
        n, h, k = where(jnp.maximum(f - 1, 0))
        return n, k, h

    return pl.pallas_call(
        functools.partial(_dense_body, g=g),
        grid=(total + 1,),
        in_specs=[
            pl.BlockSpec((1, gw, 2 * tq), q_map),
            pl.BlockSpec((1, t, HEAD_DIM), k_map),
            pl.BlockSpec((1, HEAD_DIM, t), v_map),
        ],
        out_specs=pl.BlockSpec((1, 2 * tq, gw), o_map),
        out_shape=jax.ShapeDtypeStruct((b, t, n_kv * gw), BF16),
        scratch_shapes=[pltpu.VMEM((2, t, g * tq), F32), pltpu.VMEM((2, t, g * tq), BF16),
                        pltpu.VMEM((2, SUBLANES, g * tq), F32), pltpu.VMEM((2, SUBLANES, g * tq), F32)],
        compiler_params=_params(("arbitrary",)),
        name="dense_attn",
    )(qt, qkv, vt)


def _band_body(*refs, n_kv, g, w, seq, tq, kw, n_cases, has_sink, want_lse, side_by_side):
    refs = list(refs)
    q_ref, k_ref, v_ref, bias_ref = refs[:4]
    refs = refs[4:]
    sink_ref = refs.pop(0) if has_sink else None
    o_ref = refs.pop(0)
    lse_ref = refs.pop(0) if want_lse else None
    gw = g * HEAD_DIM
    qw = n_kv * gw
    nq = seq // tq
    for sq, j in [(a, c) for a in range(q_ref.shape[0]) for c in range(q_ref.shape[1] // tq)]:
        qi = pl.program_id(1) * (q_ref.shape[1] // tq) + j
        kstart = pl.multiple_of(jnp.clip(qi * tq - w, 0, seq - kw), 64)
        case = 0 if n_cases == 1 else jnp.where(qi == 0, 0, jnp.where(qi == nq - 1, 2, 1))
        trows = slice(j * tq, (j + 1) * tq)
        osq, ocol0, lcol0 = (0, sq * qw, sq * LANES) if side_by_side else (sq, 0, 0)
        lse_tile = jnp.zeros((tq, LANES), F32)
        head_of_lane = lax.broadcasted_iota(jnp.int32, (tq, LANES), 1) // (LANES // (n_kv * g))
        for h in range(n_kv):
            k = k_ref[sq, pl.ds(kstart, kw), h * HEAD_DIM:(h + 1) * HEAD_DIM]
            v = v_ref[sq, pl.ds(kstart, kw), h * HEAD_DIM:(h + 1) * HEAD_DIM]
            q = _stack_heads(q_ref[sq, trows, h * gw:(h + 1) * gw], g)
            s = _qk(q, k) + bias_ref[case, h]
            o, m, l = _softmax_pv(s, v)
            lse = m + jnp.log(l)
            for i in range(g):
                rows = slice(i * tq, (i + 1) * tq)
                cols = slice(h * gw + i * HEAD_DIM, h * gw + (i + 1) * HEAD_DIM)
                oi = o[rows]
                if has_sink:
                    oi = oi * jax.nn.sigmoid(lse[rows] - sink_ref[h * g + i])
                o_ref[osq, trows, ocol0 + cols.start:ocol0 + cols.stop] = oi.astype(o_ref.dtype)
                if want_lse:
                    lse_tile = jnp.where(head_of_lane == h * g + i, lse[rows], lse_tile)
        if want_lse:
            lse_ref[osq, trows, lcol0:lcol0 + LANES] = lse_tile


def _band_geometry(seq, w):
    if seq <= 4 * w:
        return seq, seq, (0,)
    tq = 2 * w if w < 128 else w
    tq = min(tq, 256)
    kw = tq + 2 * w
    assert seq % tq == 0 and seq >= kw
    return tq, kw, (0, -w, -2 * w)


def _band_attention(q_arr, q_col, k_arr, k_col, v_arr, v_col, bias, n_kv, g, w, geometry,
                    sink=None, want_lse=False, out_dtype=BF16, dil=1):
    n, seq, _ = q_arr.shape
    tq, kw, shifts = geometry
    nq = seq // tq
    gw = g * HEAD_DIM
    qw, kvw = n_kv * gw, n_kv * HEAD_DIM
    assert q_col % qw == 0 and k_col % kvw == 0 and v_col % kvw == 0
    qb, kb, vb = q_col // qw, k_col // kvw, v_col // kvw
    tps = next(c for c in BAND_TILES_PER_STEP if nq % c == 0)
    nq_steps = nq // tps
    sps = 1
    if nq == 1:
        sps = next(c for c in BAND_TILES_PER_STEP if n % c == 0 and (dil == 1 or dil % c == 0))
    in_specs = [
        pl.BlockSpec((sps, tps * tq, qw), lambda b, i: (b, i, qb)),
        pl.BlockSpec((sps, seq, kvw), lambda b, i: (b, 0, kb)),
        pl.BlockSpec((sps, seq, kvw), lambda b, i: (b, 0, vb)),
        pl.BlockSpec((len(shifts), n_kv, g * tq, kw), lambda b, i: (0, 0, 0, 0)),
    ]
    args = [q_arr, k_arr, v_arr, bias]
    if sink is not None:
        in_specs.append(pl.BlockSpec(memory_space=pltpu.SMEM))
        args.append(sink.astype(F32))
    def out_like(width, dtype):
        if dil == 1:
            return (pl.BlockSpec((sps, tps * tq, width), lambda b, i: (b, i, 0)),
                    jax.ShapeDtypeStruct((n, seq, width), dtype))
        return (pl.BlockSpec((1, tps * tq, sps * width),
                             lambda b, i: ((b * sps) // dil, i, ((b * sps) % dil) // sps)),
                jax.ShapeDtypeStruct((n // dil, seq, dil * width), dtype))

    assert dil == 1 or (n % dil == 0 and dil % sps == 0)
    out_specs, out_shape = out_like(qw, out_dtype)
    if want_lse:
        assert LANES % (n_kv * g) == 0
        l_spec, l_shape = out_like(LANES, F32)
        out_specs, out_shape = [out_specs, l_spec], [out_shape, l_shape]
    return pl.pallas_call(
        functools.partial(_band_body, n_kv=n_kv, g=g, w=w, seq=seq, tq=tq, kw=kw, n_cases=len(shifts),
                          has_sink=sink is not None, want_lse=want_lse, side_by_side=dil > 1),
        grid=(n // sps, nq_steps),
        in_specs=in_specs,
        out_specs=out_specs,
        out_shape=out_shape,
        compiler_params=_params(("parallel", "arbitrary")),
        name="band_attn",
    )(*args)


def _na_body(q_ref, k_ref, v_ref, bias_ref, o_ref, *, g, rows, rows_per_step):
    rb = pl.program_id(2)
    kwin = NA_ROWS * GRID_W

    def one_row(rr, carry):
        r = rb * rows_per_step + rr
        rs = jnp.clip(r - NA_ROWS // 2, 0, rows - NA_ROWS)
        off = rs - r + NA_ROWS - 1
        kstart = pl.multiple_of(rs * GRID_W, GRID_W)
        qstart = pl.multiple_of(rr * GRID_W, GRID_W)
        k = k_ref[0, pl.ds(kstart, kwin), :]
        v = v_ref[0, pl.ds(kstart, kwin), :]
        q = _stack_heads(q_ref[0, pl.ds(qstart, GRID_W), :], g)
        s = _qk(q, k) + bias_ref[0, off]
        o, _, _ = _softmax_pv(s, v)
        for i in range(g):
            o_ref[0, pl.ds(qstart, GRID_W), i * HEAD_DIM:(i + 1) * HEAD_DIM] = (
                o[i * GRID_W:(i + 1) * GRID_W].astype(o_ref.dtype))
        return carry

    lax.fori_loop(0, rows_per_step, one_row, 0, unroll=True)


def _na_attention(qkv, q_col, k_col, v_col, bias, n_kv, g, *, rows_per_step=NA_ROWS_PER_STEP):
    b, t, _ = qkv.shape
    rows = t // GRID_W
    gw = g * HEAD_DIM
    assert rows % rows_per_step == 0 and rows >= NA_ROWS
    qb, kb, vb = q_col // gw, k_col // HEAD_DIM, v_col // HEAD_DIM
    tq = rows_per_step * GRID_W
    return pl.pallas_call(
        functools.partial(_na_body, g=g, rows=rows, rows_per_step=rows_per_step),
        grid=(b, n_kv, rows // rows_per_step),
        in_specs=[
            pl.BlockSpec((1, tq, gw), lambda n, h, i: (n, i, qb + h)),
            pl.BlockSpec((1, t, HEAD_DIM), lambda n, h, i: (n, 0, kb + h)),
            pl.BlockSpec((1, t, HEAD_DIM), lambda n, h, i: (n, 0, vb + h)),
            pl.BlockSpec((1, NA_ROWS, g * GRID_W, NA_ROWS * GRID_W), lambda n, h, i: (h, 0, 0, 0)),
        ],
        out_specs=pl.BlockSpec((1, tq, gw), lambda n, h, i: (n, i, h)),
        out_shape=jax.ShapeDtypeStruct((b, t, n_kv * gw), BF16),
        compiler_params=_params(("parallel", "parallel", "arbitrary")),
        name="na_attn",
    )(qkv, qkv, qkv, bias)


def _t5_bucket(rel):
    half = T5_BUCKETS // 2
    max_exact = half // 2
    n = jnp.abs(rel)
    nf = jnp.maximum(n, 1).astype(jnp.float32)
    large = max_exact + (jnp.log(nf / max_exact) / math.log(T5_MAX_DIST / max_exact)
                         * (half - max_exact)).astype(jnp.int32)
    large = jnp.minimum(large, half - 1)
    return jnp.where(rel > 0, half, 0) + jnp.where(n < max_exact, n, large)


def _band_bias(table_cols, n_kv, g, w, dil, geometry):
    tq, kw, shifts = geometry
    iq = jnp.arange(tq)[:, None]
    jk = jnp.arange(kw)[None, :]
    tiles = []
    for shift in shifts:
        rel = jk + shift - iq
        bucket = _t5_bucket(rel * dil)
        tb = jnp.zeros((tq, kw, table_cols.shape[1]), F32)
        for r in range(T5_BUCKETS):
            tb = jnp.where((bucket == r)[..., None], table_cols[r].astype(F32), tb)
        tb = jnp.where((jnp.abs(rel) <= w)[..., None], tb, NEG_INF)
        tiles.append(jnp.moveaxis(tb, -1, 0).reshape(n_kv, g * tq, kw))
    return jnp.stack(tiles)


def _na_bias(rpb, n_kv, g):
    col = jnp.arange(GRID_W)
    qc, kc = col[:, None], col[None, :]
    dc = jnp.clip(kc - qc + NA_COLS - 1, 0, 2 * NA_COLS - 2)
    cs = jnp.clip(qc - NA_COLS // 2, 0, GRID_W - NA_COLS)
    mask = (kc >= cs) & (kc < cs + NA_COLS)
    rpb = rpb.astype(F32)
    t = jnp.zeros(rpb.shape[:2] + dc.shape, F32)
    for c in range(2 * NA_COLS - 1):
        t = jnp.where(dc == c, rpb[:, :, c][:, :, None, None], t)
    t = jnp.where(mask, t, NEG_INF)
    cls = jnp.stack([t[:, o:o + NA_ROWS] for o in range(NA_ROWS)])
    cls = cls.reshape(NA_ROWS, n_kv, g, NA_ROWS, GRID_W, GRID_W)
    cls = jnp.transpose(cls, (1, 0, 2, 4, 3, 5))
    return cls.reshape(n_kv, NA_ROWS, g * GRID_W, NA_ROWS * GRID_W)


def _rope_tables(t):
    n_pairs = HEAD_DIM // 2
    n_freq = n_pairs // 2
    pos = jnp.arange(t)
    row = (pos // GRID_W).astype(jnp.float32)
    col = (pos % GRID_W).astype(jnp.float32)
    omega = ROPE_THETA ** (-(jnp.arange(n_freq, dtype=jnp.float32) * 2.0 / n_pairs))
    ang = jnp.concatenate([row[:, None] * omega, col[:, None] * omega], axis=-1)
    cos, sin = jnp.cos(ang), jnp.sin(ang)
    return jnp.concatenate([cos, cos], axis=-1), jnp.concatenate([-sin, sin], axis=-1)


def _deinterleave_heads(w, n_heads):
    lead = w.shape[:-1]
    w = w.reshape(lead + (n_heads, HEAD_DIM // 2, 2))
    return jnp.swapaxes(w, -1, -2).reshape(lead + (n_heads * HEAD_DIM,))


AB_KINDS = ("scale",) * 4 + ("qnorm",) * 4 + ("plain", "plain", "knorm", "plain_t")
CD_KINDS = ("scale",) * 4 + ("plain",) * 2 + ("scale",) * 6 + ("plain",) * 4


def kernel(x_prompt, x_sample, norm_ffn1, ffn1_w_in, ffn1_w_out, norm_mix, ab_w_in, ab_sink, ab_q_gain, ab_k_gain, ab_w_out, cd_w_in, cd_rpb, cd_w_out, norm_ffn2, ffn2_w_in, ffn2_w_out, t5_table, final_norm):
    (bp, t, d), bs = x_prompt.shape, x_sample.shape[0]
    assert x_sample.shape[1:] == (t, d)
    b = bp + bs
    depth = norm_ffn1.shape[0]
    x = (x_prompt.reshape(bp * t, d), x_sample.reshape(bs * t, d))

    ffn1_w_in, ffn1_w_out = ffn1_w_in.astype(BF16), ffn1_w_out.astype(BF16)
    ffn2_w_in, ffn2_w_out = ffn2_w_in.astype(BF16), ffn2_w_out.astype(BF16)
    qa_w, kva_w = A_HEADS * HEAD_DIM, 2 * A_KV * HEAD_DIM
    qb_lo = qa_w + kva_w
    qb_hi = qb_lo + B_HEADS * HEAD_DIM
    kb_hi = qb_hi + B_KV * HEAD_DIM
    ab_w_in = jnp.concatenate([
        ab_w_in[..., :qa_w],
        _deinterleave_heads(ab_w_in[..., qb_lo:qb_hi], B_HEADS),
        ab_w_in[..., qa_w:qb_lo],
        _deinterleave_heads(ab_w_in[..., qb_hi:kb_hi], B_KV),
        ab_w_in[..., kb_hi:]], axis=-1).astype(BF16)
    ab_q_gain = _deinterleave_heads(ab_q_gain, 1)
    ab_k_gain = _deinterleave_heads(ab_k_gain, 1)
    ab_w_out, cd_w_in, cd_w_out = ab_w_out.astype(BF16), cd_w_in.astype(BF16), cd_w_out.astype(BF16)

    cos, sin = _rope_tables(t)
    ga, gc = A_HEADS // A_KV, C_HEADS // C_KV
    geo_a = _band_geometry(t, A_WINDOW)
    bias_a = _band_bias(t5_table[:, :A_HEADS], A_KV, ga, A_WINDOW, 1, geo_a)
    d_groups = []
    for gi, (win, dil) in enumerate(D_PAIRS):
        hs = win // (2 * dil)
        geo = _band_geometry(t // dil, hs)
        col0 = A_HEADS + gi * D_SLOTS
        d_groups.append((dil, hs, geo, _band_bias(t5_table[:, col0:col0 + D_SLOTS], D_SLOTS, 1, hs, dil, geo)))

    a_q = 0
    a_k = qa_w + B_HEADS * HEAD_DIM
    a_v = a_k + A_KV * HEAD_DIM
    b_k = a_v + A_KV * HEAD_DIM
    c_q, c_k, c_v = 0, C_HEADS * HEAD_DIM, (C_HEADS + C_KV) * HEAD_DIM
    d_q = (C_HEADS + 2 * C_KV) * HEAD_DIM
    d_k = d_q + D_HEADS * HEAD_DIM
    d_v = d_k + D_SLOTS * HEAD_DIM
    dw = D_SLOTS * HEAD_DIM
    sub_dils = tuple(dil for _, dil in D_PAIRS if dil > 1)
    sub_plan = {}
    for gi, (_, dil) in enumerate(D_PAIRS):
        if dil > 1:
            si = sub_dils.index(dil)
            for src, dst in ((d_q + gi * dw, 0), (d_k, dw), (d_v, 2 * dw)):
                for off in range(0, dw, MXU_COLS):
                    sub_plan.setdefault((src + off) // MXU_COLS, []).append((si, dst + off))

    for l in range(depth):
        i = l // 2
        x = _ffn(x, norm_ffn1[l], ffn1_w_in, ffn1_w_out, l)
        if l % 2 == 0:
            qkv, qbt, vbt = _inproj(x, norm_mix[l], ab_w_in, i, AB_KINDS, t,
                                    (ab_q_gain[i], ab_k_gain[i], cos, sin))
            qkv = qkv.reshape(b, t, -1)
            o_a = _band_attention(qkv, a_q, qkv, a_k, qkv, a_v, bias_a, A_KV, ga, A_WINDOW, geo_a,
                                  sink=ab_sink[i])
            o_b = _dense_attention(qbt, qkv, vbt, b_k, B_KV, B_HEADS // B_KV)
            x = _outproj_ab(x, o_a.reshape(b * t, -1), o_b.reshape(b * t, -1), ab_w_out, i)
        else:
            qkv, *subs = _inproj(x, norm_mix[l], cd_w_in, i, CD_KINDS, t,
                                 sub_args=(sub_dils, 3 * dw, sub_plan))
            qkv = qkv.reshape(b, t, -1)
            o_c = _na_attention(qkv, c_q, c_k, c_v, _na_bias(cd_rpb[i], C_KV, gc), C_KV, gc)
            outs, lses = [], []
            for gi, (dil, hs, geo, bias_d) in enumerate(d_groups):
                if dil == 1:
                    o_g, lse_g = _band_attention(qkv, d_q + gi * dw, qkv, d_k, qkv, d_v, bias_d,
                                                 D_SLOTS, 1, hs, geo, want_lse=True, out_dtype=F32)
                    o_g, lse_g = o_g.reshape(b * t, dw), lse_g.reshape(b * t, LANES)
                else:
                    sub = subs[sub_dils.index(dil)].reshape(b * dil, t // dil, 3 * dw)
                    o_g, lse_g = _band_attention(sub, 0, sub, dw, sub, 2 * dw, bias_d, D_SLOTS, 1, hs, geo,
                                                 want_lse=True, out_dtype=F32, dil=dil)
                outs.append(o_g)
                lses.append(lse_g)
            x = _outproj_cd(x, o_c.reshape(b * t, -1), outs, lses, [g[0] for g in d_groups], t, cd_w_out, i)
        last = l == depth - 1
        x = _ffn(x, norm_ffn2[l], ffn2_w_in, ffn2_w_out, l, final_gain=final_norm if last else None,
                 out_rows=(bp * t, bs * t) if last else None)

    return (x[0].reshape(bp, t, d), x[1].reshape(bs, t, d))
```

```python
import functools
import math

import jax
import jax.numpy as jnp
from jax import lax
from jax.experimental import pallas as pl
from jax.experimental.pallas import tpu as pltpu

HEAD_DIM = 128
GRID_W = 64
A_HEADS, A_KV, A_WINDOW = 8, 2, 128
B_HEADS, B_KV = 8, 2
ROPE_THETA = 10000.0
C_HEADS, C_KV = 8, 2
NA_ROWS, NA_COLS = 8, 16
D_PAIRS = ((128, 1), (512, 4), (2048, 16))
D_SLOTS = 4
D_HEADS = D_SLOTS * len(D_PAIRS)
T5_BUCKETS = 32
T5_MAX_DIST = 2048
NORM_EPS = 1e-6
NEG_INF = -1e30
ATTN_SCALE = HEAD_DIM ** -0.5
LOG2_E = math.log2(math.e)

LANES = 128
SUBLANES = 8
MXU_COLS = 256
FFN_ROW_TILE = 1024
FFN_FF_TILE = 2 * MXU_COLS
DENSE_KEY_CHUNK = MXU_COLS
CHEAP_ROW_STRIDE = 4
BAND_TILES_PER_STEP = (8, 4, 2, 1)
NA_ROWS_PER_STEP = 16
MIB = 1024 * 1024
VMEM_LIMIT_BYTES = 56 * MIB

BF16 = jnp.bfloat16
F32 = jnp.float32


def _params(semantics):
    return pltpu.CompilerParams(dimension_semantics=semantics, vmem_limit_bytes=VMEM_LIMIT_BYTES)


def _resident_weight(shape, layer):
    return pl.BlockSpec((None,) + shape, lambda *_: (layer,) + (0,) * len(shape),
                        pipeline_mode=pl.Buffered(1))


def _rms(x, gain):
    ms = jnp.mean(x * x, axis=-1, keepdims=True)
    return x * lax.rsqrt(ms + NORM_EPS) * gain


def _ffn_body(*refs, n_ff, has_dest, final_norm):
    refs = list(refs)
    x_ref, gain_ref, wg_ref, wu_ref, wo_ref = [refs.pop(0) for _ in range(5)]
    fgain_ref = refs.pop(0) if final_norm else None
    if has_dest:
        refs.pop(0)
    o_ref, h_scr = refs
    j = pl.program_id(1)

    @pl.when(j == 0)
    def _():
        x = x_ref[...]
        h_scr[...] = _rms(x, gain_ref[...]).astype(BF16)
        o_ref[...] = x

    h = h_scr[...]
    g = jnp.dot(h, wg_ref[...], preferred_element_type=F32)
    u = jnp.dot(h, wu_ref[...], preferred_element_type=F32)
    a = (g * jax.nn.sigmoid(g) * (0.5 * u)).astype(BF16)
    o_ref[...] += jnp.dot(a, wo_ref[...], preferred_element_type=F32)

    if final_norm:
        @pl.when(j == n_ff - 1)
        def _():
            o_ref[...] = _rms(o_ref[...], fgain_ref[...])


def _ffn(x, gain, w_in, w_out, layer, final_gain=None, *, x_row0=0, x_rows=None, out_rows=None,
         out_row0=0, dest=None, tm=FFN_ROW_TILE):
    d = x.shape[1]
    x_rows = x.shape[0] - x_row0 if x_rows is None else x_rows
    out_rows = (x_rows if dest is None else dest.shape[0]) if out_rows is None else out_rows
    f = w_out.shape[1]
    tf = math.gcd(FFN_FF_TILE, f)
    n_ff = f // tf
    tm = math.gcd(tm, x_rows, x_row0, out_row0)
    assert w_in.shape[1:] == (d, 2 * f) and w_out.shape[2] == d and out_row0 + x_rows <= out_rows
    final_norm = final_gain is not None
    xt0, ot0 = x_row0 // tm, out_row0 // tm
    in_specs = [
        pl.BlockSpec((tm, d), lambda i, j: (i + xt0, 0)),
        pl.BlockSpec((1, d), lambda i, j: (0, 0)),
        pl.BlockSpec((None, d, tf), lambda i, j: (layer, 0, j)),
        pl.BlockSpec((None, d, tf), lambda i, j: (layer, 0, j + n_ff)),
        pl.BlockSpec((None, tf, d), lambda i, j: (layer, j, 0)),
    ]
    args = [x, gain.reshape(1, d), w_in, w_in, w_out]
    if final_norm:
        in_specs.append(pl.BlockSpec((1, d), lambda i, j: (0, 0)))
        args.append(final_gain.reshape(1, d))
    aliases = {}
    if dest is not None:
        assert dest.shape == (out_rows, d) and dest.dtype == F32
        in_specs.append(pl.BlockSpec(memory_space=pl.ANY))
        aliases = {len(args): 0}
        args.append(dest)
    return pl.pallas_call(
        functools.partial(_ffn_body, n_ff=n_ff, has_dest=dest is not None, final_norm=final_norm),
        grid=(x_rows // tm, n_ff),
        in_specs=in_specs,
        out_specs=pl.BlockSpec((tm, d), lambda i, j: (i + ot0, 0)),
        out_shape=jax.ShapeDtypeStruct((out_rows, d), F32),
        scratch_shapes=[pltpu.VMEM((tm, d), BF16)],
        input_output_aliases=aliases,
        compiler_params=_params(("parallel", "arbitrary")),
        name="ffn",
    )(*args)


def _rope_norm(r, gain, cos, sin):
    y = _rms(r, gain)
    return y * cos + pltpu.roll(y, HEAD_DIM // 2, axis=1) * sin


def _inproj_body(*refs, kinds, rope, sub_plan, dils):
    if rope:
        x_ref, gain_ref, w_ref, qg_ref, kg_ref, cos_ref, sin_ref, o_ref, qt_ref, vt_ref = refs
    else:
        x_ref, gain_ref, w_ref, o_ref = refs[:4]
        sub_refs, slab_scr = refs[4:4 + len(dils)], refs[-1] if dils else None
    tm = x_ref.shape[0]
    h = _rms(x_ref[...], gain_ref[...]).astype(BF16)
    n_qt = 0
    for c, kind in enumerate(kinds):
        lo = c * MXU_COLS
        r = jnp.dot(h, w_ref[:, lo:lo + MXU_COLS], preferred_element_type=F32)
        if kind == "scale":
            r = r * ATTN_SCALE
        elif kind in ("qnorm", "knorm"):
            hg = qg_ref[...] if kind == "qnorm" else kg_ref[...]
            cos = cos_ref[...]
            sin = sin_ref[...]
            parts = []
            for hh in range(MXU_COLS // HEAD_DIM):
                y = _rope_norm(r[:, hh * HEAD_DIM:(hh + 1) * HEAD_DIM], hg, cos, sin)
                if kind == "qnorm":
                    y = y * (ATTN_SCALE * LOG2_E)
                parts.append(y)
            r = jnp.concatenate(parts, axis=1)
            if kind == "qnorm":
                qt_ref[0, n_qt * MXU_COLS:(n_qt + 1) * MXU_COLS, :] = r.T.astype(BF16)
                n_qt += 1
        elif kind == "plain_t":
            vt_ref[0] = r.T.astype(BF16)
        o_ref[:, lo:lo + MXU_COLS] = r.astype(BF16)
        if c in sub_plan:
            heads = range(MXU_COLS // HEAD_DIM)
            q_rows = tm // CHEAP_ROW_STRIDE
            for hh in heads:
                slab_scr[0, hh] = r[:, hh * HEAD_DIM:(hh + 1) * HEAD_DIM]
            if any(dils[si] > CHEAP_ROW_STRIDE for si, _ in sub_plan[c]):
                for hh in heads:
                    for r4 in range(CHEAP_ROW_STRIDE):
                        slab_scr[1, hh, r4 * q_rows:(r4 + 1) * q_rows, :] = (
                            slab_scr[0, hh, pl.ds(r4, q_rows, stride=CHEAP_ROW_STRIDE), :])
            for si, col0 in sub_plan[c]:
                d = dils[si]
                for rho in range(d):
                    for hh in heads:
                        if d <= CHEAP_ROW_STRIDE:
                            part = slab_scr[0, hh, pl.ds(rho, tm // d, stride=d), :]
                        else:
                            r4, r2 = rho % CHEAP_ROW_STRIDE, rho // CHEAP_ROW_STRIDE
                            part = slab_scr[1, hh, pl.ds(r4 * q_rows + r2, tm // d,
                                                         stride=d // CHEAP_ROW_STRIDE), :]
                        cols = slice(col0 + hh * HEAD_DIM, col0 + (hh + 1) * HEAD_DIM)
                        sub_refs[si][0, rho, :, cols] = part.astype(BF16)


def _inproj(x, gain, w, layer, kinds, seq_len, rope_args=None, sub_args=None, *, tm=512):
    n, d = x.shape
    c = w.shape[2]
    assert n % tm == 0 and seq_len % tm == 0 and c == MXU_COLS * len(kinds)
    rope = rope_args is not None
    assert not (rope and sub_args)
    dils, sub_width, sub_plan = sub_args if sub_args else ((), 0, {})
    scratch = [pltpu.VMEM((2, MXU_COLS // HEAD_DIM, tm, HEAD_DIM), F32)] if dils else []
    assert all(dl <= CHEAP_ROW_STRIDE or (dl % CHEAP_ROW_STRIDE == 0 and dl <= CHEAP_ROW_STRIDE ** 2)
               for dl in dils)
    in_specs = [
        pl.BlockSpec((tm, d), lambda i: (i, 0)),
        pl.BlockSpec((1, d), lambda i: (0, 0)),
        _resident_weight((d, c), layer),
    ]
    args = [x, gain.reshape(1, d), w]
    if rope:
        q_gain, k_gain, cos, sin = rope_args
        tiles_per_seq = seq_len // tm
        in_specs += [
            pl.BlockSpec((1, HEAD_DIM), lambda i: (0, 0)),
            pl.BlockSpec((1, HEAD_DIM), lambda i: (0, 0)),
            pl.BlockSpec((tm, HEAD_DIM), lambda i: (i % tiles_per_seq, 0)),
            pl.BlockSpec((tm, HEAD_DIM), lambda i: (i % tiles_per_seq, 0)),
        ]
        args += [q_gain.reshape(1, HEAD_DIM), k_gain.reshape(1, HEAD_DIM), cos, sin]
    out_specs = pl.BlockSpec((tm, c), lambda i: (i, 0))
    out_shape = jax.ShapeDtypeStruct((n, c), BF16)
    if rope:
        assert kinds.count("plain_t") == 1
        qt_rows = MXU_COLS * kinds.count("qnorm")
        by_seq = lambda i: (i // tiles_per_seq, 0, i % tiles_per_seq)
        out_specs = [out_specs, pl.BlockSpec((1, qt_rows, tm), by_seq),
                     pl.BlockSpec((1, MXU_COLS, tm), by_seq)]
        out_shape = [out_shape, jax.ShapeDtypeStruct((n // seq_len, qt_rows, seq_len), BF16),
                     jax.ShapeDtypeStruct((n // seq_len, MXU_COLS, seq_len), BF16)]
    if dils:
        tiles_per_seq = seq_len // tm
        assert all(tm % (dl * 16) == 0 for dl in dils)
        out_specs = [out_specs] + [
            pl.BlockSpec((1, dl, tm // dl, sub_width), lambda i: (i // tiles_per_seq, 0, i % tiles_per_seq, 0))
            for dl in dils]
        out_shape = [out_shape] + [
            jax.ShapeDtypeStruct((n // seq_len, dl, seq_len // dl, sub_width), BF16) for dl in dils]
    return pl.pallas_call(
        functools.partial(_inproj_body, kinds=tuple(kinds), rope=rope, sub_plan=sub_plan, dils=tuple(dils)),
        grid=(n // tm,),
        in_specs=in_specs,
        out_specs=out_specs,
        out_shape=out_shape,
        scratch_shapes=scratch,
        compiler_params=_params(("parallel",)),
        name="inproj",
    )(*args)


def _outproj_ab_body(x_ref, a_ref, b_ref, w_ref, o_ref):
    ka = a_ref.shape[1]
    acc = jnp.dot(a_ref[...], w_ref[0:ka, :], preferred_element_type=F32)
    acc = acc + jnp.dot(b_ref[...], w_ref[ka:, :], preferred_element_type=F32)
    o_ref[...] = x_ref[...] + acc


def _outproj_ab(x, oa, ob, w, layer, *, tm=512):
    n, d = x.shape
    ka, kb = oa.shape[1], ob.shape[1]
    assert n % tm == 0 and w.shape[1:] == (ka + kb, d)
    return pl.pallas_call(
        _outproj_ab_body,
        grid=(n // tm,),
        in_specs=[
            pl.BlockSpec((tm, d), lambda i: (i, 0)),
            pl.BlockSpec((tm, ka), lambda i: (i, 0)),
            pl.BlockSpec((tm, kb), lambda i: (i, 0)),
            _resident_weight((ka + kb, d), layer),
        ],
        out_specs=pl.BlockSpec((tm, d), lambda i: (i, 0)),
        out_shape=jax.ShapeDtypeStruct((n, d), F32),
        compiler_params=_params(("parallel",)),
        name="outproj_ab",
    )(x, oa, ob, w)


def _outproj_cd_body(*refs, dils, kd):
    ng = len(dils)
    x_ref, c_ref = refs[:2]
    o_refs, l_refs = refs[2:2 + ng], refs[2 + ng:2 + 2 * ng]
    w_ref, o_ref, row_scr = refs[2 + 2 * ng:]
    tm = x_ref.shape[0]
    heads = kd // HEAD_DIM

    def token_rows(ref, d, slot, width):
        if d == 1:
            return ref[...]
        for rho in range(d):
            for s in range(width // LANES):
                lo = rho * width + s * LANES
                row_scr[slot, s, pl.ds(rho, tm // d, stride=d), :] = ref[0, :, lo:lo + LANES]
        return jnp.concatenate([row_scr[slot, s] for s in range(width // LANES)], axis=1)

    outs = [token_rows(r, d, 2 * i, kd) for i, (r, d) in enumerate(zip(o_refs, dils))]
    lses = [token_rows(r, d, 2 * i + 1, LANES) for i, (r, d) in enumerate(zip(l_refs, dils))]
    m = functools.reduce(jnp.maximum, lses)
    es = [jnp.exp(l - m) for l in lses]
    den = functools.reduce(lambda a, b: a + b, es)
    wts = [e / den for e in es]
    parts = []
    for hd in range(heads):
        lane = hd * (LANES // heads)
        cols = slice(hd * HEAD_DIM, (hd + 1) * HEAD_DIM)
        terms = [jnp.broadcast_to(wt[:, lane:lane + 1], (tm, HEAD_DIM)) * o[:, cols] for wt, o in zip(wts, outs)]
        parts.append(functools.reduce(lambda a, b: a + b, terms))
    od = jnp.concatenate(parts, axis=1)
    kc = c_ref.shape[1]
    acc = jnp.dot(c_ref[...], w_ref[0:kc, :], preferred_element_type=F32)
    acc = acc + jnp.dot(od.astype(BF16), w_ref[kc:, :], preferred_element_type=F32)
    o_ref[...] = x_ref[...] + acc


def _outproj_cd(x, oc, outs, lses, dils, seq_len, w, layer, *, tm=512):
    n, d = x.shape
    kc = oc.shape[1]
    kd = w.shape[1] - kc
    assert n % tm == 0 and seq_len % tm == 0 and all(tm % (dl * SUBLANES) == 0 for dl in dils)
    tiles_per_seq = seq_len // tm
    row = lambda i: (i, 0)
    by_seq = lambda i: (i // tiles_per_seq, i % tiles_per_seq, 0)

    def g_specs(width):
        return [pl.BlockSpec((tm, width), row) if dl == 1 else pl.BlockSpec((1, tm // dl, dl * width), by_seq)
                for dl in dils]

    return pl.pallas_call(
        functools.partial(_outproj_cd_body, dils=tuple(dils), kd=kd),
        grid=(n // tm,),
        in_specs=[pl.BlockSpec((tm, d), row), pl.BlockSpec((tm, kc), row)] + g_specs(kd) + g_specs(LANES)
        + [_resident_weight((kc + kd, d), layer)],
        out_specs=pl.BlockSpec((tm, d), row),
        out_shape=jax.ShapeDtypeStruct((n, d), F32),
        scratch_shapes=[pltpu.VMEM((2 * len(dils), kd // HEAD_DIM, tm, HEAD_DIM), F32)],
        compiler_params=_params(("parallel",)),
        name="outproj_cd",
    )(x, oc, *outs, *lses, w)


def _stack_heads(q, g):
    return jnp.concatenate([q[:, i * HEAD_DIM:(i + 1) * HEAD_DIM] for i in range(g)], axis=0)


def _softmax_pv(s, v):
    m = jnp.max(s, axis=-1, keepdims=True)
    p = jnp.exp(s - m)
    l = jnp.sum(p, axis=-1, keepdims=True)
    o = jnp.dot(p.astype(BF16), v, preferred_element_type=F32) / l
    return o, m, l


def _qk(q, k):
    return lax.dot_general(q, k, (((1,), (1,)), ((), ())), preferred_element_type=F32)


def _dense_body(qt_ref, k_ref, vt_ref, o_ref, s_scr, p_scr, m_scr, l_scr, *, g):
    step = pl.program_id(0)
    tq = qt_ref.shape[2] // 2
    t = k_ref.shape[1]
    gq = g * tq
    kc = min(DENSE_KEY_CHUNK, t)

    @pl.when(step == 0)
    def _():
        s_scr[1] = jnp.zeros(s_scr.shape[1:], s_scr.dtype)
        m_scr[1] = jnp.zeros(m_scr.shape[1:], m_scr.dtype)
        p_scr[...] = jnp.zeros(p_scr.shape, p_scr.dtype)
        l_scr[...] = jnp.ones(l_scr.shape, l_scr.dtype)

    def fold(x, op):
        return op(x.reshape(x.shape[0] // SUBLANES, SUBLANES, x.shape[1]), axis=0)

    for cur in range(2):
        prev = 1 - cur
        toks = slice(cur * tq, (cur + 1) * tq)
        qt = jnp.concatenate([qt_ref[0, i * HEAD_DIM:(i + 1) * HEAD_DIM, toks] for i in range(g)], axis=1)
        m = jnp.max(m_scr[prev], axis=0, keepdims=True)
        ot = jnp.zeros((HEAD_DIM, gq), F32)
        l_acc = jnp.zeros((SUBLANES, gq), F32)
        m_acc = None
        for c in range(t // kc):
            keys = slice(c * kc, (c + 1) * kc)
            p = jnp.exp2(s_scr[prev, keys, :] - m)
            l_acc = l_acc + fold(p, jnp.sum)
            p_scr[prev, keys, :] = p.astype(BF16)
        for c in range(t // kc):
            keys = slice(c * kc, (c + 1) * kc)
            ot = ot + jnp.dot(vt_ref[0, :, keys], p_scr[cur, keys, :], preferred_element_type=F32)
            s = jnp.dot(k_ref[0, keys, :], qt, preferred_element_type=F32)
            s_scr[cur, keys, :] = s
            m_chunk = fold(s, jnp.max)
            m_acc = m_chunk if m_acc is None else jnp.maximum(m_acc, m_chunk)
        ot = ot / jnp.sum(l_scr[cur], axis=0, keepdims=True)
        l_scr[prev] = l_acc
        m_scr[cur] = m_acc
        for i in range(g):
            o_ref[0, toks, i * HEAD_DIM:(i + 1) * HEAD_DIM] = ot[:, i * tq:(i + 1) * tq].T.astype(o_ref.dtype)


def _dense_attention(qt, qkv, vt, k_col, n_kv, g, *, tq=128):
    b, t, _ = qkv.shape
    gw = g * HEAD_DIM
    n_pairs = t // (2 * tq)
    assert t % (2 * tq) == 0 and k_col % HEAD_DIM == 0
    assert qt.shape == (b, n_kv * gw, t) and vt.shape == (b, n_kv * HEAD_DIM, t)
    kb = k_col // HEAD_DIM
    total = b * n_kv * n_pairs

    def where(f):
        return f // (n_kv * n_pairs), (f // n_pairs) % n_kv, f % n_pairs

    def q_map(f):
        n, h, k = where(jnp.minimum(f, total - 1))
        return n, h, k

    def k_map(f):
        n, h, _ = where(jnp.minimum(f, total - 1))
        return n, 0, kb + h

    def v_map(f):
        n, h, _ = where(jnp.maximum(f - 1, 0))
        return n, h, 0

    def o_map(f):
        n, h, k = where(jnp.maximum(f - 1, 0))
        return n, k, h

    return pl.pallas_call(
        functools.partial(_dense_body, g=g),
        grid=(total + 1,),
        in_specs=[
            pl.BlockSpec((1, gw, 2 * tq), q_map),
            pl.BlockSpec((1, t, HEAD_DIM), k_map),
            pl.BlockSpec((1, HEAD_DIM, t), v_map),
        ],
        out_specs=pl.BlockSpec((1, 2 * tq, gw), o_map),
        out_shape=jax.ShapeDtypeStruct((b, t, n_kv * gw), BF16),
        scratch_shapes=[pltpu.VMEM((2, t, g * tq), F32), pltpu.VMEM((2, t, g * tq), BF16),
                        pltpu.VMEM((2, SUBLANES, g * tq), F32), pltpu.VMEM((2, SUBLANES, g * tq), F32)],
        compiler_params=_params(("arbitrary",)),
        name="dense_attn",
    )(qt, qkv, vt)


def _band_body(*refs, n_kv, g, w, seq, tq, kw, n_cases, has_sink, want_lse, side_by_side):
    refs = list(refs)
    q_ref, k_ref, v_ref, bias_ref = refs[:4]
    refs = refs[4:]
    sink_ref = refs.pop(0) if has_sink else None
    o_ref = refs.pop(0)
    lse_ref = refs.pop(0) if want_lse else None
    gw = g * HEAD_DIM
    qw = n_kv * gw
    nq = seq // tq
    for sq, j in [(a, c) for a in range(q_ref.shape[0]) for c in range(q_ref.shape[1] // tq)]:
        qi = pl.program_id(1) * (q_ref.shape[1] // tq) + j
        kstart = pl.multiple_of(jnp.clip(qi * tq - w, 0, seq - kw), 64)
        case = 0 if n_cases == 1 else jnp.where(qi == 0, 0, jnp.where(qi == nq - 1, 2, 1))
        trows = slice(j * tq, (j + 1) * tq)
        osq, ocol0, lcol0 = (0, sq * qw, sq * LANES) if side_by_side else (sq, 0, 0)
        lse_tile = jnp.zeros((tq, LANES), F32)
        head_of_lane = lax.broadcasted_iota(jnp.int32, (tq, LANES), 1) // (LANES // (n_kv * g))
        for h in range(n_kv):
            k = k_ref[sq, pl.ds(kstart, kw), h * HEAD_DIM:(h + 1) * HEAD_DIM]
            v = v_ref[sq, pl.ds(kstart, kw), h * HEAD_DIM:(h + 1) * HEAD_DIM]
            q = _stack_heads(q_ref[sq, trows, h * gw:(h + 1) * gw], g)
            s = _qk(q, k) + bias_ref[case, h]
            o, m, l = _softmax_pv(s, v)
            lse = m + jnp.log(l)
            for i in range(g):
                rows = slice(i * tq, (i + 1) * tq)
                cols = slice(h * gw + i * HEAD_DIM, h * gw + (i + 1) * HEAD_DIM)
                oi = o[rows]
                if has_sink:
                    oi = oi * jax.nn.sigmoid(lse[rows] - sink_ref[h * g + i])
                o_ref[osq, trows, ocol0 + cols.start:ocol0 + cols.stop] = oi.astype(o_ref.dtype)
                if want_lse:
                    lse_tile = jnp.where(head_of_lane == h * g + i, lse[rows], lse_tile)
        if want_lse:
            lse_ref[osq, trows, lcol0:lcol0 + LANES] = lse_tile


def _band_geometry(seq, w):
    if seq <= 4 * w:
        return seq, seq, (0,)
    tq = 2 * w if w < 128 else w
    tq = min(tq, 256)
    kw = tq + 2 * w
    assert seq % tq == 0 and seq >= kw
    return tq, kw, (0, -w, -2 * w)


def _band_attention(q_arr, q_col, k_arr, k_col, v_arr, v_col, bias, n_kv, g, w, geometry,
                    sink=None, want_lse=False, out_dtype=BF16, dil=1):
    n, seq, _ = q_arr.shape
    tq, kw, shifts = geometry
    nq = seq // tq
    gw = g * HEAD_DIM
    qw, kvw = n_kv * gw, n_kv * HEAD_DIM
    assert q_col % qw == 0 and k_col % kvw == 0 and v_col % kvw == 0
    qb, kb, vb = q_col // qw, k_col // kvw, v_col // kvw
    tps = next(c for c in BAND_TILES_PER_STEP if nq % c == 0)
    nq_steps = nq // tps
    sps = 1
    if nq == 1:
        sps = next(c for c in BAND_TILES_PER_STEP if n % c == 0 and (dil == 1 or dil % c == 0))
    in_specs = [
        pl.BlockSpec((sps, tps * tq, qw), lambda b, i: (b, i, qb)),
        pl.BlockSpec((sps, seq, kvw), lambda b, i: (b, 0, kb)),
        pl.BlockSpec((sps, seq, kvw), lambda b, i: (b, 0, vb)),
        pl.BlockSpec((len(shifts), n_kv, g * tq, kw), lambda b, i: (0, 0, 0, 0)),
    ]
    args = [q_arr, k_arr, v_arr, bias]
    if sink is not None:
        in_specs.append(pl.BlockSpec(memory_space=pltpu.SMEM))
        args.append(sink.astype(F32))
    def out_like(width, dtype):
        if dil == 1:
            return (pl.BlockSpec((sps, tps * tq, width), lambda b, i: (b, i, 0)),
                    jax.ShapeDtypeStruct((n, seq, width), dtype))
        return (pl.BlockSpec((1, tps * tq, sps * width),
                             lambda b, i: ((b * sps) // dil, i, ((b * sps) % dil) // sps)),
                jax.ShapeDtypeStruct((n // dil, seq, dil * width), dtype))

    assert dil == 1 or (n % dil == 0 and dil % sps == 0)
    out_specs, out_shape = out_like(qw, out_dtype)
    if want_lse:
        assert LANES % (n_kv * g) == 0
        l_spec, l_shape = out_like(LANES, F32)
        out_specs, out_shape = [out_specs, l_spec], [out_shape, l_shape]
    return pl.pallas_call(
        functools.partial(_band_body, n_kv=n_kv, g=g, w=w, seq=seq, tq=tq, kw=kw, n_cases=len(shifts),
                          has_sink=sink is not None, want_lse=want_lse, side_by_side=dil > 1),
        grid=(n // sps, nq_steps),
        in_specs=in_specs,
        out_specs=out_specs,
        out_shape=out_shape,
        compiler_params=_params(("parallel", "arbitrary")),
        name="band_attn",
    )(*args)


def _na_body(q_ref, k_ref, v_ref, bias_ref, o_ref, *, g, rows, rows_per_step):
    rb = pl.program_id(2)
    kwin = NA_ROWS * GRID_W

    def one_row(rr, carry):
        r = rb * rows_per_step + rr
        rs = jnp.clip(r - NA_ROWS // 2, 0, rows - NA_ROWS)
        off = rs - r + NA_ROWS - 1
        kstart = pl.multiple_of(rs * GRID_W, GRID_W)
        qstart = pl.multiple_of(rr * GRID_W, GRID_W)
        k = k_ref[0, pl.ds(kstart, kwin), :]
        v = v_ref[0, pl.ds(kstart, kwin), :]
        q = _stack_heads(q_ref[0, pl.ds(qstart, GRID_W), :], g)
        s = _qk(q, k) + bias_ref[0, off]
        o, _, _ = _softmax_pv(s, v)
        for i in range(g):
            o_ref[0, pl.ds(qstart, GRID_W), i * HEAD_DIM:(i + 1) * HEAD_DIM] = (
                o[i * GRID_W:(i + 1) * GRID_W].astype(o_ref.dtype))
        return carry

    lax.fori_loop(0, rows_per_step, one_row, 0, unroll=True)


def _na_attention(qkv, q_col, k_col, v_col, bias, n_kv, g, *, rows_per_step=NA_ROWS_PER_STEP):
    b, t, _ = qkv.shape
    rows = t // GRID_W
    gw = g * HEAD_DIM
    assert rows % rows_per_step == 0 and rows >= NA_ROWS
    qb, kb, vb = q_col // gw, k_col // HEAD_DIM, v_col // HEAD_DIM
    tq = rows_per_step * GRID_W
    return pl.pallas_call(
        functools.partial(_na_body, g=g, rows=rows, rows_per_step=rows_per_step),
        grid=(b, n_kv, rows // rows_per_step),
        in_specs=[
            pl.BlockSpec((1, tq, gw), lambda n, h, i: (n, i, qb + h)),
            pl.BlockSpec((1, t, HEAD_DIM), lambda n, h, i: (n, 0, kb + h)),
            pl.BlockSpec((1, t, HEAD_DIM), lambda n, h, i: (n, 0, vb + h)),
            pl.BlockSpec((1, NA_ROWS, g * GRID_W, NA_ROWS * GRID_W), lambda n, h, i: (h, 0, 0, 0)),
        ],
        out_specs=pl.BlockSpec((1, tq, gw), lambda n, h, i: (n, i, h)),
        out_shape=jax.ShapeDtypeStruct((b, t, n_kv * gw), BF16),
        compiler_params=_params(("parallel", "parallel", "arbitrary")),
        name="na_attn",
    )(qkv, qkv, qkv, bias)


def _t5_bucket(rel):
    half = T5_BUCKETS // 2
    max_exact = half // 2
    n = jnp.abs(rel)
    nf = jnp.maximum(n, 1).astype(jnp.float32)
    large = max_exact + (jnp.log(nf / max_exact) / math.log(T5_MAX_DIST / max_exact)
                         * (half - max_exact)).astype(jnp.int32)
    large = jnp.minimum(large, half - 1)
    return jnp.where(rel > 0, half, 0) + jnp.where(n < max_exact, n, large)


def _band_bias(table_cols, n_kv, g, w, dil, geometry):
    tq, kw, shifts = geometry
    iq = jnp.arange(tq)[:, None]
    jk = jnp.arange(kw)[None, :]
    tiles = []
    for shift in shifts:
        rel = jk + shift - iq
        bucket = _t5_bucket(rel * dil)
        tb = jnp.zeros((tq, kw, table_cols.shape[1]), F32)
        for r in range(T5_BUCKETS):
            tb = jnp.where((bucket == r)[..., None], table_cols[r].astype(F32), tb)
        tb = jnp.where((jnp.abs(rel) <= w)[..., None], tb, NEG_INF)
        tiles.append(jnp.moveaxis(tb, -1, 0).reshape(n_kv, g * tq, kw))
    return jnp.stack(tiles)


def _na_bias(rpb, n_kv, g):
    col = jnp.arange(GRID_W)
    qc, kc = col[:, None], col[None, :]
    dc = jnp.clip(kc - qc + NA_COLS - 1, 0, 2 * NA_COLS - 2)
    cs = jnp.clip(qc - NA_COLS // 2, 0, GRID_W - NA_COLS)
    mask = (kc >= cs) & (kc < cs + NA_COLS)
    rpb = rpb.astype(F32)
    t = jnp.zeros(rpb.shape[:2] + dc.shape, F32)
    for c in range(2 * NA_COLS - 1):
        t = jnp.where(dc == c, rpb[:, :, c][:, :, None, None], t)
    t = jnp.where(mask, t, NEG_INF)
    cls = jnp.stack([t[:, o:o + NA_ROWS] for o in range(NA_ROWS)])
    cls = cls.reshape(NA_ROWS, n_kv, g, NA_ROWS, GRID_W, GRID_W)
    cls = jnp.transpose(cls, (1, 0, 2, 4, 3, 5))
    return cls.reshape(n_kv, NA_ROWS, g * GRID_W, NA_ROWS * GRID_W)


def _rope_tables(t):
    n_pairs = HEAD_DIM // 2
    n_freq = n_pairs // 2
    pos = jnp.arange(t)
    row = (pos // GRID_W).astype(jnp.float32)
    col = (pos % GRID_W).astype(jnp.float32)
    omega = ROPE_THETA ** (-(jnp.arange(n_freq, dtype=jnp.float32) * 2.0 / n_pairs))
    ang = jnp.concatenate([row[:, None] * omega, col[:, None] * omega], axis=-1)
    cos, sin = jnp.cos(ang), jnp.sin(ang)
    return jnp.concatenate([cos, cos], axis=-1), jnp.concatenate([-sin, sin], axis=-1)


def _deinterleave_heads(w, n_heads):
    lead = w.shape[:-1]
    w = w.reshape(lead + (n_heads, HEAD_DIM // 2, 2))
    return jnp.swapaxes(w, -1, -2).reshape(lead + (n_heads * HEAD_DIM,))


AB_KINDS = ("scale",) * 4 + ("qnorm",) * 4 + ("plain", "plain", "knorm", "plain_t")
CD_KINDS = ("scale",) * 4 + ("plain",) * 2 + ("scale",) * 6 + ("plain",) * 4


def kernel(x_prompt, x_sample, norm_ffn1, ffn1_w_in, ffn1_w_out, norm_mix, ab_w_in, ab_sink, ab_q_gain, ab_k_gain, ab_w_out, cd_w_in, cd_rpb, cd_w_out, norm_ffn2, ffn2_w_in, ffn2_w_out, t5_table, final_norm):
    (bp, t, d), bs = x_prompt.shape, x_sample.shape[0]
    assert x_sample.shape[1:] == (t, d)
    b = bp + bs
    depth = norm_ffn1.shape[0]

    ffn1_w_in, ffn1_w_out = ffn1_w_in.astype(BF16), ffn1_w_out.astype(BF16)
    ffn2_w_in, ffn2_w_out = ffn2_w_in.astype(BF16), ffn2_w_out.astype(BF16)
    qa_w, kva_w = A_HEADS * HEAD_DIM, 2 * A_KV * HEAD_DIM
    qb_lo = qa_w + kva_w
    qb_hi = qb_lo + B_HEADS * HEAD_DIM
    kb_hi = qb_hi + B_KV * HEAD_DIM
    ab_w_in = jnp.concatenate([
        ab_w_in[..., :qa_w],
        _deinterleave_heads(ab_w_in[..., qb_lo:qb_hi], B_HEADS),
        ab_w_in[..., qa_w:qb_lo],
        _deinterleave_heads(ab_w_in[..., qb_hi:kb_hi], B_KV),
        ab_w_in[..., kb_hi:]], axis=-1).astype(BF16)
    ab_q_gain = _deinterleave_heads(ab_q_gain, 1)
    ab_k_gain = _deinterleave_heads(ab_k_gain, 1)
    ab_w_out, cd_w_in, cd_w_out = ab_w_out.astype(BF16), cd_w_in.astype(BF16), cd_w_out.astype(BF16)

    cos, sin = _rope_tables(t)
    ga, gc = A_HEADS // A_KV, C_HEADS // C_KV
    geo_a = _band_geometry(t, A_WINDOW)
    bias_a = _band_bias(t5_table[:, :A_HEADS], A_KV, ga, A_WINDOW, 1, geo_a)
    d_groups = []
    for gi, (win, dil) in enumerate(D_PAIRS):
        hs = win // (2 * dil)
        geo = _band_geometry(t // dil, hs)
        col0 = A_HEADS + gi * D_SLOTS
        d_groups.append((dil, hs, geo, _band_bias(t5_table[:, col0:col0 + D_SLOTS], D_SLOTS, 1, hs, dil, geo)))

    a_q = 0
    a_k = qa_w + B_HEADS * HEAD_DIM
    a_v = a_k + A_KV * HEAD_DIM
    b_k = a_v + A_KV * HEAD_DIM
    c_q, c_k, c_v = 0, C_HEADS * HEAD_DIM, (C_HEADS + C_KV) * HEAD_DIM
    d_q = (C_HEADS + 2 * C_KV) * HEAD_DIM
    d_k = d_q + D_HEADS * HEAD_DIM
    d_v = d_k + D_SLOTS * HEAD_DIM
    dw = D_SLOTS * HEAD_DIM
    sub_dils = tuple(dil for _, dil in D_PAIRS if dil > 1)
    sub_plan = {}
    for gi, (_, dil) in enumerate(D_PAIRS):
        if dil > 1:
            si = sub_dils.index(dil)
            for src, dst in ((d_q + gi * dw, 0), (d_k, dw), (d_v, 2 * dw)):
                for off in range(0, dw, MXU_COLS):
                    sub_plan.setdefault((src + off) // MXU_COLS, []).append((si, dst + off))

    for l in range(depth):
        i = l // 2
        if l == 0:
            x = _ffn(x_prompt.reshape(bp * t, d), norm_ffn1[l], ffn1_w_in, ffn1_w_out, l, out_rows=b * t)
            x = _ffn(x_sample.reshape(bs * t, d), norm_ffn1[l], ffn1_w_in, ffn1_w_out, l,
                     out_row0=bp * t, dest=x)
        else:
            x = _ffn(x, norm_ffn1[l], ffn1_w_in, ffn1_w_out, l)
        if l % 2 == 0:
            qkv, qbt, vbt = _inproj(x, norm_mix[l], ab_w_in, i, AB_KINDS, t,
                                    (ab_q_gain[i], ab_k_gain[i], cos, sin))
            qkv = qkv.reshape(b, t, -1)
            o_a = _band_attention(qkv, a_q, qkv, a_k, qkv, a_v, bias_a, A_KV, ga, A_WINDOW, geo_a,
                                  sink=ab_sink[i])
            o_b = _dense_attention(qbt, qkv, vbt, b_k, B_KV, B_HEADS // B_KV)
            x = _outproj_ab(x, o_a.reshape(b * t, -1), o_b.reshape(b * t, -1), ab_w_out, i)
        else:
            qkv, *subs = _inproj(x, norm_mix[l], cd_w_in, i, CD_KINDS, t,
                                 sub_args=(sub_dils, 3 * dw, sub_plan))
            qkv = qkv.reshape(b, t, -1)
            o_c = _na_attention(qkv, c_q, c_k, c_v, _na_bias(cd_rpb[i], C_KV, gc), C_KV, gc)
            outs, lses = [], []
            for gi, (dil, hs, geo, bias_d) in enumerate(d_groups):
                if dil == 1:
                    o_g, lse_g = _band_attention(qkv, d_q + gi * dw, qkv, d_k, qkv, d_v, bias_d,
                                                 D_SLOTS, 1, hs, geo, want_lse=True, out_dtype=F32)
                    o_g, lse_g = o_g.reshape(b * t, dw), lse_g.reshape(b * t, LANES)
                else:
                    sub = subs[sub_dils.index(dil)].reshape(b * dil, t // dil, 3 * dw)
                    o_g, lse_g = _band_attention(sub, 0, sub, dw, sub, 2 * dw, bias_d, D_SLOTS, 1, hs, geo,
                                                 want_lse=True, out_dtype=F32, dil=dil)
                outs.append(o_g)
                lses.append(lse_g)
            x = _outproj_cd(x, o_c.reshape(b * t, -1), outs, lses, [g[0] for g in d_groups], t, cd_w_out, i)
        if l < depth - 1:
            x = _ffn(x, norm_ffn2[l], ffn2_w_in, ffn2_w_out, l)

    last = depth - 1
    y_prompt = _ffn(x, norm_ffn2[last], ffn2_w_in, ffn2_w_out, last, final_norm, x_rows=bp * t)
    y_sample = _ffn(x, norm_ffn2[last], ffn2_w_in, ffn2_w_out, last, final_norm, x_row0=bp * t)
    return (y_prompt.reshape(bp, t, d), y_sample.reshape(bs, t, d))
```

```python
import functools
import math

import jax
import jax.numpy as jnp
from jax import lax
from jax.experimental import pallas as pl
from jax.experimental.pallas import tpu as pltpu

HEAD_DIM = 128
GRID_W = 64
A_HEADS, A_KV, A_WINDOW = 8, 2, 128
B_HEADS, B_KV = 8, 2
ROPE_THETA = 10000.0
C_HEADS, C_KV = 8, 2
NA_ROWS, NA_COLS = 8, 16
D_PAIRS = ((128, 1), (512, 4), (2048, 16))
D_SLOTS = 4
D_HEADS = D_SLOTS * len(D_PAIRS)
T5_BUCKETS = 32
T5_MAX_DIST = 2048
NORM_EPS = 1e-6
NEG_INF = -1e30
ATTN_SCALE = HEAD_DIM ** -0.5
LOG2_E = math.log2(math.e)

LANES = 128
SUBLANES = 8
MXU_COLS = 256
FFN_ROW_TILE = 1024
FFN_FF_TILE = 2 * MXU_COLS
DENSE_KEY_CHUNK = MXU_COLS
CHEAP_ROW_STRIDE = 4
BAND_TILES_PER_STEP = (8, 4, 2, 1)
NA_ROWS_PER_STEP = 16
MIB = 1024 * 1024
VMEM_LIMIT_BYTES = 56 * MIB

BF16 = jnp.bfloat16
F32 = jnp.float32


def _params(semantics):
    return pltpu.CompilerParams(dimension_semantics=semantics, vmem_limit_bytes=VMEM_LIMIT_BYTES)


def _resident_weight(shape, layer):
    return pl.BlockSpec((None,) + shape, lambda *_: (layer,) + (0,) * len(shape),
                        pipeline_mode=pl.Buffered(1))


def _rms(x, gain):
    ms = jnp.mean(x * x, axis=-1, keepdims=True)
    return x * lax.rsqrt(ms + NORM_EPS) * gain


def _ffn_body(*refs, n_ff, has_dest, final_norm):
    refs = list(refs)
    x_ref, gain_ref, wg_ref, wu_ref, wo_ref = [refs.pop(0) for _ in range(5)]
    fgain_ref = refs.pop(0) if final_norm else None
    if has_dest:
        refs.pop(0)
    o_ref, h_scr = refs
    j = pl.program_id(1)

    def step(first):
        if first:
            h_scr[...] = _rms(x_ref[...], gain_ref[...]).astype(BF16)
        h = h_scr[...]
        g = jnp.dot(h, wg_ref[...], preferred_element_type=F32)
        u = jnp.dot(h, wu_ref[...], preferred_element_type=F32)
        a = (g * jax.nn.sigmoid(g) * (0.5 * u)).astype(BF16)
        y = jnp.dot(a, wo_ref[...], preferred_element_type=F32)
        if first:
            o_ref[...] = x_ref[...] + y
        else:
            o_ref[...] += y

    pl.when(j == 0)(lambda: step(True))
    pl.when(j > 0)(lambda: step(False))

    if final_norm:
        @pl.when(j == n_ff - 1)
        def _():
            o_ref[...] = _rms(o_ref[...], fgain_ref[...])


def _ffn(x, gain, w_in, w_out, layer, final_gain=None, *, x_row0=0, x_rows=None, out_rows=None,
         out_row0=0, dest=None, tm=FFN_ROW_TILE):
    d = x.shape[1]
    x_rows = x.shape[0] - x_row0 if x_rows is None else x_rows
    out_rows = (x_rows if dest is None else dest.shape[0]) if out_rows is None else out_rows
    f = w_out.shape[1]
    tf = math.gcd(FFN_FF_TILE, f)
    n_ff = f // tf
    tm = math.gcd(tm, x_rows, x_row0, out_row0)
    assert w_in.shape[1:] == (d, 2 * f) and w_out.shape[2] == d and out_row0 + x_rows <= out_rows
    final_norm = final_gain is not None
    xt0, ot0 = x_row0 // tm, out_row0 // tm
    in_specs = [
        pl.BlockSpec((tm, d), lambda i, j: (i + xt0, 0)),
        pl.BlockSpec((1, d), lambda i, j: (0, 0)),
        pl.BlockSpec((None, d, tf), lambda i, j: (layer, 0, j)),
        pl.BlockSpec((None, d, tf), lambda i, j: (layer, 0, j + n_ff)),
        pl.BlockSpec((None, tf, d), lambda i, j: (layer, j, 0)),
    ]
    args = [x, gain.reshape(1, d), w_in, w_in, w_out]
    if final_norm:
        in_specs.append(pl.BlockSpec((1, d), lambda i, j: (0, 0)))
        args.append(final_gain.reshape(1, d))
    aliases = {}
    if dest is not None:
        assert dest.shape == (out_rows, d) and dest.dtype == F32
        in_specs.append(pl.BlockSpec(memory_space=pl.ANY))
        aliases = {len(args): 0}
        args.append(dest)
    return pl.pallas_call(
        functools.partial(_ffn_body, n_ff=n_ff, has_dest=dest is not None, final_norm=final_norm),
        grid=(x_rows // tm, n_ff),
        in_specs=in_specs,
        out_specs=pl.BlockSpec((tm, d), lambda i, j: (i + ot0, 0)),
        out_shape=jax.ShapeDtypeStruct((out_rows, d), F32),
        scratch_shapes=[pltpu.VMEM((tm, d), BF16)],
        input_output_aliases=aliases,
        compiler_params=_params(("parallel", "arbitrary")),
        name="ffn",
    )(*args)


def _rope_norm(r, gain, cos, sin):
    y = _rms(r, gain)
    return y * cos + pltpu.roll(y, HEAD_DIM // 2, axis=1) * sin


def _inproj_body(*refs, kinds, rope, sub_plan, dils):
    if rope:
        x_ref, gain_ref, w_ref, qg_ref, kg_ref, cos_ref, sin_ref, o_ref, qt_ref, vt_ref = refs
    else:
        x_ref, gain_ref, w_ref, o_ref = refs[:4]
        sub_refs, slab_scr = refs[4:4 + len(dils)], refs[-1] if dils else None
    tm = x_ref.shape[0]
    h = _rms(x_ref[...], gain_ref[...]).astype(BF16)
    n_qt = 0
    for c, kind in enumerate(kinds):
        lo = c * MXU_COLS
        r = jnp.dot(h, w_ref[:, lo:lo + MXU_COLS], preferred_element_type=F32)
        if kind == "scale":
            r = r * ATTN_SCALE
        elif kind in ("qnorm", "knorm"):
            hg = qg_ref[...] if kind == "qnorm" else kg_ref[...]
            cos = cos_ref[...]
            sin = sin_ref[...]
            parts = []
            for hh in range(MXU_COLS // HEAD_DIM):
                y = _rope_norm(r[:, hh * HEAD_DIM:(hh + 1) * HEAD_DIM], hg, cos, sin)
                if kind == "qnorm":
                    y = y * (ATTN_SCALE * LOG2_E)
                parts.append(y)
            r = jnp.concatenate(parts, axis=1)
            if kind == "qnorm":
                qt_ref[0, n_qt * MXU_COLS:(n_qt + 1) * MXU_COLS, :] = r.T.astype(BF16)
                n_qt += 1
        elif kind == "plain_t":
            vt_ref[0] = r.T.astype(BF16)
        o_ref[:, lo:lo + MXU_COLS] = r.astype(BF16)
        if c in sub_plan:
            heads = range(MXU_COLS // HEAD_DIM)
            q_rows = tm // CHEAP_ROW_STRIDE
            for hh in heads:
                slab_scr[0, hh] = r[:, hh * HEAD_DIM:(hh + 1) * HEAD_DIM]
            if any(dils[si] > CHEAP_ROW_STRIDE for si, _ in sub_plan[c]):
                for hh in heads:
                    for r4 in range(CHEAP_ROW_STRIDE):
                        slab_scr[1, hh, r4 * q_rows:(r4 + 1) * q_rows, :] = (
                            slab_scr[0, hh, pl.ds(r4, q_rows, stride=CHEAP_ROW_STRIDE), :])
            for si, col0 in sub_plan[c]:
                d = dils[si]
                for rho in range(d):
                    for hh in heads:
                        if d <= CHEAP_ROW_STRIDE:
                            part = slab_scr[0, hh, pl.ds(rho, tm // d, stride=d), :]
                        else:
                            r4, r2 = rho % CHEAP_ROW_STRIDE, rho // CHEAP_ROW_STRIDE
                            part = slab_scr[1, hh, pl.ds(r4 * q_rows + r2, tm // d,
                                                         stride=d // CHEAP_ROW_STRIDE), :]
                        cols = slice(col0 + hh * HEAD_DIM, col0 + (hh + 1) * HEAD_DIM)
                        sub_refs[si][0, rho, :, cols] = part.astype(BF16)


def _inproj(x, gain, w, layer, kinds, seq_len, rope_args=None, sub_args=None, *, tm=512):
    n, d = x.shape
    c = w.shape[2]
    assert n % tm == 0 and seq_len % tm == 0 and c == MXU_COLS * len(kinds)
    rope = rope_args is not None
    assert not (rope and sub_args)
    dils, sub_width, sub_plan = sub_args if sub_args else ((), 0, {})
    scratch = [pltpu.VMEM((2, MXU_COLS // HEAD_DIM, tm, HEAD_DIM), F32)] if dils else []
    assert all(dl <= CHEAP_ROW_STRIDE or (dl % CHEAP_ROW_STRIDE == 0 and dl <= CHEAP_ROW_STRIDE ** 2)
               for dl in dils)
    in_specs = [
        pl.BlockSpec((tm, d), lambda i: (i, 0)),
        pl.BlockSpec((1, d), lambda i: (0, 0)),
        _resident_weight((d, c), layer),
    ]
    args = [x, gain.reshape(1, d), w]
    if rope:
        q_gain, k_gain, cos, sin = rope_args
        tiles_per_seq = seq_len // tm
        in_specs += [
            pl.BlockSpec((1, HEAD_DIM), lambda i: (0, 0)),
            pl.BlockSpec((1, HEAD_DIM), lambda i: (0, 0)),
            pl.BlockSpec((tm, HEAD_DIM), lambda i: (i % tiles_per_seq, 0)),
            pl.BlockSpec((tm, HEAD_DIM), lambda i: (i % tiles_per_seq, 0)),
        ]
        args += [q_gain.reshape(1, HEAD_DIM), k_gain.reshape(1, HEAD_DIM), cos, sin]
    out_specs = pl.BlockSpec((tm, c), lambda i: (i, 0))
    out_shape = jax.ShapeDtypeStruct((n, c), BF16)
    if rope:
        assert kinds.count("plain_t") == 1
        qt_rows = MXU_COLS * kinds.count("qnorm")
        by_seq = lambda i: (i // tiles_per_seq, 0, i % tiles_per_seq)
        out_specs = [out_specs, pl.BlockSpec((1, qt_rows, tm), by_seq),
                     pl.BlockSpec((1, MXU_COLS, tm), by_seq)]
        out_shape = [out_shape, jax.ShapeDtypeStruct((n // seq_len, qt_rows, seq_len), BF16),
                     jax.ShapeDtypeStruct((n // seq_len, MXU_COLS, seq_len), BF16)]
    if dils:
        tiles_per_seq = seq_len // tm
        assert all(tm % (dl * 16) == 0 for dl in dils)
        out_specs = [out_specs] + [
            pl.BlockSpec((1, dl, tm // dl, sub_width), lambda i: (i // tiles_per_seq, 0, i % tiles_per_seq, 0))
            for dl in dils]
        out_shape = [out_shape] + [
            jax.ShapeDtypeStruct((n // seq_len, dl, seq_len // dl, sub_width), BF16) for dl in dils]
    return pl.pallas_call(
        functools.partial(_inproj_body, kinds=tuple(kinds), rope=rope, sub_plan=sub_plan, dils=tuple(dils)),
        grid=(n // tm,),
        in_specs=in_specs,
        out_specs=out_specs,
        out_shape=out_shape,
        scratch_shapes=scratch,
        compiler_params=_params(("parallel",)),
        name="inproj",
    )(*args)


def _outproj_ab_body(x_ref, a_ref, b_ref, w_ref, o_ref):
    ka = a_ref.shape[1]
    acc = jnp.dot(a_ref[...], w_ref[0:ka, :], preferred_element_type=F32)
    acc = acc + jnp.dot(b_ref[...], w_ref[ka:, :], preferred_element_type=F32)
    o_ref[...] = x_ref[...] + acc


def _outproj_ab(x, oa, ob, w, layer, *, tm=512):
    n, d = x.shape
    ka, kb = oa.shape[1], ob.shape[1]
    assert n % tm == 0 and w.shape[1:] == (ka + kb, d)
    return pl.pallas_call(
        _outproj_ab_body,
        grid=(n // tm,),
        in_specs=[
            pl.BlockSpec((tm, d), lambda i: (i, 0)),
            pl.BlockSpec((tm, ka), lambda i: (i, 0)),
            pl.BlockSpec((tm, kb), lambda i: (i, 0)),
            _resident_weight((ka + kb, d), layer),
        ],
        out_specs=pl.BlockSpec((tm, d), lambda i: (i, 0)),
        out_shape=jax.ShapeDtypeStruct((n, d), F32),
        compiler_params=_params(("parallel",)),
        name="outproj_ab",
    )(x, oa, ob, w)


def _outproj_cd_body(*refs, dils, kd):
    ng = len(dils)
    x_ref, c_ref = refs[:2]
    o_refs, l_refs = refs[2:2 + ng], refs[2 + ng:2 + 2 * ng]
    w_ref, o_ref, row_scr = refs[2 + 2 * ng:]
    tm = x_ref.shape[0]
    heads = kd // HEAD_DIM

    def token_rows(ref, d, slot, width):
        if d == 1:
            return ref[...]
        for rho in range(d):
            for s in range(width // LANES):
                lo = rho * width + s * LANES
                row_scr[slot, s, pl.ds(rho, tm // d, stride=d), :] = ref[0, :, lo:lo + LANES]
        return jnp.concatenate([row_scr[slot, s] for s in range(width // LANES)], axis=1)

    outs = [token_rows(r, d, 2 * i, kd) for i, (r, d) in enumerate(zip(o_refs, dils))]
    lses = [token_rows(r, d, 2 * i + 1, LANES) for i, (r, d) in enumerate(zip(l_refs, dils))]
    m = functools.reduce(jnp.maximum, lses)
    es = [jnp.exp(l - m) for l in lses]
    den = functools.reduce(lambda a, b: a + b, es)
    wts = [e / den for e in es]
    parts = []
    for hd in range(heads):
        lane = hd * (LANES // heads)
        cols = slice(hd * HEAD_DIM, (hd + 1) * HEAD_DIM)
        terms = [jnp.broadcast_to(wt[:, lane:lane + 1], (tm, HEAD_DIM)) * o[:, cols] for wt, o in zip(wts, outs)]
        parts.append(functools.reduce(lambda a, b: a + b, terms))
    od = jnp.concatenate(parts, axis=1)
    kc = c_ref.shape[1]
    acc = jnp.dot(c_ref[...], w_ref[0:kc, :], preferred_element_type=F32)
    acc = acc + jnp.dot(od.astype(BF16), w_ref[kc:, :], preferred_element_type=F32)
    o_ref[...] = x_ref[...] + acc


def _outproj_cd(x, oc, outs, lses, dils, seq_len, w, layer, *, tm=512):
    n, d = x.shape
    kc = oc.shape[1]
    kd = w.shape[1] - kc
    assert n % tm == 0 and seq_len % tm == 0 and all(tm % (dl * SUBLANES) == 0 for dl in dils)
    tiles_per_seq = seq_len // tm
    row = lambda i: (i, 0)
    by_seq = lambda i: (i // tiles_per_seq, i % tiles_per_seq, 0)

    def g_specs(width):
        return [pl.BlockSpec((tm, width), row) if dl == 1 else pl.BlockSpec((1, tm // dl, dl * width), by_seq)
                for dl in dils]

    return pl.pallas_call(
        functools.partial(_outproj_cd_body, dils=tuple(dils), kd=kd),
        grid=(n // tm,),
        in_specs=[pl.BlockSpec((tm, d), row), pl.BlockSpec((tm, kc), row)] + g_specs(kd) + g_specs(LANES)
        + [_resident_weight((kc + kd, d), layer)],
        out_specs=pl.BlockSpec((tm, d), row),
        out_shape=jax.ShapeDtypeStruct((n, d), F32),
        scratch_shapes=[pltpu.VMEM((2 * len(dils), kd // HEAD_DIM, tm, HEAD_DIM), F32)],
        compiler_params=_params(("parallel",)),
        name="outproj_cd",
    )(x, oc, *outs, *lses, w)


def _stack_heads(q, g):
    return jnp.concatenate([q[:, i * HEAD_DIM:(i + 1) * HEAD_DIM] for i in range(g)], axis=0)


def _softmax_pv(s, v):
    m = jnp.max(s, axis=-1, keepdims=True)
    p = jnp.exp(s - m)
    l = jnp.sum(p, axis=-1, keepdims=True)
    o = jnp.dot(p.astype(BF16), v, preferred_element_type=F32) / l
    return o, m, l


def _qk(q, k):
    return lax.dot_general(q, k, (((1,), (1,)), ((), ())), preferred_element_type=F32)


def _dense_body(qt_ref, k_ref, vt_ref, o_ref, s_scr, p_scr, m_scr, l_scr, *, g):
    step = pl.program_id(0)
    tq = qt_ref.shape[2] // 2
    t = k_ref.shape[1]
    gq = g * tq
    kc = min(DENSE_KEY_CHUNK, t)

    @pl.when(step == 0)
    def _():
        s_scr[1] = jnp.zeros(s_scr.shape[1:], s_scr.dtype)
        m_scr[1] = jnp.zeros(m_scr.shape[1:], m_scr.dtype)
        p_scr[...] = jnp.zeros(p_scr.shape, p_scr.dtype)
        l_scr[...] = jnp.ones(l_scr.shape, l_scr.dtype)

    def fold(x, op):
        return op(x.reshape(x.shape[0] // SUBLANES, SUBLANES, x.shape[1]), axis=0)

    for cur in range(2):
        prev = 1 - cur
        toks = slice(cur * tq, (cur + 1) * tq)
        qt = jnp.concatenate([qt_ref[0, i * HEAD_DIM:(i + 1) * HEAD_DIM, toks] for i in range(g)], axis=1)
        m = jnp.max(m_scr[prev], axis=0, keepdims=True)
        ot = jnp.zeros((HEAD_DIM, gq), F32)
        l_acc = jnp.zeros((SUBLANES, gq), F32)
        m_acc = None
        for c in range(t // kc):
            keys = slice(c * kc, (c + 1) * kc)
            p = jnp.exp2(s_scr[prev, keys, :] - m)
            l_acc = l_acc + fold(p, jnp.sum)
            p_scr[prev, keys, :] = p.astype(BF16)
        for c in range(t // kc):
            keys = slice(c * kc, (c + 1) * kc)
            ot = ot + jnp.dot(vt_ref[0, :, keys], p_scr[cur, keys, :], preferred_element_type=F32)
            s = jnp.dot(k_ref[0, keys, :], qt, preferred_element_type=F32)
            s_scr[cur, keys, :] = s
            m_chunk = fold(s, jnp.max)
            m_acc = m_chunk if m_acc is None else jnp.maximum(m_acc, m_chunk)
        ot = ot / jnp.sum(l_scr[cur], axis=0, keepdims=True)
        l_scr[prev] = l_acc
        m_scr[cur] = m_acc
        for i in range(g):
            o_ref[0, toks, i * HEAD_DIM:(i + 1) * HEAD_DIM] = ot[:, i * tq:(i + 1) * tq].T.astype(o_ref.dtype)


def _dense_attention(qt, qkv, vt, k_col, n_kv, g, *, tq=128):
    b, t, _ = qkv.shape
    gw = g * HEAD_DIM
    n_pairs = t // (2 * tq)
    assert t % (2 * tq) == 0 and k_col % HEAD_DIM == 0
    assert qt.shape == (b, n_kv * gw, t) and vt.shape == (b, n_kv * HEAD_DIM, t)
    kb = k_col // HEAD_DIM
    total = b * n_kv * n_pairs

    def where(f):
        return f // (n_kv * n_pairs), (f // n_pairs) % n_kv, f % n_pairs

    def q_map(f):
        n, h, k = where(jnp.minimum(f, total - 1))
        return n, h, k

    def k_map(f):
        n, h, _ = where(jnp.minimum(f, total - 1))
        return n, 0, kb + h

    def v_map(f):
        n, h, _ = where(jnp.maximum(f - 1, 0))
        return n, h, 0

    def o_map(f):
        n, h, k = where(jnp.maximum(f - 1, 0))
        return n, k, h

    return pl.pallas_call(
        functools.partial(_dense_body, g=g),
        grid=(total + 1,),
        in_specs=[
            pl.BlockSpec((1, gw, 2 * tq), q_map),
            pl.BlockSpec((1, t, HEAD_DIM), k_map),
            pl.BlockSpec((1, HEAD_DIM, t), v_map),
        ],
        out_specs=pl.BlockSpec((1, 2 * tq, gw), o_map),
        out_shape=jax.ShapeDtypeStruct((b, t, n_kv * gw), BF16),
        scratch_shapes=[pltpu.VMEM((2, t, g * tq), F32), pltpu.VMEM((2, t, g * tq), BF16),
                        pltpu.VMEM((2, SUBLANES, g * tq), F32), pltpu.VMEM((2, SUBLANES, g * tq), F32)],
        compiler_params=_params(("arbitrary",)),
        name="dense_attn",
    )(qt, qkv, vt)


def _band_body(*refs, n_kv, g, w, seq, tq, kw, n_cases, has_sink, want_lse, side_by_side):
    refs = list(refs)
    q_ref, k_ref, v_ref, bias_ref = refs[:4]
    refs = refs[4:]
    sink_ref = refs.pop(0) if has_sink else None
    o_ref = refs.pop(0)
    lse_ref = refs.pop(0) if want_lse else None
    gw = g * HEAD_DIM
    qw = n_kv * gw
    nq = seq // tq
    for sq, j in [(a, c) for a in range(q_ref.shape[0]) for c in range(q_ref.shape[1] // tq)]:
        qi = pl.program_id(1) * (q_ref.shape[1] // tq) + j
        kstart = pl.multiple_of(jnp.clip(qi * tq - w, 0, seq - kw), 64)
        case = 0 if n_cases == 1 else jnp.where(qi == 0, 0, jnp.where(qi == nq - 1, 2, 1))
        trows = slice(j * tq, (j + 1) * tq)
        osq, ocol0, lcol0 = (0, sq * qw, sq * LANES) if side_by_side else (sq, 0, 0)
        lse_tile = jnp.zeros((tq, LANES), F32)
        head_of_lane = lax.broadcasted_iota(jnp.int32, (tq, LANES), 1) // (LANES // (n_kv * g))
        for h in range(n_kv):
            k = k_ref[sq, pl.ds(kstart, kw), h * HEAD_DIM:(h + 1) * HEAD_DIM]
            v = v_ref[sq, pl.ds(kstart, kw), h * HEAD_DIM:(h + 1) * HEAD_DIM]
            q = _stack_heads(q_ref[sq, trows, h * gw:(h + 1) * gw], g)
            s = _qk(q, k) + bias_ref[case, h]
            o, m, l = _softmax_pv(s, v)
            lse = m + jnp.log(l)
            for i in range(g):
                rows = slice(i * tq, (i + 1) * tq)
                cols = slice(h * gw + i * HEAD_DIM, h * gw + (i + 1) * HEAD_DIM)
                oi = o[rows]
                if has_sink:
                    oi = oi * jax.nn.sigmoid(lse[rows] - sink_ref[h * g + i])
                o_ref[osq, trows, ocol0 + cols.start:ocol0 + cols.stop] = oi.astype(o_ref.dtype)
                if want_lse:
                    lse_tile = jnp.where(head_of_lane == h * g + i, lse[rows], lse_tile)
        if want_lse:
            lse_ref[osq, trows, lcol0:lcol0 + LANES] = lse_tile


def _band_geometry(seq, w):
    if seq <= 4 * w:
        return seq, seq, (0,)
    tq = 2 * w if w < 128 else w
    tq = min(tq, 256)
    kw = tq + 2 * w
    assert seq % tq == 0 and seq >= kw
    return tq, kw, (0, -w, -2 * w)


def _band_attention(q_arr, q_col, k_arr, k_col, v_arr, v_col, bias, n_kv, g, w, geometry,
                    sink=None, want_lse=False, out_dtype=BF16, dil=1):
    n, seq, _ = q_arr.shape
    tq, kw, shifts = geometry
    nq = seq // tq
    gw = g * HEAD_DIM
    qw, kvw = n_kv * gw, n_kv * HEAD_DIM
    assert q_col % qw == 0 and k_col % kvw == 0 and v_col % kvw == 0
    qb, kb, vb = q_col // qw, k_col // kvw, v_col // kvw
    tps = next(c for c in BAND_TILES_PER_STEP if nq % c == 0)
    nq_steps = nq // tps
    sps = 1
    if nq == 1:
        sps = next(c for c in BAND_TILES_PER_STEP if n % c == 0 and (dil == 1 or dil % c == 0))
    in_specs = [
        pl.BlockSpec((sps, tps * tq, qw), lambda b, i: (b, i, qb)),
        pl.BlockSpec((sps, seq, kvw), lambda b, i: (b, 0, kb)),
        pl.BlockSpec((sps, seq, kvw), lambda b, i: (b, 0, vb)),
        pl.BlockSpec((len(shifts), n_kv, g * tq, kw), lambda b, i: (0, 0, 0, 0)),
    ]
    args = [q_arr, k_arr, v_arr, bias]
    if sink is not None:
        in_specs.append(pl.BlockSpec(memory_space=pltpu.SMEM))
        args.append(sink.astype(F32))
    def out_like(width, dtype):
        if dil == 1:
            return (pl.BlockSpec((sps, tps * tq, width), lambda b, i: (b, i, 0)),
                    jax.ShapeDtypeStruct((n, seq, width), dtype))
        return (pl.BlockSpec((1, tps * tq, sps * width),
                             lambda b, i: ((b * sps) // dil, i, ((b * sps) % dil) // sps)),
                jax.ShapeDtypeStruct((n // dil, seq, dil * width), dtype))

    assert dil == 1 or (n % dil == 0 and dil % sps == 0)
    out_specs, out_shape = out_like(qw, out_dtype)
    if want_lse:
        assert LANES % (n_kv * g) == 0
        l_spec, l_shape = out_like(LANES, F32)
        out_specs, out_shape = [out_specs, l_spec], [out_shape, l_shape]
    return pl.pallas_call(
        functools.partial(_band_body, n_kv=n_kv, g=g, w=w, seq=seq, tq=tq, kw=kw, n_cases=len(shifts),
                          has_sink=sink is not None, want_lse=want_lse, side_by_side=dil > 1),
        grid=(n // sps, nq_steps),
        in_specs=in_specs,
        out_specs=out_specs,
        out_shape=out_shape,
        compiler_params=_params(("parallel", "arbitrary")),
        name="band_attn",
    )(*args)


def _na_body(q_ref, k_ref, v_ref, bias_ref, o_ref, *, g, rows, rows_per_step):
    rb = pl.program_id(2)
    kwin = NA_ROWS * GRID_W

    def one_row(rr, carry):
        r = rb * rows_per_step + rr
        rs = jnp.clip(r - NA_ROWS // 2, 0, rows - NA_ROWS)
        off = rs - r + NA_ROWS - 1
        kstart = pl.multiple_of(rs * GRID_W, GRID_W)
        qstart = pl.multiple_of(rr * GRID_W, GRID_W)
        k = k_ref[0, pl.ds(kstart, kwin), :]
        v = v_ref[0, pl.ds(kstart, kwin), :]
        q = _stack_heads(q_ref[0, pl.ds(qstart, GRID_W), :], g)
        s = _qk(q, k) + bias_ref[0, off]
        o, _, _ = _softmax_pv(s, v)
        for i in range(g):
            o_ref[0, pl.ds(qstart, GRID_W), i * HEAD_DIM:(i + 1) * HEAD_DIM] = (
                o[i * GRID_W:(i + 1) * GRID_W].astype(o_ref.dtype))
        return carry

    lax.fori_loop(0, rows_per_step, one_row, 0, unroll=True)


def _na_attention(qkv, q_col, k_col, v_col, bias, n_kv, g, *, rows_per_step=NA_ROWS_PER_STEP):
    b, t, _ = qkv.shape
    rows = t // GRID_W
    gw = g * HEAD_DIM
    assert rows % rows_per_step == 0 and rows >= NA_ROWS
    qb, kb, vb = q_col // gw, k_col // HEAD_DIM, v_col // HEAD_DIM
    tq = rows_per_step * GRID_W
    return pl.pallas_call(
        functools.partial(_na_body, g=g, rows=rows, rows_per_step=rows_per_step),
        grid=(b, n_kv, rows // rows_per_step),
        in_specs=[
            pl.BlockSpec((1, tq, gw), lambda n, h, i: (n, i, qb + h)),
            pl.BlockSpec((1, t, HEAD_DIM), lambda n, h, i: (n, 0, kb + h)),
            pl.BlockSpec((1, t, HEAD_DIM), lambda n, h, i: (n, 0, vb + h)),
            pl.BlockSpec((1, NA_ROWS, g * GRID_W, NA_ROWS * GRID_W), lambda n, h, i: (h, 0, 0, 0)),
        ],
        out_specs=pl.BlockSpec((1, tq, gw), lambda n, h, i: (n, i, h)),
        out_shape=jax.ShapeDtypeStruct((b, t, n_kv * gw), BF16),
        compiler_params=_params(("parallel", "parallel", "arbitrary")),
        name="na_attn",
    )(qkv, qkv, qkv, bias)


def _t5_bucket(rel):
    half = T5_BUCKETS // 2
    max_exact = half // 2
    n = jnp.abs(rel)
    nf = jnp.maximum(n, 1).astype(jnp.float32)
    large = max_exact + (jnp.log(nf / max_exact) / math.log(T5_MAX_DIST / max_exact)
                         * (half - max_exact)).astype(jnp.int32)
    large = jnp.minimum(large, half - 1)
    return jnp.where(rel > 0, half, 0) + jnp.where(n < max_exact, n, large)


def _band_bias(table_cols, n_kv, g, w, dil, geometry):
    tq, kw, shifts = geometry
    iq = jnp.arange(tq)[:, None]
    jk = jnp.arange(kw)[None, :]
    tiles = []
    for shift in shifts:
        rel = jk + shift - iq
        bucket = _t5_bucket(rel * dil)
        tb = jnp.zeros((tq, kw, table_cols.shape[1]), F32)
        for r in range(T5_BUCKETS):
            tb = jnp.where((bucket == r)[..., None], table_cols[r].astype(F32), tb)
        tb = jnp.where((jnp.abs(rel) <= w)[..., None], tb, NEG_INF)
        tiles.append(jnp.moveaxis(tb, -1, 0).reshape(n_kv, g * tq, kw))
    return jnp.stack(tiles)


def _na_bias(rpb, n_kv, g):
    col = jnp.arange(GRID_W)
    qc, kc = col[:, None], col[None, :]
    dc = jnp.clip(kc - qc + NA_COLS - 1, 0, 2 * NA_COLS - 2)
    cs = jnp.clip(qc - NA_COLS // 2, 0, GRID_W - NA_COLS)
    mask = (kc >= cs) & (kc < cs + NA_COLS)
    rpb = rpb.astype(F32)
    t = jnp.zeros(rpb.shape[:2] + dc.shape, F32)
    for c in range(2 * NA_COLS - 1):
        t = jnp.where(dc == c, rpb[:, :, c][:, :, None, None], t)
    t = jnp.where(mask, t, NEG_INF)
    cls = jnp.stack([t[:, o:o + NA_ROWS] for o in range(NA_ROWS)])
    cls = cls.reshape(NA_ROWS, n_kv, g, NA_ROWS, GRID_W, GRID_W)
    cls = jnp.transpose(cls, (1, 0, 2, 4, 3, 5))
    return cls.reshape(n_kv, NA_ROWS, g * GRID_W, NA_ROWS * GRID_W)


def _rope_tables(t):
    n_pairs = HEAD_DIM // 2
    n_freq = n_pairs // 2
    pos = jnp.arange(t)
    row = (pos // GRID_W).astype(jnp.float32)
    col = (pos % GRID_W).astype(jnp.float32)
    omega = ROPE_THETA ** (-(jnp.arange(n_freq, dtype=jnp.float32) * 2.0 / n_pairs))
    ang = jnp.concatenate([row[:, None] * omega, col[:, None] * omega], axis=-1)
    cos, sin = jnp.cos(ang), jnp.sin(ang)
    return jnp.concatenate([cos, cos], axis=-1), jnp.concatenate([-sin, sin], axis=-1)


def _deinterleave_heads(w, n_heads):
    lead = w.shape[:-1]
    w = w.reshape(lead + (n_heads, HEAD_DIM // 2, 2))
    return jnp.swapaxes(w, -1, -2).reshape(lead + (n_heads * HEAD_DIM,))


AB_KINDS = ("scale",) * 4 + ("qnorm",) * 4 + ("plain", "plain", "knorm", "plain_t")
CD_KINDS = ("scale",) * 4 + ("plain",) * 2 + ("scale",) * 6 + ("plain",) * 4


def kernel(x_prompt, x_sample, norm_ffn1, ffn1_w_in, ffn1_w_out, norm_mix, ab_w_in, ab_sink, ab_q_gain, ab_k_gain, ab_w_out, cd_w_in, cd_rpb, cd_w_out, norm_ffn2, ffn2_w_in, ffn2_w_out, t5_table, final_norm):
    (bp, t, d), bs = x_prompt.shape, x_sample.shape[0]
    assert x_sample.shape[1:] == (t, d)
    b = bp + bs
    depth = norm_ffn1.shape[0]

    ffn1_w_in, ffn1_w_out = ffn1_w_in.astype(BF16), ffn1_w_out.astype(BF16)
    ffn2_w_in, ffn2_w_out = ffn2_w_in.astype(BF16), ffn2_w_out.astype(BF16)
    qa_w, kva_w = A_HEADS * HEAD_DIM, 2 * A_KV * HEAD_DIM
    qb_lo = qa_w + kva_w
    qb_hi = qb_lo + B_HEADS * HEAD_DIM
    kb_hi = qb_hi + B_KV * HEAD_DIM
    ab_w_in = jnp.concatenate([
        ab_w_in[..., :qa_w],
        _deinterleave_heads(ab_w_in[..., qb_lo:qb_hi], B_HEADS),
        ab_w_in[..., qa_w:qb_lo],
        _deinterleave_heads(ab_w_in[..., qb_hi:kb_hi], B_KV),
        ab_w_in[..., kb_hi:]], axis=-1).astype(BF16)
    ab_q_gain = _deinterleave_heads(ab_q_gain, 1)
    ab_k_gain = _deinterleave_heads(ab_k_gain, 1)
    ab_w_out, cd_w_in, cd_w_out = ab_w_out.astype(BF16), cd_w_in.astype(BF16), cd_w_out.astype(BF16)

    cos, sin = _rope_tables(t)
    ga, gc = A_HEADS // A_KV, C_HEADS // C_KV
    geo_a = _band_geometry(t, A_WINDOW)
    bias_a = _band_bias(t5_table[:, :A_HEADS], A_KV, ga, A_WINDOW, 1, geo_a)
    d_groups = []
    for gi, (win, dil) in enumerate(D_PAIRS):
        hs = win // (2 * dil)
        geo = _band_geometry(t // dil, hs)
        col0 = A_HEADS + gi * D_SLOTS
        d_groups.append((dil, hs, geo, _band_bias(t5_table[:, col0:col0 + D_SLOTS], D_SLOTS, 1, hs, dil, geo)))

    a_q = 0
    a_k = qa_w + B_HEADS * HEAD_DIM
    a_v = a_k + A_KV * HEAD_DIM
    b_k = a_v + A_KV * HEAD_DIM
    c_q, c_k, c_v = 0, C_HEADS * HEAD_DIM, (C_HEADS + C_KV) * HEAD_DIM
    d_q = (C_HEADS + 2 * C_KV) * HEAD_DIM
    d_k = d_q + D_HEADS * HEAD_DIM
    d_v = d_k + D_SLOTS * HEAD_DIM
    dw = D_SLOTS * HEAD_DIM
    sub_dils = tuple(dil for _, dil in D_PAIRS if dil > 1)
    sub_plan = {}
    for gi, (_, dil) in enumerate(D_PAIRS):
        if dil > 1:
            si = sub_dils.index(dil)
            for src, dst in ((d_q + gi * dw, 0), (d_k, dw), (d_v, 2 * dw)):
                for off in range(0, dw, MXU_COLS):
                    sub_plan.setdefault((src + off) // MXU_COLS, []).append((si, dst + off))

    for l in range(depth):
        i = l // 2
        if l == 0:
            x = _ffn(x_prompt.reshape(bp * t, d), norm_ffn1[l], ffn1_w_in, ffn1_w_out, l, out_rows=b * t)
            x = _ffn(x_sample.reshape(bs * t, d), norm_ffn1[l], ffn1_w_in, ffn1_w_out, l,
                     out_row0=bp * t, dest=x)
        else:
            x = _ffn(x, norm_ffn1[l], ffn1_w_in, ffn1_w_out, l)
        if l % 2 == 0:
            qkv, qbt, vbt = _inproj(x, norm_mix[l], ab_w_in, i, AB_KINDS, t,
                                    (ab_q_gain[i], ab_k_gain[i], cos, sin))
            qkv = qkv.reshape(b, t, -1)
            o_a = _band_attention(qkv, a_q, qkv, a_k, qkv, a_v, bias_a, A_KV, ga, A_WINDOW, geo_a,
                                  sink=ab_sink[i])
            o_b = _dense_attention(qbt, qkv, vbt, b_k, B_KV, B_HEADS // B_KV)
            x = _outproj_ab(x, o_a.reshape(b * t, -1), o_b.reshape(b * t, -1), ab_w_out, i)
        else:
            qkv, *subs = _inproj(x, norm_mix[l], cd_w_in, i, CD_KINDS, t,
                                 sub_args=(sub_dils, 3 * dw, sub_plan))
            qkv = qkv.reshape(b, t, -1)
            o_c = _na_attention(qkv, c_q, c_k, c_v, _na_bias(cd_rpb[i], C_KV, gc), C_KV, gc)
            outs, lses = [], []
            for gi, (dil, hs, geo, bias_d) in enumerate(d_groups):
                if dil == 1:
                    o_g, lse_g = _band_attention(qkv, d_q + gi * dw, qkv, d_k, qkv, d_v, bias_d,
                                                 D_SLOTS, 1, hs, geo, want_lse=True, out_dtype=F32)
                    o_g, lse_g = o_g.reshape(b * t, dw), lse_g.reshape(b * t, LANES)
                else:
                    sub = subs[sub_dils.index(dil)].reshape(b * dil, t // dil, 3 * dw)
                    o_g, lse_g = _band_attention(sub, 0, sub, dw, sub, 2 * dw, bias_d, D_SLOTS, 1, hs, geo,
                                                 want_lse=True, out_dtype=F32, dil=dil)
                outs.append(o_g)
                lses.append(lse_g)
            x = _outproj_cd(x, o_c.reshape(b * t, -1), outs, lses, [g[0] for g in d_groups], t, cd_w_out, i)
        if l < depth - 1:
            x = _ffn(x, norm_ffn2[l], ffn2_w_in, ffn2_w_out, l)

    last = depth - 1
    y_prompt = _ffn(x, norm_ffn2[last], ffn2_w_in, ffn2_w_out, last, final_norm, x_rows=bp * t)
    y_sample = _ffn(x, norm_ffn2[last], ffn2_w_in, ffn2_w_out, last, final_norm, x_row0=bp * t)
    return (y_prompt.reshape(bp, t, d), y_sample.reshape(bs, t, d))
```

```python
import functools
import math

import jax
import jax.numpy as jnp
from jax import lax
from jax.experimental import pallas as pl
from jax.experimental.pallas import tpu as pltpu

HEAD_DIM = 128
GRID_W = 64
A_HEADS, A_KV, A_WINDOW = 8, 2, 128
B_HEADS, B_KV = 8, 2
ROPE_THETA = 10000.0
C_HEADS, C_KV = 8, 2
NA_ROWS, NA_COLS = 8, 16
D_PAIRS = ((128, 1), (512, 4), (2048, 16))
D_SLOTS = 4
D_HEADS = D_SLOTS * len(D_PAIRS)
T5_BUCKETS = 32
T5_MAX_DIST = 2048
NORM_EPS = 1e-6
NEG_INF = -1e30
ATTN_SCALE = HEAD_DIM ** -0.5
LOG2_E = math.log2(math.e)

LANES = 128
SUBLANES = 8
MXU_COLS = 256
FFN_ROW_TILE = 1024
FFN_FF_TILE = 2 * MXU_COLS
DENSE_KEY_CHUNK = MXU_COLS
CHEAP_ROW_STRIDE = 4
BAND_TILES_PER_STEP = (8, 4, 2, 1)
NA_ROWS_PER_STEP = 16
MIB = 1024 * 1024
VMEM_LIMIT_BYTES = 56 * MIB

BF16 = jnp.bfloat16
F32 = jnp.float32


def _params(semantics):
    return pltpu.CompilerParams(dimension_semantics=semantics, vmem_limit_bytes=VMEM_LIMIT_BYTES)


def _resident_weight(shape, layer):
    return pl.BlockSpec((None,) + shape, lambda *_: (layer,) + (0,) * len(shape),
                        pipeline_mode=pl.Buffered(1))


def _rms(x, gain):
    ms = jnp.mean(x * x, axis=-1, keepdims=True)
    return x * lax.rsqrt(ms + NORM_EPS) * gain


def _ffn_body(*refs, n_ff, has_dest, final_norm):
    refs = list(refs)
    x_ref, gain_ref, wg_ref, wu_ref, wo_ref = [refs.pop(0) for _ in range(5)]
    fgain_ref = refs.pop(0) if final_norm else None
    if has_dest:
        refs.pop(0)
    o_ref, h_scr = refs
    j = pl.program_id(1)

    def step(first):
        if first:
            h_scr[...] = _rms(x_ref[...], gain_ref[...]).astype(BF16)
        h = h_scr[...]
        g = jnp.dot(h, wg_ref[...], preferred_element_type=F32)
        u = jnp.dot(h, wu_ref[...], preferred_element_type=F32)
        a = (g * jax.nn.sigmoid(g) * (0.5 * u)).astype(BF16)
        y = jnp.dot(a, wo_ref[...], preferred_element_type=F32)
        if first:
            o_ref[...] = x_ref[...] + y
        else:
            o_ref[...] += y

    pl.when(j == 0)(lambda: step(True))
    pl.when(j > 0)(lambda: step(False))

    if final_norm:
        @pl.when(j == n_ff - 1)
        def _():
            o_ref[...] = _rms(o_ref[...], fgain_ref[...])


def _ffn(x, gain, w_in, w_out, layer, final_gain=None, *, x_row0=0, x_rows=None, out_rows=None,
         out_row0=0, dest=None, tm=FFN_ROW_TILE):
    d = x.shape[1]
    x_rows = x.shape[0] - x_row0 if x_rows is None else x_rows
    out_rows = (x_rows if dest is None else dest.shape[0]) if out_rows is None else out_rows
    f = w_out.shape[1]
    tf = math.gcd(FFN_FF_TILE, f)
    n_ff = f // tf
    tm = math.gcd(tm, x_rows, x_row0, out_row0)
    assert w_in.shape[1:] == (d, 2 * f) and w_out.shape[2] == d and out_row0 + x_rows <= out_rows
    final_norm = final_gain is not None
    xt0, ot0 = x_row0 // tm, out_row0 // tm
    in_specs = [
        pl.BlockSpec((tm, d), lambda i, j: (i + xt0, 0)),
        pl.BlockSpec((1, d), lambda i, j: (0, 0)),
        pl.BlockSpec((None, d, tf), lambda i, j: (layer, 0, j)),
        pl.BlockSpec((None, d, tf), lambda i, j: (layer, 0, j + n_ff)),
        pl.BlockSpec((None, tf, d), lambda i, j: (layer, j, 0)),
    ]
    args = [x, gain.reshape(1, d), w_in, w_in, w_out]
    if final_norm:
        in_specs.append(pl.BlockSpec((1, d), lambda i, j: (0, 0)))
        args.append(final_gain.reshape(1, d))
    aliases = {}
    if dest is not None:
        assert dest.shape == (out_rows, d) and dest.dtype == F32
        in_specs.append(pl.BlockSpec(memory_space=pl.ANY))
        aliases = {len(args): 0}
        args.append(dest)
    return pl.pallas_call(
        functools.partial(_ffn_body, n_ff=n_ff, has_dest=dest is not None, final_norm=final_norm),
        grid=(x_rows // tm, n_ff),
        in_specs=in_specs,
        out_specs=pl.BlockSpec((tm, d), lambda i, j: (i + ot0, 0)),
        out_shape=jax.ShapeDtypeStruct((out_rows, d), F32),
        scratch_shapes=[pltpu.VMEM((tm, d), BF16)],
        input_output_aliases=aliases,
        compiler_params=_params(("parallel", "arbitrary")),
        name="ffn",
    )(*args)


def _rope_norm(r, gain, cos, sin):
    y = _rms(r, gain)
    return y * cos + pltpu.roll(y, HEAD_DIM // 2, axis=1) * sin


def _inproj_body(*refs, kinds, rope, sub_plan, dils):
    if rope:
        x_ref, gain_ref, w_ref, qg_ref, kg_ref, cos_ref, sin_ref, o_ref, qt_ref, vt_ref = refs
    else:
        x_ref, gain_ref, w_ref, o_ref = refs[:4]
        sub_refs, slab_scr = refs[4:4 + len(dils)], refs[-1] if dils else None
    tm = x_ref.shape[0]
    h = _rms(x_ref[...], gain_ref[...]).astype(BF16)
    n_qt = 0
    for c, kind in enumerate(kinds):
        lo = c * MXU_COLS
        r = jnp.dot(h, w_ref[:, lo:lo + MXU_COLS], preferred_element_type=F32)
        if kind == "scale":
            r = r * ATTN_SCALE
        elif kind in ("qnorm", "knorm"):
            hg = qg_ref[...] if kind == "qnorm" else kg_ref[...]
            cos = cos_ref[...]
            sin = sin_ref[...]
            parts = []
            for hh in range(MXU_COLS // HEAD_DIM):
                y = _rope_norm(r[:, hh * HEAD_DIM:(hh + 1) * HEAD_DIM], hg, cos, sin)
                if kind == "qnorm":
                    y = y * (ATTN_SCALE * LOG2_E)
                parts.append(y)
            r = jnp.concatenate(parts, axis=1)
            if kind == "qnorm":
                qt_ref[0, n_qt * MXU_COLS:(n_qt + 1) * MXU_COLS, :] = r.T.astype(BF16)
                n_qt += 1
        elif kind == "plain_t":
            vt_ref[0] = r.T.astype(BF16)
        o_ref[:, lo:lo + MXU_COLS] = r.astype(BF16)
        if c in sub_plan:
            heads = range(MXU_COLS // HEAD_DIM)
            q_rows = tm // CHEAP_ROW_STRIDE
            for hh in heads:
                slab_scr[0, hh] = r[:, hh * HEAD_DIM:(hh + 1) * HEAD_DIM]
            if any(dils[si] > CHEAP_ROW_STRIDE for si, _ in sub_plan[c]):
                for hh in heads:
                    for r4 in range(CHEAP_ROW_STRIDE):
                        slab_scr[1, hh, r4 * q_rows:(r4 + 1) * q_rows, :] = (
                            slab_scr[0, hh, pl.ds(r4, q_rows, stride=CHEAP_ROW_STRIDE), :])
            for si, col0 in sub_plan[c]:
                d = dils[si]
                for rho in range(d):
                    for hh in heads:
                        if d <= CHEAP_ROW_STRIDE:
                            part = slab_scr[0, hh, pl.ds(rho, tm // d, stride=d), :]
                        else:
                            r4, r2 = rho % CHEAP_ROW_STRIDE, rho // CHEAP_ROW_STRIDE
                            part = slab_scr[1, hh, pl.ds(r4 * q_rows + r2, tm // d,
                                                         stride=d // CHEAP_ROW_STRIDE), :]
                        cols = slice(col0 + hh * HEAD_DIM, col0 + (hh + 1) * HEAD_DIM)
                        sub_refs[si][0, rho, :, cols] = part.astype(BF16)


def _inproj(x, gain, w, layer, kinds, seq_len, rope_args=None, sub_args=None, *, tm=512):
    n, d = x.shape
    c = w.shape[2]
    assert n % tm == 0 and seq_len % tm == 0 and c == MXU_COLS * len(kinds)
    rope = rope_args is not None
    assert not (rope and sub_args)
    dils, sub_width, sub_plan = sub_args if sub_args else ((), 0, {})
    scratch = [pltpu.VMEM((2, MXU_COLS // HEAD_DIM, tm, HEAD_DIM), F32)] if dils else []
    assert all(dl <= CHEAP_ROW_STRIDE or (dl % CHEAP_ROW_STRIDE == 0 and dl <= CHEAP_ROW_STRIDE ** 2)
               for dl in dils)
    in_specs = [
        pl.BlockSpec((tm, d), lambda i: (i, 0)),
        pl.BlockSpec((1, d), lambda i: (0, 0)),
        _resident_weight((d, c), layer),
    ]
    args = [x, gain.reshape(1, d), w]
    if rope:
        q_gain, k_gain, cos, sin = rope_args
        tiles_per_seq = seq_len // tm
        in_specs += [
            pl.BlockSpec((1, HEAD_DIM), lambda i: (0, 0)),
            pl.BlockSpec((1, HEAD_DIM), lambda i: (0, 0)),
            pl.BlockSpec((tm, HEAD_DIM), lambda i: (i % tiles_per_seq, 0)),
            pl.BlockSpec((tm, HEAD_DIM), lambda i: (i % tiles_per_seq, 0)),
        ]
        args += [q_gain.reshape(1, HEAD_DIM), k_gain.reshape(1, HEAD_DIM), cos, sin]
    out_specs = pl.BlockSpec((tm, c), lambda i: (i, 0))
    out_shape = jax.ShapeDtypeStruct((n, c), BF16)
    if rope:
        assert kinds.count("plain_t") == 1
        qt_rows = MXU_COLS * kinds.count("qnorm")
        by_seq = lambda i: (i // tiles_per_seq, 0, i % tiles_per_seq)
        out_specs = [out_specs, pl.BlockSpec((1, qt_rows, tm), by_seq),
                     pl.BlockSpec((1, MXU_COLS, tm), by_seq)]
        out_shape = [out_shape, jax.ShapeDtypeStruct((n // seq_len, qt_rows, seq_len), BF16),
                     jax.ShapeDtypeStruct((n // seq_len, MXU_COLS, seq_len), BF16)]
    if dils:
        tiles_per_seq = seq_len // tm
        assert all(tm % (dl * 16) == 0 for dl in dils)
        out_specs = [out_specs] + [
            pl.BlockSpec((1, dl, tm // dl, sub_width), lambda i: (i // tiles_per_seq, 0, i % tiles_per_seq, 0))
            for dl in dils]
        out_shape = [out_shape] + [
            jax.ShapeDtypeStruct((n // seq_len, dl, seq_len // dl, sub_width), BF16) for dl in dils]
    return pl.pallas_call(
        functools.partial(_inproj_body, kinds=tuple(kinds), rope=rope, sub_plan=sub_plan, dils=tuple(dils)),
        grid=(n // tm,),
        in_specs=in_specs,
        out_specs=out_specs,
        out_shape=out_shape,
        scratch_shapes=scratch,
        compiler_params=_params(("parallel",)),
        name="inproj",
    )(*args)


def _outproj_ab_body(x_ref, a_ref, b_ref, w_ref, o_ref):
    ka = a_ref.shape[1]
    acc = jnp.dot(a_ref[...], w_ref[0:ka, :], preferred_element_type=F32)
    acc = acc + jnp.dot(b_ref[...], w_ref[ka:, :], preferred_element_type=F32)
    o_ref[...] = x_ref[...] + acc


def _outproj_ab(x, oa, ob, w, layer, *, tm=512):
    n, d = x.shape
    ka, kb = oa.shape[1], ob.shape[1]
    assert n % tm == 0 and w.shape[1:] == (ka + kb, d)
    return pl.pallas_call(
        _outproj_ab_body,
        grid=(n // tm,),
        in_specs=[
            pl.BlockSpec((tm, d), lambda i: (i, 0)),
            pl.BlockSpec((tm, ka), lambda i: (i, 0)),
            pl.BlockSpec((tm, kb), lambda i: (i, 0)),
            _resident_weight((ka + kb, d), layer),
        ],
        out_specs=pl.BlockSpec((tm, d), lambda i: (i, 0)),
        out_shape=jax.ShapeDtypeStruct((n, d), F32),
        compiler_params=_params(("parallel",)),
        name="outproj_ab",
    )(x, oa, ob, w)


def _outproj_cd_body(*refs, dils, kd):
    ng = len(dils)
    x_ref, c_ref = refs[:2]
    o_refs, l_refs = refs[2:2 + ng], refs[2 + ng:2 + 2 * ng]
    w_ref, o_ref, row_scr = refs[2 + 2 * ng:]
    tm = x_ref.shape[0]
    heads = kd // HEAD_DIM

    def token_rows(ref, d, slot, width):
        if d == 1:
            return ref[...]
        for rho in range(d):
            for s in range(width // LANES):
                lo = rho * width + s * LANES
                row_scr[slot, s, pl.ds(rho, tm // d, stride=d), :] = ref[0, :, lo:lo + LANES]
        return jnp.concatenate([row_scr[slot, s] for s in range(width // LANES)], axis=1)

    outs = [token_rows(r, d, 2 * i, kd) for i, (r, d) in enumerate(zip(o_refs, dils))]
    lses = [token_rows(r, d, 2 * i + 1, LANES) for i, (r, d) in enumerate(zip(l_refs, dils))]
    m = functools.reduce(jnp.maximum, lses)
    es = [jnp.exp(l - m) for l in lses]
    den = functools.reduce(lambda a, b: a + b, es)
    wts = [e / den for e in es]
    parts = []
    for hd in range(heads):
        lane = hd * (LANES // heads)
        cols = slice(hd * HEAD_DIM, (hd + 1) * HEAD_DIM)
        terms = [jnp.broadcast_to(wt[:, lane:lane + 1], (tm, HEAD_DIM)) * o[:, cols] for wt, o in zip(wts, outs)]
        parts.append(functools.reduce(lambda a, b: a + b, terms))
    od = jnp.concatenate(parts, axis=1)
    kc = c_ref.shape[1]
    acc = jnp.dot(c_ref[...], w_ref[0:kc, :], preferred_element_type=F32)
    acc = acc + jnp.dot(od.astype(BF16), w_ref[kc:, :], preferred_element_type=F32)
    o_ref[...] = x_ref[...] + acc


def _outproj_cd(x, oc, outs, lses, dils, seq_len, w, layer, *, tm=512):
    n, d = x.shape
    kc = oc.shape[1]
    kd = w.shape[1] - kc
    assert n % tm == 0 and seq_len % tm == 0 and all(tm % (dl * SUBLANES) == 0 for dl in dils)
    tiles_per_seq = seq_len // tm
    row = lambda i: (i, 0)
    by_seq = lambda i: (i // tiles_per_seq, i % tiles_per_seq, 0)

    def g_specs(width):
        return [pl.BlockSpec((tm, width), row) if dl == 1 else pl.BlockSpec((1, tm // dl, dl * width), by_seq)
                for dl in dils]

    return pl.pallas_call(
        functools.partial(_outproj_cd_body, dils=tuple(dils), kd=kd),
        grid=(n // tm,),
        in_specs=[pl.BlockSpec((tm, d), row), pl.BlockSpec((tm, kc), row)] + g_specs(kd) + g_specs(LANES)
        + [_resident_weight((kc + kd, d), layer)],
        out_specs=pl.BlockSpec((tm, d), row),
        out_shape=jax.ShapeDtypeStruct((n, d), F32),
        scratch_shapes=[pltpu.VMEM((2 * len(dils), kd // HEAD_DIM, tm, HEAD_DIM), F32)],
        compiler_params=_params(("parallel",)),
        name="outproj_cd",
    )(x, oc, *outs, *lses, w)


def _stack_heads(q, g):
    return jnp.concatenate([q[:, i * HEAD_DIM:(i + 1) * HEAD_DIM] for i in range(g)], axis=0)


def _softmax_pv(s, v):
    m = jnp.max(s, axis=-1, keepdims=True)
    p = jnp.exp(s - m)
    l = jnp.sum(p, axis=-1, keepdims=True)
    o = jnp.dot(p.astype(BF16), v, preferred_element_type=F32) / l
    return o, m, l


def _qk(q, k):
    return lax.dot_general(q, k, (((1,), (1,)), ((), ())), preferred_element_type=F32)


def _dense_body(qt_ref, k_ref, vt_ref, o_ref, s_scr, m_scr, *, g):
    step = pl.program_id(0)
    tq = qt_ref.shape[2] // 2
    t = k_ref.shape[1]
    gq = g * tq
    kc = min(DENSE_KEY_CHUNK, t)

    @pl.when(step == 0)
    def _():
        s_scr[2:4] = jnp.zeros((2,) + s_scr.shape[1:], s_scr.dtype)
        m_scr[2:4] = jnp.zeros((2,) + m_scr.shape[1:], m_scr.dtype)

    def fold(x, op):
        return op(x.reshape(x.shape[0] // SUBLANES, SUBLANES, x.shape[1]), axis=0)

    def run(new, old):
        for j in range(2):
            toks = slice(j * tq, (j + 1) * tq)
            m = jnp.max(m_scr[old + j], axis=0, keepdims=True)
            ot = jnp.zeros((HEAD_DIM, gq), F32)
            l_acc = jnp.zeros((SUBLANES, gq), F32)
            for c in range(t // kc):
                keys = slice(c * kc, (c + 1) * kc)
                p = jnp.exp2(s_scr[old + j, keys, :] - m)
                l_acc = l_acc + fold(p, jnp.sum)
                ot = ot + jnp.dot(vt_ref[0, :, keys], p.astype(BF16), preferred_element_type=F32)
            ot = ot / jnp.sum(l_acc, axis=0, keepdims=True)
            for i in range(g):
                o_ref[0, toks, i * HEAD_DIM:(i + 1) * HEAD_DIM] = ot[:, i * tq:(i + 1) * tq].T.astype(o_ref.dtype)
        for j in range(2):
            toks = slice(j * tq, (j + 1) * tq)
            qt = jnp.concatenate([qt_ref[0, i * HEAD_DIM:(i + 1) * HEAD_DIM, toks] for i in range(g)], axis=1)
            m_acc = None
            for c in range(t // kc):
                keys = slice(c * kc, (c + 1) * kc)
                s = jnp.dot(k_ref[0, keys, :], qt, preferred_element_type=F32)
                s_scr[new + j, keys, :] = s
                m_chunk = fold(s, jnp.max)
                m_acc = m_chunk if m_acc is None else jnp.maximum(m_acc, m_chunk)
            m_scr[new + j] = m_acc

    pl.when(step % 2 == 0)(lambda: run(0, 2))
    pl.when(step % 2 == 1)(lambda: run(2, 0))


def _dense_attention(qt, qkv, vt, k_col, n_kv, g, *, tq=128):
    b, t, _ = qkv.shape
    gw = g * HEAD_DIM
    n_pairs = t // (2 * tq)
    assert t % (2 * tq) == 0 and k_col % HEAD_DIM == 0
    assert qt.shape == (b, n_kv * gw, t) and vt.shape == (b, n_kv * HEAD_DIM, t)
    kb = k_col // HEAD_DIM
    total = b * n_kv * n_pairs

    def where(f):
        return f // (n_kv * n_pairs), (f // n_pairs) % n_kv, f % n_pairs

    def q_map(f):
        n, h, k = where(jnp.minimum(f, total - 1))
        return n, h, k

    def k_map(f):
        n, h, _ = where(jnp.minimum(f, total - 1))
        return n, 0, kb + h

    def v_map(f):
        n, h, _ = where(jnp.maximum(f - 1, 0))
        return n, h, 0

    def o_map(f):
        n, h, k = where(jnp.maximum(f - 1, 0))
        return n, k, h

    return pl.pallas_call(
        functools.partial(_dense_body, g=g),
        grid=(total + 1,),
        in_specs=[
            pl.BlockSpec((1, gw, 2 * tq), q_map),
            pl.BlockSpec((1, t, HEAD_DIM), k_map),
            pl.BlockSpec((1, HEAD_DIM, t), v_map),
        ],
        out_specs=pl.BlockSpec((1, 2 * tq, gw), o_map),
        out_shape=jax.ShapeDtypeStruct((b, t, n_kv * gw), BF16),
        scratch_shapes=[pltpu.VMEM((4, t, g * tq), F32), pltpu.VMEM((4, SUBLANES, g * tq), F32)],
        compiler_params=_params(("arbitrary",)),
        name="dense_attn",
    )(qt, qkv, vt)


def _band_body(*refs, n_kv, g, w, seq, tq, kw, n_cases, has_sink, want_lse, side_by_side):
    refs = list(refs)
    q_ref, k_ref, v_ref, bias_ref = refs[:4]
    refs = refs[4:]
    sink_ref = refs.pop(0) if has_sink else None
    o_ref = refs.pop(0)
    lse_ref = refs.pop(0) if want_lse else None
    gw = g * HEAD_DIM
    qw = n_kv * gw
    nq = seq // tq
    for sq, j in [(a, c) for a in range(q_ref.shape[0]) for c in range(q_ref.shape[1] // tq)]:
        qi = pl.program_id(1) * (q_ref.shape[1] // tq) + j
        kstart = pl.multiple_of(jnp.clip(qi * tq - w, 0, seq - kw), 64)
        case = 0 if n_cases == 1 else jnp.where(qi == 0, 0, jnp.where(qi == nq - 1, 2, 1))
        trows = slice(j * tq, (j + 1) * tq)
        osq, ocol0, lcol0 = (0, sq * qw, sq * LANES) if side_by_side else (sq, 0, 0)
        lse_tile = jnp.zeros((tq, LANES), F32)
        head_of_lane = lax.broadcasted_iota(jnp.int32, (tq, LANES), 1) // (LANES // (n_kv * g))
        for h in range(n_kv):
            k = k_ref[sq, pl.ds(kstart, kw), h * HEAD_DIM:(h + 1) * HEAD_DIM]
            v = v_ref[sq, pl.ds(kstart, kw), h * HEAD_DIM:(h + 1) * HEAD_DIM]
            q = _stack_heads(q_ref[sq, trows, h * gw:(h + 1) * gw], g)
            s = _qk(q, k) + bias_ref[case, h]
            o, m, l = _softmax_pv(s, v)
            lse = m + jnp.log(l)
            for i in range(g):
                rows = slice(i * tq, (i + 1) * tq)
                cols = slice(h * gw + i * HEAD_DIM, h * gw + (i + 1) * HEAD_DIM)
                oi = o[rows]
                if has_sink:
                    oi = oi * jax.nn.sigmoid(lse[rows] - sink_ref[h * g + i])
                o_ref[osq, trows, ocol0 + cols.start:ocol0 + cols.stop] = oi.astype(o_ref.dtype)
                if want_lse:
                    lse_tile = jnp.where(head_of_lane == h * g + i, lse[rows], lse_tile)
        if want_lse:
            lse_ref[osq, trows, lcol0:lcol0 + LANES] = lse_tile


def _band_geometry(seq, w):
    if seq <= 4 * w:
        return seq, seq, (0,)
    tq = 2 * w if w < 128 else w
    tq = min(tq, 256)
    kw = tq + 2 * w
    assert seq % tq == 0 and seq >= kw
    return tq, kw, (0, -w, -2 * w)


def _band_attention(q_arr, q_col, k_arr, k_col, v_arr, v_col, bias, n_kv, g, w, geometry,
                    sink=None, want_lse=False, out_dtype=BF16, dil=1):
    n, seq, _ = q_arr.shape
    tq, kw, shifts = geometry
    nq = seq // tq
    gw = g * HEAD_DIM
    qw, kvw = n_kv * gw, n_kv * HEAD_DIM
    assert q_col % qw == 0 and k_col % kvw == 0 and v_col % kvw == 0
    qb, kb, vb = q_col // qw, k_col // kvw, v_col // kvw
    tps = next(c for c in BAND_TILES_PER_STEP if nq % c == 0)
    nq_steps = nq // tps
    sps = 1
    if nq == 1:
        sps = next(c for c in BAND_TILES_PER_STEP if n % c == 0 and (dil == 1 or dil % c == 0))
    in_specs = [
        pl.BlockSpec((sps, tps * tq, qw), lambda b, i: (b, i, qb)),
        pl.BlockSpec((sps, seq, kvw), lambda b, i: (b, 0, kb)),
        pl.BlockSpec((sps, seq, kvw), lambda b, i: (b, 0, vb)),
        pl.BlockSpec((len(shifts), n_kv, g * tq, kw), lambda b, i: (0, 0, 0, 0)),
    ]
    args = [q_arr, k_arr, v_arr, bias]
    if sink is not None:
        in_specs.append(pl.BlockSpec(memory_space=pltpu.SMEM))
        args.append(sink.astype(F32))
    def out_like(width, dtype):
        if dil == 1:
            return (pl.BlockSpec((sps, tps * tq, width), lambda b, i: (b, i, 0)),
                    jax.ShapeDtypeStruct((n, seq, width), dtype))
        return (pl.BlockSpec((1, tps * tq, sps * width),
                             lambda b, i: ((b * sps) // dil, i, ((b * sps) % dil) // sps)),
                jax.ShapeDtypeStruct((n // dil, seq, dil * width), dtype))

    assert dil == 1 or (n % dil == 0 and dil % sps == 0)
    out_specs, out_shape = out_like(qw, out_dtype)
    if want_lse:
        assert LANES % (n_kv * g) == 0
        l_spec, l_shape = out_like(LANES, F32)
        out_specs, out_shape = [out_specs, l_spec], [out_shape, l_shape]
    return pl.pallas_call(
        functools.partial(_band_body, n_kv=n_kv, g=g, w=w, seq=seq, tq=tq, kw=kw, n_cases=len(shifts),
                          has_sink=sink is not None, want_lse=want_lse, side_by_side=dil > 1),
        grid=(n // sps, nq_steps),
        in_specs=in_specs,
        out_specs=out_specs,
        out_shape=out_shape,
        compiler_params=_params(("parallel", "arbitrary")),
        name="band_attn",
    )(*args)


def _na_body(q_ref, k_ref, v_ref, bias_ref, o_ref, *, g, rows, rows_per_step):
    rb = pl.program_id(2)
    kwin = NA_ROWS * GRID_W

    def one_row(rr, carry):
        r = rb * rows_per_step + rr
        rs = jnp.clip(r - NA_ROWS // 2, 0, rows - NA_ROWS)
        off = rs - r + NA_ROWS - 1
        kstart = pl.multiple_of(rs * GRID_W, GRID_W)
        qstart = pl.multiple_of(rr * GRID_W, GRID_W)
        k = k_ref[0, pl.ds(kstart, kwin), :]
        v = v_ref[0, pl.ds(kstart, kwin), :]
        q = _stack_heads(q_ref[0, pl.ds(qstart, GRID_W), :], g)
        s = _qk(q, k) + bias_ref[0, off]
        o, _, _ = _softmax_pv(s, v)
        for i in range(g):
            o_ref[0, pl.ds(qstart, GRID_W), i * HEAD_DIM:(i + 1) * HEAD_DIM] = (
                o[i * GRID_W:(i + 1) * GRID_W].astype(o_ref.dtype))
        return carry

    lax.fori_loop(0, rows_per_step, one_row, 0, unroll=True)


def _na_attention(qkv, q_col, k_col, v_col, bias, n_kv, g, *, rows_per_step=NA_ROWS_PER_STEP):
    b, t, _ = qkv.shape
    rows = t // GRID_W
    gw = g * HEAD_DIM
    assert rows % rows_per_step == 0 and rows >= NA_ROWS
    qb, kb, vb = q_col // gw, k_col // HEAD_DIM, v_col // HEAD_DIM
    tq = rows_per_step * GRID_W
    return pl.pallas_call(
        functools.partial(_na_body, g=g, rows=rows, rows_per_step=rows_per_step),
        grid=(b, n_kv, rows // rows_per_step),
        in_specs=[
            pl.BlockSpec((1, tq, gw), lambda n, h, i: (n, i, qb + h)),
            pl.BlockSpec((1, t, HEAD_DIM), lambda n, h, i: (n, 0, kb + h)),
            pl.BlockSpec((1, t, HEAD_DIM), lambda n, h, i: (n, 0, vb + h)),
            pl.BlockSpec((1, NA_ROWS, g * GRID_W, NA_ROWS * GRID_W), lambda n, h, i: (h, 0, 0, 0)),
        ],
        out_specs=pl.BlockSpec((1, tq, gw), lambda n, h, i: (n, i, h)),
        out_shape=jax.ShapeDtypeStruct((b, t, n_kv * gw), BF16),
        compiler_params=_params(("parallel", "parallel", "arbitrary")),
        name="na_attn",
    )(qkv, qkv, qkv, bias)


def _t5_bucket(rel):
    half = T5_BUCKETS // 2
    max_exact = half // 2
    n = jnp.abs(rel)
    nf = jnp.maximum(n, 1).astype(jnp.float32)
    large = max_exact + (jnp.log(nf / max_exact) / math.log(T5_MAX_DIST / max_exact)
                         * (half - max_exact)).astype(jnp.int32)
    large = jnp.minimum(large, half - 1)
    return jnp.where(rel > 0, half, 0) + jnp.where(n < max_exact, n, large)


def _band_bias(table_cols, n_kv, g, w, dil, geometry):
    tq, kw, shifts = geometry
    iq = jnp.arange(tq)[:, None]
    jk = jnp.arange(kw)[None, :]
    tiles = []
    for shift in shifts:
        rel = jk + shift - iq
        bucket = _t5_bucket(rel * dil)
        tb = jnp.zeros((tq, kw, table_cols.shape[1]), F32)
        for r in range(T5_BUCKETS):
            tb = jnp.where((bucket == r)[..., None], table_cols[r].astype(F32), tb)
        tb = jnp.where((jnp.abs(rel) <= w)[..., None], tb, NEG_INF)
        tiles.append(jnp.moveaxis(tb, -1, 0).reshape(n_kv, g * tq, kw))
    return jnp.stack(tiles)


def _na_bias(rpb, n_kv, g):
    col = jnp.arange(GRID_W)
    qc, kc = col[:, None], col[None, :]
    dc = jnp.clip(kc - qc + NA_COLS - 1, 0, 2 * NA_COLS - 2)
    cs = jnp.clip(qc - NA_COLS // 2, 0, GRID_W - NA_COLS)
    mask = (kc >= cs) & (kc < cs + NA_COLS)
    rpb = rpb.astype(F32)
    t = jnp.zeros(rpb.shape[:2] + dc.shape, F32)
    for c in range(2 * NA_COLS - 1):
        t = jnp.where(dc == c, rpb[:, :, c][:, :, None, None], t)
    t = jnp.where(mask, t, NEG_INF)
    cls = jnp.stack([t[:, o:o + NA_ROWS] for o in range(NA_ROWS)])
    cls = cls.reshape(NA_ROWS, n_kv, g, NA_ROWS, GRID_W, GRID_W)
    cls = jnp.transpose(cls, (1, 0, 2, 4, 3, 5))
    return cls.reshape(n_kv, NA_ROWS, g * GRID_W, NA_ROWS * GRID_W)


def _rope_tables(t):
    n_pairs = HEAD_DIM // 2
    n_freq = n_pairs // 2
    pos = jnp.arange(t)
    row = (pos // GRID_W).astype(jnp.float32)
    col = (pos % GRID_W).astype(jnp.float32)
    omega = ROPE_THETA ** (-(jnp.arange(n_freq, dtype=jnp.float32) * 2.0 / n_pairs))
    ang = jnp.concatenate([row[:, None] * omega, col[:, None] * omega], axis=-1)
    cos, sin = jnp.cos(ang), jnp.sin(ang)
    return jnp.concatenate([cos, cos], axis=-1), jnp.concatenate([-sin, sin], axis=-1)


def _deinterleave_heads(w, n_heads):
    lead = w.shape[:-1]
    w = w.reshape(lead + (n_heads, HEAD_DIM // 2, 2))
    return jnp.swapaxes(w, -1, -2).reshape(lead + (n_heads * HEAD_DIM,))


AB_KINDS = ("scale",) * 4 + ("qnorm",) * 4 + ("plain", "plain", "knorm", "plain_t")
CD_KINDS = ("scale",) * 4 + ("plain",) * 2 + ("scale",) * 6 + ("plain",) * 4


def kernel(x_prompt, x_sample, norm_ffn1, ffn1_w_in, ffn1_w_out, norm_mix, ab_w_in, ab_sink, ab_q_gain, ab_k_gain, ab_w_out, cd_w_in, cd_rpb, cd_w_out, norm_ffn2, ffn2_w_in, ffn2_w_out, t5_table, final_norm):
    (bp, t, d), bs = x_prompt.shape, x_sample.shape[0]
    assert x_sample.shape[1:] == (t, d)
    b = bp + bs
    depth = norm_ffn1.shape[0]

    ffn1_w_in, ffn1_w_out = ffn1_w_in.astype(BF16), ffn1_w_out.astype(BF16)
    ffn2_w_in, ffn2_w_out = ffn2_w_in.astype(BF16), ffn2_w_out.astype(BF16)
    qa_w, kva_w = A_HEADS * HEAD_DIM, 2 * A_KV * HEAD_DIM
    qb_lo = qa_w + kva_w
    qb_hi = qb_lo + B_HEADS * HEAD_DIM
    kb_hi = qb_hi + B_KV * HEAD_DIM
    ab_w_in = jnp.concatenate([
        ab_w_in[..., :qa_w],
        _deinterleave_heads(ab_w_in[..., qb_lo:qb_hi], B_HEADS),
        ab_w_in[..., qa_w:qb_lo],
        _deinterleave_heads(ab_w_in[..., qb_hi:kb_hi], B_KV),
        ab_w_in[..., kb_hi:]], axis=-1).astype(BF16)
    ab_q_gain = _deinterleave_heads(ab_q_gain, 1)
    ab_k_gain = _deinterleave_heads(ab_k_gain, 1)
    ab_w_out, cd_w_in, cd_w_out = ab_w_out.astype(BF16), cd_w_in.astype(BF16), cd_w_out.astype(BF16)

    cos, sin = _rope_tables(t)
    ga, gc = A_HEADS // A_KV, C_HEADS // C_KV
    geo_a = _band_geometry(t, A_WINDOW)
    bias_a = _band_bias(t5_table[:, :A_HEADS], A_KV, ga, A_WINDOW, 1, geo_a)
    d_groups = []
    for gi, (win, dil) in enumerate(D_PAIRS):
        hs = win // (2 * dil)
        geo = _band_geometry(t // dil, hs)
        col0 = A_HEADS + gi * D_SLOTS
        d_groups.append((dil, hs, geo, _band_bias(t5_table[:, col0:col0 + D_SLOTS], D_SLOTS, 1, hs, dil, geo)))

    a_q = 0
    a_k = qa_w + B_HEADS * HEAD_DIM
    a_v = a_k + A_KV * HEAD_DIM
    b_k = a_v + A_KV * HEAD_DIM
    c_q, c_k, c_v = 0, C_HEADS * HEAD_DIM, (C_HEADS + C_KV) * HEAD_DIM
    d_q = (C_HEADS + 2 * C_KV) * HEAD_DIM
    d_k = d_q + D_HEADS * HEAD_DIM
    d_v = d_k + D_SLOTS * HEAD_DIM
    dw = D_SLOTS * HEAD_DIM
    sub_dils = tuple(dil for _, dil in D_PAIRS if dil > 1)
    sub_plan = {}
    for gi, (_, dil) in enumerate(D_PAIRS):
        if dil > 1:
            si = sub_dils.index(dil)
            for src, dst in ((d_q + gi * dw, 0), (d_k, dw), (d_v, 2 * dw)):
                for off in range(0, dw, MXU_COLS):
                    sub_plan.setdefault((src + off) // MXU_COLS, []).append((si, dst + off))

    for l in range(depth):
        i = l // 2
        if l == 0:
            x = _ffn(x_prompt.reshape(bp * t, d), norm_ffn1[l], ffn1_w_in, ffn1_w_out, l, out_rows=b * t)
            x = _ffn(x_sample.reshape(bs * t, d), norm_ffn1[l], ffn1_w_in, ffn1_w_out, l,
                     out_row0=bp * t, dest=x)
        else:
            x = _ffn(x, norm_ffn1[l], ffn1_w_in, ffn1_w_out, l)
        if l % 2 == 0:
            qkv, qbt, vbt = _inproj(x, norm_mix[l], ab_w_in, i, AB_KINDS, t,
                                    (ab_q_gain[i], ab_k_gain[i], cos, sin))
            qkv = qkv.reshape(b, t, -1)
            o_a = _band_attention(qkv, a_q, qkv, a_k, qkv, a_v, bias_a, A_KV, ga, A_WINDOW, geo_a,
                                  sink=ab_sink[i])
            o_b = _dense_attention(qbt, qkv, vbt, b_k, B_KV, B_HEADS // B_KV)
            x = _outproj_ab(x, o_a.reshape(b * t, -1), o_b.reshape(b * t, -1), ab_w_out, i)
        else:
            qkv, *subs = _inproj(x, norm_mix[l], cd_w_in, i, CD_KINDS, t,
                                 sub_args=(sub_dils, 3 * dw, sub_plan))
            qkv = qkv.reshape(b, t, -1)
            o_c = _na_attention(qkv, c_q, c_k, c_v, _na_bias(cd_rpb[i], C_KV, gc), C_KV, gc)
            outs, lses = [], []
            for gi, (dil, hs, geo, bias_d) in enumerate(d_groups):
                if dil == 1:
                    o_g, lse_g = _band_attention(qkv, d_q + gi * dw, qkv, d_k, qkv, d_v, bias_d,
                                                 D_SLOTS, 1, hs, geo, want_lse=True, out_dtype=F32)
                    o_g, lse_g = o_g.reshape(b * t, dw), lse_g.reshape(b * t, LANES)
                else:
                    sub = subs[sub_dils.index(dil)].reshape(b * dil, t // dil, 3 * dw)
                    o_g, lse_g = _band_attention(sub, 0, sub, dw, sub, 2 * dw, bias_d, D_SLOTS, 1, hs, geo,
                                                 want_lse=True, out_dtype=F32, dil=dil)
                outs.append(o_g)
                lses.append(lse_g)
            x = _outproj_cd(x, o_c.reshape(b * t, -1), outs, lses, [g[0] for g in d_groups], t, cd_w_out, i)
        if l < depth - 1:
            x = _ffn(x, norm_ffn2[l], ffn2_w_in, ffn2_w_out, l)

    last = depth - 1
    y_prompt = _ffn(x, norm_ffn2[last], ffn2_w_in, ffn2_w_out, last, final_norm, x_rows=bp * t)
    y_sample = _ffn(x, norm_ffn2[last], ffn2_w_in, ffn2_w_out, last, final_norm, x_row0=bp * t)
    return (y_prompt.reshape(bp, t, d), y_sample.reshape(bs, t, d))
```

```python
import functools
import math

import jax
import jax.numpy as jnp
from jax import lax
from jax.experimental import pallas as pl
from jax.experimental.pallas import tpu as pltpu

HEAD_DIM = 128
GRID_W = 64
A_HEADS, A_KV, A_WINDOW = 8, 2, 128
B_HEADS, B_KV = 8, 2
ROPE_THETA = 10000.0
C_HEADS, C_KV = 8, 2
NA_ROWS, NA_COLS = 8, 16
D_PAIRS = ((128, 1), (512, 4), (2048, 16))
D_SLOTS = 4
D_HEADS = D_SLOTS * len(D_PAIRS)
T5_BUCKETS = 32
T5_MAX_DIST = 2048
NORM_EPS = 1e-6
NEG_INF = -1e30
ATTN_SCALE = HEAD_DIM ** -0.5
LOG2_E = math.log2(math.e)

LANES = 128
SUBLANES = 8
MXU_COLS = 256
FFN_ROW_TILE = 1024
FFN_FF_TILE = 2 * MXU_COLS
DENSE_KEY_CHUNK = MXU_COLS
CHEAP_ROW_STRIDE = 4
BAND_TILES_PER_STEP = (8, 4, 2, 1)
NA_ROWS_PER_STEP = 16
MIB = 1024 * 1024
VMEM_LIMIT_BYTES = 56 * MIB

BF16 = jnp.bfloat16
F32 = jnp.float32


def _params(semantics):
    return pltpu.CompilerParams(dimension_semantics=semantics, vmem_limit_bytes=VMEM_LIMIT_BYTES)


def _resident_weight(shape, layer):
    return pl.BlockSpec((None,) + shape, lambda *_: (layer,) + (0,) * len(shape),
                        pipeline_mode=pl.Buffered(1))


def _rms(x, gain):
    ms = jnp.mean(x * x, axis=-1, keepdims=True)
    return x * lax.rsqrt(ms + NORM_EPS) * gain


def _ffn_body(*refs, n_ff, has_dest, final_norm):
    refs = list(refs)
    x_ref, gain_ref, wg_ref, wu_ref, wo_ref = [refs.pop(0) for _ in range(5)]
    fgain_ref = refs.pop(0) if final_norm else None
    if has_dest:
        refs.pop(0)
    o_ref, h_scr = refs
    j = pl.program_id(1)

    def step(first):
        if first:
            h_scr[...] = _rms(x_ref[...], gain_ref[...]).astype(BF16)
        h = h_scr[...]
        g = jnp.dot(h, wg_ref[...], preferred_element_type=F32)
        u = jnp.dot(h, wu_ref[...], preferred_element_type=F32)
        a = (g * jax.nn.sigmoid(g) * (0.5 * u)).astype(BF16)
        y = jnp.dot(a, wo_ref[...], preferred_element_type=F32)
        if first:
            o_ref[...] = x_ref[...] + y
        else:
            o_ref[...] += y

    pl.when(j == 0)(lambda: step(True))
    pl.when(j > 0)(lambda: step(False))

    if final_norm:
        @pl.when(j == n_ff - 1)
        def _():
            o_ref[...] = _rms(o_ref[...], fgain_ref[...])


def _ffn(x, gain, w_in, w_out, layer, final_gain=None, *, x_row0=0, x_rows=None, out_rows=None,
         out_row0=0, dest=None, tm=FFN_ROW_TILE):
    d = x.shape[1]
    x_rows = x.shape[0] - x_row0 if x_rows is None else x_rows
    out_rows = (x_rows if dest is None else dest.shape[0]) if out_rows is None else out_rows
    f = w_out.shape[1]
    tf = math.gcd(FFN_FF_TILE, f)
    n_ff = f // tf
    tm = math.gcd(tm, x_rows, x_row0, out_row0)
    assert w_in.shape[1:] == (d, 2 * f) and w_out.shape[2] == d and out_row0 + x_rows <= out_rows
    final_norm = final_gain is not None
    xt0, ot0 = x_row0 // tm, out_row0 // tm
    in_specs = [
        pl.BlockSpec((tm, d), lambda i, j: (i + xt0, 0)),
        pl.BlockSpec((1, d), lambda i, j: (0, 0)),
        pl.BlockSpec((None, d, tf), lambda i, j: (layer, 0, j)),
        pl.BlockSpec((None, d, tf), lambda i, j: (layer, 0, j + n_ff)),
        pl.BlockSpec((None, tf, d), lambda i, j: (layer, j, 0)),
    ]
    args = [x, gain.reshape(1, d), w_in, w_in, w_out]
    if final_norm:
        in_specs.append(pl.BlockSpec((1, d), lambda i, j: (0, 0)))
        args.append(final_gain.reshape(1, d))
    aliases = {}
    if dest is not None:
        assert dest.shape == (out_rows, d) and dest.dtype == F32
        in_specs.append(pl.BlockSpec(memory_space=pl.ANY))
        aliases = {len(args): 0}
        args.append(dest)
    return pl.pallas_call(
        functools.partial(_ffn_body, n_ff=n_ff, has_dest=dest is not None, final_norm=final_norm),
        grid=(x_rows // tm, n_ff),
        in_specs=in_specs,
        out_specs=pl.BlockSpec((tm, d), lambda i, j: (i + ot0, 0)),
        out_shape=jax.ShapeDtypeStruct((out_rows, d), F32),
        scratch_shapes=[pltpu.VMEM((tm, d), BF16)],
        input_output_aliases=aliases,
        compiler_params=_params(("parallel", "arbitrary")),
        name="ffn",
    )(*args)


def _rope_norm(r, gain, cos, sin):
    y = _rms(r, gain)
    return y * cos + pltpu.roll(y, HEAD_DIM // 2, axis=1) * sin


def _inproj_body(*refs, kinds, rope, sub_plan, dils):
    if rope:
        x_ref, gain_ref, w_ref, qg_ref, kg_ref, cos_ref, sin_ref, o_ref, qt_ref, vt_ref = refs
    else:
        x_ref, gain_ref, w_ref, o_ref = refs[:4]
        sub_refs, slab_scr = refs[4:4 + len(dils)], refs[-1] if dils else None
    tm = x_ref.shape[0]
    n_parts = 1 if dils else 2
    for part, c, kind in [(a, b, k) for a in range(n_parts) for b, k in enumerate(kinds)]:
        rows = slice(part * (tm // n_parts), (part + 1) * (tm // n_parts))
        if c == 0:
            h = _rms(x_ref[rows, :], gain_ref[...]).astype(BF16)
            n_qt = 0
        lo = c * MXU_COLS
        r = jnp.dot(h, w_ref[:, lo:lo + MXU_COLS], preferred_element_type=F32)
        if kind == "scale":
            r = r * ATTN_SCALE
        elif kind in ("qnorm", "knorm"):
            hg = qg_ref[...] if kind == "qnorm" else kg_ref[...]
            cos = cos_ref[rows, :]
            sin = sin_ref[rows, :]
            parts = []
            for hh in range(MXU_COLS // HEAD_DIM):
                y = _rope_norm(r[:, hh * HEAD_DIM:(hh + 1) * HEAD_DIM], hg, cos, sin)
                if kind == "qnorm":
                    y = y * (ATTN_SCALE * LOG2_E)
                parts.append(y)
            r = jnp.concatenate(parts, axis=1)
            if kind == "qnorm":
                qt_ref[0, n_qt * MXU_COLS:(n_qt + 1) * MXU_COLS, rows] = r.T.astype(BF16)
                n_qt += 1
        elif kind == "plain_t":
            vt_ref[0, :, rows] = r.T.astype(BF16)
        o_ref[rows, lo:lo + MXU_COLS] = r.astype(BF16)
        if c in sub_plan:
            heads = range(MXU_COLS // HEAD_DIM)
            q_rows = tm // CHEAP_ROW_STRIDE
            for hh in heads:
                slab_scr[0, hh] = r[:, hh * HEAD_DIM:(hh + 1) * HEAD_DIM]
            if any(dils[si] > CHEAP_ROW_STRIDE for si, _ in sub_plan[c]):
                for hh in heads:
                    for r4 in range(CHEAP_ROW_STRIDE):
                        slab_scr[1, hh, r4 * q_rows:(r4 + 1) * q_rows, :] = (
                            slab_scr[0, hh, pl.ds(r4, q_rows, stride=CHEAP_ROW_STRIDE), :])
            for si, col0 in sub_plan[c]:
                d = dils[si]
                for rho in range(d):
                    for hh in heads:
                        if d <= CHEAP_ROW_STRIDE:
                            part = slab_scr[0, hh, pl.ds(rho, tm // d, stride=d), :]
                        else:
                            r4, r2 = rho % CHEAP_ROW_STRIDE, rho // CHEAP_ROW_STRIDE
                            part = slab_scr[1, hh, pl.ds(r4 * q_rows + r2, tm // d,
                                                         stride=d // CHEAP_ROW_STRIDE), :]
                        cols = slice(col0 + hh * HEAD_DIM, col0 + (hh + 1) * HEAD_DIM)
                        sub_refs[si][0, rho, :, cols] = part.astype(BF16)


def _inproj(x, gain, w, layer, kinds, seq_len, rope_args=None, sub_args=None, *, tm=512):
    n, d = x.shape
    c = w.shape[2]
    assert n % tm == 0 and seq_len % tm == 0 and c == MXU_COLS * len(kinds)
    rope = rope_args is not None
    assert not (rope and sub_args)
    dils, sub_width, sub_plan = sub_args if sub_args else ((), 0, {})
    scratch = [pltpu.VMEM((2, MXU_COLS // HEAD_DIM, tm, HEAD_DIM), F32)] if dils else []
    assert all(dl <= CHEAP_ROW_STRIDE or (dl % CHEAP_ROW_STRIDE == 0 and dl <= CHEAP_ROW_STRIDE ** 2)
               for dl in dils)
    in_specs = [
        pl.BlockSpec((tm, d), lambda i: (i, 0)),
        pl.BlockSpec((1, d), lambda i: (0, 0)),
        _resident_weight((d, c), layer),
    ]
    args = [x, gain.reshape(1, d), w]
    if rope:
        q_gain, k_gain, cos, sin = rope_args
        tiles_per_seq = seq_len // tm
        in_specs += [
            pl.BlockSpec((1, HEAD_DIM), lambda i: (0, 0)),
            pl.BlockSpec((1, HEAD_DIM), lambda i: (0, 0)),
            pl.BlockSpec((tm, HEAD_DIM), lambda i: (i % tiles_per_seq, 0)),
            pl.BlockSpec((tm, HEAD_DIM), lambda i: (i % tiles_per_seq, 0)),
        ]
        args += [q_gain.reshape(1, HEAD_DIM), k_gain.reshape(1, HEAD_DIM), cos, sin]
    out_specs = pl.BlockSpec((tm, c), lambda i: (i, 0))
    out_shape = jax.ShapeDtypeStruct((n, c), BF16)
    if rope:
        assert kinds.count("plain_t") == 1
        qt_rows = MXU_COLS * kinds.count("qnorm")
        by_seq = lambda i: (i // tiles_per_seq, 0, i % tiles_per_seq)
        out_specs = [out_specs, pl.BlockSpec((1, qt_rows, tm), by_seq),
                     pl.BlockSpec((1, MXU_COLS, tm), by_seq)]
        out_shape = [out_shape, jax.ShapeDtypeStruct((n // seq_len, qt_rows, seq_len), BF16),
                     jax.ShapeDtypeStruct((n // seq_len, MXU_COLS, seq_len), BF16)]
    if dils:
        tiles_per_seq = seq_len // tm
        assert all(tm % (dl * 16) == 0 for dl in dils)
        out_specs = [out_specs] + [
            pl.BlockSpec((1, dl, tm // dl, sub_width), lambda i: (i // tiles_per_seq, 0, i % tiles_per_seq, 0))
            for dl in dils]
        out_shape = [out_shape] + [
            jax.ShapeDtypeStruct((n // seq_len, dl, seq_len // dl, sub_width), BF16) for dl in dils]
    return pl.pallas_call(
        functools.partial(_inproj_body, kinds=tuple(kinds), rope=rope, sub_plan=sub_plan, dils=tuple(dils)),
        grid=(n // tm,),
        in_specs=in_specs,
        out_specs=out_specs,
        out_shape=out_shape,
        scratch_shapes=scratch,
        compiler_params=_params(("parallel",)),
        name="inproj",
    )(*args)


def _outproj_ab_body(x_ref, a_ref, b_ref, w_ref, o_ref):
    ka = a_ref.shape[1]
    acc = jnp.dot(a_ref[...], w_ref[0:ka, :], preferred_element_type=F32)
    acc = acc + jnp.dot(b_ref[...], w_ref[ka:, :], preferred_element_type=F32)
    o_ref[...] = x_ref[...] + acc


def _outproj_ab(x, oa, ob, w, layer, *, tm=512):
    n, d = x.shape
    ka, kb = oa.shape[1], ob.shape[1]
    assert n % tm == 0 and w.shape[1:] == (ka + kb, d)
    return pl.pallas_call(
        _outproj_ab_body,
        grid=(n // tm,),
        in_specs=[
            pl.BlockSpec((tm, d), lambda i: (i, 0)),
            pl.BlockSpec((tm, ka), lambda i: (i, 0)),
            pl.BlockSpec((tm, kb), lambda i: (i, 0)),
            _resident_weight((ka + kb, d), layer),
        ],
        out_specs=pl.BlockSpec((tm, d), lambda i: (i, 0)),
        out_shape=jax.ShapeDtypeStruct((n, d), F32),
        compiler_params=_params(("parallel",)),
        name="outproj_ab",
    )(x, oa, ob, w)


def _outproj_cd_body(*refs, dils, kd):
    ng = len(dils)
    x_ref, c_ref = refs[:2]
    o_refs, l_refs = refs[2:2 + ng], refs[2 + ng:2 + 2 * ng]
    w_ref, o_ref, row_scr = refs[2 + 2 * ng:]
    tm = x_ref.shape[0]
    heads = kd // HEAD_DIM

    def token_rows(ref, d, slot, width):
        if d == 1:
            return ref[...]
        for rho in range(d):
            for s in range(width // LANES):
                lo = rho * width + s * LANES
                row_scr[slot, s, pl.ds(rho, tm // d, stride=d), :] = ref[0, :, lo:lo + LANES]
        return jnp.concatenate([row_scr[slot, s] for s in range(width // LANES)], axis=1)

    outs = [token_rows(r, d, 2 * i, kd) for i, (r, d) in enumerate(zip(o_refs, dils))]
    lses = [token_rows(r, d, 2 * i + 1, LANES) for i, (r, d) in enumerate(zip(l_refs, dils))]
    m = functools.reduce(jnp.maximum, lses)
    es = [jnp.exp(l - m) for l in lses]
    den = functools.reduce(lambda a, b: a + b, es)
    wts = [e / den for e in es]
    parts = []
    for hd in range(heads):
        lane = hd * (LANES // heads)
        cols = slice(hd * HEAD_DIM, (hd + 1) * HEAD_DIM)
        terms = [jnp.broadcast_to(wt[:, lane:lane + 1], (tm, HEAD_DIM)) * o[:, cols] for wt, o in zip(wts, outs)]
        parts.append(functools.reduce(lambda a, b: a + b, terms))
    od = jnp.concatenate(parts, axis=1)
    kc = c_ref.shape[1]
    acc = jnp.dot(c_ref[...], w_ref[0:kc, :], preferred_element_type=F32)
    acc = acc + jnp.dot(od.astype(BF16), w_ref[kc:, :], preferred_element_type=F32)
    o_ref[...] = x_ref[...] + acc


def _outproj_cd(x, oc, outs, lses, dils, seq_len, w, layer, *, tm=512):
    n, d = x.shape
    kc = oc.shape[1]
    kd = w.shape[1] - kc
    assert n % tm == 0 and seq_len % tm == 0 and all(tm % (dl * SUBLANES) == 0 for dl in dils)
    tiles_per_seq = seq_len // tm
    row = lambda i: (i, 0)
    by_seq = lambda i: (i // tiles_per_seq, i % tiles_per_seq, 0)

    def g_specs(width):
        return [pl.BlockSpec((tm, width), row) if dl == 1 else pl.BlockSpec((1, tm // dl, dl * width), by_seq)
                for dl in dils]

    return pl.pallas_call(
        functools.partial(_outproj_cd_body, dils=tuple(dils), kd=kd),
        grid=(n // tm,),
        in_specs=[pl.BlockSpec((tm, d), row), pl.BlockSpec((tm, kc), row)] + g_specs(kd) + g_specs(LANES)
        + [_resident_weight((kc + kd, d), layer)],
        out_specs=pl.BlockSpec((tm, d), row),
        out_shape=jax.ShapeDtypeStruct((n, d), F32),
        scratch_shapes=[pltpu.VMEM((2 * len(dils), kd // HEAD_DIM, tm, HEAD_DIM), F32)],
        compiler_params=_params(("parallel",)),
        name="outproj_cd",
    )(x, oc, *outs, *lses, w)


def _stack_heads(q, g):
    return jnp.concatenate([q[:, i * HEAD_DIM:(i + 1) * HEAD_DIM] for i in range(g)], axis=0)


def _softmax_pv(s, v):
    m = jnp.max(s, axis=-1, keepdims=True)
    p = jnp.exp(s - m)
    l = jnp.sum(p, axis=-1, keepdims=True)
    o = jnp.dot(p.astype(BF16), v, preferred_element_type=F32) / l
    return o, m, l


def _qk(q, k):
    return lax.dot_general(q, k, (((1,), (1,)), ((), ())), preferred_element_type=F32)


def _dense_body(qt_ref, k_ref, vt_ref, o_ref, s_scr, p_scr, m_scr, l_scr, *, g):
    step = pl.program_id(0)
    tq = qt_ref.shape[2] // 2
    t = k_ref.shape[1]
    gq = g * tq
    kc = min(DENSE_KEY_CHUNK, t)

    @pl.when(step == 0)
    def _():
        s_scr[1] = jnp.zeros(s_scr.shape[1:], s_scr.dtype)
        m_scr[1] = jnp.zeros(m_scr.shape[1:], m_scr.dtype)
        p_scr[...] = jnp.zeros(p_scr.shape, p_scr.dtype)
        l_scr[...] = jnp.ones(l_scr.shape, l_scr.dtype)

    def fold(x, op):
        return op(x.reshape(x.shape[0] // SUBLANES, SUBLANES, x.shape[1]), axis=0)

    for cur in range(2):
        prev = 1 - cur
        toks = slice(cur * tq, (cur + 1) * tq)
        qt = jnp.concatenate([qt_ref[0, i * HEAD_DIM:(i + 1) * HEAD_DIM, toks] for i in range(g)], axis=1)
        m = jnp.max(m_scr[prev], axis=0, keepdims=True)
        ot = jnp.zeros((HEAD_DIM, gq), F32)
        l_acc = jnp.zeros((SUBLANES, gq), F32)
        m_acc = None
        for c in range(t // kc):
            keys = slice(c * kc, (c + 1) * kc)
            p = jnp.exp2(s_scr[prev, keys, :] - m)
            l_acc = l_acc + fold(p, jnp.sum)
            p_scr[prev, keys, :] = p.astype(BF16)
        for c in range(t // kc):
            keys = slice(c * kc, (c + 1) * kc)
            ot = ot + jnp.dot(vt_ref[0, :, keys], p_scr[cur, keys, :], preferred_element_type=F32)
            s = jnp.dot(k_ref[0, keys, :], qt, preferred_element_type=F32)
            s_scr[cur, keys, :] = s
            m_chunk = fold(s, jnp.max)
            m_acc = m_chunk if m_acc is None else jnp.maximum(m_acc, m_chunk)
        ot = ot / jnp.sum(l_scr[cur], axis=0, keepdims=True)
        l_scr[prev] = l_acc
        m_scr[cur] = m_acc
        for i in range(g):
            o_ref[0, toks, i * HEAD_DIM:(i + 1) * HEAD_DIM] = ot[:, i * tq:(i + 1) * tq].T.astype(o_ref.dtype)


def _dense_attention(qt, qkv, vt, k_col, n_kv, g, *, tq=128):
    b, t, _ = qkv.shape
    gw = g * HEAD_DIM
    n_pairs = t // (2 * tq)
    assert t % (2 * tq) == 0 and k_col % HEAD_DIM == 0
    assert qt.shape == (b, n_kv * gw, t) and vt.shape == (b, n_kv * HEAD_DIM, t)
    kb = k_col // HEAD_DIM
    total = b * n_kv * n_pairs

    def where(f):
        return f // (n_kv * n_pairs), (f // n_pairs) % n_kv, f % n_pairs

    def q_map(f):
        n, h, k = where(jnp.minimum(f, total - 1))
        return n, h, k

    def k_map(f):
        n, h, _ = where(jnp.minimum(f, total - 1))
        return n, 0, kb + h

    def v_map(f):
        n, h, _ = where(jnp.maximum(f - 1, 0))
        return n, h, 0

    def o_map(f):
        n, h, k = where(jnp.maximum(f - 1, 0))
        return n, k, h

    return pl.pallas_call(
        functools.partial(_dense_body, g=g),
        grid=(total + 1,),
        in_specs=[
            pl.BlockSpec((1, gw, 2 * tq), q_map),
            pl.BlockSpec((1, t, HEAD_DIM), k_map),
            pl.BlockSpec((1, HEAD_DIM, t), v_map),
        ],
        out_specs=pl.BlockSpec((1, 2 * tq, gw), o_map),
        out_shape=jax.ShapeDtypeStruct((b, t, n_kv * gw), BF16),
        scratch_shapes=[pltpu.VMEM((2, t, g * tq), F32), pltpu.VMEM((2, t, g * tq), BF16),
                        pltpu.VMEM((2, SUBLANES, g * tq), F32), pltpu.VMEM((2, SUBLANES, g * tq), F32)],
        compiler_params=_params(("arbitrary",)),
        name="dense_attn",
    )(qt, qkv, vt)


def _band_body(*refs, n_kv, g, w, seq, tq, kw, n_cases, has_sink, want_lse, side_by_side):
    refs = list(refs)
    q_ref, k_ref, v_ref, bias_ref = refs[:4]
    refs = refs[4:]
    sink_ref = refs.pop(0) if has_sink else None
    o_ref = refs.pop(0)
    lse_ref = refs.pop(0) if want_lse else None
    gw = g * HEAD_DIM
    qw = n_kv * gw
    nq = seq // tq
    for sq, j in [(a, c) for a in range(q_ref.shape[0]) for c in range(q_ref.shape[1] // tq)]:
        qi = pl.program_id(1) * (q_ref.shape[1] // tq) + j
        kstart = pl.multiple_of(jnp.clip(qi * tq - w, 0, seq - kw), 64)
        case = 0 if n_cases == 1 else jnp.where(qi == 0, 0, jnp.where(qi == nq - 1, 2, 1))
        trows = slice(j * tq, (j + 1) * tq)
        osq, ocol0, lcol0 = (0, sq * qw, sq * LANES) if side_by_side else (sq, 0, 0)
        lse_tile = jnp.zeros((tq, LANES), F32)
        head_of_lane = lax.broadcasted_iota(jnp.int32, (tq, LANES), 1) // (LANES // (n_kv * g))
        for h in range(n_kv):
            k = k_ref[sq, pl.ds(kstart, kw), h * HEAD_DIM:(h + 1) * HEAD_DIM]
            v = v_ref[sq, pl.ds(kstart, kw), h * HEAD_DIM:(h + 1) * HEAD_DIM]
            q = _stack_heads(q_ref[sq, trows, h * gw:(h + 1) * gw], g)
            s = _qk(q, k) + bias_ref[case, h]
            o, m, l = _softmax_pv(s, v)
            lse = m + jnp.log(l)
            for i in range(g):
                rows = slice(i * tq, (i + 1) * tq)
                cols = slice(h * gw + i * HEAD_DIM, h * gw + (i + 1) * HEAD_DIM)
                oi = o[rows]
                if has_sink:
                    oi = oi * jax.nn.sigmoid(lse[rows] - sink_ref[h * g + i])
                o_ref[osq, trows, ocol0 + cols.start:ocol0 + cols.stop] = oi.astype(o_ref.dtype)
                if want_lse:
                    lse_tile = jnp.where(head_of_lane == h * g + i, lse[rows], lse_tile)
        if want_lse:
            lse_ref[osq, trows, lcol0:lcol0 + LANES] = lse_tile


def _band_geometry(seq, w):
    if seq <= 4 * w:
        return seq, seq, (0,)
    tq = 2 * w if w < 128 else w
    tq = min(tq, 256)
    kw = tq + 2 * w
    assert seq % tq == 0 and seq >= kw
    return tq, kw, (0, -w, -2 * w)


def _band_attention(q_arr, q_col, k_arr, k_col, v_arr, v_col, bias, n_kv, g, w, geometry,
                    sink=None, want_lse=False, out_dtype=BF16, dil=1):
    n, seq, _ = q_arr.shape
    tq, kw, shifts = geometry
    nq = seq // tq
    gw = g * HEAD_DIM
    qw, kvw = n_kv * gw, n_kv * HEAD_DIM
    assert q_col % qw == 0 and k_col % kvw == 0 and v_col % kvw == 0
    qb, kb, vb = q_col // qw, k_col // kvw, v_col // kvw
    tps = next(c for c in BAND_TILES_PER_STEP if nq % c == 0)
    nq_steps = nq // tps
    sps = 1
    if nq == 1:
        sps = next(c for c in BAND_TILES_PER_STEP if n % c == 0 and (dil == 1 or dil % c == 0))
    in_specs = [
        pl.BlockSpec((sps, tps * tq, qw), lambda b, i: (b, i, qb)),
        pl.BlockSpec((sps, seq, kvw), lambda b, i: (b, 0, kb)),
        pl.BlockSpec((sps, seq, kvw), lambda b, i: (b, 0, vb)),
        pl.BlockSpec((len(shifts), n_kv, g * tq, kw), lambda b, i: (0, 0, 0, 0)),
    ]
    args = [q_arr, k_arr, v_arr, bias]
    if sink is not None:
        in_specs.append(pl.BlockSpec(memory_space=pltpu.SMEM))
        args.append(sink.astype(F32))
    def out_like(width, dtype):
        if dil == 1:
            return (pl.BlockSpec((sps, tps * tq, width), lambda b, i: (b, i, 0)),
                    jax.ShapeDtypeStruct((n, seq, width), dtype))
        return (pl.BlockSpec((1, tps * tq, sps * width),
                             lambda b, i: ((b * sps) // dil, i, ((b * sps) % dil) // sps)),
                jax.ShapeDtypeStruct((n // dil, seq, dil * width), dtype))

    assert dil == 1 or (n % dil == 0 and dil % sps == 0)
    out_specs, out_shape = out_like(qw, out_dtype)
    if want_lse:
        assert LANES % (n_kv * g) == 0
        l_spec, l_shape = out_like(LANES, F32)
        out_specs, out_shape = [out_specs, l_spec], [out_shape, l_shape]
    return pl.pallas_call(
        functools.partial(_band_body, n_kv=n_kv, g=g, w=w, seq=seq, tq=tq, kw=kw, n_cases=len(shifts),
                          has_sink=sink is not None, want_lse=want_lse, side_by_side=dil > 1),
        grid=(n // sps, nq_steps),
        in_specs=in_specs,
        out_specs=out_specs,
        out_shape=out_shape,
        compiler_params=_params(("parallel", "arbitrary")),
        name="band_attn",
    )(*args)


def _na_body(q_ref, k_ref, v_ref, bias_ref, o_ref, *, g, rows, rows_per_step):
    rb = pl.program_id(2)
    kwin = NA_ROWS * GRID_W

    def one_row(rr, carry):
        r = rb * rows_per_step + rr
        rs = jnp.clip(r - NA_ROWS // 2, 0, rows - NA_ROWS)
        off = rs - r + NA_ROWS - 1
        kstart = pl.multiple_of(rs * GRID_W, GRID_W)
        qstart = pl.multiple_of(rr * GRID_W, GRID_W)
        k = k_ref[0, pl.ds(kstart, kwin), :]
        v = v_ref[0, pl.ds(kstart, kwin), :]
        q = _stack_heads(q_ref[0, pl.ds(qstart, GRID_W), :], g)
        s = _qk(q, k) + bias_ref[0, off]
        o, _, _ = _softmax_pv(s, v)
        for i in range(g):
            o_ref[0, pl.ds(qstart, GRID_W), i * HEAD_DIM:(i + 1) * HEAD_DIM] = (
                o[i * GRID_W:(i + 1) * GRID_W].astype(o_ref.dtype))
        return carry

    lax.fori_loop(0, rows_per_step, one_row, 0, unroll=True)


def _na_attention(qkv, q_col, k_col, v_col, bias, n_kv, g, *, rows_per_step=NA_ROWS_PER_STEP):
    b, t, _ = qkv.shape
    rows = t // GRID_W
    gw = g * HEAD_DIM
    assert rows % rows_per_step == 0 and rows >= NA_ROWS
    qb, kb, vb = q_col // gw, k_col // HEAD_DIM, v_col // HEAD_DIM
    tq = rows_per_step * GRID_W
    return pl.pallas_call(
        functools.partial(_na_body, g=g, rows=rows, rows_per_step=rows_per_step),
        grid=(b, n_kv, rows // rows_per_step),
        in_specs=[
            pl.BlockSpec((1, tq, gw), lambda n, h, i: (n, i, qb + h)),
            pl.BlockSpec((1, t, HEAD_DIM), lambda n, h, i: (n, 0, kb + h)),
            pl.BlockSpec((1, t, HEAD_DIM), lambda n, h, i: (n, 0, vb + h)),
            pl.BlockSpec((1, NA_ROWS, g * GRID_W, NA_ROWS * GRID_W), lambda n, h, i: (h, 0, 0, 0)),
        ],
        out_specs=pl.BlockSpec((1, tq, gw), lambda n, h, i: (n, i, h)),
        out_shape=jax.ShapeDtypeStruct((b, t, n_kv * gw), BF16),
        compiler_params=_params(("parallel", "parallel", "arbitrary")),
        name="na_attn",
    )(qkv, qkv, qkv, bias)


def _t5_bucket(rel):
    half = T5_BUCKETS // 2
    max_exact = half // 2
    n = jnp.abs(rel)
    nf = jnp.maximum(n, 1).astype(jnp.float32)
    large = max_exact + (jnp.log(nf / max_exact) / math.log(T5_MAX_DIST / max_exact)
                         * (half - max_exact)).astype(jnp.int32)
    large = jnp.minimum(large, half - 1)
    return jnp.where(rel > 0, half, 0) + jnp.where(n < max_exact, n, large)


def _band_bias(table_cols, n_kv, g, w, dil, geometry):
    tq, kw, shifts = geometry
    iq = jnp.arange(tq)[:, None]
    jk = jnp.arange(kw)[None, :]
    tiles = []
    for shift in shifts:
        rel = jk + shift - iq
        bucket = _t5_bucket(rel * dil)
        tb = jnp.zeros((tq, kw, table_cols.shape[1]), F32)
        for r in range(T5_BUCKETS):
            tb = jnp.where((bucket == r)[..., None], table_cols[r].astype(F32), tb)
        tb = jnp.where((jnp.abs(rel) <= w)[..., None], tb, NEG_INF)
        tiles.append(jnp.moveaxis(tb, -1, 0).reshape(n_kv, g * tq, kw))
    return jnp.stack(tiles)


def _na_bias(rpb, n_kv, g):
    col = jnp.arange(GRID_W)
    qc, kc = col[:, None], col[None, :]
    dc = jnp.clip(kc - qc + NA_COLS - 1, 0, 2 * NA_COLS - 2)
    cs = jnp.clip(qc - NA_COLS // 2, 0, GRID_W - NA_COLS)
    mask = (kc >= cs) & (kc < cs + NA_COLS)
    rpb = rpb.astype(F32)
    t = jnp.zeros(rpb.shape[:2] + dc.shape, F32)
    for c in range(2 * NA_COLS - 1):
        t = jnp.where(dc == c, rpb[:, :, c][:, :, None, None], t)
    t = jnp.where(mask, t, NEG_INF)
    cls = jnp.stack([t[:, o:o + NA_ROWS] for o in range(NA_ROWS)])
    cls = cls.reshape(NA_ROWS, n_kv, g, NA_ROWS, GRID_W, GRID_W)
    cls = jnp.transpose(cls, (1, 0, 2, 4, 3, 5))
    return cls.reshape(n_kv, NA_ROWS, g * GRID_W, NA_ROWS * GRID_W)


def _rope_tables(t):
    n_pairs = HEAD_DIM // 2
    n_freq = n_pairs // 2
    pos = jnp.arange(t)
    row = (pos // GRID_W).astype(jnp.float32)
    col = (pos % GRID_W).astype(jnp.float32)
    omega = ROPE_THETA ** (-(jnp.arange(n_freq, dtype=jnp.float32) * 2.0 / n_pairs))
    ang = jnp.concatenate([row[:, None] * omega, col[:, None] * omega], axis=-1)
    cos, sin = jnp.cos(ang), jnp.sin(ang)
    return jnp.concatenate([cos, cos], axis=-1), jnp.concatenate([-sin, sin], axis=-1)


def _deinterleave_heads(w, n_heads):
    lead = w.shape[:-1]
    w = w.reshape(lead + (n_heads, HEAD_DIM // 2, 2))
    return jnp.swapaxes(w, -1, -2).reshape(lead + (n_heads * HEAD_DIM,))


AB_KINDS = ("scale",) * 4 + ("qnorm",) * 4 + ("plain", "plain", "knorm", "plain_t")
CD_KINDS = ("scale",) * 4 + ("plain",) * 2 + ("scale",) * 6 + ("plain",) * 4


def kernel(x_prompt, x_sample, norm_ffn1, ffn1_w_in, ffn1_w_out, norm_mix, ab_w_in, ab_sink, ab_q_gain, ab_k_gain, ab_w_out, cd_w_in, cd_rpb, cd_w_out, norm_ffn2, ffn2_w_in, ffn2_w_out, t5_table, final_norm):
    (bp, t, d), bs = x_prompt.shape, x_sample.shape[0]
    assert x_sample.shape[1:] == (t, d)
    b = bp + bs
    depth = norm_ffn1.shape[0]

    ffn1_w_in, ffn1_w_out = ffn1_w_in.astype(BF16), ffn1_w_out.astype(BF16)
    ffn2_w_in, ffn2_w_out = ffn2_w_in.astype(BF16), ffn2_w_out.astype(BF16)
    qa_w, kva_w = A_HEADS * HEAD_DIM, 2 * A_KV * HEAD_DIM
    qb_lo = qa_w + kva_w
    qb_hi = qb_lo + B_HEADS * HEAD_DIM
    kb_hi = qb_hi + B_KV * HEAD_DIM
    ab_w_in = jnp.concatenate([
        ab_w_in[..., :qa_w],
        _deinterleave_heads(ab_w_in[..., qb_lo:qb_hi], B_HEADS),
        ab_w_in[..., qa_w:qb_lo],
        _deinterleave_heads(ab_w_in[..., qb_hi:kb_hi], B_KV),
        ab_w_in[..., kb_hi:]], axis=-1).astype(BF16)
    ab_q_gain = _deinterleave_heads(ab_q_gain, 1)
    ab_k_gain = _deinterleave_heads(ab_k_gain, 1)
    ab_w_out, cd_w_in, cd_w_out = ab_w_out.astype(BF16), cd_w_in.astype(BF16), cd_w_out.astype(BF16)

    cos, sin = _rope_tables(t)
    ga, gc = A_HEADS // A_KV, C_HEADS // C_KV
    geo_a = _band_geometry(t, A_WINDOW)
    bias_a = _band_bias(t5_table[:, :A_HEADS], A_KV, ga, A_WINDOW, 1, geo_a)
    d_groups = []
    for gi, (win, dil) in enumerate(D_PAIRS):
        hs = win // (2 * dil)
        geo = _band_geometry(t // dil, hs)
        col0 = A_HEADS + gi * D_SLOTS
        d_groups.append((dil, hs, geo, _band_bias(t5_table[:, col0:col0 + D_SLOTS], D_SLOTS, 1, hs, dil, geo)))

    a_q = 0
    a_k = qa_w + B_HEADS * HEAD_DIM
    a_v = a_k + A_KV * HEAD_DIM
    b_k = a_v + A_KV * HEAD_DIM
    c_q, c_k, c_v = 0, C_HEADS * HEAD_DIM, (C_HEADS + C_KV) * HEAD_DIM
    d_q = (C_HEADS + 2 * C_KV) * HEAD_DIM
    d_k = d_q + D_HEADS * HEAD_DIM
    d_v = d_k + D_SLOTS * HEAD_DIM
    dw = D_SLOTS * HEAD_DIM
    sub_dils = tuple(dil for _, dil in D_PAIRS if dil > 1)
    sub_plan = {}
    for gi, (_, dil) in enumerate(D_PAIRS):
        if dil > 1:
            si = sub_dils.index(dil)
            for src, dst in ((d_q + gi * dw, 0), (d_k, dw), (d_v, 2 * dw)):
                for off in range(0, dw, MXU_COLS):
                    sub_plan.setdefault((src + off) // MXU_COLS, []).append((si, dst + off))

    for l in range(depth):
        i = l // 2
        if l == 0:
            x = _ffn(x_prompt.reshape(bp * t, d), norm_ffn1[l], ffn1_w_in, ffn1_w_out, l, out_rows=b * t)
            x = _ffn(x_sample.reshape(bs * t, d), norm_ffn1[l], ffn1_w_in, ffn1_w_out, l,
                     out_row0=bp * t, dest=x)
        else:
            x = _ffn(x, norm_ffn1[l], ffn1_w_in, ffn1_w_out, l)
        if l % 2 == 0:
            qkv, qbt, vbt = _inproj(x, norm_mix[l], ab_w_in, i, AB_KINDS, t,
                                    (ab_q_gain[i], ab_k_gain[i], cos, sin))
            qkv = qkv.reshape(b, t, -1)
            o_a = _band_attention(qkv, a_q, qkv, a_k, qkv, a_v, bias_a, A_KV, ga, A_WINDOW, geo_a,
                                  sink=ab_sink[i])
            o_b = _dense_attention(qbt, qkv, vbt, b_k, B_KV, B_HEADS // B_KV)
            x = _outproj_ab(x, o_a.reshape(b * t, -1), o_b.reshape(b * t, -1), ab_w_out, i)
        else:
            qkv, *subs = _inproj(x, norm_mix[l], cd_w_in, i, CD_KINDS, t,
                                 sub_args=(sub_dils, 3 * dw, sub_plan))
            qkv = qkv.reshape(b, t, -1)
            o_c = _na_attention(qkv, c_q, c_k, c_v, _na_bias(cd_rpb[i], C_KV, gc), C_KV, gc)
            outs, lses = [], []
            for gi, (dil, hs, geo, bias_d) in enumerate(d_groups):
                if dil == 1:
                    o_g, lse_g = _band_attention(qkv, d_q + gi * dw, qkv, d_k, qkv, d_v, bias_d,
                                                 D_SLOTS, 1, hs, geo, want_lse=True, out_dtype=F32)
                    o_g, lse_g = o_g.reshape(b * t, dw), lse_g.reshape(b * t, LANES)
                else:
                    sub = subs[sub_dils.index(dil)].reshape(b * dil, t // dil, 3 * dw)
                    o_g, lse_g = _band_attention(sub, 0, sub, dw, sub, 2 * dw, bias_d, D_SLOTS, 1, hs, geo,
                                                 want_lse=True, out_dtype=F32, dil=dil)
                outs.append(o_g)
                lses.append(lse_g)
            x = _outproj_cd(x, o_c.reshape(b * t, -1), outs, lses, [g[0] for g in d_groups], t, cd_w_out, i)
        if l < depth - 1:
            x = _ffn(x, norm_ffn2[l], ffn2_w_in, ffn2_w_out, l)

    last = depth - 1
    y_prompt = _ffn(x, norm_ffn2[last], ffn2_w_in, ffn2_w_out, last, final_norm, x_rows=bp * t)
    y_sample = _ffn(x, norm_ffn2[last], ffn2_w_in, ffn2_w_out, last, final_norm, x_row0=bp * t)
    return (y_prompt.reshape(bp, t, d), y_sample.reshape(bs, t, d))
```
